```python
import math
import jax
import jax.numpy as jnp
from jax import lax
import numpy as np

D_MODEL = 1024
BATCH = 16
SEQ = 256
DEPTH = 2
DEC_BATCH = 2
DEC_SEQ = 1024
PAST_LEN = 256

GRID_W = 64
NORM_EPS = 1e-6

D_LRU = 512
LRU_BLOCKS = 8
LRU_BLOCK_W = D_LRU // LRU_BLOCKS
LRU_CONV = 4
LRU_PAD = (2, 1)
LRU_C = 8.0

D_RWKV = 512
RWKV_HEAD_DIM = 64
N_RWKV_HEADS = D_RWKV // RWKV_HEAD_DIM
W_RANK = 64
A_RANK = 64
G_RANK = 128
RWKV_CONV = 3
RWKV_PAD = (1, 1)
RWKV_LN_EPS = 64e-5

N_DIFF_HEADS = 4
DIFF_HEAD_DIM = 64
D_DIFF = N_DIFF_HEADS * 2 * DIFF_HEAD_DIM
ROPE_THETA = 10000.0

N_BRANCHES = 3

N_EXPERTS = 16
EXPERT_FF = 1024
EC_CAPACITY = 2

RWKV_COLS = 3 * D_RWKV + 2 * W_RANK + 2 * A_RANK + G_RANK
IN_COLS = 2 * D_LRU + RWKV_COLS + 3 * D_DIFF + N_BRANCHES * D_MODEL

kernel_name = 'hybrid_diffusion_lru_rwkv7_diffattn_ec_moe'


def _split(z, sizes):
    parts = []
    start = 0
    for s in sizes:
        parts.append(z[..., start:start + s])
        start += s
    return parts


def rmsnorm(x, g):
    xf = x.astype(jnp.float32)
    y = xf * lax.rsqrt(jnp.mean(xf * xf, axis=-1, keepdims=True) + NORM_EPS)
    return (y * g.astype(jnp.float32)).astype(x.dtype)


def dwconv(x, w, pad):
    return lax.conv_general_dilated(
        x, w.astype(x.dtype)[:, None, :], window_strides=(1,), padding=[pad],
        dimension_numbers=('NWC', 'WIO', 'NWC'), feature_group_count=x.shape[-1])


def axial_rope(x):
    n = x.shape[1]
    rows = n // GRID_W
    row = jnp.repeat(jnp.arange(rows), GRID_W).astype(jnp.float32)
    col = jnp.tile(jnp.arange(GRID_W), rows).astype(jnp.float32)
    nf = DIFF_HEAD_DIM // 4
    freqs = ROPE_THETA ** (-jnp.arange(nf, dtype=jnp.float32) / nf)

    def rot(xh, pos):
        ang = (pos[:, None] * freqs)[None, :, None, None, :]
        cos, sin = jnp.cos(ang), jnp.sin(ang)
        x1, x2 = xh[..., :nf], xh[..., nf:]
        return jnp.concatenate([x1 * cos - x2 * sin, x2 * cos + x1 * sin], axis=-1)

    xf = x.astype(jnp.float32)
    half = DIFF_HEAD_DIM // 2
    out = jnp.concatenate([rot(xf[..., :half], row), rot(xf[..., half:], col)], axis=-1)
    return out.astype(x.dtype)


def rglru_scan(x, wa, ba, wx, bx, lam, h0, reverse):
    B, L, _ = x.shape
    f32 = jnp.float32
    xf = x.astype(f32)
    xb = xf.reshape(B, L, LRU_BLOCKS, LRU_BLOCK_W)
    r = jax.nn.sigmoid(
        jnp.einsum('blnc,ncd->blnd', xb, wa.astype(f32)).reshape(B, L, D_LRU) + ba.astype(f32))
    i = jax.nn.sigmoid(
        jnp.einsum('blnc,ncd->blnd', xb, wx.astype(f32)).reshape(B, L, D_LRU) + bx.astype(f32))
    log_a = -LRU_C * r * jax.nn.softplus(-lam.astype(f32))
    a = jnp.exp(log_a)
    b = jnp.sqrt(-jnp.expm1(2.0 * log_a)) * (i * xf)
    first = -1 if reverse else 0
    last = 0 if reverse else -1
    b = b.at[:, first].add(a[:, first] * h0.astype(f32))

    def combine(c1, c2):
        a1, b1 = c1
        a2, b2 = c2
        return a1 * a2, a2 * b1 + b2

    _, h = lax.associative_scan(combine, (a, b), reverse=reverse, axis=1)
    return h, h[:, last]


def rwkv7_scan(r, w, k, v, kk, a, S0, reverse):
    xs = (jnp.swapaxes(r, 0, 1), jnp.swapaxes(w, 0, 1), jnp.swapaxes(k, 0, 1),
          jnp.swapaxes(v, 0, 1), jnp.swapaxes(kk, 0, 1), jnp.swapaxes(a, 0, 1))

    def step(S, inp):
        r_t, w_t, k_t, v_t, kk_t, a_t = inp
        sa = jnp.einsum('bhvk,bhk->bhv', S, -kk_t)
        S = (S * w_t[:, :, None, :]
             + sa[..., None] * (kk_t * a_t)[:, :, None, :]
             + v_t[..., None] * k_t[:, :, None, :])
        y = jnp.einsum('bhvk,bhk->bhv', S, r_t)
        return S, y

    S_fin, ys = lax.scan(step, S0.astype(jnp.float32), xs, reverse=reverse)
    return jnp.swapaxes(ys, 0, 1), S_fin


def rwkv_branch(zr, p, S0):
    B, L, _ = zr.shape
    f32 = jnp.float32
    r, k, v, wl_f, wl_b, al_f, al_b, gl = _split(
        zr.astype(f32), (D_RWKV, D_RWKV, D_RWKV, W_RANK, W_RANK, A_RANK, A_RANK, G_RANK))

    def heads(t):
        return t.reshape(B, L, N_RWKV_HEADS, RWKV_HEAD_DIM)

    rh, vh = heads(r), heads(v)
    kk = heads(k * p['rwkv_k_k'].astype(f32))
    kk = kk / jnp.maximum(jnp.sqrt(jnp.sum(kk * kk, axis=-1, keepdims=True)), 1e-12)
    r_k = p['rwkv_r_k'].astype(f32)
    outs = []
    states = []
    for d, (wl, al) in enumerate(((wl_f, al_f), (wl_b, al_b))):
        w_log = -jax.nn.softplus(
            -(p['rwkv_w0'][d].astype(f32) + jnp.tanh(wl) @ p['rwkv_w2'][d].astype(f32))) - 0.5
        decay = jnp.exp(-jnp.exp(w_log))
        a = jax.nn.sigmoid(p['rwkv_a0'][d].astype(f32) + al @ p['rwkv_a2'][d].astype(f32))
        kd = heads(k * (1.0 + (a - 1.0) * p['rwkv_k_a'].astype(f32)))
        y, S = rwkv7_scan(rh, heads(decay), kd, vh, kk, heads(a), S0[:, d], reverse=(d == 1))
        outs.append((y, jnp.sum(rh * kd * r_k, axis=-1, keepdims=True) * vh))
        states.append(S)
    y_sum = outs[0][0] + outs[1][0]
    bonus = outs[0][1] + outs[1][1]
    mu = jnp.mean(y_sum, axis=-1, keepdims=True)
    var = jnp.mean(jnp.square(y_sum - mu), axis=-1, keepdims=True)
    yn = ((y_sum - mu) * lax.rsqrt(var + RWKV_LN_EPS)).reshape(B, L, D_RWKV)
    y = yn * p['rwkv_ln_g'].astype(f32) + p['rwkv_ln_b'].astype(f32) + bonus.reshape(B, L, D_RWKV)
    g = jax.nn.sigmoid(gl) @ p['rwkv_g2'].astype(f32)
    return y * g, jnp.stack(states, axis=1)


def diff_branch(q, k, v, p, lam_init, ctx_k, ctx_v):
    B, L, _ = q.shape
    f32 = jnp.float32
    q = q.reshape(B, L, N_DIFF_HEADS, 2, DIFF_HEAD_DIM)
    k = k.reshape(B, L, N_DIFF_HEADS, 2, DIFF_HEAD_DIM)
    v = v.reshape(B, L, N_DIFF_HEADS, 2 * DIFF_HEAD_DIM)
    if ctx_k is None:
        keys, vals = k, v
    else:
        q = axial_rope(q)
        keys = jnp.concatenate([axial_rope(k), ctx_k.astype(k.dtype)], axis=1)
        vals = jnp.concatenate([v, ctx_v.astype(v.dtype)], axis=1)
    lv = p['diff_lambda'].astype(f32)
    lam = jnp.exp(jnp.sum(lv[0] * lv[1])) - jnp.exp(jnp.sum(lv[2] * lv[3])) + lam_init
    s = jnp.einsum('bqhmd,bkhmd->bhmqk', q.astype(f32), keys.astype(f32)) * (DIFF_HEAD_DIM ** -0.5)
    prob = jax.nn.softmax(s, axis=-1)
    attn = prob[:, :, 0] - lam * prob[:, :, 1]
    o = jnp.einsum('bhqk,bkhe->bqhe', attn, vals.astype(f32))
    o = (o * lax.rsqrt(jnp.mean(o * o, axis=-1, keepdims=True) + NORM_EPS)
         * p['diff_subln_g'].astype(f32) * (1.0 - lam_init))
    return o.reshape(B, L, D_DIFF).astype(v.dtype), k, v


def token_mixer(h, p, lam_init, ctx):
    B, L, _ = h.shape
    f32 = jnp.float32
    z = h @ p['w_in']
    lru_x, lru_gate, zr, q, k, v, gate_pre = _split(
        z, (D_LRU, D_LRU, RWKV_COLS, D_DIFF, D_DIFF, D_DIFF, N_BRANCHES * D_MODEL))
    if ctx is None:
        ctx_k = None
        ctx_v = None
        h0 = jnp.zeros((B, 2, D_LRU), f32)
        S0 = jnp.zeros((B, 2, N_RWKV_HEADS, RWKV_HEAD_DIM, RWKV_HEAD_DIM), f32)
    else:
        ctx_k, ctx_v, h0, S0 = ctx

    xc = dwconv(lru_x, p['lru_conv_w'], LRU_PAD) + p['lru_conv_b'].astype(h.dtype)
    y_f, h_f = rglru_scan(xc, p['lru_wa'][0], p['lru_ba'][0], p['lru_wx'][0], p['lru_bx'][0],
                          p['lru_lambda'][0], h0[:, 0], reverse=False)
    y_b, h_b = rglru_scan(xc, p['lru_wa'][1], p['lru_ba'][1], p['lru_wx'][1], p['lru_bx'][1],
                          p['lru_lambda'][1], h0[:, 1], reverse=True)
    lru_out = ((y_f + y_b) * jax.nn.gelu(lru_gate.astype(f32))).astype(h.dtype)

    rwkv_out, S_fin = rwkv_branch(dwconv(zr, p['rwkv_conv_w'], RWKV_PAD), p, S0)

    diff_out, k_c, v_c = diff_branch(q, k, v, p, lam_init, ctx_k, ctx_v)

    gates = jax.nn.sigmoid(gate_pre.astype(f32)).astype(h.dtype)
    g_lru, g_rwkv, g_diff = _split(gates, (D_MODEL, D_MODEL, D_MODEL))
    merged = (g_lru * (lru_out @ p['w_lru_out'])
              + g_rwkv * (rwkv_out.astype(h.dtype) @ p['w_rwkv_out'])
              + g_diff * (diff_out @ p['w_diff_out']))
    out = merged @ p['w_out']
    if ctx is None:
        return out, (k_c, v_c, jnp.stack([h_f, h_b], axis=1), S_fin)
    return out, None


def expert_choice_ffn(h, p):
    N, D = h.shape
    cap = EC_CAPACITY * N // N_EXPERTS
    aff = jax.nn.softmax((h @ p['router_w']).astype(jnp.float32), axis=-1)
    gate, idx = lax.top_k(aff.T, cap)
    xe = h[idx]
    hid = (jax.nn.silu(jnp.einsum('ecd,edf->ecf', xe, p['exp_w_gate']))
           * jnp.einsum('ecd,edf->ecf', xe, p['exp_w_up']))
    ye = jnp.einsum('ecf,efd->ecd', hid, p['exp_w_down']) * gate[..., None].astype(h.dtype)
    return jnp.zeros_like(h).at[idx.reshape(-1)].add(ye.reshape(-1, D))


def layer(x, mod, p, lam_init, ctx):
    shift1, scale1, gate1, shift2, scale2, gate2 = jnp.split(mod, 6, axis=-1)
    h = rmsnorm(x, p['norm_g'][0]) * (1.0 + scale1) + shift1
    mix, new_ctx = token_mixer(h, p, lam_init, ctx)
    x = x + gate1 * mix
    h = rmsnorm(x, p['norm_g'][1]) * (1.0 + scale2) + shift2
    B, L, D = h.shape
    x = x + gate2 * expert_choice_ffn(h.reshape(B * L, D), p).reshape(B, L, D)
    return x, new_ctx


def setup_inputs(seed: int = 0) -> dict:
    key = jax.random.key(seed)
    ks = iter(jax.random.split(key, 64))

    def nrm(shape, scale):
        return scale * jax.random.normal(next(ks), shape, jnp.float32)

    a_lru = jax.random.uniform(next(ks), (DEPTH, 2, D_LRU), jnp.float32, minval=0.9, maxval=0.999)
    shift_init = jnp.array([0.25, 0.5, 0.25], jnp.float32)[None, :, None]
    return {
        'x_prompt': nrm((BATCH, SEQ, D_MODEL), 1.0),
        'x_sample': nrm((DEC_BATCH, DEC_SEQ, D_MODEL), 1.0),
        'cache_k': nrm((DEC_BATCH, DEPTH, PAST_LEN, N_DIFF_HEADS, 2, DIFF_HEAD_DIM), 1.0),
        'cache_v': nrm((DEC_BATCH, DEPTH, PAST_LEN, N_DIFF_HEADS, 2 * DIFF_HEAD_DIM), 1.0),
        'state_lru': nrm((DEC_BATCH, DEPTH, 2, D_LRU), 0.5),
        'state_rwkv': nrm((DEC_BATCH, DEPTH, 2, N_RWKV_HEADS, RWKV_HEAD_DIM, RWKV_HEAD_DIM), 0.1),
        'c': nrm((DEC_BATCH, D_MODEL), 1.0),
        'c_ctx': nrm((D_MODEL,), 1.0),
        'norm_g': 1.0 + nrm((DEPTH, 2, D_MODEL), 0.05),
        'final_norm_g': 1.0 + nrm((D_MODEL,), 0.05),
        'w_mod': nrm((DEPTH, D_MODEL, 6 * D_MODEL), 0.5 * D_MODEL ** -0.5),
        'b_mod': nrm((DEPTH, 6 * D_MODEL), 0.02),
        'w_in': nrm((DEPTH, D_MODEL, IN_COLS), D_MODEL ** -0.5),
        'lru_conv_w': nrm((DEPTH, LRU_CONV, D_LRU), LRU_CONV ** -0.5),
        'lru_conv_b': nrm((DEPTH, D_LRU), 0.02),
        'lru_wa': nrm((DEPTH, 2, LRU_BLOCKS, LRU_BLOCK_W, LRU_BLOCK_W), LRU_BLOCK_W ** -0.5),
        'lru_ba': nrm((DEPTH, 2, D_LRU), 0.02),
        'lru_wx': nrm((DEPTH, 2, LRU_BLOCKS, LRU_BLOCK_W, LRU_BLOCK_W), LRU_BLOCK_W ** -0.5),
        'lru_bx': nrm((DEPTH, 2, D_LRU), 0.02),
        'lru_lambda': jnp.log(a_lru) - jnp.log1p(-a_lru),
        'rwkv_conv_w': shift_init + nrm((DEPTH, RWKV_CONV, RWKV_COLS), 0.05),
        'rwkv_w0': -2.5 + nrm((DEPTH, 2, D_RWKV), 0.7),
        'rwkv_w2': nrm((DEPTH, 2, W_RANK, D_RWKV), 0.1),
        'rwkv_a0': nrm((DEPTH, 2, D_RWKV), 0.1),
        'rwkv_a2': nrm((DEPTH, 2, A_RANK, D_RWKV), A_RANK ** -0.5),
        'rwkv_k_k': 0.85 + nrm((DEPTH, D_RWKV), 0.05),
        'rwkv_k_a': 1.0 + nrm((DEPTH, D_RWKV), 0.05),
        'rwkv_r_k': nrm((DEPTH, N_RWKV_HEADS, RWKV_HEAD_DIM), 0.1),
        'rwkv_g2': nrm((DEPTH, G_RANK, D_RWKV), G_RANK ** -0.5),
        'rwkv_ln_g': 1.0 + nrm((DEPTH, D_RWKV), 0.05),
        'rwkv_ln_b': nrm((DEPTH, D_RWKV), 0.02),
        'diff_lambda': nrm((DEPTH, 4, DIFF_HEAD_DIM), 0.1),
        'diff_subln_g': 1.0 + nrm((DEPTH, 2 * DIFF_HEAD_DIM), 0.05),
        'w_lru_out': nrm((DEPTH, D_LRU, D_MODEL), D_LRU ** -0.5),
        'w_rwkv_out': nrm((DEPTH, D_RWKV, D_MODEL), D_RWKV ** -0.5),
        'w_diff_out': nrm((DEPTH, D_DIFF, D_MODEL), D_DIFF ** -0.5),
        'w_out': nrm((DEPTH, D_MODEL, D_MODEL), D_MODEL ** -0.5),
        'router_w': nrm((DEPTH, D_MODEL, N_EXPERTS), D_MODEL ** -0.5),
        'exp_w_gate': nrm((DEPTH, N_EXPERTS, D_MODEL, EXPERT_FF), D_MODEL ** -0.5),
        'exp_w_up': nrm((DEPTH, N_EXPERTS, D_MODEL, EXPERT_FF), D_MODEL ** -0.5),
        'exp_w_down': nrm((DEPTH, N_EXPERTS, EXPERT_FF, D_MODEL), EXPERT_FF ** -0.5),
    }


def reference(x_prompt, x_sample, cache_k, cache_v, state_lru, state_rwkv, c, c_ctx,
              norm_g, final_norm_g, w_mod, b_mod, w_in, lru_conv_w, lru_conv_b,
              lru_wa, lru_ba, lru_wx, lru_bx, lru_lambda, rwkv_conv_w, rwkv_w0, rwkv_w2,
              rwkv_a0, rwkv_a2, rwkv_k_k, rwkv_k_a, rwkv_r_k, rwkv_g2, rwkv_ln_g, rwkv_ln_b,
              diff_lambda, diff_subln_g, w_lru_out, w_rwkv_out, w_diff_out, w_out,
              router_w, exp_w_gate, exp_w_up, exp_w_down):
    xp = x_prompt
    xs = x_sample
    new_k, new_v, new_lru, new_rwkv = [], [], [], []
    for l in range(DEPTH):
        p = {
            'norm_g': norm_g[l], 'w_in': w_in[l],
            'lru_conv_w': lru_conv_w[l], 'lru_conv_b': lru_conv_b[l],
            'lru_wa': lru_wa[l], 'lru_ba': lru_ba[l], 'lru_wx': lru_wx[l], 'lru_bx': lru_bx[l],
            'lru_lambda': lru_lambda[l],
            'rwkv_conv_w': rwkv_conv_w[l], 'rwkv_w0': rwkv_w0[l], 'rwkv_w2': rwkv_w2[l],
            'rwkv_a0': rwkv_a0[l], 'rwkv_a2': rwkv_a2[l], 'rwkv_k_k': rwkv_k_k[l],
            'rwkv_k_a': rwkv_k_a[l], 'rwkv_r_k': rwkv_r_k[l], 'rwkv_g2': rwkv_g2[l],
            'rwkv_ln_g': rwkv_ln_g[l], 'rwkv_ln_b': rwkv_ln_b[l],
            'diff_lambda': diff_lambda[l], 'diff_subln_g': diff_subln_g[l],
            'w_lru_out': w_lru_out[l], 'w_rwkv_out': w_rwkv_out[l], 'w_diff_out': w_diff_out[l],
            'w_out': w_out[l], 'router_w': router_w[l],
            'exp_w_gate': exp_w_gate[l], 'exp_w_up': exp_w_up[l], 'exp_w_down': exp_w_down[l],
        }
        lam_init = 0.8 - 0.6 * math.exp(-0.3 * l)

        mod_ctx = (jax.nn.silu(c_ctx) @ w_mod[l] + b_mod[l])[None, None, :]
        xp, (k_c, v_c, h_c, S_c) = layer(xp, mod_ctx, p, lam_init, None)
        new_k.append(k_c)
        new_v.append(v_c)
        new_lru.append(h_c.astype(xp.dtype))
        new_rwkv.append(S_c.astype(xp.dtype))

        mod_lat = (jax.nn.silu(c) @ w_mod[l] + b_mod[l])[:, None, :]
        xs, _ = layer(xs, mod_lat, p, lam_init,
                      (cache_k[:, l], cache_v[:, l], state_lru[:, l], state_rwkv[:, l]))

    y_prompt = rmsnorm(xp, final_norm_g)
    y_sample = rmsnorm(xs, final_norm_g)
    new_cache_k = jnp.stack(new_k, axis=1)
    new_cache_v = jnp.stack(new_v, axis=1)
    new_state_lru = jnp.stack(new_lru, axis=1)
    new_state_rwkv = jnp.stack(new_rwkv, axis=1)
    return (y_prompt, y_sample, new_cache_k, new_cache_v, new_state_lru, new_state_rwkv)
```

```python
import functools
import math

import jax
import jax.numpy as jnp
from jax import lax
from jax.experimental import pallas as pl
from jax.experimental.pallas import tpu as pltpu

D_MODEL = 1024
DEPTH = 2
GRID_W = 64
NORM_EPS = 1e-6

D_LRU = 512
LRU_BLOCKS = 8
LRU_BLOCK_W = D_LRU // LRU_BLOCKS
LRU_PAD = (2, 1)
LRU_C = 8.0

D_RWKV = 512
RWKV_HEAD_DIM = 64
N_RWKV_HEADS = D_RWKV // RWKV_HEAD_DIM
W_RANK = 64
A_RANK = 64
G_RANK = 128
RWKV_PAD = (1, 1)
RWKV_LN_EPS = 64e-5

N_DIFF_HEADS = 4
DIFF_HEAD_DIM = 64
D_DIFF = N_DIFF_HEADS * 2 * DIFF_HEAD_DIM
ROPE_THETA = 10000.0

N_BRANCHES = 3
N_EXPERTS = 16
EC_CAPACITY = 2

RWKV_COLS = 3 * D_RWKV + 2 * W_RANK + 2 * A_RANK + G_RANK

LANES = 128
SUBLANES = 8


def _split(z, sizes):
    parts = []
    start = 0
    for s in sizes:
        parts.append(z[..., start:start + s])
        start += s
    return parts


def _rwkv_scan_kernel(w_ref, ka_ref, kd_ref, nkk_ref, r_ref, vv_ref, s0_ref, y_ref, s_ref, *,
                      steps, v_rows, lane_groups, unroll):
    @pl.when(pl.program_id(0) == 0)
    def _():
        s_ref[...] = s0_ref[...]

    def step(t, carry):
        for g in range(lane_groups):
            lanes = slice(g * LANES, (g + 1) * LANES)
            w = w_ref[t, :, lanes]
            ka = ka_ref[t, :, lanes]
            kd = kd_ref[t, :, lanes]
            nkk = nkk_ref[t, :, lanes]
            r = r_ref[t, :, lanes]

            def rows(vb, c):
                v0 = pl.multiple_of(vb * SUBLANES, SUBLANES)
                vv = vv_ref[t, pl.ds(v0, SUBLANES), lanes]
                ys = []
                for j in range(SUBLANES):
                    s = s_ref[v0 + j, :, lanes]
                    sa = jnp.sum(s * nkk, axis=0, keepdims=True)
                    s_new = s * w + ka * sa + kd * vv[j:j + 1]
                    s_ref[v0 + j, :, lanes] = s_new
                    ys.append(jnp.sum(s_new * r, axis=0, keepdims=True))
                y_ref[t, pl.ds(v0, SUBLANES), lanes] = jnp.concatenate(ys, axis=0)
                return c

            lax.fori_loop(0, v_rows // SUBLANES, rows, 0, unroll=unroll)
        return carry

    lax.fori_loop(0, steps, step, 0)


def _rwkv_scan_packed(w, ka, kd, nkk, r, vv, s0, *, steps):
    L, K, NL = w.shape
    VR = vv.shape[1]
    assert L % steps == 0 and NL % LANES == 0
    kvec = pl.BlockSpec((steps, K, NL), lambda i: (i, 0, 0))
    vspec = pl.BlockSpec((steps, VR, NL), lambda i: (i, 0, 0))
    sspec = pl.BlockSpec((VR, K, NL), lambda i: (0, 0, 0))
    kern = functools.partial(_rwkv_scan_kernel, steps=steps, v_rows=VR, lane_groups=NL // LANES,
                             unroll=1)
    y, s_fin = pl.pallas_call(
        kern,
        grid=(L // steps,),
        in_specs=[kvec, kvec, kvec, kvec, kvec, vspec, sspec],
        out_specs=[vspec, sspec],
        out_shape=[jax.ShapeDtypeStruct((L, VR, NL), jnp.float32),
                   jax.ShapeDtypeStruct((VR, K, NL), jnp.float32)],
        compiler_params=pltpu.CompilerParams(dimension_semantics=("arbitrary",),
                                             vmem_limit_bytes=48 * 1024 * 1024),
        name="rwkv7_scan",
    )(w, ka, kd, nkk, r, vv, s0)
    return y, s_fin


def _rwkv_scan_both(rh, vh, kk, per_dir, S0):
    B, L, H, K = rh.shape
    nd = 2
    n_states = nd * B * H
    v_lo = max(1, LANES // n_states)
    VR = K // v_lo

    def by_dir(fn):
        return jnp.stack([fn(0), jnp.flip(fn(1), axis=1)], axis=0)

    def kpack(x):
        x = jnp.transpose(x, (2, 4, 0, 1, 3)).reshape(L, K, n_states)
        if v_lo > 1:
            x = jnp.tile(x, (1, 1, v_lo))
        return x

    w = kpack(by_dir(lambda d: per_dir[d][0]))
    kd = kpack(by_dir(lambda d: per_dir[d][1]))
    ka = kpack(by_dir(lambda d: kk * per_dir[d][2]))
    nkk = kpack(by_dir(lambda d: -kk))
    r = kpack(by_dir(lambda d: rh))
    vv = by_dir(lambda d: vh).reshape(nd, B, L, H, VR, v_lo)
    vv = jnp.transpose(vv, (2, 4, 5, 0, 1, 3)).reshape(L, VR, v_lo * n_states)
    s0 = S0.astype(jnp.float32).reshape(B, nd, H, VR, v_lo, K)
    s0 = jnp.transpose(s0, (3, 5, 4, 1, 0, 2)).reshape(VR, K, v_lo * n_states)

    steps = 8 if v_lo == 1 else 16
    y, s_fin = _rwkv_scan_packed(w, ka, kd, nkk, r, vv, s0, steps=steps)

    y = y.reshape(L, VR, v_lo, nd, B, H)
    y = jnp.transpose(y, (3, 4, 0, 5, 1, 2)).reshape(nd, B, L, H, K)
    y_f = y[0]
    y_b = jnp.flip(y[1], axis=1)
    s_fin = s_fin.reshape(VR, K, v_lo, nd, B, H)
    s_fin = jnp.transpose(s_fin, (4, 3, 5, 0, 2, 1)).reshape(B, nd, H, K, K)
    return y_f, y_b, s_fin


def _rmsnorm(x, g):
    xf = x.astype(jnp.float32)
    y = xf * lax.rsqrt(jnp.mean(xf * xf, axis=-1, keepdims=True) + NORM_EPS)
    return (y * g.astype(jnp.float32)).astype(x.dtype)


def _dwconv(x, w, pad):
    return lax.conv_general_dilated(
        x, w.astype(x.dtype)[:, None, :], window_strides=(1,), padding=[pad],
        dimension_numbers=('NWC', 'WIO', 'NWC'), feature_group_count=x.shape[-1])


def _axial_rope(x):
    n = x.shape[1]
    rows = n // GRID_W
    row = jnp.repeat(jnp.arange(rows), GRID_W).astype(jnp.float32)
    col = jnp.tile(jnp.arange(GRID_W), rows).astype(jnp.float32)
    nf = DIFF_HEAD_DIM // 4
    freqs = ROPE_THETA ** (-jnp.arange(nf, dtype=jnp.float32) / nf)

    def rot(xh, pos):
        ang = (pos[:, None] * freqs)[None, :, None, None, :]
        cos, sin = jnp.cos(ang), jnp.sin(ang)
        x1, x2 = xh[..., :nf], xh[..., nf:]
        return jnp.concatenate([x1 * cos - x2 * sin, x2 * cos + x1 * sin], axis=-1)

    xf = x.astype(jnp.float32)
    half = DIFF_HEAD_DIM // 2
    out = jnp.concatenate([rot(xf[..., :half], row), rot(xf[..., half:], col)], axis=-1)
    return out.astype(x.dtype)


def _rglru_scan(x, wa, ba, wx, bx, lam, h0, reverse):
    B, L, _ = x.shape
    f32 = jnp.float32
    xf = x.astype(f32)
    xb = xf.reshape(B, L, LRU_BLOCKS, LRU_BLOCK_W)
    r = jax.nn.sigmoid(
        jnp.einsum('blnc,ncd->blnd', xb, wa.astype(f32)).reshape(B, L, D_LRU) + ba.astype(f32))
    i = jax.nn.sigmoid(
        jnp.einsum('blnc,ncd->blnd', xb, wx.astype(f32)).reshape(B, L, D_LRU) + bx.astype(f32))
    log_a = -LRU_C * r * jax.nn.softplus(-lam.astype(f32))
    a = jnp.exp(log_a)
    b = jnp.sqrt(-jnp.expm1(2.0 * log_a)) * (i * xf)
    first = -1 if reverse else 0
    last = 0 if reverse else -1
    b = b.at[:, first].add(a[:, first] * h0.astype(f32))

    def combine(c1, c2):
        a1, b1 = c1
        a2, b2 = c2
        return a1 * a2, a2 * b1 + b2

    _, h = lax.associative_scan(combine, (a, b), reverse=reverse, axis=1)
    return h, h[:, last]


def _rwkv_branch(zr, p, S0):
    B, L, _ = zr.shape
    f32 = jnp.float32
    r, k, v, wl_f, wl_b, al_f, al_b, gl = _split(
        zr.astype(f32), (D_RWKV, D_RWKV, D_RWKV, W_RANK, W_RANK, A_RANK, A_RANK, G_RANK))

    def heads(t):
        return t.reshape(B, L, N_RWKV_HEADS, RWKV_HEAD_DIM)

    rh, vh = heads(r), heads(v)
    kk = heads(k * p['rwkv_k_k'].astype(f32))
    kk = kk / jnp.maximum(jnp.sqrt(jnp.sum(kk * kk, axis=-1, keepdims=True)), 1e-12)
    r_k = p['rwkv_r_k'].astype(f32)
    per_dir = []
    bonus = 0.0
    for d, (wl, al) in enumerate(((wl_f, al_f), (wl_b, al_b))):
        w_log = -jax.nn.softplus(
            -(p['rwkv_w0'][d].astype(f32) + jnp.tanh(wl) @ p['rwkv_w2'][d].astype(f32))) - 0.5
        decay = jnp.exp(-jnp.exp(w_log))
        a = jax.nn.sigmoid(p['rwkv_a0'][d].astype(f32) + al @ p['rwkv_a2'][d].astype(f32))
        kd = heads(k * (1.0 + (a - 1.0) * p['rwkv_k_a'].astype(f32)))
        per_dir.append((heads(decay), kd, heads(a)))
        bonus = bonus + jnp.sum(rh * kd * r_k, axis=-1, keepdims=True) * vh
    y_f, y_b, S_fin = _rwkv_scan_both(rh, vh, kk, per_dir, S0)
    y_sum = y_f + y_b
    mu = jnp.mean(y_sum, axis=-1, keepdims=True)
    var = jnp.mean(jnp.square(y_sum - mu), axis=-1, keepdims=True)
    yn = ((y_sum - mu) * lax.rsqrt(var + RWKV_LN_EPS)).reshape(B, L, D_RWKV)
    y = yn * p['rwkv_ln_g'].astype(f32) + p['rwkv_ln_b'].astype(f32) + bonus.reshape(B, L, D_RWKV)
    g = jax.nn.sigmoid(gl) @ p['rwkv_g2'].astype(f32)
    return y * g, S_fin


def _diff_branch(q, k, v, p, lam_init, ctx_k, ctx_v):
    B, L, _ = q.shape
    f32 = jnp.float32
    q = q.reshape(B, L, N_DIFF_HEADS, 2, DIFF_HEAD_DIM)
    k = k.reshape(B, L, N_DIFF_HEADS, 2, DIFF_HEAD_DIM)
    v = v.reshape(B, L, N_DIFF_HEADS, 2 * DIFF_HEAD_DIM)
    if ctx_k is None:
        keys, vals = k, v
    else:
        q = _axial_rope(q)
        keys = jnp.concatenate([_axial_rope(k), ctx_k.astype(k.dtype)], axis=1)
        vals = jnp.concatenate([v, ctx_v.astype(v.dtype)], axis=1)
    lv = p['diff_lambda'].astype(f32)
    lam = jnp.exp(jnp.sum(lv[0] * lv[1])) - jnp.exp(jnp.sum(lv[2] * lv[3])) + lam_init
    s = jnp.einsum('bqhmd,bkhmd->bhmqk', q.astype(f32), keys.astype(f32)) * (DIFF_HEAD_DIM ** -0.5)
    prob = jax.nn.softmax(s, axis=-1)
    attn = prob[:, :, 0] - lam * prob[:, :, 1]
    o = jnp.einsum('bhqk,bkhe->bqhe', attn, vals.astype(f32))
    o = (o * lax.rsqrt(jnp.mean(o * o, axis=-1, keepdims=True) + NORM_EPS)
         * p['diff_subln_g'].astype(f32) * (1.0 - lam_init))
    return o.reshape(B, L, D_DIFF).astype(v.dtype), k, v


def _token_mixer(h, p, lam_init, ctx):
    B, L, _ = h.shape
    f32 = jnp.float32
    z = h @ p['w_in']
    lru_x, lru_gate, zr, q, k, v, gate_pre = _split(
        z, (D_LRU, D_LRU, RWKV_COLS, D_DIFF, D_DIFF, D_DIFF, N_BRANCHES * D_MODEL))
    if ctx is None:
        ctx_k = None
        ctx_v = None
        h0 = jnp.zeros((B, 2, D_LRU), f32)
        S0 = jnp.zeros((B, 2, N_RWKV_HEADS, RWKV_HEAD_DIM, RWKV_HEAD_DIM), f32)
    else:
        ctx_k, ctx_v, h0, S0 = ctx

    xc = _dwconv(lru_x, p['lru_conv_w'], LRU_PAD) + p['lru_conv_b'].astype(h.dtype)
    y_f, h_f = _rglru_scan(xc, p['lru_wa'][0], p['lru_ba'][0], p['lru_wx'][0], p['lru_bx'][0],
                           p['lru_lambda'][0], h0[:, 0], reverse=False)
    y_b, h_b = _rglru_scan(xc, p['lru_wa'][1], p['lru_ba'][1], p['lru_wx'][1], p['lru_bx'][1],
                           p['lru_lambda'][1], h0[:, 1], reverse=True)
    lru_out = ((y_f + y_b) * jax.nn.gelu(lru_gate.astype(f32))).astype(h.dtype)

    rwkv_out, S_fin = _rwkv_branch(_dwconv(zr, p['rwkv_conv_w'], RWKV_PAD), p, S0)

    diff_out, k_c, v_c = _diff_branch(q, k, v, p, lam_init, ctx_k, ctx_v)

    gates = jax.nn.sigmoid(gate_pre.astype(f32)).astype(h.dtype)
    g_lru, g_rwkv, g_diff = _split(gates, (D_MODEL, D_MODEL, D_MODEL))
    merged = (g_lru * (lru_out @ p['w_lru_out'])
              + g_rwkv * (rwkv_out.astype(h.dtype) @ p['w_rwkv_out'])
              + g_diff * (diff_out @ p['w_diff_out']))
    out = merged @ p['w_out']
    if ctx is None:
        return out, (k_c, v_c, jnp.stack([h_f, h_b], axis=1), S_fin)
    return out, None


def _expert_choice_ffn(h, p):
    N, D = h.shape
    cap = EC_CAPACITY * N // N_EXPERTS
    aff = jax.nn.softmax((h @ p['router_w']).astype(jnp.float32), axis=-1)
    gate, idx = lax.top_k(aff.T, cap)
    xe = h[idx]
    hid = (jax.nn.silu(jnp.einsum('ecd,edf->ecf', xe, p['exp_w_gate']))
           * jnp.einsum('ecd,edf->ecf', xe, p['exp_w_up']))
    ye = jnp.einsum('ecf,efd->ecd', hid, p['exp_w_down']) * gate[..., None].astype(h.dtype)
    return jnp.zeros_like(h).at[idx.reshape(-1)].add(ye.reshape(-1, D))


def _layer(x, mod, p, lam_init, ctx):
    shift1, scale1, gate1, shift2, scale2, gate2 = jnp.split(mod, 6, axis=-1)
    h = _rmsnorm(x, p['norm_g'][0]) * (1.0 + scale1) + shift1
    mix, new_ctx = _token_mixer(h, p, lam_init, ctx)
    x = x + gate1 * mix
    h = _rmsnorm(x, p['norm_g'][1]) * (1.0 + scale2) + shift2
    B, L, D = h.shape
    x = x + gate2 * _expert_choice_ffn(h.reshape(B * L, D), p).reshape(B, L, D)
    return x, new_ctx


def kernel(x_prompt, x_sample, cache_k, cache_v, state_lru, state_rwkv, c, c_ctx, norm_g, final_norm_g, w_mod, b_mod, w_in, lru_conv_w, lru_conv_b, lru_wa, lru_ba, lru_wx, lru_bx, lru_lambda, rwkv_conv_w, rwkv_w0, rwkv_w2, rwkv_a0, rwkv_a2, rwkv_k_k, rwkv_k_a, rwkv_r_k, rwkv_g2, rwkv_ln_g, rwkv_ln_b, diff_lambda, diff_subln_g, w_lru_out, w_rwkv_out, w_diff_out, w_out, router_w, exp_w_gate, exp_w_up, exp_w_down):
    xp = x_prompt
    xs = x_sample
    new_k, new_v, new_lru, new_rwkv = [], [], [], []
    for l in range(DEPTH):
        p = {
            'norm_g': norm_g[l], 'w_in': w_in[l],
            'lru_conv_w': lru_conv_w[l], 'lru_conv_b': lru_conv_b[l],
            'lru_wa': lru_wa[l], 'lru_ba': lru_ba[l], 'lru_wx': lru_wx[l], 'lru_bx': lru_bx[l],
            'lru_lambda': lru_lambda[l],
            'rwkv_conv_w': rwkv_conv_w[l], 'rwkv_w0': rwkv_w0[l], 'rwkv_w2': rwkv_w2[l],
            'rwkv_a0': rwkv_a0[l], 'rwkv_a2': rwkv_a2[l], 'rwkv_k_k': rwkv_k_k[l],
            'rwkv_k_a': rwkv_k_a[l], 'rwkv_r_k': rwkv_r_k[l], 'rwkv_g2': rwkv_g2[l],
            'rwkv_ln_g': rwkv_ln_g[l], 'rwkv_ln_b': rwkv_ln_b[l],
            'diff_lambda': diff_lambda[l], 'diff_subln_g': diff_subln_g[l],
            'w_lru_out': w_lru_out[l], 'w_rwkv_out': w_rwkv_out[l], 'w_diff_out': w_diff_out[l],
            'w_out': w_out[l], 'router_w': router_w[l],
            'exp_w_gate': exp_w_gate[l], 'exp_w_up': exp_w_up[l], 'exp_w_down': exp_w_down[l],
        }
        lam_init = 0.8 - 0.6 * math.exp(-0.3 * l)

        mod_ctx = (jax.nn.silu(c_ctx) @ w_mod[l] + b_mod[l])[None, None, :]
        xp, (k_c, v_c, h_c, S_c) = _layer(xp, mod_ctx, p, lam_init, None)
        new_k.append(k_c)
        new_v.append(v_c)
        new_lru.append(h_c.astype(xp.dtype))
        new_rwkv.append(S_c.astype(xp.dtype))

        mod_lat = (jax.nn.silu(c) @ w_mod[l] + b_mod[l])[:, None, :]
        xs, _ = _layer(xs, mod_lat, p, lam_init,
                       (cache_k[:, l], cache_v[:, l], state_lru[:, l], state_rwkv[:, l]))

    y_prompt = _rmsnorm(xp, final_norm_g)
    y_sample = _rmsnorm(xs, final_norm_g)
    return (y_prompt, y_sample, jnp.stack(new_k, axis=1), jnp.stack(new_v, axis=1),
            jnp.stack(new_lru, axis=1), jnp.stack(new_rwkv, axis=1))
```

```python
import functools
import math

import jax
import jax.numpy as jnp
from jax import lax
from jax.experimental import pallas as pl
from jax.experimental.pallas import tpu as pltpu

f32 = jnp.float32
bf16 = jnp.bfloat16

D_MODEL = 1024
DEPTH = 2
GRID_W = 64
NORM_EPS = 1e-6

N_CTX_SEQ = 16
CTX_LEN = 256
N_LAT_SEQ = 2
LAT_LEN = 1024
PAST_LEN = 256
N_CTX_TOK = N_CTX_SEQ * CTX_LEN
N_TOK = N_CTX_TOK + N_LAT_SEQ * LAT_LEN

D_LRU = 512
LRU_BLOCKS = 8
LRU_BLOCK_W = D_LRU // LRU_BLOCKS
LRU_C = 8.0

D_RWKV = 512
RWKV_HEAD_DIM = 64
N_RWKV_HEADS = D_RWKV // RWKV_HEAD_DIM
W_RANK = 64
A_RANK = 64
G_RANK = 128
RWKV_LN_EPS = 64e-5
RWKV_COLS = 3 * D_RWKV + 2 * W_RANK + 2 * A_RANK + G_RANK
RWKV_COLS_PAD = 2048

N_DIFF_HEADS = 4
DIFF_HEAD_DIM = 64
D_DIFF = N_DIFF_HEADS * 2 * DIFF_HEAD_DIM
ROPE_THETA = 10000.0

N_BRANCHES = 3
N_EXPERTS = 16
EXPERT_FF = 1024
EC_CAPACITY = 2
ROUTER_PAD = 128

LANES = 128
SUBLANES = 8

SEQ_BLOCK = 1024
N_SEQ_BLOCKS = N_TOK // SEQ_BLOCK
N_CTX_BLOCKS = N_CTX_TOK // SEQ_BLOCK
N_MOD_GROUPS = 1 + N_LAT_SEQ

Z_RWKV = 0
Z_GATES = RWKV_COLS_PAD
Z_LRU_X = Z_GATES + N_BRANCHES * D_MODEL
Z_LRU_G = Z_LRU_X + D_LRU
Z_Q = Z_LRU_G + D_LRU
Z_K = Z_Q + D_DIFF
Z_V = Z_K + D_DIFF
Z_COLS = Z_V + D_DIFF

VMEM_LIMIT = 56 * 1024 * 1024


def _cparams(n_axes):
    return pltpu.CompilerParams(dimension_semantics=("arbitrary",) * n_axes,
                                vmem_limit_bytes=VMEM_LIMIT)


def _seg_len(block_idx):
    return jnp.where(block_idx < N_CTX_BLOCKS, CTX_LEN, LAT_LEN)


def _mod_group_of_rows(row_block, rows_per_block):
    first_lat = N_CTX_TOK // rows_per_block
    per_seq = LAT_LEN // rows_per_block
    return jnp.where(row_block < first_lat, 0, (row_block - first_lat) // per_seq + 1)


def _shift_rows(v, d, tm, seg):
    n = v.shape[0]
    r = pltpu.roll(v, (-d) % n, axis=0)
    ok = (tm + d >= 0) & (tm + d < seg)
    return jnp.where(ok, r, 0.0)


def _split_dot(x, w_bf16):
    hi = x.astype(bf16)
    lo = (x - hi.astype(f32)).astype(bf16)
    return (jnp.dot(hi, w_bf16, preferred_element_type=f32)
            + jnp.dot(lo, w_bf16, preferred_element_type=f32))


MOD_TN = 1536


def _mod_kernel(c_ref, w_ref, b_ref, o_ref):
    c = c_ref[...]
    s = (c * jax.nn.sigmoid(c)).astype(bf16)
    o_ref[0] = jnp.dot(s, w_ref[0].astype(bf16), preferred_element_type=f32) + b_ref[0]


def _modulation(cvec, w_mod, b_mod):
    n_out = w_mod.shape[-1]
    return pl.pallas_call(
        _mod_kernel,
        grid=(DEPTH, n_out // MOD_TN),
        in_specs=[pl.BlockSpec((SUBLANES, D_MODEL), lambda l, j: (0, 0)),
                  pl.BlockSpec((1, D_MODEL, MOD_TN), lambda l, j: (l, 0, j)),
                  pl.BlockSpec((1, 1, MOD_TN), lambda l, j: (l, 0, j))],
        out_specs=pl.BlockSpec((1, SUBLANES, MOD_TN), lambda l, j: (l, 0, j)),
        out_shape=jax.ShapeDtypeStruct((DEPTH, SUBLANES, n_out), f32),
        compiler_params=_cparams(2),
        name="modulation",
    )(cvec, w_mod, b_mod.reshape(DEPTH, 1, n_out))


INPROJ_TM = 1024
INPROJ_TN = 512


def _inproj_kernel(x_ref, mod_ref, g_ref, w_ref, o_ref, h_ref):
    @pl.when(pl.program_id(1) == 0)
    def _():
        x = x_ref[...]
        y = x * lax.rsqrt(jnp.mean(x * x, axis=-1, keepdims=True) + NORM_EPS) * g_ref[...]
        shift = mod_ref[0, :, 0:D_MODEL]
        scale = mod_ref[0, :, D_MODEL:2 * D_MODEL]
        h_ref[...] = (y * (1.0 + scale) + shift).astype(bf16)

    o_ref[...] = jnp.dot(h_ref[...], w_ref[...], preferred_element_type=f32)


def _in_projection(x, mod_l, norm_g_row, w_in_p):
    return pl.pallas_call(
        _inproj_kernel,
        grid=(N_TOK // INPROJ_TM, Z_COLS // INPROJ_TN),
        in_specs=[pl.BlockSpec((INPROJ_TM, D_MODEL), lambda i, j: (i, 0)),
                  pl.BlockSpec((1, 1, 2 * D_MODEL),
                               lambda i, j: (_mod_group_of_rows(i, INPROJ_TM), 0, 0)),
                  pl.BlockSpec((1, D_MODEL), lambda i, j: (0, 0)),
                  pl.BlockSpec((D_MODEL, INPROJ_TN), lambda i, j: (0, j))],
        out_specs=pl.BlockSpec((INPROJ_TM, INPROJ_TN), lambda i, j: (i, j)),
        out_shape=jax.ShapeDtypeStruct((N_TOK, Z_COLS), f32),
        scratch_shapes=[pltpu.VMEM((INPROJ_TM, D_MODEL), bf16)],
        compiler_params=_cparams(2),
        name="in_projection",
    )(x, mod_l, norm_g_row, w_in_p)


def _lru_kernel(x_ref, gate_ref, cw_ref, cb_ref, wg_ref, bg_ref, lam_ref, h0_ref, out_ref, hfin_ref):
    blk = pl.program_id(0)
    seg = _seg_len(blk)
    n = x_ref.shape[0]
    t = lax.broadcasted_iota(jnp.int32, (n, D_LRU), 0)
    tm = t & (seg - 1)

    x = x_ref[...]
    xc = (cb_ref[...]
          + cw_ref[0:1, :] * _shift_rows(x, -2, tm, seg)
          + cw_ref[1:2, :] * _shift_rows(x, -1, tm, seg)
          + cw_ref[2:3, :] * x
          + cw_ref[3:4, :] * _shift_rows(x, 1, tm, seg))
    pre = jnp.dot(xc.astype(bf16), wg_ref[...], preferred_element_type=f32) + bg_ref[...]

    ys = []
    for d in range(2):
        r = jax.nn.sigmoid(pre[:, (2 * d) * D_LRU:(2 * d + 1) * D_LRU])
        i = jax.nn.sigmoid(pre[:, (2 * d + 1) * D_LRU:(2 * d + 2) * D_LRU])
        log_a = -LRU_C * r * jax.nn.softplus(-lam_ref[d:d + 1, :])
        a = jnp.exp(log_a)
        b = jnp.sqrt(1.0 - a * a) * (i * xc)
        first = 0 if d == 0 else n - 1
        b = b + jnp.where(t == first, a * h0_ref[0, d:d + 1, :], 0.0)
        s = 1
        while s < n:
            if d == 0:
                ok = tm >= s
                shift = s
            else:
                ok = tm < seg - s
                shift = n - s
            a_sh = jnp.where(ok, pltpu.roll(a, shift, axis=0), 1.0)
            b_sh = jnp.where(ok, pltpu.roll(b, shift, axis=0), 0.0)
            b = a * b_sh + b
            a = a * a_sh
            s *= 2
        ys.append(b)

    y_f, y_b = ys
    out_ref[...] = ((y_f + y_b) * jax.nn.gelu(gate_ref[...])).astype(out_ref.dtype)
    fins = ([y_f[(j + 1) * CTX_LEN - 1:(j + 1) * CTX_LEN, :] for j in range(SEQ_BLOCK // CTX_LEN)]
            + [y_b[j * CTX_LEN:j * CTX_LEN + 1, :] for j in range(SEQ_BLOCK // CTX_LEN)])
    hfin_ref[0] = jnp.concatenate(fins, axis=0)


def _lru_branch(z, p, h0_blocks):
    full = lambda shape: pl.BlockSpec(shape, lambda i: (0,) * len(shape))
    return pl.pallas_call(
        _lru_kernel,
        grid=(N_SEQ_BLOCKS,),
        in_specs=[pl.BlockSpec((SEQ_BLOCK, D_LRU), lambda i: (i, Z_LRU_X // D_LRU)),
                  pl.BlockSpec((SEQ_BLOCK, D_LRU), lambda i: (i, Z_LRU_G // D_LRU)),
                  full((4, D_LRU)), full((1, D_LRU)), full((D_LRU, 4 * D_LRU)), full((1, 4 * D_LRU)),
                  full((2, D_LRU)),
                  pl.BlockSpec((1, 2, D_LRU), lambda i: (i, 0, 0))],
        out_specs=[pl.BlockSpec((SEQ_BLOCK, D_LRU), lambda i: (i, 0)),
                   pl.BlockSpec((1, SUBLANES, D_LRU), lambda i: (i, 0, 0))],
        out_shape=[jax.ShapeDtypeStruct((N_TOK, D_LRU), bf16),
                   jax.ShapeDtypeStruct((N_SEQ_BLOCKS, SUBLANES, D_LRU), f32)],
        compiler_params=_cparams(1),
        name="lru_branch",
    )(z, z, p['lru_conv_w'], p['lru_conv_b'], p['lru_wg'], p['lru_bg'], p['lru_lambda'], h0_blocks)


RWKV_CT = 256
N_RWKV_CT = D_RWKV // RWKV_CT
RWKV_LOWRANK_OFF = 3 * D_RWKV
RWKV_LOWRANK_W = RWKV_COLS_PAD - RWKV_LOWRANK_OFF
RWKV_PRE_OUTS = ('r', 'v', 'nkk', 'w_f', 'w_b', 'ka_f', 'ka_b', 'kd_f', 'kd_b', 'bonus', 'g')


def _rwkv_pre_kernel(r_ref, k_ref, v_ref, lr_ref, cwr_ref, cwk_ref, cwv_ref, cwl_ref,
                     w2_ref, a2_ref, g2_ref, pc_ref, ones_ref,
                     ro_ref, vo_ref, nkk_ref, wf_ref, wb_ref, kaf_ref, kab_ref, kdf_ref, kdb_ref,
                     bonus_ref, g_ref):
    blk = pl.program_id(0)
    seg = _seg_len(blk)
    n = r_ref.shape[0]

    def conv(x_ref, cw_ref):
        width = x_ref.shape[1]
        t = lax.broadcasted_iota(jnp.int32, (n, width), 0)
        tm = t & (seg - 1)
        x = x_ref[...]
        return (cw_ref[0:1, :] * _shift_rows(x, -1, tm, seg) + cw_ref[1:2, :] * x
                + cw_ref[2:3, :] * _shift_rows(x, 1, tm, seg))

    r = conv(r_ref, cwr_ref)
    k = conv(k_ref, cwk_ref)
    v = conv(v_ref, cwv_ref)
    lr = conv(lr_ref, cwl_ref)
    wl = lr[:, 0:2 * W_RANK]
    al = lr[:, 2 * W_RANK:2 * W_RANK + 2 * A_RANK]
    gl = lr[:, 2 * W_RANK + 2 * A_RANK:2 * W_RANK + 2 * A_RANK + G_RANK]

    w0 = [pc_ref[0:1, :], pc_ref[1:2, :]]
    a0 = [pc_ref[2:3, :], pc_ref[3:4, :]]
    k_k = pc_ref[4:5, :]
    k_a = pc_ref[5:6, :]
    r_k = pc_ref[6:7, :]

    w_pre = jnp.dot(jnp.tanh(wl).astype(bf16), w2_ref[0], preferred_element_type=f32)
    a_pre = jnp.dot(al.astype(bf16), a2_ref[0], preferred_element_type=f32)

    kk = k * k_k
    ss = _split_dot(kk * kk, ones_ref[...])
    kk = kk / jnp.maximum(jnp.sqrt(ss), 1e-12)

    kd_sum = None
    w_refs = (wf_ref, wb_ref)
    ka_refs = (kaf_ref, kab_ref)
    kd_refs = (kdf_ref, kdb_ref)
    for d in range(2):
        cols = slice(d * RWKV_CT, (d + 1) * RWKV_CT)
        w_log = -jax.nn.softplus(-(w0[d] + w_pre[:, cols])) - 0.5
        w_refs[d][...] = jnp.exp(-jnp.exp(w_log))
        a = jax.nn.sigmoid(a0[d] + a_pre[:, cols])
        ka_refs[d][...] = kk * a
        kd = k * (1.0 + (a - 1.0) * k_a)
        kd_refs[d][...] = kd
        kd_sum = kd if kd_sum is None else kd_sum + kd

    ro_ref[...] = r
    vo_ref[...] = v
    nkk_ref[...] = -kk
    bonus_ref[...] = _split_dot(r * kd_sum * r_k, ones_ref[...]) * v
    g_ref[...] = jnp.dot(jax.nn.sigmoid(gl).astype(bf16), g2_ref[...], preferred_element_type=f32)


def _rwkv_pre(z, p):
    nct = D_RWKV // RWKV_CT
    zcol = lambda base: pl.BlockSpec((SEQ_BLOCK, RWKV_CT), lambda i, c, base=base: (i, base + c))
    cwcol = lambda base: pl.BlockSpec((3, RWKV_CT), lambda i, c, base=base: (0, base + c))
    out_spec = pl.BlockSpec((SEQ_BLOCK, RWKV_CT), lambda i, c: (i, c))
    outs = pl.pallas_call(
        _rwkv_pre_kernel,
        grid=(N_SEQ_BLOCKS, nct),
        in_specs=[zcol(0), zcol(nct), zcol(2 * nct),
                  pl.BlockSpec((SEQ_BLOCK, RWKV_LOWRANK_W),
                               lambda i, c: (i, RWKV_LOWRANK_OFF // RWKV_LOWRANK_W)),
                  cwcol(0), cwcol(nct), cwcol(2 * nct),
                  pl.BlockSpec((3, RWKV_LOWRANK_W), lambda i, c: (0, RWKV_LOWRANK_OFF // RWKV_LOWRANK_W)),
                  pl.BlockSpec((1, 2 * W_RANK, 2 * RWKV_CT), lambda i, c: (c, 0, 0)),
                  pl.BlockSpec((1, 2 * A_RANK, 2 * RWKV_CT), lambda i, c: (c, 0, 0)),
                  pl.BlockSpec((G_RANK, RWKV_CT), lambda i, c: (0, c)),
                  pl.BlockSpec((SUBLANES, RWKV_CT), lambda i, c: (0, c)),
                  pl.BlockSpec((RWKV_CT, RWKV_CT), lambda i, c: (0, 0))],
        out_specs=[out_spec] * len(RWKV_PRE_OUTS),
        out_shape=[jax.ShapeDtypeStruct((N_TOK, D_RWKV), f32)] * len(RWKV_PRE_OUTS),
        compiler_params=_cparams(2),
        name="rwkv_pre",
    )(z, z, z, z, p['rwkv_conv_w'], p['rwkv_conv_w'], p['rwkv_conv_w'], p['rwkv_conv_w'],
      p['rwkv_w2'], p['rwkv_a2'], p['rwkv_g2'], p['rwkv_pc'], p['head_ones'][:RWKV_CT, :RWKV_CT])
    return dict(zip(RWKV_PRE_OUTS, outs))


def _rwkv_scan_kernel(w_ref, ka_ref, kd_ref, nkk_ref, r_ref, vv_ref, s0_ref, y_ref, s_ref, *,
                      steps, v_rows, lane_groups, unroll):
    @pl.when(pl.program_id(0) == 0)
    def _():
        s_ref[...] = s0_ref[...]

    def step(t, carry):
        for g in range(lane_groups):
            lanes = slice(g * LANES, (g + 1) * LANES)
            w = w_ref[t, :, lanes]
            ka = ka_ref[t, :, lanes]
            kd = kd_ref[t, :, lanes]
            nkk = nkk_ref[t, :, lanes]
            r = r_ref[t, :, lanes]

            def rows(vb, c):
                v0 = pl.multiple_of(vb * SUBLANES, SUBLANES)
                vv = vv_ref[t, pl.ds(v0, SUBLANES), lanes]
                ys = []
                for j in range(SUBLANES):
                    s = s_ref[v0 + j, :, lanes]
                    sa = jnp.sum(s * nkk, axis=0, keepdims=True)
                    s_new = s * w + ka * sa + kd * vv[j:j + 1]
                    s_ref[v0 + j, :, lanes] = s_new
                    ys.append(jnp.sum(s_new * r, axis=0, keepdims=True))
                y_ref[t, pl.ds(v0, SUBLANES), lanes] = jnp.concatenate(ys, axis=0)
                return c

            lax.fori_loop(0, v_rows // SUBLANES, rows, 0, unroll=unroll)
        return carry

    lax.fori_loop(0, steps, step, 0)


def _rwkv_scan_packed(w, ka, kd, nkk, r, vv, s0, *, steps):
    L, K, NL = w.shape
    VR = vv.shape[1]
    assert L % steps == 0 and NL % LANES == 0
    kvec = pl.BlockSpec((steps, K, NL), lambda i: (i, 0, 0))
    vspec = pl.BlockSpec((steps, VR, NL), lambda i: (i, 0, 0))
    sspec = pl.BlockSpec((VR, K, NL), lambda i: (0, 0, 0))
    kern = functools.partial(_rwkv_scan_kernel, steps=steps, v_rows=VR, lane_groups=NL // LANES,
                             unroll=1)
    y, s_fin = pl.pallas_call(
        kern,
        grid=(L // steps,),
        in_specs=[kvec, kvec, kvec, kvec, kvec, vspec, sspec],
        out_specs=[vspec, sspec],
        out_shape=[jax.ShapeDtypeStruct((L, VR, NL), f32),
                   jax.ShapeDtypeStruct((VR, K, NL), f32)],
        compiler_params=_cparams(1),
        name="rwkv7_scan",
    )(w, ka, kd, nkk, r, vv, s0)
    return y, s_fin


def _rwkv_scan_path(pre, rows, B, L, S0):
    H, K = N_RWKV_HEADS, RWKV_HEAD_DIM
    nd = 2
    n_states = nd * B * H
    v_lo = max(1, LANES // n_states)
    VR = K // v_lo

    def tok(name):
        return pre[name][rows].reshape(B, L, H, K)

    def by_dir(xf, xb):
        return jnp.stack([xf, jnp.flip(xb, axis=1)], axis=0)

    def kpack(x):
        x = jnp.transpose(x, (2, 4, 0, 1, 3)).reshape(L, K, n_states)
        if v_lo > 1:
            x = jnp.tile(x, (1, 1, v_lo))
        return x

    w = kpack(by_dir(tok('w_f'), tok('w_b')))
    kd = kpack(by_dir(tok('kd_f'), tok('kd_b')))
    ka = kpack(by_dir(tok('ka_f'), tok('ka_b')))
    nkk = kpack(by_dir(tok('nkk'), tok('nkk')))
    r = kpack(by_dir(tok('r'), tok('r')))
    vv = by_dir(tok('v'), tok('v')).reshape(nd, B, L, H, VR, v_lo)
    vv = jnp.transpose(vv, (2, 4, 5, 0, 1, 3)).reshape(L, VR, v_lo * n_states)
    s0 = S0.astype(f32).reshape(B, nd, H, VR, v_lo, K)
    s0 = jnp.transpose(s0, (3, 5, 4, 1, 0, 2)).reshape(VR, K, v_lo * n_states)

    steps = 8 if v_lo == 1 else 16
    y, s_fin = _rwkv_scan_packed(w, ka, kd, nkk, r, vv, s0, steps=steps)

    y = y.reshape(L, VR, v_lo, nd, B, H)
    y = jnp.transpose(y, (3, 4, 0, 5, 1, 2)).reshape(nd, B, L, H * K)
    y_f = y[0].reshape(B * L, H * K)
    y_b = jnp.flip(y[1], axis=1).reshape(B * L, H * K)
    s_fin = s_fin.reshape(VR, K, v_lo, nd, B, H)
    s_fin = jnp.transpose(s_fin, (4, 3, 5, 0, 2, 1)).reshape(B, nd, H, K, K)
    return y_f, y_b, s_fin


ATT_TQ = 256
HEAD_W = 2 * DIFF_HEAD_DIM


def _rope(x, cos, sin_signed, lo_mask):
    n = x.shape[1]
    partner = jnp.where(lo_mask, pltpu.roll(x, n - DIFF_HEAD_DIM // 4, axis=1),
                        pltpu.roll(x, DIFF_HEAD_DIM // 4, axis=1))
    return x * cos + partner * sin_signed


def _attn_heads(q, k, v, lam, subln_g, out_scale):
    lane = lax.broadcasted_iota(jnp.int32, (1, HEAD_W), 1)
    first = lane < DIFF_HEAD_DIM
    outs = []
    for h in range(N_DIFF_HEADS):
        cols = slice(h * HEAD_W, (h + 1) * HEAD_W)
        qh = (q[:, cols] * (DIFF_HEAD_DIM ** -0.5)).astype(bf16)
        kh = k[:, cols]
        vh = v[:, cols].astype(bf16)
        probs = []
        for m in range(2):
            km = jnp.where(first if m == 0 else ~first, kh, 0.0).astype(bf16)
            s = lax.dot_general(qh, km, (((1,), (1,)), ((), ())), preferred_element_type=f32)
            s = s - jnp.max(s, axis=-1, keepdims=True)
            e = jnp.exp(s)
            probs.append(e / jnp.sum(e, axis=-1, keepdims=True))
        attn = (probs[0] - lam * probs[1]).astype(bf16)
        o = jnp.dot(attn, vh, preferred_element_type=f32)
        o = o * lax.rsqrt(jnp.mean(o * o, axis=-1, keepdims=True) + NORM_EPS) * subln_g * out_scale
        outs.append(o)
    return jnp.concatenate(outs, axis=1)


def _attn_kernel(q_ref, k_ref, v_ref, ck_ref, cv_ref, cos_ref, sin_ref, lv_ref, g_ref, o_ref,
                 kall_ref, vall_ref, *, lam_init):
    blk = pl.program_id(0)
    qt = pl.program_id(1)
    lv = lv_ref[...]
    lam = (jnp.exp(jnp.sum(lv[0:1] * lv[1:2], axis=-1, keepdims=True))
           - jnp.exp(jnp.sum(lv[2:3] * lv[3:4], axis=-1, keepdims=True)) + lam_init)
    subln_g = g_ref[...]
    out_scale = 1.0 - lam_init
    row0 = pl.multiple_of(qt * ATT_TQ, ATT_TQ)

    @pl.when(blk < N_CTX_BLOCKS)
    def _():
        q = q_ref[pl.ds(row0, ATT_TQ), :]
        k = k_ref[pl.ds(row0, ATT_TQ), :]
        v = v_ref[pl.ds(row0, ATT_TQ), :]
        o_ref[...] = _attn_heads(q, k, v, lam, subln_g, out_scale).astype(o_ref.dtype)

    @pl.when(blk >= N_CTX_BLOCKS)
    def _():
        lane = lax.broadcasted_iota(jnp.int32, (1, D_DIFF), 1)
        lo_mask = (lane % (DIFF_HEAD_DIM // 2)) < (DIFF_HEAD_DIM // 4)

        @pl.when(qt == 0)
        def _():
            kall_ref[0:LAT_LEN, :] = _rope(k_ref[...], cos_ref[...], sin_ref[...], lo_mask)
            kall_ref[LAT_LEN:LAT_LEN + PAST_LEN, :] = ck_ref[0]
            vall_ref[0:LAT_LEN, :] = v_ref[...]
            vall_ref[LAT_LEN:LAT_LEN + PAST_LEN, :] = cv_ref[0]

        q = _rope(q_ref[pl.ds(row0, ATT_TQ), :], cos_ref[pl.ds(row0, ATT_TQ), :],
                  sin_ref[pl.ds(row0, ATT_TQ), :], lo_mask)
        o_ref[...] = _attn_heads(q, kall_ref[...], vall_ref[...], lam, subln_g,
                                 out_scale).astype(o_ref.dtype)


def _attention(z, cache_k_l, cache_v_l, rope_cos, rope_sin, p, lam_init):
    zcol = lambda off: pl.BlockSpec((SEQ_BLOCK, D_DIFF), lambda i, j, off=off: (i, off // D_DIFF))
    cache = pl.BlockSpec((1, PAST_LEN, D_DIFF), lambda i, j: (jnp.maximum(i - N_CTX_BLOCKS, 0), 0, 0))
    full = lambda shape: pl.BlockSpec(shape, lambda i, j: (0,) * len(shape))
    return pl.pallas_call(
        functools.partial(_attn_kernel, lam_init=lam_init),
        grid=(N_SEQ_BLOCKS, SEQ_BLOCK // ATT_TQ),
        in_specs=[zcol(Z_Q), zcol(Z_K), zcol(Z_V), cache, cache,
                  full((LAT_LEN, D_DIFF)), full((LAT_LEN, D_DIFF)),
                  full((4, DIFF_HEAD_DIM)), full((1, HEAD_W))],
        out_specs=pl.BlockSpec((ATT_TQ, D_DIFF), lambda i, j: (i * (SEQ_BLOCK // ATT_TQ) + j, 0)),
        out_shape=jax.ShapeDtypeStruct((N_TOK, D_DIFF), bf16),
        scratch_shapes=[pltpu.VMEM((LAT_LEN + PAST_LEN, D_DIFF), f32),
                        pltpu.VMEM((LAT_LEN + PAST_LEN, D_DIFF), f32)],
        compiler_params=_cparams(2),
        name="diff_attention",
    )(z, z, z, cache_k_l, cache_v_l, rope_cos, rope_sin, p['diff_lambda'], p['diff_subln_g'])


def _rope_tables():
    t = jnp.arange(LAT_LEN)
    row = (t // GRID_W).astype(f32)
    col = (t % GRID_W).astype(f32)
    nf = DIFF_HEAD_DIM // 4
    freqs = ROPE_THETA ** (-jnp.arange(nf, dtype=f32) / nf)
    d = jnp.arange(DIFF_HEAD_DIM)
    pos = jnp.where((d < DIFF_HEAD_DIM // 2)[None, :], row[:, None], col[:, None])
    ang = pos * freqs[d % nf][None, :]
    sign = jnp.where((d % (2 * nf)) < nf, -1.0, 1.0)[None, :]
    cos = jnp.tile(jnp.cos(ang), (1, D_DIFF // DIFF_HEAD_DIM))
    sin = jnp.tile(jnp.sin(ang) * sign, (1, D_DIFF // DIFF_HEAD_DIM))
    return cos, sin


MERGE_TM = 512


def _merge_kernel(x_ref, lru_ref, yf_ref, yb_ref, bonus_ref, g_ref, diff_ref, gp0_ref, gp1_ref, gp2_ref,
                  mod_ref, lnp_ref, ones_ref, wlo_ref, wro_ref, wdo_ref, wout_ref, ng_ref, rhi_ref, rlo_ref,
                  xo_ref, h2_ref, logit_ref):
    y = yf_ref[...] + yb_ref[...]
    inv = 1.0 / RWKV_HEAD_DIM
    mu = _split_dot(y, ones_ref[...]) * inv
    dlt = y - mu
    var = _split_dot(dlt * dlt, ones_ref[...]) * inv
    yn = dlt * lax.rsqrt(var + RWKV_LN_EPS)
    rwkv_out = ((yn * lnp_ref[0:1, :] + lnp_ref[1:2, :] + bonus_ref[...]) * g_ref[...]).astype(bf16)

    merged = (jax.nn.sigmoid(gp0_ref[...]) * jnp.dot(lru_ref[...], wlo_ref[...], preferred_element_type=f32)
              + jax.nn.sigmoid(gp1_ref[...]) * jnp.dot(rwkv_out, wro_ref[...], preferred_element_type=f32)
              + jax.nn.sigmoid(gp2_ref[...]) * jnp.dot(diff_ref[...], wdo_ref[...], preferred_element_type=f32))
    mix = jnp.dot(merged.astype(bf16), wout_ref[...], preferred_element_type=f32)

    gate1 = mod_ref[0, :, 2 * D_MODEL:3 * D_MODEL]
    shift2 = mod_ref[0, :, 3 * D_MODEL:4 * D_MODEL]
    scale2 = mod_ref[0, :, 4 * D_MODEL:5 * D_MODEL]
    x = x_ref[...] + gate1 * mix
    xo_ref[...] = x
    yn2 = x * lax.rsqrt(jnp.mean(x * x, axis=-1, keepdims=True) + NORM_EPS) * ng_ref[...]
    h2 = yn2 * (1.0 + scale2) + shift2
    hi = h2.astype(bf16)
    lo = (h2 - hi.astype(f32)).astype(bf16)
    h2_ref[...] = hi
    logit_ref[...] = (jnp.dot(hi, rhi_ref[...], preferred_element_type=f32)
                      + jnp.dot(lo, rhi_ref[...], preferred_element_type=f32)
                      + jnp.dot(hi, rlo_ref[...], preferred_element_type=f32))


def _merge(x, z, lru_out, y_f, y_b, bonus, g, diff_out, mod_l, p):
    row = lambda w: pl.BlockSpec((MERGE_TM, w), lambda i: (i, 0))
    gate = lambda b: pl.BlockSpec((MERGE_TM, D_MODEL), lambda i, b=b: (i, Z_GATES // D_MODEL + b))
    full = lambda shape: pl.BlockSpec(shape, lambda i: (0,) * len(shape))
    return pl.pallas_call(
        _merge_kernel,
        grid=(N_TOK // MERGE_TM,),
        in_specs=[row(D_MODEL), row(D_LRU), row(D_RWKV), row(D_RWKV), row(D_RWKV), row(D_RWKV), row(D_DIFF),
                  gate(0), gate(1), gate(2),
                  pl.BlockSpec((1, 1, 6 * D_MODEL), lambda i: (_mod_group_of_rows(i, MERGE_TM), 0, 0)),
                  full((2, D_RWKV)), full((D_RWKV, D_RWKV)),
                  full((D_LRU, D_MODEL)), full((D_RWKV, D_MODEL)), full((D_DIFF, D_MODEL)),
                  full((D_MODEL, D_MODEL)), full((1, D_MODEL)),
                  full((D_MODEL, ROUTER_PAD)), full((D_MODEL, ROUTER_PAD))],
        out_specs=[row(D_MODEL), row(D_MODEL), row(ROUTER_PAD)],
        out_shape=[jax.ShapeDtypeStruct((N_TOK, D_MODEL), f32),
                   jax.ShapeDtypeStruct((N_TOK, D_MODEL), bf16),
                   jax.ShapeDtypeStruct((N_TOK, ROUTER_PAD), f32)],
        compiler_params=_cparams(1),
        name="merge",
    )(x, lru_out, y_f, y_b, bonus, g, diff_out, z, z, z, mod_l, p['rwkv_lnp'], p['head_ones'],
      p['w_lru_out'], p['w_rwkv_out'], p['w_diff_out'], p['w_out'], p['norm_g2'],
      p['router_hi'], p['router_lo'])


FFN_TF = 512


def _ffn_kernel(xe_ref, wg_ref, wu_ref, wd_ref, o_ref):
    x = xe_ref[0]
    gate = jnp.dot(x, wg_ref[0, 0].astype(bf16), preferred_element_type=f32)
    up = jnp.dot(x, wu_ref[0, 0].astype(bf16), preferred_element_type=f32)
    hid = (gate * jax.nn.sigmoid(gate) * up).astype(bf16)
    part = jnp.dot(hid, wd_ref[0, 0].astype(bf16), preferred_element_type=f32)

    @pl.when(pl.program_id(1) == 0)
    def _():
        o_ref[0] = part

    @pl.when(pl.program_id(1) != 0)
    def _():
        o_ref[0] += part


def _expert_ffn(xe, w_gate, w_up, w_down, layer):
    n_e, cap, _ = xe.shape
    return pl.pallas_call(
        _ffn_kernel,
        grid=(n_e, EXPERT_FF // FFN_TF),
        in_specs=[pl.BlockSpec((1, cap, D_MODEL), lambda e, f: (e, 0, 0)),
                  pl.BlockSpec((1, 1, D_MODEL, FFN_TF), lambda e, f: (layer, e, 0, f)),
                  pl.BlockSpec((1, 1, D_MODEL, FFN_TF), lambda e, f: (layer, e, 0, f)),
                  pl.BlockSpec((1, 1, FFN_TF, D_MODEL), lambda e, f: (layer, e, f, 0))],
        out_specs=pl.BlockSpec((1, cap, D_MODEL), lambda e, f: (e, 0, 0)),
        out_shape=jax.ShapeDtypeStruct((n_e, cap, D_MODEL), f32),
        compiler_params=_cparams(2),
        name="expert_ffn",
    )(xe, w_gate, w_up, w_down)


def _route(logits, n):
    cap = EC_CAPACITY * n // N_EXPERTS
    aff = jax.nn.softmax(logits, axis=-1)
    return lax.top_k(aff.T, cap)


def _moe(h2, logits, w_gate, w_up, w_down, layer):
    lg = logits[:, :N_EXPERTS]
    gate_c, idx_c = _route(lg[:N_CTX_TOK], N_CTX_TOK)
    gate_l, idx_l = _route(lg[N_CTX_TOK:], N_TOK - N_CTX_TOK)
    idx = jnp.concatenate([idx_c, idx_l + N_CTX_TOK], axis=1)
    gate = jnp.concatenate([gate_c, gate_l], axis=1)
    ye = _expert_ffn(h2[idx], w_gate, w_up, w_down, layer) * gate[..., None]
    return jnp.zeros((N_TOK, D_MODEL), f32).at[idx.reshape(-1)].add(ye.reshape(-1, D_MODEL))


RESID_TM = 512


def _resid_kernel(x_ref, y_ref, mod_ref, g_ref, xo_ref, *, final_norm):
    gate2 = mod_ref[0, :, 5 * D_MODEL:6 * D_MODEL]
    x = x_ref[...] + gate2 * y_ref[...]
    if final_norm:
        x = x * lax.rsqrt(jnp.mean(x * x, axis=-1, keepdims=True) + NORM_EPS) * g_ref[...]
    xo_ref[...] = x


def _residual(x, y, mod_l, final_g, final_norm):
    row = pl.BlockSpec((RESID_TM, D_MODEL), lambda i: (i, 0))
    return pl.pallas_call(
        functools.partial(_resid_kernel, final_norm=final_norm),
        grid=(N_TOK // RESID_TM,),
        in_specs=[row, row,
                  pl.BlockSpec((1, 1, 6 * D_MODEL), lambda i: (_mod_group_of_rows(i, RESID_TM), 0, 0)),
                  pl.BlockSpec((1, D_MODEL), lambda i: (0, 0))],
        out_specs=row,
        out_shape=jax.ShapeDtypeStruct((N_TOK, D_MODEL), f32),
        compiler_params=_cparams(1),
        name="residual",
    )(x, y, mod_l, final_g)


def _block_diag(blocks):
    n, a, b = blocks.shape
    eye = jnp.eye(n, dtype=blocks.dtype)
    return (eye[:, None, :, None] * blocks[:, :, None, :]).reshape(n * a, n * b)


def _prep_layer(l, w):
    p = {}
    w_in = w['w_in'][l]
    lru_x, lru_g, zr, q, k, v, gates = (
        w_in[:, 0:D_LRU], w_in[:, D_LRU:2 * D_LRU], w_in[:, 2 * D_LRU:2 * D_LRU + RWKV_COLS],
        w_in[:, 2 * D_LRU + RWKV_COLS:2 * D_LRU + RWKV_COLS + D_DIFF],
        w_in[:, 2 * D_LRU + RWKV_COLS + D_DIFF:2 * D_LRU + RWKV_COLS + 2 * D_DIFF],
        w_in[:, 2 * D_LRU + RWKV_COLS + 2 * D_DIFF:2 * D_LRU + RWKV_COLS + 3 * D_DIFF],
        w_in[:, 2 * D_LRU + RWKV_COLS + 3 * D_DIFF:])
    pad = jnp.zeros((D_MODEL, RWKV_COLS_PAD - RWKV_COLS), f32)
    p['w_in'] = jnp.concatenate([zr, pad, gates, lru_x, lru_g, q, k, v], axis=1).astype(bf16)
    p['norm_g1'] = w['norm_g'][l, 0].reshape(1, D_MODEL)
    p['norm_g2'] = w['norm_g'][l, 1].reshape(1, D_MODEL)

    p['lru_conv_w'] = w['lru_conv_w'][l]
    p['lru_conv_b'] = w['lru_conv_b'][l].reshape(1, D_LRU)
    p['lru_wg'] = jnp.concatenate(
        [_block_diag(w[name][l, d]) for d in range(2) for name in ('lru_wa', 'lru_wx')], axis=1).astype(bf16)
    p['lru_bg'] = jnp.concatenate(
        [w[name][l, d] for d in range(2) for name in ('lru_ba', 'lru_bx')]).reshape(1, 4 * D_LRU)
    p['lru_lambda'] = w['lru_lambda'][l]

    p['rwkv_conv_w'] = jnp.pad(w['rwkv_conv_w'][l], ((0, 0), (0, RWKV_COLS_PAD - RWKV_COLS)))

    def lowrank_pair(m):
        rank = m.shape[1]
        out = jnp.zeros((N_RWKV_CT, 2 * rank, 2 * RWKV_CT), f32)
        for d in range(2):
            tiles = m[d].reshape(rank, N_RWKV_CT, RWKV_CT).transpose(1, 0, 2)
            out = out.at[:, d * rank:(d + 1) * rank, d * RWKV_CT:(d + 1) * RWKV_CT].set(tiles)
        return out.astype(bf16)

    p['rwkv_w2'] = lowrank_pair(w['rwkv_w2'][l])
    p['rwkv_a2'] = lowrank_pair(w['rwkv_a2'][l])
    p['rwkv_g2'] = w['rwkv_g2'][l].astype(bf16)
    p['rwkv_pc'] = jnp.concatenate(
        [w['rwkv_w0'][l], w['rwkv_a0'][l], w['rwkv_k_k'][l][None], w['rwkv_k_a'][l][None],
         w['rwkv_r_k'][l].reshape(1, D_RWKV), jnp.zeros((1, D_RWKV), f32)], axis=0)
    p['rwkv_lnp'] = jnp.stack([w['rwkv_ln_g'][l], w['rwkv_ln_b'][l]], axis=0)
    p['head_ones'] = _block_diag(jnp.ones((N_RWKV_HEADS, RWKV_HEAD_DIM, RWKV_HEAD_DIM), bf16))

    p['diff_lambda'] = w['diff_lambda'][l]
    p['diff_subln_g'] = w['diff_subln_g'][l].reshape(1, HEAD_W)
    p['w_lru_out'] = w['w_lru_out'][l].astype(bf16)
    p['w_rwkv_out'] = w['w_rwkv_out'][l].astype(bf16)
    p['w_diff_out'] = w['w_diff_out'][l].astype(bf16)
    p['w_out'] = w['w_out'][l].astype(bf16)
    rw = jnp.pad(w['router_w'][l], ((0, 0), (0, ROUTER_PAD - N_EXPERTS)))
    p['router_hi'] = rw.astype(bf16)
    p['router_lo'] = (rw - p['router_hi'].astype(f32)).astype(bf16)
    return p


def kernel(x_prompt, x_sample, cache_k, cache_v, state_lru, state_rwkv, c, c_ctx, norm_g, final_norm_g, w_mod, b_mod, w_in, lru_conv_w, lru_conv_b, lru_wa, lru_ba, lru_wx, lru_bx, lru_lambda, rwkv_conv_w, rwkv_w0, rwkv_w2, rwkv_a0, rwkv_a2, rwkv_k_k, rwkv_k_a, rwkv_r_k, rwkv_g2, rwkv_ln_g, rwkv_ln_b, diff_lambda, diff_subln_g, w_lru_out, w_rwkv_out, w_diff_out, w_out, router_w, exp_w_gate, exp_w_up, exp_w_down):
    w = dict(norm_g=norm_g, w_in=w_in, lru_conv_w=lru_conv_w, lru_conv_b=lru_conv_b, lru_wa=lru_wa,
             lru_ba=lru_ba, lru_wx=lru_wx, lru_bx=lru_bx, lru_lambda=lru_lambda, rwkv_conv_w=rwkv_conv_w,
             rwkv_w0=rwkv_w0, rwkv_w2=rwkv_w2, rwkv_a0=rwkv_a0, rwkv_a2=rwkv_a2, rwkv_k_k=rwkv_k_k,
             rwkv_k_a=rwkv_k_a, rwkv_r_k=rwkv_r_k, rwkv_g2=rwkv_g2, rwkv_ln_g=rwkv_ln_g,
             rwkv_ln_b=rwkv_ln_b, diff_lambda=diff_lambda, diff_subln_g=diff_subln_g, w_lru_out=w_lru_out,
             w_rwkv_out=w_rwkv_out, w_diff_out=w_diff_out, w_out=w_out, router_w=router_w)

    x = jnp.concatenate([x_prompt.reshape(N_CTX_TOK, D_MODEL), x_sample.reshape(-1, D_MODEL)], axis=0)
    cvec = jnp.concatenate([c_ctx[None], c, jnp.zeros((SUBLANES - N_MOD_GROUPS, D_MODEL), f32)], axis=0)
    mod = _modulation(cvec, w_mod, b_mod)
    rope_cos, rope_sin = _rope_tables()
    final_g = final_norm_g.reshape(1, D_MODEL)
    ctx_rows = slice(0, N_CTX_TOK)
    lat_rows = slice(N_CTX_TOK, N_TOK)
    zero_ctx_state = jnp.zeros((N_CTX_SEQ, 2, N_RWKV_HEADS, RWKV_HEAD_DIM, RWKV_HEAD_DIM), f32)

    new_k, new_v, new_lru, new_rwkv = [], [], [], []
    for l in range(DEPTH):
        p = _prep_layer(l, w)
        lam_init = 0.8 - 0.6 * math.exp(-0.3 * l)
        mod_l = mod[l].reshape(SUBLANES, 1, 6 * D_MODEL)

        z = _in_projection(x, mod_l, p['norm_g1'], p['w_in'])
        new_k.append(z[ctx_rows, Z_K:Z_K + D_DIFF].reshape(N_CTX_SEQ, CTX_LEN, N_DIFF_HEADS, 2, DIFF_HEAD_DIM))
        new_v.append(z[ctx_rows, Z_V:Z_V + D_DIFF].reshape(N_CTX_SEQ, CTX_LEN, N_DIFF_HEADS, HEAD_W))

        h0_blocks = jnp.concatenate(
            [jnp.zeros((N_CTX_BLOCKS, 2, D_LRU), f32), state_lru[:, l].astype(f32)], axis=0)
        lru_out, lru_fin = _lru_branch(z, p, h0_blocks)
        n_per = SEQ_BLOCK // CTX_LEN
        fin = lru_fin[:N_CTX_BLOCKS].reshape(N_CTX_BLOCKS, 2, n_per, D_LRU)
        new_lru.append(jnp.transpose(fin, (0, 2, 1, 3)).reshape(N_CTX_SEQ, 2, D_LRU))

        pre = _rwkv_pre(z, p)
        yf_c, yb_c, s_c = _rwkv_scan_path(pre, ctx_rows, N_CTX_SEQ, CTX_LEN, zero_ctx_state)
        yf_l, yb_l, _ = _rwkv_scan_path(pre, lat_rows, N_LAT_SEQ, LAT_LEN, state_rwkv[:, l])
        new_rwkv.append(s_c)
        y_f = jnp.concatenate([yf_c, yf_l], axis=0)
        y_b = jnp.concatenate([yb_c, yb_l], axis=0)

        diff_out = _attention(z, cache_k[:, l].reshape(N_LAT_SEQ, PAST_LEN, D_DIFF),
                              cache_v[:, l].reshape(N_LAT_SEQ, PAST_LEN, D_DIFF),
                              rope_cos, rope_sin, p, lam_init)

        x, h2, logits = _merge(x, z, lru_out, y_f, y_b, pre['bonus'], pre['g'], diff_out, mod_l, p)
        moe = _moe(h2, logits, exp_w_gate, exp_w_up, exp_w_down, l)
        x = _residual(x, moe, mod_l, final_g, final_norm=(l == DEPTH - 1))

    y_prompt = x[ctx_rows].reshape(x_prompt.shape)
    y_sample = x[lat_rows].reshape(x_sample.shape)
    return (y_prompt, y_sample, jnp.stack(new_k, axis=1), jnp.stack(new_v, axis=1),
            jnp.stack(new_lru, axis=1), jnp.stack(new_rwkv, axis=1))
```

```python
import functools
import math

import jax
import jax.numpy as jnp
from jax import lax
from jax.experimental import pallas as pl
from jax.experimental.pallas import tpu as pltpu

f32 = jnp.float32
bf16 = jnp.bfloat16

D_MODEL = 1024
DEPTH = 2
GRID_W = 64
NORM_EPS = 1e-6

N_CTX_SEQ = 16
CTX_LEN = 256
N_LAT_SEQ = 2
LAT_LEN = 1024
PAST_LEN = 256
N_CTX_TOK = N_CTX_SEQ * CTX_LEN
N_TOK = N_CTX_TOK + N_LAT_SEQ * LAT_LEN

D_LRU = 512
LRU_BLOCKS = 8
LRU_BLOCK_W = D_LRU // LRU_BLOCKS
LRU_C = 8.0

D_RWKV = 512
RWKV_HEAD_DIM = 64
N_RWKV_HEADS = D_RWKV // RWKV_HEAD_DIM
W_RANK = 64
A_RANK = 64
G_RANK = 128
RWKV_LN_EPS = 64e-5
RWKV_COLS = 3 * D_RWKV + 2 * W_RANK + 2 * A_RANK + G_RANK
RWKV_COLS_PAD = 2048

N_DIFF_HEADS = 4
DIFF_HEAD_DIM = 64
D_DIFF = N_DIFF_HEADS * 2 * DIFF_HEAD_DIM
ROPE_THETA = 10000.0

N_BRANCHES = 3
N_EXPERTS = 16
EXPERT_FF = 1024
EC_CAPACITY = 2
ROUTER_PAD = 128

LANES = 128
SUBLANES = 8

SEQ_BLOCK = 1024
N_SEQ_BLOCKS = N_TOK // SEQ_BLOCK
N_CTX_BLOCKS = N_CTX_TOK // SEQ_BLOCK
N_MOD_GROUPS = 1 + N_LAT_SEQ

Z_RWKV = 0
Z_GATES = RWKV_COLS_PAD
Z_LRU_X = Z_GATES + N_BRANCHES * D_MODEL
Z_LRU_G = Z_LRU_X + D_LRU
Z_Q = Z_LRU_G + D_LRU
Z_K = Z_Q + D_DIFF
Z_V = Z_K + D_DIFF
Z_COLS = Z_V + D_DIFF

VMEM_LIMIT = 56 * 1024 * 1024


def _cparams(n_axes):
    return pltpu.CompilerParams(dimension_semantics=("arbitrary",) * n_axes,
                                vmem_limit_bytes=VMEM_LIMIT)


def _seg_len(block_idx):
    return jnp.where(block_idx < N_CTX_BLOCKS, CTX_LEN, LAT_LEN)


def _mod_group_of_rows(row_block, rows_per_block):
    first_lat = N_CTX_TOK // rows_per_block
    per_seq = LAT_LEN // rows_per_block
    return jnp.where(row_block < first_lat, 0, (row_block - first_lat) // per_seq + 1)


def _shift_rows(v, d, tm, seg):
    n = v.shape[0]
    r = pltpu.roll(v, (-d) % n, axis=0)
    ok = (tm + d >= 0) & (tm + d < seg)
    return jnp.where(ok, r, 0.0)


def _split_dot(x, w_bf16):
    hi = x.astype(bf16)
    lo = (x - hi.astype(f32)).astype(bf16)
    return (jnp.dot(hi, w_bf16, preferred_element_type=f32)
            + jnp.dot(lo, w_bf16, preferred_element_type=f32))


MOD_TN = 1536


def _mod_kernel(c_ref, w_ref, b_ref, o_ref):
    c = c_ref[...]
    s = (c * jax.nn.sigmoid(c)).astype(bf16)
    o_ref[0] = jnp.dot(s, w_ref[0].astype(bf16), preferred_element_type=f32) + b_ref[0]


def _modulation(cvec, w_mod, b_mod):
    n_out = w_mod.shape[-1]
    return pl.pallas_call(
        _mod_kernel,
        grid=(DEPTH, n_out // MOD_TN),
        in_specs=[pl.BlockSpec((SUBLANES, D_MODEL), lambda l, j: (0, 0)),
                  pl.BlockSpec((1, D_MODEL, MOD_TN), lambda l, j: (l, 0, j)),
                  pl.BlockSpec((1, 1, MOD_TN), lambda l, j: (l, 0, j))],
        out_specs=pl.BlockSpec((1, SUBLANES, MOD_TN), lambda l, j: (l, 0, j)),
        out_shape=jax.ShapeDtypeStruct((DEPTH, SUBLANES, n_out), f32),
        compiler_params=_cparams(2),
        name="modulation",
    )(cvec, w_mod, b_mod.reshape(DEPTH, 1, n_out))


INPROJ_TM = 1024
INPROJ_TN = 512


def _inproj_kernel(x_ref, mod_ref, g_ref, w_ref, o_ref, h_ref):
    @pl.when(pl.program_id(1) == 0)
    def _():
        x = x_ref[...]
        y = x * lax.rsqrt(jnp.mean(x * x, axis=-1, keepdims=True) + NORM_EPS) * g_ref[...]
        shift = mod_ref[0, :, 0:D_MODEL]
        scale = mod_ref[0, :, D_MODEL:2 * D_MODEL]
        h_ref[...] = (y * (1.0 + scale) + shift).astype(bf16)

    o_ref[...] = jnp.dot(h_ref[...], w_ref[...], preferred_element_type=f32)


def _in_projection(x, mod_l, norm_g_row, w_in_p):
    return pl.pallas_call(
        _inproj_kernel,
        grid=(N_TOK // INPROJ_TM, Z_COLS // INPROJ_TN),
        in_specs=[pl.BlockSpec((INPROJ_TM, D_MODEL), lambda i, j: (i, 0)),
                  pl.BlockSpec((1, 1, 2 * D_MODEL),
                               lambda i, j: (_mod_group_of_rows(i, INPROJ_TM), 0, 0)),
                  pl.BlockSpec((1, D_MODEL), lambda i, j: (0, 0)),
                  pl.BlockSpec((D_MODEL, INPROJ_TN), lambda i, j: (0, j))],
        out_specs=pl.BlockSpec((INPROJ_TM, INPROJ_TN), lambda i, j: (i, j)),
        out_shape=jax.ShapeDtypeStruct((N_TOK, Z_COLS), f32),
        scratch_shapes=[pltpu.VMEM((INPROJ_TM, D_MODEL), bf16)],
        compiler_params=_cparams(2),
        name="in_projection",
    )(x, mod_l, norm_g_row, w_in_p)


def _lru_kernel(x_ref, gate_ref, cw_ref, cb_ref, wg_ref, bg_ref, lam_ref, h0_ref, out_ref, hfin_ref):
    blk = pl.program_id(0)
    seg = _seg_len(blk)
    n = x_ref.shape[0]
    t = lax.broadcasted_iota(jnp.int32, (n, D_LRU), 0)
    tm = t & (seg - 1)

    x = x_ref[...]
    xc = (cb_ref[...]
          + cw_ref[0:1, :] * _shift_rows(x, -2, tm, seg)
          + cw_ref[1:2, :] * _shift_rows(x, -1, tm, seg)
          + cw_ref[2:3, :] * x
          + cw_ref[3:4, :] * _shift_rows(x, 1, tm, seg))
    pre = jnp.dot(xc.astype(bf16), wg_ref[...], preferred_element_type=f32) + bg_ref[...]

    ys = []
    for d in range(2):
        r = jax.nn.sigmoid(pre[:, (2 * d) * D_LRU:(2 * d + 1) * D_LRU])
        i = jax.nn.sigmoid(pre[:, (2 * d + 1) * D_LRU:(2 * d + 2) * D_LRU])
        log_a = -LRU_C * r * jax.nn.softplus(-lam_ref[d:d + 1, :])
        a = jnp.exp(log_a)
        b = jnp.sqrt(1.0 - a * a) * (i * xc)
        first = 0 if d == 0 else n - 1
        b = b + jnp.where(t == first, a * h0_ref[0, d:d + 1, :], 0.0)
        s = 1
        while s < n:
            if d == 0:
                ok = tm >= s
                shift = s
            else:
                ok = tm < seg - s
                shift = n - s
            a_sh = jnp.where(ok, pltpu.roll(a, shift, axis=0), 1.0)
            b_sh = jnp.where(ok, pltpu.roll(b, shift, axis=0), 0.0)
            b = a * b_sh + b
            a = a * a_sh
            s *= 2
        ys.append(b)

    y_f, y_b = ys
    out_ref[...] = ((y_f + y_b) * jax.nn.gelu(gate_ref[...])).astype(out_ref.dtype)
    fins = ([y_f[(j + 1) * CTX_LEN - 1:(j + 1) * CTX_LEN, :] for j in range(SEQ_BLOCK // CTX_LEN)]
            + [y_b[j * CTX_LEN:j * CTX_LEN + 1, :] for j in range(SEQ_BLOCK // CTX_LEN)])
    hfin_ref[0] = jnp.concatenate(fins, axis=0)


def _lru_branch(z, p, h0_blocks):
    full = lambda shape: pl.BlockSpec(shape, lambda i: (0,) * len(shape))
    return pl.pallas_call(
        _lru_kernel,
        grid=(N_SEQ_BLOCKS,),
        in_specs=[pl.BlockSpec((SEQ_BLOCK, D_LRU), lambda i: (i, Z_LRU_X // D_LRU)),
                  pl.BlockSpec((SEQ_BLOCK, D_LRU), lambda i: (i, Z_LRU_G // D_LRU)),
                  full((4, D_LRU)), full((1, D_LRU)), full((D_LRU, 4 * D_LRU)), full((1, 4 * D_LRU)),
                  full((2, D_LRU)),
                  pl.BlockSpec((1, 2, D_LRU), lambda i: (i, 0, 0))],
        out_specs=[pl.BlockSpec((SEQ_BLOCK, D_LRU), lambda i: (i, 0)),
                   pl.BlockSpec((1, SUBLANES, D_LRU), lambda i: (i, 0, 0))],
        out_shape=[jax.ShapeDtypeStruct((N_TOK, D_LRU), bf16),
                   jax.ShapeDtypeStruct((N_SEQ_BLOCKS, SUBLANES, D_LRU), f32)],
        compiler_params=_cparams(1),
        name="lru_branch",
    )(z, z, p['lru_conv_w'], p['lru_conv_b'], p['lru_wg'], p['lru_bg'], p['lru_lambda'], h0_blocks)


RWKV_CT = 256
N_RWKV_CT = D_RWKV // RWKV_CT
RWKV_LOWRANK_OFF = 3 * D_RWKV
RWKV_LOWRANK_W = RWKV_COLS_PAD - RWKV_LOWRANK_OFF
RWKV_PRE_OUTS = ('r', 'v', 'nkk', 'w_f', 'w_b', 'ka_f', 'ka_b', 'kd_f', 'kd_b', 'bonus', 'g')


def _rwkv_pre_kernel(r_ref, k_ref, v_ref, lr_ref, cwr_ref, cwk_ref, cwv_ref, cwl_ref,
                     w2_ref, a2_ref, g2_ref, pc_ref, ones_ref,
                     ro_ref, vo_ref, nkk_ref, wf_ref, wb_ref, kaf_ref, kab_ref, kdf_ref, kdb_ref,
                     bonus_ref, g_ref):
    blk = pl.program_id(0)
    seg = _seg_len(blk)
    n = r_ref.shape[0]

    def conv(x_ref, cw_ref):
        width = x_ref.shape[1]
        t = lax.broadcasted_iota(jnp.int32, (n, width), 0)
        tm = t & (seg - 1)
        x = x_ref[...]
        return (cw_ref[0:1, :] * _shift_rows(x, -1, tm, seg) + cw_ref[1:2, :] * x
                + cw_ref[2:3, :] * _shift_rows(x, 1, tm, seg))

    r = conv(r_ref, cwr_ref)
    k = conv(k_ref, cwk_ref)
    v = conv(v_ref, cwv_ref)
    lr = conv(lr_ref, cwl_ref)
    wl = lr[:, 0:2 * W_RANK]
    al = lr[:, 2 * W_RANK:2 * W_RANK + 2 * A_RANK]
    gl = lr[:, 2 * W_RANK + 2 * A_RANK:2 * W_RANK + 2 * A_RANK + G_RANK]

    w0 = [pc_ref[0:1, :], pc_ref[1:2, :]]
    a0 = [pc_ref[2:3, :], pc_ref[3:4, :]]
    k_k = pc_ref[4:5, :]
    k_a = pc_ref[5:6, :]
    r_k = pc_ref[6:7, :]

    w_pre = jnp.dot(jnp.tanh(wl).astype(bf16), w2_ref[0], preferred_element_type=f32)
    a_pre = jnp.dot(al.astype(bf16), a2_ref[0], preferred_element_type=f32)

    kk = k * k_k
    ss = _split_dot(kk * kk, ones_ref[...])
    kk = kk / jnp.maximum(jnp.sqrt(ss), 1e-12)

    kd_sum = None
    w_refs = (wf_ref, wb_ref)
    ka_refs = (kaf_ref, kab_ref)
    kd_refs = (kdf_ref, kdb_ref)
    for d in range(2):
        cols = slice(d * RWKV_CT, (d + 1) * RWKV_CT)
        w_log = -jax.nn.softplus(-(w0[d] + w_pre[:, cols])) - 0.5
        w_refs[d][...] = jnp.exp(-jnp.exp(w_log))
        a = jax.nn.sigmoid(a0[d] + a_pre[:, cols])
        ka_refs[d][...] = kk * a
        kd = k * (1.0 + (a - 1.0) * k_a)
        kd_refs[d][...] = kd
        kd_sum = kd if kd_sum is None else kd_sum + kd

    ro_ref[...] = r
    vo_ref[...] = v
    nkk_ref[...] = -kk
    bonus_ref[...] = _split_dot(r * kd_sum * r_k, ones_ref[...]) * v
    g_ref[...] = jnp.dot(jax.nn.sigmoid(gl).astype(bf16), g2_ref[...], preferred_element_type=f32)


def _rwkv_pre(z, p):
    nct = D_RWKV // RWKV_CT
    zcol = lambda base: pl.BlockSpec((SEQ_BLOCK, RWKV_CT), lambda i, c, base=base: (i, base + c))
    cwcol = lambda base: pl.BlockSpec((3, RWKV_CT), lambda i, c, base=base: (0, base + c))
    out_spec = pl.BlockSpec((SEQ_BLOCK, RWKV_CT), lambda i, c: (i, c))
    outs = pl.pallas_call(
        _rwkv_pre_kernel,
        grid=(N_SEQ_BLOCKS, nct),
        in_specs=[zcol(0), zcol(nct), zcol(2 * nct),
                  pl.BlockSpec((SEQ_BLOCK, RWKV_LOWRANK_W),
                               lambda i, c: (i, RWKV_LOWRANK_OFF // RWKV_LOWRANK_W)),
                  cwcol(0), cwcol(nct), cwcol(2 * nct),
                  pl.BlockSpec((3, RWKV_LOWRANK_W), lambda i, c: (0, RWKV_LOWRANK_OFF // RWKV_LOWRANK_W)),
                  pl.BlockSpec((1, 2 * W_RANK, 2 * RWKV_CT), lambda i, c: (c, 0, 0)),
                  pl.BlockSpec((1, 2 * A_RANK, 2 * RWKV_CT), lambda i, c: (c, 0, 0)),
                  pl.BlockSpec((G_RANK, RWKV_CT), lambda i, c: (0, c)),
                  pl.BlockSpec((SUBLANES, RWKV_CT), lambda i, c: (0, c)),
                  pl.BlockSpec((RWKV_CT, RWKV_CT), lambda i, c: (0, 0))],
        out_specs=[out_spec] * len(RWKV_PRE_OUTS),
        out_shape=[jax.ShapeDtypeStruct((N_TOK, D_RWKV), f32)] * len(RWKV_PRE_OUTS),
        compiler_params=_cparams(2),
        name="rwkv_pre",
    )(z, z, z, z, p['rwkv_conv_w'], p['rwkv_conv_w'], p['rwkv_conv_w'], p['rwkv_conv_w'],
      p['rwkv_w2'], p['rwkv_a2'], p['rwkv_g2'], p['rwkv_pc'], p['head_ones'][:RWKV_CT, :RWKV_CT])
    return dict(zip(RWKV_PRE_OUTS, outs))


def _rwkv_scan_kernel(wf_ref, wb_ref, kaf_ref, kab_ref, kdf_ref, kdb_ref, nkkf_ref, nkkb_ref,
                      rf_ref, rb_ref, vvf_ref, vvb_ref, s0_ref, yf_ref, yb_ref, s_ref, *,
                      steps, v_rows, unroll):
    @pl.when(pl.program_id(0) == 0)
    def _():
        s_ref[...] = s0_ref[...]

    dirs = ((wf_ref, kaf_ref, kdf_ref, nkkf_ref, rf_ref, vvf_ref, yf_ref),
            (wb_ref, kab_ref, kdb_ref, nkkb_ref, rb_ref, vvb_ref, yb_ref))

    def step(t, carry):
        for g, (w_ref, ka_ref, kd_ref, nkk_ref, r_ref, vv_ref, y_ref) in enumerate(dirs):
            tt = t if g == 0 else steps - 1 - t
            lanes = slice(g * LANES, (g + 1) * LANES)
            w = w_ref[tt]
            ka = ka_ref[tt]
            kd = kd_ref[tt]
            nkk = nkk_ref[tt]
            r = r_ref[tt]

            def rows(vb, c):
                v0 = pl.multiple_of(vb * SUBLANES, SUBLANES)
                vv = vv_ref[tt, pl.ds(v0, SUBLANES), :]
                ys = []
                for j in range(SUBLANES):
                    s = s_ref[v0 + j, :, lanes]
                    sa = jnp.sum(s * nkk, axis=0, keepdims=True)
                    s_new = s * w + ka * sa + kd * vv[j:j + 1]
                    s_ref[v0 + j, :, lanes] = s_new
                    ys.append(jnp.sum(s_new * r, axis=0, keepdims=True))
                y_ref[tt, pl.ds(v0, SUBLANES), :] = jnp.concatenate(ys, axis=0)
                return c

            lax.fori_loop(0, v_rows // SUBLANES, rows, 0, unroll=unroll)
        return carry

    lax.fori_loop(0, steps, step, 0)


def _rwkv_scan_packed(w_f, w_b, ka_f, ka_b, kd_f, kd_b, nkk, r, vv, s0, *, steps):
    L, K, _ = nkk.shape
    VR = vv.shape[1]
    nblk = L // steps
    fwd = lambda rows: pl.BlockSpec((steps, rows, LANES), lambda i: (i, 0, 0))
    bwd = lambda rows: pl.BlockSpec((steps, rows, LANES), lambda i: (nblk - 1 - i, 0, 0))
    sspec = pl.BlockSpec((VR, K, 2 * LANES), lambda i: (0, 0, 0))
    kern = functools.partial(_rwkv_scan_kernel, steps=steps, v_rows=VR,
                             unroll=min(2, VR // SUBLANES))
    return pl.pallas_call(
        kern,
        grid=(nblk,),
        in_specs=[fwd(K), bwd(K), fwd(K), bwd(K), fwd(K), bwd(K), fwd(K), bwd(K), fwd(K), bwd(K),
                  fwd(VR), bwd(VR), sspec],
        out_specs=[fwd(VR), bwd(VR), sspec],
        out_shape=[jax.ShapeDtypeStruct((L, VR, LANES), f32),
                   jax.ShapeDtypeStruct((L, VR, LANES), f32),
                   jax.ShapeDtypeStruct((VR, K, 2 * LANES), f32)],
        compiler_params=_cparams(1),
        name="rwkv7_scan",
    )(w_f, w_b, ka_f, ka_b, kd_f, kd_b, nkk, nkk, r, r, vv, vv, s0)


def _rwkv_scan_path(pre, row0, B, L, S0):
    H, K = N_RWKV_HEADS, RWKV_HEAD_DIM
    nd = 2
    v_lo = LANES // (B * H)
    VR = K // v_lo

    def kpack(name):
        x = lax.slice_in_dim(pre[name], row0, row0 + B * L, axis=0).reshape(B, L, H, K)
        x = jnp.transpose(x, (1, 3, 0, 2)).reshape(L, K, B * H)
        return jnp.tile(x, (1, 1, v_lo)) if v_lo > 1 else x

    vv = lax.slice_in_dim(pre['v'], row0, row0 + B * L, axis=0).reshape(B, L, H, VR, v_lo)
    vv = jnp.transpose(vv, (1, 3, 4, 0, 2)).reshape(L, VR, LANES)
    if S0 is None:
        s0 = jnp.zeros((VR, K, nd * LANES), f32)
    else:
        s0 = S0.astype(f32).reshape(B, nd, H, VR, v_lo, K)
        s0 = jnp.transpose(s0, (3, 5, 1, 4, 0, 2)).reshape(VR, K, nd * LANES)

    y_f, y_b, s_fin = _rwkv_scan_packed(
        kpack('w_f'), kpack('w_b'), kpack('ka_f'), kpack('ka_b'), kpack('kd_f'), kpack('kd_b'),
        kpack('nkk'), kpack('r'), vv, s0, steps=8 if v_lo == 1 else 16)

    def unpack(y):
        y = y.reshape(L, VR, v_lo, B, H)
        return jnp.transpose(y, (3, 0, 4, 1, 2)).reshape(B * L, H * K)

    s_fin = s_fin.reshape(VR, K, nd, v_lo, B, H)
    s_fin = jnp.transpose(s_fin, (4, 2, 5, 0, 3, 1)).reshape(B, nd, H, K, K)
    return unpack(y_f), unpack(y_b), s_fin


ATT_TQ = 256
HEAD_W = 2 * DIFF_HEAD_DIM


def _rope(x, cos, sin_signed, lo_mask):
    n = x.shape[1]
    partner = jnp.where(lo_mask, pltpu.roll(x, n - DIFF_HEAD_DIM // 4, axis=1),
                        pltpu.roll(x, DIFF_HEAD_DIM // 4, axis=1))
    return x * cos + partner * sin_signed


def _attn_heads(q, k, v, lam, subln_g, out_scale):
    lane = lax.broadcasted_iota(jnp.int32, (1, HEAD_W), 1)
    first = lane < DIFF_HEAD_DIM
    outs = []
    for h in range(N_DIFF_HEADS):
        cols = slice(h * HEAD_W, (h + 1) * HEAD_W)
        qh = (q[:, cols] * (DIFF_HEAD_DIM ** -0.5)).astype(bf16)
        kh = k[:, cols]
        vh = v[:, cols].astype(bf16)
        probs = []
        for m in range(2):
            km = jnp.where(first if m == 0 else ~first, kh, 0.0).astype(bf16)
            s = lax.dot_general(qh, km, (((1,), (1,)), ((), ())), preferred_element_type=f32)
            s = s - jnp.max(s, axis=-1, keepdims=True)
            e = jnp.exp(s)
            probs.append(e / jnp.sum(e, axis=-1, keepdims=True))
        attn = (probs[0] - lam * probs[1]).astype(bf16)
        o = jnp.dot(attn, vh, preferred_element_type=f32)
        o = o * lax.rsqrt(jnp.mean(o * o, axis=-1, keepdims=True) + NORM_EPS) * subln_g * out_scale
        outs.append(o)
    return jnp.concatenate(outs, axis=1)


def _attn_kernel(q_ref, k_ref, v_ref, ck_ref, cv_ref, cos_ref, sin_ref, lv_ref, g_ref, o_ref,
                 kall_ref, vall_ref, *, lam_init):
    blk = pl.program_id(0)
    qt = pl.program_id(1)
    lv = lv_ref[...]
    lam = (jnp.exp(jnp.sum(lv[0:1] * lv[1:2], axis=-1, keepdims=True))
           - jnp.exp(jnp.sum(lv[2:3] * lv[3:4], axis=-1, keepdims=True)) + lam_init)
    subln_g = g_ref[...]
    out_scale = 1.0 - lam_init
    row0 = pl.multiple_of(qt * ATT_TQ, ATT_TQ)

    @pl.when(blk < N_CTX_BLOCKS)
    def _():
        q = q_ref[pl.ds(row0, ATT_TQ), :]
        k = k_ref[pl.ds(row0, ATT_TQ), :]
        v = v_ref[pl.ds(row0, ATT_TQ), :]
        o_ref[...] = _attn_heads(q, k, v, lam, subln_g, out_scale).astype(o_ref.dtype)

    @pl.when(blk >= N_CTX_BLOCKS)
    def _():
        lane = lax.broadcasted_iota(jnp.int32, (1, D_DIFF), 1)
        lo_mask = (lane % (DIFF_HEAD_DIM // 2)) < (DIFF_HEAD_DIM // 4)

        @pl.when(qt == 0)
        def _():
            kall_ref[0:LAT_LEN, :] = _rope(k_ref[...], cos_ref[...], sin_ref[...], lo_mask)
            kall_ref[LAT_LEN:LAT_LEN + PAST_LEN, :] = ck_ref[0]
            vall_ref[0:LAT_LEN, :] = v_ref[...]
            vall_ref[LAT_LEN:LAT_LEN + PAST_LEN, :] = cv_ref[0]

        q = _rope(q_ref[pl.ds(row0, ATT_TQ), :], cos_ref[pl.ds(row0, ATT_TQ), :],
                  sin_ref[pl.ds(row0, ATT_TQ), :], lo_mask)
        o_ref[...] = _attn_heads(q, kall_ref[...], vall_ref[...], lam, subln_g,
                                 out_scale).astype(o_ref.dtype)


def _attention(z, cache_k_l, cache_v_l, rope_cos, rope_sin, p, lam_init):
    zcol = lambda off: pl.BlockSpec((SEQ_BLOCK, D_DIFF), lambda i, j, off=off: (i, off // D_DIFF))
    cache = pl.BlockSpec((1, PAST_LEN, D_DIFF), lambda i, j: (jnp.maximum(i - N_CTX_BLOCKS, 0), 0, 0))
    full = lambda shape: pl.BlockSpec(shape, lambda i, j: (0,) * len(shape))
    return pl.pallas_call(
        functools.partial(_attn_kernel, lam_init=lam_init),
        grid=(N_SEQ_BLOCKS, SEQ_BLOCK // ATT_TQ),
        in_specs=[zcol(Z_Q), zcol(Z_K), zcol(Z_V), cache, cache,
                  full((LAT_LEN, D_DIFF)), full((LAT_LEN, D_DIFF)),
                  full((4, DIFF_HEAD_DIM)), full((1, HEAD_W))],
        out_specs=pl.BlockSpec((ATT_TQ, D_DIFF), lambda i, j: (i * (SEQ_BLOCK // ATT_TQ) + j, 0)),
        out_shape=jax.ShapeDtypeStruct((N_TOK, D_DIFF), bf16),
        scratch_shapes=[pltpu.VMEM((LAT_LEN + PAST_LEN, D_DIFF), f32),
                        pltpu.VMEM((LAT_LEN + PAST_LEN, D_DIFF), f32)],
        compiler_params=_cparams(2),
        name="diff_attention",
    )(z, z, z, cache_k_l, cache_v_l, rope_cos, rope_sin, p['diff_lambda'], p['diff_subln_g'])


def _rope_tables():
    t = jnp.arange(LAT_LEN)
    row = (t // GRID_W).astype(f32)
    col = (t % GRID_W).astype(f32)
    nf = DIFF_HEAD_DIM // 4
    freqs = ROPE_THETA ** (-jnp.arange(nf, dtype=f32) / nf)
    d = jnp.arange(DIFF_HEAD_DIM)
    pos = jnp.where((d < DIFF_HEAD_DIM // 2)[None, :], row[:, None], col[:, None])
    ang = pos * freqs[d % nf][None, :]
    sign = jnp.where((d % (2 * nf)) < nf, -1.0, 1.0)[None, :]
    cos = jnp.tile(jnp.cos(ang), (1, D_DIFF // DIFF_HEAD_DIM))
    sin = jnp.tile(jnp.sin(ang) * sign, (1, D_DIFF // DIFF_HEAD_DIM))
    return cos, sin


MERGE_TM = 512


def _merge_kernel(x_ref, lru_ref, yf_ref, yb_ref, bonus_ref, g_ref, diff_ref, gp0_ref, gp1_ref, gp2_ref,
                  mod_ref, lnp_ref, ones_ref, wlo_ref, wro_ref, wdo_ref, wout_ref, ng_ref, rhi_ref, rlo_ref,
                  xo_ref, h2_ref, logit_ref):
    y = yf_ref[...] + yb_ref[...]
    inv = 1.0 / RWKV_HEAD_DIM
    mu = _split_dot(y, ones_ref[...]) * inv
    dlt = y - mu
    var = _split_dot(dlt * dlt, ones_ref[...]) * inv
    yn = dlt * lax.rsqrt(var + RWKV_LN_EPS)
    rwkv_out = ((yn * lnp_ref[0:1, :] + lnp_ref[1:2, :] + bonus_ref[...]) * g_ref[...]).astype(bf16)

    merged = (jax.nn.sigmoid(gp0_ref[...]) * jnp.dot(lru_ref[...], wlo_ref[...], preferred_element_type=f32)
              + jax.nn.sigmoid(gp1_ref[...]) * jnp.dot(rwkv_out, wro_ref[...], preferred_element_type=f32)
              + jax.nn.sigmoid(gp2_ref[...]) * jnp.dot(diff_ref[...], wdo_ref[...], preferred_element_type=f32))
    mix = jnp.dot(merged.astype(bf16), wout_ref[...], preferred_element_type=f32)

    gate1 = mod_ref[0, :, 2 * D_MODEL:3 * D_MODEL]
    shift2 = mod_ref[0, :, 3 * D_MODEL:4 * D_MODEL]
    scale2 = mod_ref[0, :, 4 * D_MODEL:5 * D_MODEL]
    x = x_ref[...] + gate1 * mix
    xo_ref[...] = x
    yn2 = x * lax.rsqrt(jnp.mean(x * x, axis=-1, keepdims=True) + NORM_EPS) * ng_ref[...]
    h2 = yn2 * (1.0 + scale2) + shift2
    hi = h2.astype(bf16)
    lo = (h2 - hi.astype(f32)).astype(bf16)
    h2_ref[...] = hi
    logit_ref[...] = (jnp.dot(hi, rhi_ref[...], preferred_element_type=f32)
                      + jnp.dot(lo, rhi_ref[...], preferred_element_type=f32)
                      + jnp.dot(hi, rlo_ref[...], preferred_element_type=f32))


def _merge(x, z, lru_out, y_f, y_b, bonus, g, diff_out, mod_l, p):
    row = lambda w: pl.BlockSpec((MERGE_TM, w), lambda i: (i, 0))
    gate = lambda b: pl.BlockSpec((MERGE_TM, D_MODEL), lambda i, b=b: (i, Z_GATES // D_MODEL + b))
    full = lambda shape: pl.BlockSpec(shape, lambda i: (0,) * len(shape))
    return pl.pallas_call(
        _merge_kernel,
        grid=(N_TOK // MERGE_TM,),
        in_specs=[row(D_MODEL), row(D_LRU), row(D_RWKV), row(D_RWKV), row(D_RWKV), row(D_RWKV), row(D_DIFF),
                  gate(0), gate(1), gate(2),
                  pl.BlockSpec((1, 1, 6 * D_MODEL), lambda i: (_mod_group_of_rows(i, MERGE_TM), 0, 0)),
                  full((2, D_RWKV)), full((D_RWKV, D_RWKV)),
                  full((D_LRU, D_MODEL)), full((D_RWKV, D_MODEL)), full((D_DIFF, D_MODEL)),
                  full((D_MODEL, D_MODEL)), full((1, D_MODEL)),
                  full((D_MODEL, ROUTER_PAD)), full((D_MODEL, ROUTER_PAD))],
        out_specs=[row(D_MODEL), row(D_MODEL), row(ROUTER_PAD)],
        out_shape=[jax.ShapeDtypeStruct((N_TOK, D_MODEL), f32),
                   jax.ShapeDtypeStruct((N_TOK, D_MODEL), bf16),
                   jax.ShapeDtypeStruct((N_TOK, ROUTER_PAD), f32)],
        compiler_params=_cparams(1),
        name="merge",
    )(x, lru_out, y_f, y_b, bonus, g, diff_out, z, z, z, mod_l, p['rwkv_lnp'], p['head_ones'],
      p['w_lru_out'], p['w_rwkv_out'], p['w_diff_out'], p['w_out'], p['norm_g2'],
      p['router_hi'], p['router_lo'])


FFN_TF = 512


def _ffn_kernel(xe_ref, wg_ref, wu_ref, wd_ref, o_ref):
    x = xe_ref[0]
    gate = jnp.dot(x, wg_ref[0, 0].astype(bf16), preferred_element_type=f32)
    up = jnp.dot(x, wu_ref[0, 0].astype(bf16), preferred_element_type=f32)
    hid = (gate * jax.nn.sigmoid(gate) * up).astype(bf16)
    part = jnp.dot(hid, wd_ref[0, 0].astype(bf16), preferred_element_type=f32)

    @pl.when(pl.program_id(1) == 0)
    def _():
        o_ref[0] = part

    @pl.when(pl.program_id(1) != 0)
    def _():
        o_ref[0] += part


def _expert_ffn(xe, w_gate, w_up, w_down, layer):
    n_e, cap, _ = xe.shape
    return pl.pallas_call(
        _ffn_kernel,
        grid=(n_e, EXPERT_FF // FFN_TF),
        in_specs=[pl.BlockSpec((1, cap, D_MODEL), lambda e, f: (e, 0, 0)),
                  pl.BlockSpec((1, 1, D_MODEL, FFN_TF), lambda e, f: (layer, e, 0, f)),
                  pl.BlockSpec((1, 1, D_MODEL, FFN_TF), lambda e, f: (layer, e, 0, f)),
                  pl.BlockSpec((1, 1, FFN_TF, D_MODEL), lambda e, f: (layer, e, f, 0))],
        out_specs=pl.BlockSpec((1, cap, D_MODEL), lambda e, f: (e, 0, 0)),
        out_shape=jax.ShapeDtypeStruct((n_e, cap, D_MODEL), f32),
        compiler_params=_cparams(2),
        name="expert_ffn",
    )(xe, w_gate, w_up, w_down)


def _route(logits, n):
    cap = EC_CAPACITY * n // N_EXPERTS
    aff = jax.nn.softmax(logits, axis=-1)
    return lax.top_k(aff.T, cap)


def _moe(h2, logits, w_gate, w_up, w_down, layer):
    lg = logits[:, :N_EXPERTS]
    gate_c, idx_c = _route(lg[:N_CTX_TOK], N_CTX_TOK)
    gate_l, idx_l = _route(lg[N_CTX_TOK:], N_TOK - N_CTX_TOK)
    idx = jnp.concatenate([idx_c, idx_l + N_CTX_TOK], axis=1)
    gate = jnp.concatenate([gate_c, gate_l], axis=1)
    ye = _expert_ffn(h2[idx], w_gate, w_up, w_down, layer) * gate[..., None]
    return jnp.zeros((N_TOK, D_MODEL), f32).at[idx.reshape(-1)].add(ye.reshape(-1, D_MODEL))


RESID_TM = 512


def _resid_kernel(x_ref, y_ref, mod_ref, g_ref, xo_ref, *, final_norm):
    gate2 = mod_ref[0, :, 5 * D_MODEL:6 * D_MODEL]
    x = x_ref[...] + gate2 * y_ref[...]
    if final_norm:
        x = x * lax.rsqrt(jnp.mean(x * x, axis=-1, keepdims=True) + NORM_EPS) * g_ref[...]
    xo_ref[...] = x


def _residual(x, y, mod_l, final_g, final_norm):
    row = pl.BlockSpec((RESID_TM, D_MODEL), lambda i: (i, 0))
    return pl.pallas_call(
        functools.partial(_resid_kernel, final_norm=final_norm),
        grid=(N_TOK // RESID_TM,),
        in_specs=[row, row,
                  pl.BlockSpec((1, 1, 6 * D_MODEL), lambda i: (_mod_group_of_rows(i, RESID_TM), 0, 0)),
                  pl.BlockSpec((1, D_MODEL), lambda i: (0, 0))],
        out_specs=row,
        out_shape=jax.ShapeDtypeStruct((N_TOK, D_MODEL), f32),
        compiler_params=_cparams(1),
        name="residual",
    )(x, y, mod_l, final_g)


def _block_diag(blocks):
    n, a, b = blocks.shape
    eye = jnp.eye(n, dtype=blocks.dtype)
    return (eye[:, None, :, None] * blocks[:, :, None, :]).reshape(n * a, n * b)


def _prep_layer(l, w):
    p = {}
    w_in = w['w_in'][l]
    lru_x, lru_g, zr, q, k, v, gates = (
        w_in[:, 0:D_LRU], w_in[:, D_LRU:2 * D_LRU], w_in[:, 2 * D_LRU:2 * D_LRU + RWKV_COLS],
        w_in[:, 2 * D_LRU + RWKV_COLS:2 * D_LRU + RWKV_COLS + D_DIFF],
        w_in[:, 2 * D_LRU + RWKV_COLS + D_DIFF:2 * D_LRU + RWKV_COLS + 2 * D_DIFF],
        w_in[:, 2 * D_LRU + RWKV_COLS + 2 * D_DIFF:2 * D_LRU + RWKV_COLS + 3 * D_DIFF],
        w_in[:, 2 * D_LRU + RWKV_COLS + 3 * D_DIFF:])
    pad = jnp.zeros((D_MODEL, RWKV_COLS_PAD - RWKV_COLS), f32)
    p['w_in'] = jnp.concatenate([zr, pad, gates, lru_x, lru_g, q, k, v], axis=1).astype(bf16)
    p['norm_g1'] = w['norm_g'][l, 0].reshape(1, D_MODEL)
    p['norm_g2'] = w['norm_g'][l, 1].reshape(1, D_MODEL)

    p['lru_conv_w'] = w['lru_conv_w'][l]
    p['lru_conv_b'] = w['lru_conv_b'][l].reshape(1, D_LRU)
    p['lru_wg'] = jnp.concatenate(
        [_block_diag(w[name][l, d]) for d in range(2) for name in ('lru_wa', 'lru_wx')], axis=1).astype(bf16)
    p['lru_bg'] = jnp.concatenate(
        [w[name][l, d] for d in range(2) for name in ('lru_ba', 'lru_bx')]).reshape(1, 4 * D_LRU)
    p['lru_lambda'] = w['lru_lambda'][l]

    p['rwkv_conv_w'] = jnp.pad(w['rwkv_conv_w'][l], ((0, 0), (0, RWKV_COLS_PAD - RWKV_COLS)))

    def lowrank_pair(m):
        rank = m.shape[1]
        out = jnp.zeros((N_RWKV_CT, 2 * rank, 2 * RWKV_CT), f32)
        for d in range(2):
            tiles = m[d].reshape(rank, N_RWKV_CT, RWKV_CT).transpose(1, 0, 2)
            out = out.at[:, d * rank:(d + 1) * rank, d * RWKV_CT:(d + 1) * RWKV_CT].set(tiles)
        return out.astype(bf16)

    p['rwkv_w2'] = lowrank_pair(w['rwkv_w2'][l])
    p['rwkv_a2'] = lowrank_pair(w['rwkv_a2'][l])
    p['rwkv_g2'] = w['rwkv_g2'][l].astype(bf16)
    p['rwkv_pc'] = jnp.concatenate(
        [w['rwkv_w0'][l], w['rwkv_a0'][l], w['rwkv_k_k'][l][None], w['rwkv_k_a'][l][None],
         w['rwkv_r_k'][l].reshape(1, D_RWKV), jnp.zeros((1, D_RWKV), f32)], axis=0)
    p['rwkv_lnp'] = jnp.stack([w['rwkv_ln_g'][l], w['rwkv_ln_b'][l]], axis=0)
    p['head_ones'] = _block_diag(jnp.ones((N_RWKV_HEADS, RWKV_HEAD_DIM, RWKV_HEAD_DIM), bf16))

    p['diff_lambda'] = w['diff_lambda'][l]
    p['diff_subln_g'] = w['diff_subln_g'][l].reshape(1, HEAD_W)
    p['w_lru_out'] = w['w_lru_out'][l].astype(bf16)
    p['w_rwkv_out'] = w['w_rwkv_out'][l].astype(bf16)
    p['w_diff_out'] = w['w_diff_out'][l].astype(bf16)
    p['w_out'] = w['w_out'][l].astype(bf16)
    rw = jnp.pad(w['router_w'][l], ((0, 0), (0, ROUTER_PAD - N_EXPERTS)))
    p['router_hi'] = rw.astype(bf16)
    p['router_lo'] = (rw - p['router_hi'].astype(f32)).astype(bf16)
    return p


def kernel(x_prompt, x_sample, cache_k, cache_v, state_lru, state_rwkv, c, c_ctx, norm_g, final_norm_g, w_mod, b_mod, w_in, lru_conv_w, lru_conv_b, lru_wa, lru_ba, lru_wx, lru_bx, lru_lambda, rwkv_conv_w, rwkv_w0, rwkv_w2, rwkv_a0, rwkv_a2, rwkv_k_k, rwkv_k_a, rwkv_r_k, rwkv_g2, rwkv_ln_g, rwkv_ln_b, diff_lambda, diff_subln_g, w_lru_out, w_rwkv_out, w_diff_out, w_out, router_w, exp_w_gate, exp_w_up, exp_w_down):
    w = dict(norm_g=norm_g, w_in=w_in, lru_conv_w=lru_conv_w, lru_conv_b=lru_conv_b, lru_wa=lru_wa,
             lru_ba=lru_ba, lru_wx=lru_wx, lru_bx=lru_bx, lru_lambda=lru_lambda, rwkv_conv_w=rwkv_conv_w,
             rwkv_w0=rwkv_w0, rwkv_w2=rwkv_w2, rwkv_a0=rwkv_a0, rwkv_a2=rwkv_a2, rwkv_k_k=rwkv_k_k,
             rwkv_k_a=rwkv_k_a, rwkv_r_k=rwkv_r_k, rwkv_g2=rwkv_g2, rwkv_ln_g=rwkv_ln_g,
             rwkv_ln_b=rwkv_ln_b, diff_lambda=diff_lambda, diff_subln_g=diff_subln_g, w_lru_out=w_lru_out,
             w_rwkv_out=w_rwkv_out, w_diff_out=w_diff_out, w_out=w_out, router_w=router_w)

    x = jnp.concatenate([x_prompt.reshape(N_CTX_TOK, D_MODEL), x_sample.reshape(-1, D_MODEL)], axis=0)
    cvec = jnp.concatenate([c_ctx[None], c, jnp.zeros((SUBLANES - N_MOD_GROUPS, D_MODEL), f32)], axis=0)
    mod = _modulation(cvec, w_mod, b_mod)
    rope_cos, rope_sin = _rope_tables()
    final_g = final_norm_g.reshape(1, D_MODEL)
    ctx_rows = slice(0, N_CTX_TOK)
    lat_rows = slice(N_CTX_TOK, N_TOK)

    new_k, new_v, new_lru, new_rwkv = [], [], [], []
    for l in range(DEPTH):
        p = _prep_layer(l, w)
        lam_init = 0.8 - 0.6 * math.exp(-0.3 * l)
        mod_l = mod[l].reshape(SUBLANES, 1, 6 * D_MODEL)

        z = _in_projection(x, mod_l, p['norm_g1'], p['w_in'])
        new_k.append(z[ctx_rows, Z_K:Z_K + D_DIFF].reshape(N_CTX_SEQ, CTX_LEN, N_DIFF_HEADS, 2, DIFF_HEAD_DIM))
        new_v.append(z[ctx_rows, Z_V:Z_V + D_DIFF].reshape(N_CTX_SEQ, CTX_LEN, N_DIFF_HEADS, HEAD_W))

        h0_blocks = jnp.concatenate(
            [jnp.zeros((N_CTX_BLOCKS, 2, D_LRU), f32), state_lru[:, l].astype(f32)], axis=0)
        lru_out, lru_fin = _lru_branch(z, p, h0_blocks)
        n_per = SEQ_BLOCK // CTX_LEN
        fin = lru_fin[:N_CTX_BLOCKS].reshape(N_CTX_BLOCKS, 2, n_per, D_LRU)
        new_lru.append(jnp.transpose(fin, (0, 2, 1, 3)).reshape(N_CTX_SEQ, 2, D_LRU))

        pre = _rwkv_pre(z, p)
        yf_c, yb_c, s_c = _rwkv_scan_path(pre, 0, N_CTX_SEQ, CTX_LEN, None)
        yf_l, yb_l, _ = _rwkv_scan_path(pre, N_CTX_TOK, N_LAT_SEQ, LAT_LEN, state_rwkv[:, l])
        new_rwkv.append(s_c)
        y_f = jnp.concatenate([yf_c, yf_l], axis=0)
        y_b = jnp.concatenate([yb_c, yb_l], axis=0)

        diff_out = _attention(z, cache_k[:, l].reshape(N_LAT_SEQ, PAST_LEN, D_DIFF),
                              cache_v[:, l].reshape(N_LAT_SEQ, PAST_LEN, D_DIFF),
                              rope_cos, rope_sin, p, lam_init)

        x, h2, logits = _merge(x, z, lru_out, y_f, y_b, pre['bonus'], pre['g'], diff_out, mod_l, p)
        moe = _moe(h2, logits, exp_w_gate, exp_w_up, exp_w_down, l)
        x = _residual(x, moe, mod_l, final_g, final_norm=(l == DEPTH - 1))

    y_prompt = x[ctx_rows].reshape(x_prompt.shape)
    y_sample = x[lat_rows].reshape(x_sample.shape)
    return (y_prompt, y_sample, jnp.stack(new_k, axis=1), jnp.stack(new_v, axis=1),
            jnp.stack(new_lru, axis=1), jnp.stack(new_rwkv, axis=1))
```

```python
import functools
import math

import jax
import jax.numpy as jnp
from jax import lax
from jax.experimental import pallas as pl
from jax.experimental.pallas import tpu as pltpu

f32 = jnp.float32
bf16 = jnp.bfloat16

D_MODEL = 1024
DEPTH = 2
GRID_W = 64
NORM_EPS = 1e-6

N_CTX_SEQ = 16
CTX_LEN = 256
N_LAT_SEQ = 2
LAT_LEN = 1024
PAST_LEN = 256
N_CTX_TOK = N_CTX_SEQ * CTX_LEN
N_TOK = N_CTX_TOK + N_LAT_SEQ * LAT_LEN

D_LRU = 512
LRU_BLOCKS = 8
LRU_BLOCK_W = D_LRU // LRU_BLOCKS
LRU_C = 8.0

D_RWKV = 512
RWKV_HEAD_DIM = 64
N_RWKV_HEADS = D_RWKV // RWKV_HEAD_DIM
W_RANK = 64
A_RANK = 64
G_RANK = 128
RWKV_LN_EPS = 64e-5
RWKV_COLS = 3 * D_RWKV + 2 * W_RANK + 2 * A_RANK + G_RANK
RWKV_COLS_PAD = 2048

N_DIFF_HEADS = 4
DIFF_HEAD_DIM = 64
D_DIFF = N_DIFF_HEADS * 2 * DIFF_HEAD_DIM
ROPE_THETA = 10000.0

N_BRANCHES = 3
N_EXPERTS = 16
EXPERT_FF = 1024
EC_CAPACITY = 2
ROUTER_PAD = 128

LANES = 128
SUBLANES = 8

SEQ_BLOCK = 1024
N_SEQ_BLOCKS = N_TOK // SEQ_BLOCK
N_CTX_BLOCKS = N_CTX_TOK // SEQ_BLOCK
N_MOD_GROUPS = 1 + N_LAT_SEQ

Z_RWKV = 0
Z_GATES = RWKV_COLS_PAD
Z_LRU_X = Z_GATES + N_BRANCHES * D_MODEL
Z_LRU_G = Z_LRU_X + D_LRU
Z_Q = Z_LRU_G + D_LRU
Z_K = Z_Q + D_DIFF
Z_V = Z_K + D_DIFF
Z_COLS = Z_V + D_DIFF

VMEM_LIMIT = 56 * 1024 * 1024


def _cparams(n_axes):
    return pltpu.CompilerParams(dimension_semantics=("arbitrary",) * n_axes,
                                vmem_limit_bytes=VMEM_LIMIT)


def _seg_len(block_idx):
    return jnp.where(block_idx < N_CTX_BLOCKS, CTX_LEN, LAT_LEN)


def _mod_group_of_rows(row_block, rows_per_block):
    first_lat = N_CTX_TOK // rows_per_block
    per_seq = LAT_LEN // rows_per_block
    return jnp.where(row_block < first_lat, 0, (row_block - first_lat) // per_seq + 1)


def _shift_rows(v, d, tm, seg):
    n = v.shape[0]
    r = pltpu.roll(v, (-d) % n, axis=0)
    ok = (tm + d >= 0) & (tm + d < seg)
    return jnp.where(ok, r, 0.0)


def _split_dot(x, w_bf16):
    hi = x.astype(bf16)
    lo = (x - hi.astype(f32)).astype(bf16)
    return (jnp.dot(hi, w_bf16, preferred_element_type=f32)
            + jnp.dot(lo, w_bf16, preferred_element_type=f32))


MOD_TN = 1536


def _mod_kernel(c_ref, w_ref, b_ref, o_ref):
    c = c_ref[...]
    s = (c * jax.nn.sigmoid(c)).astype(bf16)
    o_ref[0] = jnp.dot(s, w_ref[0].astype(bf16), preferred_element_type=f32) + b_ref[0]


def _modulation(cvec, w_mod, b_mod):
    n_out = w_mod.shape[-1]
    return pl.pallas_call(
        _mod_kernel,
        grid=(DEPTH, n_out // MOD_TN),
        in_specs=[pl.BlockSpec((SUBLANES, D_MODEL), lambda l, j: (0, 0)),
                  pl.BlockSpec((1, D_MODEL, MOD_TN), lambda l, j: (l, 0, j)),
                  pl.BlockSpec((1, 1, MOD_TN), lambda l, j: (l, 0, j))],
        out_specs=pl.BlockSpec((1, SUBLANES, MOD_TN), lambda l, j: (l, 0, j)),
        out_shape=jax.ShapeDtypeStruct((DEPTH, SUBLANES, n_out), f32),
        compiler_params=_cparams(2),
        name="modulation",
    )(cvec, w_mod, b_mod.reshape(DEPTH, 1, n_out))


INPROJ_TM = 1024
INPROJ_TN = 512


def _inproj_kernel(x_ref, mod_ref, g_ref, w_ref, o_ref, h_ref):
    @pl.when(pl.program_id(1) == 0)
    def _():
        x = x_ref[...]
        y = x * lax.rsqrt(jnp.mean(x * x, axis=-1, keepdims=True) + NORM_EPS) * g_ref[...]
        shift = mod_ref[0, :, 0:D_MODEL]
        scale = mod_ref[0, :, D_MODEL:2 * D_MODEL]
        h_ref[...] = (y * (1.0 + scale) + shift).astype(bf16)

    o_ref[...] = jnp.dot(h_ref[...], w_ref[...], preferred_element_type=f32)


def _in_projection(x, mod_l, norm_g_row, w_in_p):
    return pl.pallas_call(
        _inproj_kernel,
        grid=(N_TOK // INPROJ_TM, Z_COLS // INPROJ_TN),
        in_specs=[pl.BlockSpec((INPROJ_TM, D_MODEL), lambda i, j: (i, 0)),
                  pl.BlockSpec((1, 1, 2 * D_MODEL),
                               lambda i, j: (_mod_group_of_rows(i, INPROJ_TM), 0, 0)),
                  pl.BlockSpec((1, D_MODEL), lambda i, j: (0, 0)),
                  pl.BlockSpec((D_MODEL, INPROJ_TN), lambda i, j: (0, j))],
        out_specs=pl.BlockSpec((INPROJ_TM, INPROJ_TN), lambda i, j: (i, j)),
        out_shape=jax.ShapeDtypeStruct((N_TOK, Z_COLS), f32),
        scratch_shapes=[pltpu.VMEM((INPROJ_TM, D_MODEL), bf16)],
        compiler_params=_cparams(2),
        name="in_projection",
    )(x, mod_l, norm_g_row, w_in_p)


def _lru_kernel(x_ref, gate_ref, cw_ref, cb_ref, wg_ref, bg_ref, lam_ref, h0_ref, out_ref, hfin_ref):
    blk = pl.program_id(0)
    seg = _seg_len(blk)
    n = x_ref.shape[0]
    t = lax.broadcasted_iota(jnp.int32, (n, D_LRU), 0)
    tm = t & (seg - 1)

    x = x_ref[...]
    xc = (cb_ref[...]
          + cw_ref[0:1, :] * _shift_rows(x, -2, tm, seg)
          + cw_ref[1:2, :] * _shift_rows(x, -1, tm, seg)
          + cw_ref[2:3, :] * x
          + cw_ref[3:4, :] * _shift_rows(x, 1, tm, seg))
    pre = jnp.dot(xc.astype(bf16), wg_ref[...], preferred_element_type=f32) + bg_ref[...]

    ys = []
    for d in range(2):
        r = jax.nn.sigmoid(pre[:, (2 * d) * D_LRU:(2 * d + 1) * D_LRU])
        i = jax.nn.sigmoid(pre[:, (2 * d + 1) * D_LRU:(2 * d + 2) * D_LRU])
        log_a = -LRU_C * r * jax.nn.softplus(-lam_ref[d:d + 1, :])
        a = jnp.exp(log_a)
        b = jnp.sqrt(1.0 - a * a) * (i * xc)
        first = 0 if d == 0 else n - 1
        b = b + jnp.where(t == first, a * h0_ref[0, d:d + 1, :], 0.0)
        s = 1
        while s < n:
            if d == 0:
                ok = tm >= s
                shift = s
            else:
                ok = tm < seg - s
                shift = n - s
            a_sh = jnp.where(ok, pltpu.roll(a, shift, axis=0), 1.0)
            b_sh = jnp.where(ok, pltpu.roll(b, shift, axis=0), 0.0)
            b = a * b_sh + b
            a = a * a_sh
            s *= 2
        ys.append(b)

    y_f, y_b = ys
    out_ref[...] = ((y_f + y_b) * jax.nn.gelu(gate_ref[...])).astype(out_ref.dtype)
    fins = ([y_f[(j + 1) * CTX_LEN - 1:(j + 1) * CTX_LEN, :] for j in range(SEQ_BLOCK // CTX_LEN)]
            + [y_b[j * CTX_LEN:j * CTX_LEN + 1, :] for j in range(SEQ_BLOCK // CTX_LEN)])
    hfin_ref[0] = jnp.concatenate(fins, axis=0)


def _lru_branch(z, p, h0_blocks):
    full = lambda shape: pl.BlockSpec(shape, lambda i: (0,) * len(shape))
    return pl.pallas_call(
        _lru_kernel,
        grid=(N_SEQ_BLOCKS,),
        in_specs=[pl.BlockSpec((SEQ_BLOCK, D_LRU), lambda i: (i, Z_LRU_X // D_LRU)),
                  pl.BlockSpec((SEQ_BLOCK, D_LRU), lambda i: (i, Z_LRU_G // D_LRU)),
                  full((4, D_LRU)), full((1, D_LRU)), full((D_LRU, 4 * D_LRU)), full((1, 4 * D_LRU)),
                  full((2, D_LRU)),
                  pl.BlockSpec((1, 2, D_LRU), lambda i: (i, 0, 0))],
        out_specs=[pl.BlockSpec((SEQ_BLOCK, D_LRU), lambda i: (i, 0)),
                   pl.BlockSpec((1, SUBLANES, D_LRU), lambda i: (i, 0, 0))],
        out_shape=[jax.ShapeDtypeStruct((N_TOK, D_LRU), bf16),
                   jax.ShapeDtypeStruct((N_SEQ_BLOCKS, SUBLANES, D_LRU), f32)],
        compiler_params=_cparams(1),
        name="lru_branch",
    )(z, z, p['lru_conv_w'], p['lru_conv_b'], p['lru_wg'], p['lru_bg'], p['lru_lambda'], h0_blocks)


RWKV_CT = 256
N_RWKV_CT = D_RWKV // RWKV_CT
RWKV_LOWRANK_OFF = 3 * D_RWKV
RWKV_LOWRANK_W = RWKV_COLS_PAD - RWKV_LOWRANK_OFF
SCAN_ARRAYS = ('w_f', 'w_b', 'ka_f', 'ka_b', 'kd_f', 'kd_b', 'nkk', 'r', 'v')
N_SCAN_ARRAYS = len(SCAN_ARRAYS)
SCAN_IDX = {name: i for i, name in enumerate(SCAN_ARRAYS)}


def _rwkv_pre_kernel(r_ref, k_ref, v_ref, lr_ref, cwr_ref, cwk_ref, cwv_ref, cwl_ref,
                     w2_ref, a2_ref, g2_ref, pc_ref, ones_ref, sc_ref, bonus_ref, g_ref):
    blk = pl.program_id(0)
    seg = _seg_len(blk)
    n = r_ref.shape[0]

    def conv(x_ref, cw_ref):
        width = x_ref.shape[1]
        t = lax.broadcasted_iota(jnp.int32, (n, width), 0)
        tm = t & (seg - 1)
        x = x_ref[...]
        return (cw_ref[0:1, :] * _shift_rows(x, -1, tm, seg) + cw_ref[1:2, :] * x
                + cw_ref[2:3, :] * _shift_rows(x, 1, tm, seg))

    r = conv(r_ref, cwr_ref)
    k = conv(k_ref, cwk_ref)
    v = conv(v_ref, cwv_ref)
    lr = conv(lr_ref, cwl_ref)
    wl = lr[:, 0:2 * W_RANK]
    al = lr[:, 2 * W_RANK:2 * W_RANK + 2 * A_RANK]
    gl = lr[:, 2 * W_RANK + 2 * A_RANK:2 * W_RANK + 2 * A_RANK + G_RANK]

    w0 = [pc_ref[0:1, :], pc_ref[1:2, :]]
    a0 = [pc_ref[2:3, :], pc_ref[3:4, :]]
    k_k = pc_ref[4:5, :]
    k_a = pc_ref[5:6, :]
    r_k = pc_ref[6:7, :]

    w_pre = jnp.dot(jnp.tanh(wl).astype(bf16), w2_ref[0], preferred_element_type=f32)
    a_pre = jnp.dot(al.astype(bf16), a2_ref[0], preferred_element_type=f32)

    kk = k * k_k
    ss = _split_dot(kk * kk, ones_ref[...])
    kk = kk / jnp.maximum(jnp.sqrt(ss), 1e-12)

    kd_sum = None
    for d, sfx in enumerate(('_f', '_b')):
        cols = slice(d * RWKV_CT, (d + 1) * RWKV_CT)
        w_log = -jax.nn.softplus(-(w0[d] + w_pre[:, cols])) - 0.5
        sc_ref[SCAN_IDX['w' + sfx]] = jnp.exp(-jnp.exp(w_log))
        a = jax.nn.sigmoid(a0[d] + a_pre[:, cols])
        sc_ref[SCAN_IDX['ka' + sfx]] = kk * a
        kd = k * (1.0 + (a - 1.0) * k_a)
        sc_ref[SCAN_IDX['kd' + sfx]] = kd
        kd_sum = kd if kd_sum is None else kd_sum + kd

    sc_ref[SCAN_IDX['r']] = r
    sc_ref[SCAN_IDX['v']] = v
    sc_ref[SCAN_IDX['nkk']] = -kk
    bonus_ref[...] = _split_dot(r * kd_sum * r_k, ones_ref[...]) * v
    g_ref[...] = jnp.dot(jax.nn.sigmoid(gl).astype(bf16), g2_ref[...], preferred_element_type=f32)


def _rwkv_pre(z, p):
    nct = D_RWKV // RWKV_CT
    zcol = lambda base: pl.BlockSpec((SEQ_BLOCK, RWKV_CT), lambda i, c, base=base: (i, base + c))
    cwcol = lambda base: pl.BlockSpec((3, RWKV_CT), lambda i, c, base=base: (0, base + c))
    out_spec = pl.BlockSpec((SEQ_BLOCK, RWKV_CT), lambda i, c: (i, c))
    return pl.pallas_call(
        _rwkv_pre_kernel,
        grid=(N_SEQ_BLOCKS, nct),
        in_specs=[zcol(0), zcol(nct), zcol(2 * nct),
                  pl.BlockSpec((SEQ_BLOCK, RWKV_LOWRANK_W),
                               lambda i, c: (i, RWKV_LOWRANK_OFF // RWKV_LOWRANK_W)),
                  cwcol(0), cwcol(nct), cwcol(2 * nct),
                  pl.BlockSpec((3, RWKV_LOWRANK_W), lambda i, c: (0, RWKV_LOWRANK_OFF // RWKV_LOWRANK_W)),
                  pl.BlockSpec((1, 2 * W_RANK, 2 * RWKV_CT), lambda i, c: (c, 0, 0)),
                  pl.BlockSpec((1, 2 * A_RANK, 2 * RWKV_CT), lambda i, c: (c, 0, 0)),
                  pl.BlockSpec((G_RANK, RWKV_CT), lambda i, c: (0, c)),
                  pl.BlockSpec((SUBLANES, RWKV_CT), lambda i, c: (0, c)),
                  pl.BlockSpec((RWKV_CT, RWKV_CT), lambda i, c: (0, 0))],
        out_specs=[pl.BlockSpec((N_SCAN_ARRAYS, SEQ_BLOCK, RWKV_CT), lambda i, c: (0, i, c)),
                   out_spec, out_spec],
        out_shape=[jax.ShapeDtypeStruct((N_SCAN_ARRAYS, N_TOK, D_RWKV), f32),
                   jax.ShapeDtypeStruct((N_TOK, D_RWKV), f32),
                   jax.ShapeDtypeStruct((N_TOK, D_RWKV), f32)],
        compiler_params=_cparams(2),
        name="rwkv_pre",
    )(z, z, z, z, p['rwkv_conv_w'], p['rwkv_conv_w'], p['rwkv_conv_w'], p['rwkv_conv_w'],
      p['rwkv_w2'], p['rwkv_a2'], p['rwkv_g2'], p['rwkv_pc'], p['head_ones'][:RWKV_CT, :RWKV_CT])


PACK_T = 128
PACK_UNROLL = 4


def _gather_states(scr, row, n_seq):
    return scr[:, pl.ds(row, N_RWKV_HEADS, stride=RWKV_HEAD_DIM), :].reshape(n_seq * N_RWKV_HEADS, PACK_T)


def _pack_kernel(x_ref, o_ref, scr, *, n_seq, reps, rows, fold_rows):
    for b in range(n_seq):
        scr[b] = x_ref[b].T

    def body(row, c):
        if fold_rows:
            m = jnp.concatenate([_gather_states(scr, row * reps + rep, n_seq) for rep in range(reps)], axis=0)
        else:
            m = _gather_states(scr, row, n_seq)
            if reps > 1:
                m = jnp.concatenate([m] * reps, axis=0)
        o_ref[pl.ds(row, PACK_T, stride=rows), :] = m.T
        return c

    lax.fori_loop(0, rows, body, 0, unroll=PACK_UNROLL)


def _pack(stacked, first, count, n_seq, seq_len, seq0, reps, fold_rows):
    rows = RWKV_HEAD_DIM // reps if fold_rows else RWKV_HEAD_DIM
    x = stacked.reshape(N_SCAN_ARRAYS, N_TOK // seq_len, seq_len, D_RWKV)
    out = pl.pallas_call(
        functools.partial(_pack_kernel, n_seq=n_seq, reps=reps, rows=rows, fold_rows=fold_rows),
        grid=(count, seq_len // PACK_T),
        in_specs=[pl.BlockSpec((None, n_seq, PACK_T, D_RWKV),
                               lambda a, j: (first + a, seq0 // n_seq, j, 0))],
        out_specs=pl.BlockSpec((None, PACK_T * rows, LANES), lambda a, j: (a, j, 0)),
        out_shape=jax.ShapeDtypeStruct((count, seq_len * rows, LANES), f32),
        scratch_shapes=[pltpu.VMEM((n_seq, D_RWKV, PACK_T), f32)],
        compiler_params=_cparams(2),
        name="rwkv_pack",
    )(x)
    return out.reshape(count, seq_len, rows, LANES)


def _unpack_kernel(yf_ref, yb_ref, of_ref, ob_ref, scr, *, n_seq, reps, rows):
    group = n_seq * N_RWKV_HEADS
    for y_ref, o_ref in ((yf_ref, of_ref), (yb_ref, ob_ref)):
        def body(row, c, y_ref=y_ref):
            m = y_ref[pl.ds(row, PACK_T, stride=rows), :].T
            for rep in range(reps):
                part = m[rep * group:(rep + 1) * group].reshape(n_seq, N_RWKV_HEADS, PACK_T)
                scr[:, pl.ds(row * reps + rep, N_RWKV_HEADS, stride=RWKV_HEAD_DIM), :] = part
            return c

        lax.fori_loop(0, rows, body, 0, unroll=PACK_UNROLL)
        for b in range(n_seq):
            o_ref[b] = scr[b].T


def _unpack(y_f, y_b, n_seq, seq_len, reps):
    rows = y_f.shape[1]
    yspec = pl.BlockSpec((PACK_T * rows, LANES), lambda j: (j, 0))
    ospec = pl.BlockSpec((n_seq, PACK_T, D_RWKV), lambda j: (0, j, 0))
    shape = jax.ShapeDtypeStruct((n_seq, seq_len, D_RWKV), f32)
    o_f, o_b = pl.pallas_call(
        functools.partial(_unpack_kernel, n_seq=n_seq, reps=reps, rows=rows),
        grid=(seq_len // PACK_T,),
        in_specs=[yspec, yspec],
        out_specs=[ospec, ospec],
        out_shape=[shape, shape],
        scratch_shapes=[pltpu.VMEM((n_seq, D_RWKV, PACK_T), f32)],
        compiler_params=_cparams(1),
        name="rwkv_unpack",
    )(y_f.reshape(seq_len * rows, LANES), y_b.reshape(seq_len * rows, LANES))
    return o_f.reshape(n_seq * seq_len, D_RWKV), o_b.reshape(n_seq * seq_len, D_RWKV)


def _rwkv_scan_kernel(wf_ref, wb_ref, kaf_ref, kab_ref, kdf_ref, kdb_ref, nkkf_ref, nkkb_ref,
                      rf_ref, rb_ref, vvf_ref, vvb_ref, s0_ref, yf_ref, yb_ref, s_ref, *,
                      steps, v_rows, unroll):
    @pl.when(pl.program_id(0) == 0)
    def _():
        s_ref[...] = s0_ref[...]

    dirs = ((wf_ref, kaf_ref, kdf_ref, nkkf_ref, rf_ref, vvf_ref, yf_ref),
            (wb_ref, kab_ref, kdb_ref, nkkb_ref, rb_ref, vvb_ref, yb_ref))

    def step(t, carry):
        for g, (w_ref, ka_ref, kd_ref, nkk_ref, r_ref, vv_ref, y_ref) in enumerate(dirs):
            tt = t if g == 0 else steps - 1 - t
            lanes = slice(g * LANES, (g + 1) * LANES)
            w = w_ref[tt]
            ka = ka_ref[tt]
            kd = kd_ref[tt]
            nkk = nkk_ref[tt]
            r = r_ref[tt]

            def rows(vb, c):
                v0 = pl.multiple_of(vb * SUBLANES, SUBLANES)
                vv = vv_ref[tt, pl.ds(v0, SUBLANES), :]
                ys = []
                for j in range(SUBLANES):
                    s = s_ref[v0 + j, :, lanes]
                    sa = jnp.sum(s * nkk, axis=0, keepdims=True)
                    s_new = s * w + ka * sa + kd * vv[j:j + 1]
                    s_ref[v0 + j, :, lanes] = s_new
                    ys.append(jnp.sum(s_new * r, axis=0, keepdims=True))
                y_ref[tt, pl.ds(v0, SUBLANES), :] = jnp.concatenate(ys, axis=0)
                return c

            lax.fori_loop(0, v_rows // SUBLANES, rows, 0, unroll=unroll)
        return carry

    lax.fori_loop(0, steps, step, 0)


def _rwkv_scan_packed(kvecs, vv, s0, *, steps):
    _, L, K, _ = kvecs.shape
    VR = vv.shape[1]
    nblk = L // steps

    def kspec(name, mirrored):
        a = SCAN_IDX[name]
        if mirrored:
            return pl.BlockSpec((None, steps, K, LANES), lambda i: (a, nblk - 1 - i, 0, 0))
        return pl.BlockSpec((None, steps, K, LANES), lambda i: (a, i, 0, 0))

    vf = pl.BlockSpec((steps, VR, LANES), lambda i: (i, 0, 0))
    vb = pl.BlockSpec((steps, VR, LANES), lambda i: (nblk - 1 - i, 0, 0))
    sspec = pl.BlockSpec((VR, K, 2 * LANES), lambda i: (0, 0, 0))
    kern = functools.partial(_rwkv_scan_kernel, steps=steps, v_rows=VR,
                             unroll=min(2, VR // SUBLANES))
    return pl.pallas_call(
        kern,
        grid=(nblk,),
        in_specs=[kspec('w_f', False), kspec('w_b', True), kspec('ka_f', False), kspec('ka_b', True),
                  kspec('kd_f', False), kspec('kd_b', True), kspec('nkk', False), kspec('nkk', True),
                  kspec('r', False), kspec('r', True), vf, vb, sspec],
        out_specs=[vf, vb, sspec],
        out_shape=[jax.ShapeDtypeStruct((L, VR, LANES), f32),
                   jax.ShapeDtypeStruct((L, VR, LANES), f32),
                   jax.ShapeDtypeStruct((VR, K, 2 * LANES), f32)],
        compiler_params=_cparams(1),
        name="rwkv7_scan",
    )(*([kvecs] * 10), vv, vv, s0)


def _rwkv_scan_path(stacked, seq0, B, L, S0):
    H, K = N_RWKV_HEADS, RWKV_HEAD_DIM
    nd = 2
    reps = LANES // (B * H)
    VR = K // reps
    kvecs = _pack(stacked, 0, N_SCAN_ARRAYS - 1, B, L, seq0, reps, fold_rows=False)
    vv = _pack(stacked, SCAN_IDX['v'], 1, B, L, seq0, reps, fold_rows=True)[0]
    if S0 is None:
        s0 = jnp.zeros((VR, K, nd * LANES), f32)
    else:
        s0 = S0.astype(f32).reshape(B, nd, H, VR, reps, K)
        s0 = jnp.transpose(s0, (3, 5, 1, 4, 0, 2)).reshape(VR, K, nd * LANES)

    y_f, y_b, s_fin = _rwkv_scan_packed(kvecs, vv, s0, steps=8 if reps == 1 else 16)
    y_f, y_b = _unpack(y_f, y_b, B, L, reps)
    s_fin = s_fin.reshape(VR, K, nd, reps, B, H)
    s_fin = jnp.transpose(s_fin, (4, 2, 5, 0, 3, 1)).reshape(B, nd, H, K, K)
    return y_f, y_b, s_fin


ATT_TQ = 256
HEAD_W = 2 * DIFF_HEAD_DIM


def _rope(x, cos, sin_signed, lo_mask):
    n = x.shape[1]
    partner = jnp.where(lo_mask, pltpu.roll(x, n - DIFF_HEAD_DIM // 4, axis=1),
                        pltpu.roll(x, DIFF_HEAD_DIM // 4, axis=1))
    return x * cos + partner * sin_signed


def _attn_heads(q, k, v, lam, subln_g, out_scale):
    lane = lax.broadcasted_iota(jnp.int32, (1, HEAD_W), 1)
    first = lane < DIFF_HEAD_DIM
    outs = []
    for h in range(N_DIFF_HEADS):
        cols = slice(h * HEAD_W, (h + 1) * HEAD_W)
        qh = (q[:, cols] * (DIFF_HEAD_DIM ** -0.5)).astype(bf16)
        kh = k[:, cols]
        vh = v[:, cols].astype(bf16)
        probs = []
        for m in range(2):
            km = jnp.where(first if m == 0 else ~first, kh, 0.0).astype(bf16)
            s = lax.dot_general(qh, km, (((1,), (1,)), ((), ())), preferred_element_type=f32)
            s = s - jnp.max(s, axis=-1, keepdims=True)
            e = jnp.exp(s)
            probs.append(e / jnp.sum(e, axis=-1, keepdims=True))
        attn = (probs[0] - lam * probs[1]).astype(bf16)
        o = jnp.dot(attn, vh, preferred_element_type=f32)
        o = o * lax.rsqrt(jnp.mean(o * o, axis=-1, keepdims=True) + NORM_EPS) * subln_g * out_scale
        outs.append(o)
    return jnp.concatenate(outs, axis=1)


def _attn_kernel(q_ref, k_ref, v_ref, ck_ref, cv_ref, cos_ref, sin_ref, lv_ref, g_ref, o_ref,
                 kall_ref, vall_ref, *, lam_init):
    blk = pl.program_id(0)
    qt = pl.program_id(1)
    lv = lv_ref[...]
    lam = (jnp.exp(jnp.sum(lv[0:1] * lv[1:2], axis=-1, keepdims=True))
           - jnp.exp(jnp.sum(lv[2:3] * lv[3:4], axis=-1, keepdims=True)) + lam_init)
    subln_g = g_ref[...]
    out_scale = 1.0 - lam_init
    row0 = pl.multiple_of(qt * ATT_TQ, ATT_TQ)

    @pl.when(blk < N_CTX_BLOCKS)
    def _():
        q = q_ref[pl.ds(row0, ATT_TQ), :]
        k = k_ref[pl.ds(row0, ATT_TQ), :]
        v = v_ref[pl.ds(row0, ATT_TQ), :]
        o_ref[...] = _attn_heads(q, k, v, lam, subln_g, out_scale).astype(o_ref.dtype)

    @pl.when(blk >= N_CTX_BLOCKS)
    def _():
        lane = lax.broadcasted_iota(jnp.int32, (1, D_DIFF), 1)
        lo_mask = (lane % (DIFF_HEAD_DIM // 2)) < (DIFF_HEAD_DIM // 4)

        @pl.when(qt == 0)
        def _():
            kall_ref[0:LAT_LEN, :] = _rope(k_ref[...], cos_ref[...], sin_ref[...], lo_mask)
            kall_ref[LAT_LEN:LAT_LEN + PAST_LEN, :] = ck_ref[0]
            vall_ref[0:LAT_LEN, :] = v_ref[...]
            vall_ref[LAT_LEN:LAT_LEN + PAST_LEN, :] = cv_ref[0]

        q = _rope(q_ref[pl.ds(row0, ATT_TQ), :], cos_ref[pl.ds(row0, ATT_TQ), :],
                  sin_ref[pl.ds(row0, ATT_TQ), :], lo_mask)
        o_ref[...] = _attn_heads(q, kall_ref[...], vall_ref[...], lam, subln_g,
                                 out_scale).astype(o_ref.dtype)


def _attention(z, cache_k_l, cache_v_l, rope_cos, rope_sin, p, lam_init):
    zcol = lambda off: pl.BlockSpec((SEQ_BLOCK, D_DIFF), lambda i, j, off=off: (i, off // D_DIFF))
    cache = pl.BlockSpec((1, PAST_LEN, D_DIFF), lambda i, j: (jnp.maximum(i - N_CTX_BLOCKS, 0), 0, 0))
    full = lambda shape: pl.BlockSpec(shape, lambda i, j: (0,) * len(shape))
    return pl.pallas_call(
        functools.partial(_attn_kernel, lam_init=lam_init),
        grid=(N_SEQ_BLOCKS, SEQ_BLOCK // ATT_TQ),
        in_specs=[zcol(Z_Q), zcol(Z_K), zcol(Z_V), cache, cache,
                  full((LAT_LEN, D_DIFF)), full((LAT_LEN, D_DIFF)),
                  full((4, DIFF_HEAD_DIM)), full((1, HEAD_W))],
        out_specs=pl.BlockSpec((ATT_TQ, D_DIFF), lambda i, j: (i * (SEQ_BLOCK // ATT_TQ) + j, 0)),
        out_shape=jax.ShapeDtypeStruct((N_TOK, D_DIFF), bf16),
        scratch_shapes=[pltpu.VMEM((LAT_LEN + PAST_LEN, D_DIFF), f32),
                        pltpu.VMEM((LAT_LEN + PAST_LEN, D_DIFF), f32)],
        compiler_params=_cparams(2),
        name="diff_attention",
    )(z, z, z, cache_k_l, cache_v_l, rope_cos, rope_sin, p['diff_lambda'], p['diff_subln_g'])


def _rope_tables():
    t = jnp.arange(LAT_LEN)
    row = (t // GRID_W).astype(f32)
    col = (t % GRID_W).astype(f32)
    nf = DIFF_HEAD_DIM // 4
    freqs = ROPE_THETA ** (-jnp.arange(nf, dtype=f32) / nf)
    d = jnp.arange(DIFF_HEAD_DIM)
    pos = jnp.where((d < DIFF_HEAD_DIM // 2)[None, :], row[:, None], col[:, None])
    ang = pos * freqs[d % nf][None, :]
    sign = jnp.where((d % (2 * nf)) < nf, -1.0, 1.0)[None, :]
    cos = jnp.tile(jnp.cos(ang), (1, D_DIFF // DIFF_HEAD_DIM))
    sin = jnp.tile(jnp.sin(ang) * sign, (1, D_DIFF // DIFF_HEAD_DIM))
    return cos, sin


MERGE_TM = 512


def _merge_kernel(x_ref, lru_ref, yf_ref, yb_ref, bonus_ref, g_ref, diff_ref, gp0_ref, gp1_ref, gp2_ref,
                  mod_ref, lnp_ref, ones_ref, wlo_ref, wro_ref, wdo_ref, wout_ref, ng_ref, rhi_ref, rlo_ref,
                  xo_ref, h2_ref, logit_ref):
    y = yf_ref[...] + yb_ref[...]
    inv = 1.0 / RWKV_HEAD_DIM
    mu = _split_dot(y, ones_ref[...]) * inv
    dlt = y - mu
    var = _split_dot(dlt * dlt, ones_ref[...]) * inv
    yn = dlt * lax.rsqrt(var + RWKV_LN_EPS)
    rwkv_out = ((yn * lnp_ref[0:1, :] + lnp_ref[1:2, :] + bonus_ref[...]) * g_ref[...]).astype(bf16)

    merged = (jax.nn.sigmoid(gp0_ref[...]) * jnp.dot(lru_ref[...], wlo_ref[...], preferred_element_type=f32)
              + jax.nn.sigmoid(gp1_ref[...]) * jnp.dot(rwkv_out, wro_ref[...], preferred_element_type=f32)
              + jax.nn.sigmoid(gp2_ref[...]) * jnp.dot(diff_ref[...], wdo_ref[...], preferred_element_type=f32))
    mix = jnp.dot(merged.astype(bf16), wout_ref[...], preferred_element_type=f32)

    gate1 = mod_ref[0, :, 2 * D_MODEL:3 * D_MODEL]
    shift2 = mod_ref[0, :, 3 * D_MODEL:4 * D_MODEL]
    scale2 = mod_ref[0, :, 4 * D_MODEL:5 * D_MODEL]
    x = x_ref[...] + gate1 * mix
    xo_ref[...] = x
    yn2 = x * lax.rsqrt(jnp.mean(x * x, axis=-1, keepdims=True) + NORM_EPS) * ng_ref[...]
    h2 = yn2 * (1.0 + scale2) + shift2
    hi = h2.astype(bf16)
    lo = (h2 - hi.astype(f32)).astype(bf16)
    h2_ref[...] = hi
    logit_ref[...] = (jnp.dot(hi, rhi_ref[...], preferred_element_type=f32)
                      + jnp.dot(lo, rhi_ref[...], preferred_element_type=f32)
                      + jnp.dot(hi, rlo_ref[...], preferred_element_type=f32))


def _merge(x, z, lru_out, y_f, y_b, bonus, g, diff_out, mod_l, p):
    row = lambda w: pl.BlockSpec((MERGE_TM, w), lambda i: (i, 0))
    gate = lambda b: pl.BlockSpec((MERGE_TM, D_MODEL), lambda i, b=b: (i, Z_GATES // D_MODEL + b))
    full = lambda shape: pl.BlockSpec(shape, lambda i: (0,) * len(shape))
    return pl.pallas_call(
        _merge_kernel,
        grid=(N_TOK // MERGE_TM,),
        in_specs=[row(D_MODEL), row(D_LRU), row(D_RWKV), row(D_RWKV), row(D_RWKV), row(D_RWKV), row(D_DIFF),
                  gate(0), gate(1), gate(2),
                  pl.BlockSpec((1, 1, 6 * D_MODEL), lambda i: (_mod_group_of_rows(i, MERGE_TM), 0, 0)),
                  full((2, D_RWKV)), full((D_RWKV, D_RWKV)),
                  full((D_LRU, D_MODEL)), full((D_RWKV, D_MODEL)), full((D_DIFF, D_MODEL)),
                  full((D_MODEL, D_MODEL)), full((1, D_MODEL)),
                  full((D_MODEL, ROUTER_PAD)), full((D_MODEL, ROUTER_PAD))],
        out_specs=[row(D_MODEL), row(D_MODEL), row(ROUTER_PAD)],
        out_shape=[jax.ShapeDtypeStruct((N_TOK, D_MODEL), f32),
                   jax.ShapeDtypeStruct((N_TOK, D_MODEL), bf16),
                   jax.ShapeDtypeStruct((N_TOK, ROUTER_PAD), f32)],
        compiler_params=_cparams(1),
        name="merge",
    )(x, lru_out, y_f, y_b, bonus, g, diff_out, z, z, z, mod_l, p['rwkv_lnp'], p['head_ones'],
      p['w_lru_out'], p['w_rwkv_out'], p['w_diff_out'], p['w_out'], p['norm_g2'],
      p['router_hi'], p['router_lo'])


FFN_TF = 512


def _ffn_kernel(xe_ref, wg_ref, wu_ref, wd_ref, o_ref):
    x = xe_ref[0]
    gate = jnp.dot(x, wg_ref[0, 0].astype(bf16), preferred_element_type=f32)
    up = jnp.dot(x, wu_ref[0, 0].astype(bf16), preferred_element_type=f32)
    hid = (gate * jax.nn.sigmoid(gate) * up).astype(bf16)
    part = jnp.dot(hid, wd_ref[0, 0].astype(bf16), preferred_element_type=f32)

    @pl.when(pl.program_id(1) == 0)
    def _():
        o_ref[0] = part

    @pl.when(pl.program_id(1) != 0)
    def _():
        o_ref[0] += part


def _expert_ffn(xe, w_gate, w_up, w_down, layer):
    n_e, cap, _ = xe.shape
    return pl.pallas_call(
        _ffn_kernel,
        grid=(n_e, EXPERT_FF // FFN_TF),
        in_specs=[pl.BlockSpec((1, cap, D_MODEL), lambda e, f: (e, 0, 0)),
                  pl.BlockSpec((1, 1, D_MODEL, FFN_TF), lambda e, f: (layer, e, 0, f)),
                  pl.BlockSpec((1, 1, D_MODEL, FFN_TF), lambda e, f: (layer, e, 0, f)),
                  pl.BlockSpec((1, 1, FFN_TF, D_MODEL), lambda e, f: (layer, e, f, 0))],
        out_specs=pl.BlockSpec((1, cap, D_MODEL), lambda e, f: (e, 0, 0)),
        out_shape=jax.ShapeDtypeStruct((n_e, cap, D_MODEL), f32),
        compiler_params=_cparams(2),
        name="expert_ffn",
    )(xe, w_gate, w_up, w_down)


def _route(logits, n):
    cap = EC_CAPACITY * n // N_EXPERTS
    aff = jax.nn.softmax(logits, axis=-1)
    return lax.top_k(aff.T, cap)


def _moe(h2, logits, w_gate, w_up, w_down, layer):
    lg = logits[:, :N_EXPERTS]
    gate_c, idx_c = _route(lg[:N_CTX_TOK], N_CTX_TOK)
    gate_l, idx_l = _route(lg[N_CTX_TOK:], N_TOK - N_CTX_TOK)
    idx = jnp.concatenate([idx_c, idx_l + N_CTX_TOK], axis=1)
    gate = jnp.concatenate([gate_c, gate_l], axis=1)
    ye = _expert_ffn(h2[idx], w_gate, w_up, w_down, layer) * gate[..., None]
    return jnp.zeros((N_TOK, D_MODEL), f32).at[idx.reshape(-1)].add(ye.reshape(-1, D_MODEL))


RESID_TM = 512


def _resid_kernel(x_ref, y_ref, mod_ref, g_ref, xo_ref, *, final_norm):
    gate2 = mod_ref[0, :, 5 * D_MODEL:6 * D_MODEL]
    x = x_ref[...] + gate2 * y_ref[...]
    if final_norm:
        x = x * lax.rsqrt(jnp.mean(x * x, axis=-1, keepdims=True) + NORM_EPS) * g_ref[...]
    xo_ref[...] = x


def _residual(x, y, mod_l, final_g, final_norm):
    row = pl.BlockSpec((RESID_TM, D_MODEL), lambda i: (i, 0))
    return pl.pallas_call(
        functools.partial(_resid_kernel, final_norm=final_norm),
        grid=(N_TOK // RESID_TM,),
        in_specs=[row, row,
                  pl.BlockSpec((1, 1, 6 * D_MODEL), lambda i: (_mod_group_of_rows(i, RESID_TM), 0, 0)),
                  pl.BlockSpec((1, D_MODEL), lambda i: (0, 0))],
        out_specs=row,
        out_shape=jax.ShapeDtypeStruct((N_TOK, D_MODEL), f32),
        compiler_params=_cparams(1),
        name="residual",
    )(x, y, mod_l, final_g)


def _block_diag(blocks):
    n, a, b = blocks.shape
    eye = jnp.eye(n, dtype=blocks.dtype)
    return (eye[:, None, :, None] * blocks[:, :, None, :]).reshape(n * a, n * b)


def _prep_layer(l, w):
    p = {}
    w_in = w['w_in'][l]
    lru_x, lru_g, zr, q, k, v, gates = (
        w_in[:, 0:D_LRU], w_in[:, D_LRU:2 * D_LRU], w_in[:, 2 * D_LRU:2 * D_LRU + RWKV_COLS],
        w_in[:, 2 * D_LRU + RWKV_COLS:2 * D_LRU + RWKV_COLS + D_DIFF],
        w_in[:, 2 * D_LRU + RWKV_COLS + D_DIFF:2 * D_LRU + RWKV_COLS + 2 * D_DIFF],
        w_in[:, 2 * D_LRU + RWKV_COLS + 2 * D_DIFF:2 * D_LRU + RWKV_COLS + 3 * D_DIFF],
        w_in[:, 2 * D_LRU + RWKV_COLS + 3 * D_DIFF:])
    pad = jnp.zeros((D_MODEL, RWKV_COLS_PAD - RWKV_COLS), f32)
    p['w_in'] = jnp.concatenate([zr, pad, gates, lru_x, lru_g, q, k, v], axis=1).astype(bf16)
    p['norm_g1'] = w['norm_g'][l, 0].reshape(1, D_MODEL)
    p['norm_g2'] = w['norm_g'][l, 1].reshape(1, D_MODEL)

    p['lru_conv_w'] = w['lru_conv_w'][l]
    p['lru_conv_b'] = w['lru_conv_b'][l].reshape(1, D_LRU)
    p['lru_wg'] = jnp.concatenate(
        [_block_diag(w[name][l, d]) for d in range(2) for name in ('lru_wa', 'lru_wx')], axis=1).astype(bf16)
    p['lru_bg'] = jnp.concatenate(
        [w[name][l, d] for d in range(2) for name in ('lru_ba', 'lru_bx')]).reshape(1, 4 * D_LRU)
    p['lru_lambda'] = w['lru_lambda'][l]

    p['rwkv_conv_w'] = jnp.pad(w['rwkv_conv_w'][l], ((0, 0), (0, RWKV_COLS_PAD - RWKV_COLS)))

    def lowrank_pair(m):
        rank = m.shape[1]
        out = jnp.zeros((N_RWKV_CT, 2 * rank, 2 * RWKV_CT), f32)
        for d in range(2):
            tiles = m[d].reshape(rank, N_RWKV_CT, RWKV_CT).transpose(1, 0, 2)
            out = out.at[:, d * rank:(d + 1) * rank, d * RWKV_CT:(d + 1) * RWKV_CT].set(tiles)
        return out.astype(bf16)

    p['rwkv_w2'] = lowrank_pair(w['rwkv_w2'][l])
    p['rwkv_a2'] = lowrank_pair(w['rwkv_a2'][l])
    p['rwkv_g2'] = w['rwkv_g2'][l].astype(bf16)
    p['rwkv_pc'] = jnp.concatenate(
        [w['rwkv_w0'][l], w['rwkv_a0'][l], w['rwkv_k_k'][l][None], w['rwkv_k_a'][l][None],
         w['rwkv_r_k'][l].reshape(1, D_RWKV), jnp.zeros((1, D_RWKV), f32)], axis=0)
    p['rwkv_lnp'] = jnp.stack([w['rwkv_ln_g'][l], w['rwkv_ln_b'][l]], axis=0)
    p['head_ones'] = _block_diag(jnp.ones((N_RWKV_HEADS, RWKV_HEAD_DIM, RWKV_HEAD_DIM), bf16))

    p['diff_lambda'] = w['diff_lambda'][l]
    p['diff_subln_g'] = w['diff_subln_g'][l].reshape(1, HEAD_W)
    p['w_lru_out'] = w['w_lru_out'][l].astype(bf16)
    p['w_rwkv_out'] = w['w_rwkv_out'][l].astype(bf16)
    p['w_diff_out'] = w['w_diff_out'][l].astype(bf16)
    p['w_out'] = w['w_out'][l].astype(bf16)
    rw = jnp.pad(w['router_w'][l], ((0, 0), (0, ROUTER_PAD - N_EXPERTS)))
    p['router_hi'] = rw.astype(bf16)
    p['router_lo'] = (rw - p['router_hi'].astype(f32)).astype(bf16)
    return p


def kernel(x_prompt, x_sample, cache_k, cache_v, state_lru, state_rwkv, c, c_ctx, norm_g, final_norm_g, w_mod, b_mod, w_in, lru_conv_w, lru_conv_b, lru_wa, lru_ba, lru_wx, lru_bx, lru_lambda, rwkv_conv_w, rwkv_w0, rwkv_w2, rwkv_a0, rwkv_a2, rwkv_k_k, rwkv_k_a, rwkv_r_k, rwkv_g2, rwkv_ln_g, rwkv_ln_b, diff_lambda, diff_subln_g, w_lru_out, w_rwkv_out, w_diff_out, w_out, router_w, exp_w_gate, exp_w_up, exp_w_down):
    w = dict(norm_g=norm_g, w_in=w_in, lru_conv_w=lru_conv_w, lru_conv_b=lru_conv_b, lru_wa=lru_wa,
             lru_ba=lru_ba, lru_wx=lru_wx, lru_bx=lru_bx, lru_lambda=lru_lambda, rwkv_conv_w=rwkv_conv_w,
             rwkv_w0=rwkv_w0, rwkv_w2=rwkv_w2, rwkv_a0=rwkv_a0, rwkv_a2=rwkv_a2, rwkv_k_k=rwkv_k_k,
             rwkv_k_a=rwkv_k_a, rwkv_r_k=rwkv_r_k, rwkv_g2=rwkv_g2, rwkv_ln_g=rwkv_ln_g,
             rwkv_ln_b=rwkv_ln_b, diff_lambda=diff_lambda, diff_subln_g=diff_subln_g, w_lru_out=w_lru_out,
             w_rwkv_out=w_rwkv_out, w_diff_out=w_diff_out, w_out=w_out, router_w=router_w)

    x = jnp.concatenate([x_prompt.reshape(N_CTX_TOK, D_MODEL), x_sample.reshape(-1, D_MODEL)], axis=0)
    cvec = jnp.concatenate([c_ctx[None], c, jnp.zeros((SUBLANES - N_MOD_GROUPS, D_MODEL), f32)], axis=0)
    mod = _modulation(cvec, w_mod, b_mod)
    rope_cos, rope_sin = _rope_tables()
    final_g = final_norm_g.reshape(1, D_MODEL)
    ctx_rows = slice(0, N_CTX_TOK)
    lat_rows = slice(N_CTX_TOK, N_TOK)

    new_k, new_v, new_lru, new_rwkv = [], [], [], []
    for l in range(DEPTH):
        p = _prep_layer(l, w)
        lam_init = 0.8 - 0.6 * math.exp(-0.3 * l)
        mod_l = mod[l].reshape(SUBLANES, 1, 6 * D_MODEL)

        z = _in_projection(x, mod_l, p['norm_g1'], p['w_in'])
        new_k.append(z[ctx_rows, Z_K:Z_K + D_DIFF].reshape(N_CTX_SEQ, CTX_LEN, N_DIFF_HEADS, 2, DIFF_HEAD_DIM))
        new_v.append(z[ctx_rows, Z_V:Z_V + D_DIFF].reshape(N_CTX_SEQ, CTX_LEN, N_DIFF_HEADS, HEAD_W))

        h0_blocks = jnp.concatenate(
            [jnp.zeros((N_CTX_BLOCKS, 2, D_LRU), f32), state_lru[:, l].astype(f32)], axis=0)
        lru_out, lru_fin = _lru_branch(z, p, h0_blocks)
        n_per = SEQ_BLOCK // CTX_LEN
        fin = lru_fin[:N_CTX_BLOCKS].reshape(N_CTX_BLOCKS, 2, n_per, D_LRU)
        new_lru.append(jnp.transpose(fin, (0, 2, 1, 3)).reshape(N_CTX_SEQ, 2, D_LRU))

        scan_in, bonus, out_gate = _rwkv_pre(z, p)
        yf_c, yb_c, s_c = _rwkv_scan_path(scan_in, 0, N_CTX_SEQ, CTX_LEN, None)
        yf_l, yb_l, _ = _rwkv_scan_path(scan_in, N_CTX_TOK // LAT_LEN, N_LAT_SEQ, LAT_LEN, state_rwkv[:, l])
        new_rwkv.append(s_c)
        y_f = jnp.concatenate([yf_c, yf_l], axis=0)
        y_b = jnp.concatenate([yb_c, yb_l], axis=0)

        diff_out = _attention(z, cache_k[:, l].reshape(N_LAT_SEQ, PAST_LEN, D_DIFF),
                              cache_v[:, l].reshape(N_LAT_SEQ, PAST_LEN, D_DIFF),
                              rope_cos, rope_sin, p, lam_init)

        x, h2, logits = _merge(x, z, lru_out, y_f, y_b, bonus, out_gate, diff_out, mod_l, p)
        moe = _moe(h2, logits, exp_w_gate, exp_w_up, exp_w_down, l)
        x = _residual(x, moe, mod_l, final_g, final_norm=(l == DEPTH - 1))

    y_prompt = x[ctx_rows].reshape(x_prompt.shape)
    y_sample = x[lat_rows].reshape(x_sample.shape)
    return (y_prompt, y_sample, jnp.stack(new_k, axis=1), jnp.stack(new_v, axis=1),
            jnp.stack(new_lru, axis=1), jnp.stack(new_rwkv, axis=1))
```

```python
import functools
import math

import jax
import jax.numpy as jnp
from jax import lax
from jax.experimental import pallas as pl
from jax.experimental.pallas import tpu as pltpu

f32 = jnp.float32
bf16 = jnp.bfloat16

D_MODEL = 1024
DEPTH = 2
GRID_W = 64
NORM_EPS = 1e-6

N_CTX_SEQ = 16
CTX_LEN = 256
N_LAT_SEQ = 2
LAT_LEN = 1024
PAST_LEN = 256
N_CTX_TOK = N_CTX_SEQ * CTX_LEN
N_TOK = N_CTX_TOK + N_LAT_SEQ * LAT_LEN

D_LRU = 512
LRU_BLOCKS = 8
LRU_BLOCK_W = D_LRU // LRU_BLOCKS
LRU_C = 8.0

D_RWKV = 512
RWKV_HEAD_DIM = 64
N_RWKV_HEADS = D_RWKV // RWKV_HEAD_DIM
W_RANK = 64
A_RANK = 64
G_RANK = 128
RWKV_LN_EPS = 64e-5
RWKV_COLS = 3 * D_RWKV + 2 * W_RANK + 2 * A_RANK + G_RANK
RWKV_COLS_PAD = 2048

N_DIFF_HEADS = 4
DIFF_HEAD_DIM = 64
D_DIFF = N_DIFF_HEADS * 2 * DIFF_HEAD_DIM
ROPE_THETA = 10000.0

N_BRANCHES = 3
N_EXPERTS = 16
EXPERT_FF = 1024
EC_CAPACITY = 2
ROUTER_PAD = 128

LANES = 128
SUBLANES = 8

SEQ_BLOCK = 1024
N_SEQ_BLOCKS = N_TOK // SEQ_BLOCK
N_CTX_BLOCKS = N_CTX_TOK // SEQ_BLOCK
N_MOD_GROUPS = 1 + N_LAT_SEQ

Z_RWKV = 0
Z_GATES = RWKV_COLS_PAD
Z_LRU_X = Z_GATES + N_BRANCHES * D_MODEL
Z_LRU_G = Z_LRU_X + D_LRU
Z_Q = Z_LRU_G + D_LRU
Z_K = Z_Q + D_DIFF
Z_V = Z_K + D_DIFF
Z_COLS = Z_V + D_DIFF

VMEM_LIMIT = 56 * 1024 * 1024


def _cparams(n_axes):
    return pltpu.CompilerParams(dimension_semantics=("arbitrary",) * n_axes,
                                vmem_limit_bytes=VMEM_LIMIT)


def _seg_len(block_idx):
    return jnp.where(block_idx < N_CTX_BLOCKS, CTX_LEN, LAT_LEN)


def _mod_group_of_rows(row_block, rows_per_block):
    first_lat = N_CTX_TOK // rows_per_block
    per_seq = LAT_LEN // rows_per_block
    return jnp.where(row_block < first_lat, 0, (row_block - first_lat) // per_seq + 1)


def _shift_rows(v, d, tm, seg):
    n = v.shape[0]
    r = pltpu.roll(v, (-d) % n, axis=0)
    ok = (tm + d >= 0) & (tm + d < seg)
    return jnp.where(ok, r, 0.0)


def _split_dot(x, w_bf16):
    hi = x.astype(bf16)
    lo = (x - hi.astype(f32)).astype(bf16)
    return (jnp.dot(hi, w_bf16, preferred_element_type=f32)
            + jnp.dot(lo, w_bf16, preferred_element_type=f32))


MOD_TN = 1536


def _mod_kernel(c_ref, w_ref, b_ref, o_ref):
    c = c_ref[...]
    s = (c * jax.nn.sigmoid(c)).astype(bf16)
    o_ref[0] = jnp.dot(s, w_ref[0].astype(bf16), preferred_element_type=f32) + b_ref[0]


def _modulation(cvec, w_mod, b_mod):
    n_out = w_mod.shape[-1]
    return pl.pallas_call(
        _mod_kernel,
        grid=(DEPTH, n_out // MOD_TN),
        in_specs=[pl.BlockSpec((SUBLANES, D_MODEL), lambda l, j: (0, 0)),
                  pl.BlockSpec((1, D_MODEL, MOD_TN), lambda l, j: (l, 0, j)),
                  pl.BlockSpec((1, 1, MOD_TN), lambda l, j: (l, 0, j))],
        out_specs=pl.BlockSpec((1, SUBLANES, MOD_TN), lambda l, j: (l, 0, j)),
        out_shape=jax.ShapeDtypeStruct((DEPTH, SUBLANES, n_out), f32),
        compiler_params=_cparams(2),
        name="modulation",
    )(cvec, w_mod, b_mod.reshape(DEPTH, 1, n_out))


INPROJ_TM = 1024
INPROJ_TN = 768


def _inproj_kernel(x_ref, mod_ref, g_ref, w_ref, o_ref, h_ref):
    @pl.when(pl.program_id(1) == 0)
    def _():
        x = x_ref[...]
        y = x * lax.rsqrt(jnp.mean(x * x, axis=-1, keepdims=True) + NORM_EPS) * g_ref[...]
        shift = mod_ref[0, :, 0:D_MODEL]
        scale = mod_ref[0, :, D_MODEL:2 * D_MODEL]
        h_ref[...] = (y * (1.0 + scale) + shift).astype(bf16)

    o_ref[...] = jnp.dot(h_ref[...], w_ref[...], preferred_element_type=f32)


def _in_projection(x, mod_l, norm_g_row, w_in_p):
    return pl.pallas_call(
        _inproj_kernel,
        grid=(N_TOK // INPROJ_TM, Z_COLS // INPROJ_TN),
        in_specs=[pl.BlockSpec((INPROJ_TM, D_MODEL), lambda i, j: (i, 0)),
                  pl.BlockSpec((1, 1, 2 * D_MODEL),
                               lambda i, j: (_mod_group_of_rows(i, INPROJ_TM), 0, 0)),
                  pl.BlockSpec((1, D_MODEL), lambda i, j: (0, 0)),
                  pl.BlockSpec((D_MODEL, INPROJ_TN), lambda i, j: (0, j))],
        out_specs=pl.BlockSpec((INPROJ_TM, INPROJ_TN), lambda i, j: (i, j)),
        out_shape=jax.ShapeDtypeStruct((N_TOK, Z_COLS), f32),
        scratch_shapes=[pltpu.VMEM((INPROJ_TM, D_MODEL), bf16)],
        compiler_params=_cparams(2),
        name="in_projection",
    )(x, mod_l, norm_g_row, w_in_p)


LRU_TILE_UNROLL = 4


def _lru_kernel(x_ref, gate_ref, cw_ref, cb_ref, wg_ref, bg_ref, lam_ref, h0_ref, out_ref, hfin_ref,
                a_scr, b_scr, yf_scr, yb_scr):
    blk = pl.program_id(0)
    seg = _seg_len(blk)
    n = x_ref.shape[0]
    t = lax.broadcasted_iota(jnp.int32, (n, D_LRU), 0)
    tm = t & (seg - 1)

    x = x_ref[...]
    xc = (cb_ref[...]
          + cw_ref[0:1, :] * _shift_rows(x, -2, tm, seg)
          + cw_ref[1:2, :] * _shift_rows(x, -1, tm, seg)
          + cw_ref[2:3, :] * x
          + cw_ref[3:4, :] * _shift_rows(x, 1, tm, seg))
    pre = jnp.dot(xc.astype(bf16), wg_ref[...], preferred_element_type=f32) + bg_ref[...]

    n_tiles = n // SUBLANES
    tiles_per_seg = seg // SUBLANES
    row = lax.broadcasted_iota(jnp.int32, (SUBLANES, D_LRU), 0)
    for d, y_scr in enumerate((yf_scr, yb_scr)):
        r = jax.nn.sigmoid(pre[:, (2 * d) * D_LRU:(2 * d + 1) * D_LRU])
        i = jax.nn.sigmoid(pre[:, (2 * d + 1) * D_LRU:(2 * d + 2) * D_LRU])
        log_a = -LRU_C * r * jax.nn.softplus(-lam_ref[d:d + 1, :])
        a = jnp.exp(log_a)
        a_scr[...] = a
        b_scr[...] = jnp.sqrt(1.0 - a * a) * (i * xc)
        h0 = h0_ref[0, d:d + 1, :]

        def tile_step(k, h_prev, d=d, y_scr=y_scr, h0=h0):
            idx = k if d == 0 else n_tiles - 1 - k
            rows = pl.ds(pl.multiple_of(idx * SUBLANES, SUBLANES), SUBLANES)
            a_t = a_scr[rows, :]
            b_t = b_scr[rows, :]
            s = 1
            while s < SUBLANES:
                ok = (row >= s) if d == 0 else (row < SUBLANES - s)
                shift = s if d == 0 else SUBLANES - s
                a_sh = jnp.where(ok, pltpu.roll(a_t, shift, axis=0), 1.0)
                b_sh = jnp.where(ok, pltpu.roll(b_t, shift, axis=0), 0.0)
                b_t = a_t * b_sh + b_t
                a_t = a_t * a_sh
                s *= 2
            first_of_seg = (idx & (tiles_per_seg - 1)) == (0 if d == 0 else tiles_per_seg - 1)
            h_t = a_t * jnp.where(first_of_seg, h0, h_prev) + b_t
            y_scr[rows, :] = h_t
            return h_t[SUBLANES - 1:SUBLANES, :] if d == 0 else h_t[0:1, :]

        lax.fori_loop(0, n_tiles, tile_step, jnp.zeros((1, D_LRU), f32), unroll=LRU_TILE_UNROLL)

    out_ref[...] = ((yf_scr[...] + yb_scr[...]) * jax.nn.gelu(gate_ref[...])).astype(out_ref.dtype)
    fins = ([yf_scr[(j + 1) * CTX_LEN - 1:(j + 1) * CTX_LEN, :] for j in range(SEQ_BLOCK // CTX_LEN)]
            + [yb_scr[j * CTX_LEN:j * CTX_LEN + 1, :] for j in range(SEQ_BLOCK // CTX_LEN)])
    hfin_ref[0] = jnp.concatenate(fins, axis=0)


def _lru_branch(z, p, h0_blocks):
    full = lambda shape: pl.BlockSpec(shape, lambda i: (0,) * len(shape))
    return pl.pallas_call(
        _lru_kernel,
        grid=(N_SEQ_BLOCKS,),
        in_specs=[pl.BlockSpec((SEQ_BLOCK, D_LRU), lambda i: (i, Z_LRU_X // D_LRU)),
                  pl.BlockSpec((SEQ_BLOCK, D_LRU), lambda i: (i, Z_LRU_G // D_LRU)),
                  full((4, D_LRU)), full((1, D_LRU)), full((D_LRU, 4 * D_LRU)), full((1, 4 * D_LRU)),
                  full((2, D_LRU)),
                  pl.BlockSpec((1, 2, D_LRU), lambda i: (i, 0, 0))],
        out_specs=[pl.BlockSpec((SEQ_BLOCK, D_LRU), lambda i: (i, 0)),
                   pl.BlockSpec((1, SUBLANES, D_LRU), lambda i: (i, 0, 0))],
        out_shape=[jax.ShapeDtypeStruct((N_TOK, D_LRU), bf16),
                   jax.ShapeDtypeStruct((N_SEQ_BLOCKS, SUBLANES, D_LRU), f32)],
        scratch_shapes=[pltpu.VMEM((SEQ_BLOCK, D_LRU), f32)] * 4,
        compiler_params=_cparams(1),
        name="lru_branch",
    )(z, z, p['lru_conv_w'], p['lru_conv_b'], p['lru_wg'], p['lru_bg'], p['lru_lambda'], h0_blocks)


RWKV_CT = 256
N_RWKV_CT = D_RWKV // RWKV_CT
RWKV_LOWRANK_OFF = 3 * D_RWKV
RWKV_LOWRANK_W = RWKV_COLS_PAD - RWKV_LOWRANK_OFF
SCAN_ARRAYS = ('w_f', 'w_b', 'ka_f', 'ka_b', 'kd_f', 'kd_b', 'nkk', 'r', 'v')
N_SCAN_ARRAYS = len(SCAN_ARRAYS)
SCAN_IDX = {name: i for i, name in enumerate(SCAN_ARRAYS)}


def _rwkv_pre_kernel(r_ref, k_ref, v_ref, lr_ref, cwr_ref, cwk_ref, cwv_ref, cwl_ref,
                     w2_ref, a2_ref, g2_ref, pc_ref, ones_ref, sc_ref, bonus_ref, g_ref):
    blk = pl.program_id(0)
    seg = _seg_len(blk)
    n = r_ref.shape[0]

    def conv(x_ref, cw_ref):
        width = x_ref.shape[1]
        t = lax.broadcasted_iota(jnp.int32, (n, width), 0)
        tm = t & (seg - 1)
        x = x_ref[...]
        return (cw_ref[0:1, :] * _shift_rows(x, -1, tm, seg) + cw_ref[1:2, :] * x
                + cw_ref[2:3, :] * _shift_rows(x, 1, tm, seg))

    r = conv(r_ref, cwr_ref)
    k = conv(k_ref, cwk_ref)
    v = conv(v_ref, cwv_ref)
    lr = conv(lr_ref, cwl_ref)
    wl = lr[:, 0:2 * W_RANK]
    al = lr[:, 2 * W_RANK:2 * W_RANK + 2 * A_RANK]
    gl = lr[:, 2 * W_RANK + 2 * A_RANK:2 * W_RANK + 2 * A_RANK + G_RANK]

    w0 = [pc_ref[0:1, :], pc_ref[1:2, :]]
    a0 = [pc_ref[2:3, :], pc_ref[3:4, :]]
    k_k = pc_ref[4:5, :]
    k_a = pc_ref[5:6, :]
    r_k = pc_ref[6:7, :]

    w_pre = jnp.dot(jnp.tanh(wl).astype(bf16), w2_ref[0], preferred_element_type=f32)
    a_pre = jnp.dot(al.astype(bf16), a2_ref[0], preferred_element_type=f32)

    kk = k * k_k
    ss = _split_dot(kk * kk, ones_ref[...])
    kk = kk / jnp.maximum(jnp.sqrt(ss), 1e-12)

    kd_sum = None
    for d, sfx in enumerate(('_f', '_b')):
        cols = slice(d * RWKV_CT, (d + 1) * RWKV_CT)
        w_log = -jax.nn.softplus(-(w0[d] + w_pre[:, cols])) - 0.5
        sc_ref[SCAN_IDX['w' + sfx]] = jnp.exp(-jnp.exp(w_log))
        a = jax.nn.sigmoid(a0[d] + a_pre[:, cols])
        sc_ref[SCAN_IDX['ka' + sfx]] = kk * a
        kd = k * (1.0 + (a - 1.0) * k_a)
        sc_ref[SCAN_IDX['kd' + sfx]] = kd
        kd_sum = kd if kd_sum is None else kd_sum + kd

    sc_ref[SCAN_IDX['r']] = r
    sc_ref[SCAN_IDX['v']] = v
    sc_ref[SCAN_IDX['nkk']] = -kk
    bonus_ref[...] = _split_dot(r * kd_sum * r_k, ones_ref[...]) * v
    g_ref[...] = jnp.dot(jax.nn.sigmoid(gl).astype(bf16), g2_ref[...], preferred_element_type=f32)


def _rwkv_pre(z, p):
    nct = D_RWKV // RWKV_CT
    zcol = lambda base: pl.BlockSpec((SEQ_BLOCK, RWKV_CT), lambda i, c, base=base: (i, base + c))
    cwcol = lambda base: pl.BlockSpec((3, RWKV_CT), lambda i, c, base=base: (0, base + c))
    out_spec = pl.BlockSpec((SEQ_BLOCK, RWKV_CT), lambda i, c: (i, c))
    return pl.pallas_call(
        _rwkv_pre_kernel,
        grid=(N_SEQ_BLOCKS, nct),
        in_specs=[zcol(0), zcol(nct), zcol(2 * nct),
                  pl.BlockSpec((SEQ_BLOCK, RWKV_LOWRANK_W),
                               lambda i, c: (i, RWKV_LOWRANK_OFF // RWKV_LOWRANK_W)),
                  cwcol(0), cwcol(nct), cwcol(2 * nct),
                  pl.BlockSpec((3, RWKV_LOWRANK_W), lambda i, c: (0, RWKV_LOWRANK_OFF // RWKV_LOWRANK_W)),
                  pl.BlockSpec((1, 2 * W_RANK, 2 * RWKV_CT), lambda i, c: (c, 0, 0)),
                  pl.BlockSpec((1, 2 * A_RANK, 2 * RWKV_CT), lambda i, c: (c, 0, 0)),
                  pl.BlockSpec((G_RANK, RWKV_CT), lambda i, c: (0, c)),
                  pl.BlockSpec((SUBLANES, RWKV_CT), lambda i, c: (0, c)),
                  pl.BlockSpec((RWKV_CT, RWKV_CT), lambda i, c: (0, 0))],
        out_specs=[pl.BlockSpec((N_SCAN_ARRAYS, SEQ_BLOCK, RWKV_CT), lambda i, c: (0, i, c)),
                   out_spec, out_spec],
        out_shape=[jax.ShapeDtypeStruct((N_SCAN_ARRAYS, N_TOK, D_RWKV), f32),
                   jax.ShapeDtypeStruct((N_TOK, D_RWKV), f32),
                   jax.ShapeDtypeStruct((N_TOK, D_RWKV), f32)],
        compiler_params=_cparams(2),
        name="rwkv_pre",
    )(z, z, z, z, p['rwkv_conv_w'], p['rwkv_conv_w'], p['rwkv_conv_w'], p['rwkv_conv_w'],
      p['rwkv_w2'], p['rwkv_a2'], p['rwkv_g2'], p['rwkv_pc'], p['head_ones'][:RWKV_CT, :RWKV_CT])


PACK_T = 128
TIME_BLOCK = SUBLANES
PACK_UNROLL = 4


def _gather_states(scr, row, n_seq):
    return scr[:, pl.ds(row, N_RWKV_HEADS, stride=RWKV_HEAD_DIM), :].reshape(n_seq * N_RWKV_HEADS, PACK_T)


def _pack_kernel(x_ref, o_ref, scr, *, n_seq, reps, rows, fold_rows):
    for b in range(n_seq):
        scr[b] = x_ref[b].T

    def body(row, c):
        if fold_rows:
            m = jnp.concatenate([_gather_states(scr, row * reps + rep, n_seq) for rep in range(reps)], axis=0)
        else:
            m = _gather_states(scr, row, n_seq)
            if reps > 1:
                m = jnp.concatenate([m] * reps, axis=0)
        mt = m.T
        line0 = pl.multiple_of(row * TIME_BLOCK, TIME_BLOCK)
        for u in range(PACK_T // TIME_BLOCK):
            o_ref[u, pl.ds(line0, TIME_BLOCK), :] = mt[u * TIME_BLOCK:(u + 1) * TIME_BLOCK]
        return c

    lax.fori_loop(0, rows, body, 0, unroll=PACK_UNROLL)


def _pack(stacked, first, count, n_seq, seq_len, seq0, reps, fold_rows):
    rows = RWKV_HEAD_DIM // reps if fold_rows else RWKV_HEAD_DIM
    x = stacked.reshape(N_SCAN_ARRAYS, N_TOK // seq_len, seq_len, D_RWKV)
    tb = PACK_T // TIME_BLOCK
    return pl.pallas_call(
        functools.partial(_pack_kernel, n_seq=n_seq, reps=reps, rows=rows, fold_rows=fold_rows),
        grid=(count, seq_len // PACK_T),
        in_specs=[pl.BlockSpec((None, n_seq, PACK_T, D_RWKV),
                               lambda a, j: (first + a, seq0 // n_seq, j, 0))],
        out_specs=pl.BlockSpec((None, tb, rows * TIME_BLOCK, LANES), lambda a, j: (a, j, 0, 0)),
        out_shape=jax.ShapeDtypeStruct((count, seq_len // TIME_BLOCK, rows * TIME_BLOCK, LANES), f32),
        scratch_shapes=[pltpu.VMEM((n_seq, D_RWKV, PACK_T), f32)],
        compiler_params=_cparams(2),
        name="rwkv_pack",
    )(x)


def _unpack_kernel(yf_ref, yb_ref, of_ref, ob_ref, scr, *, n_seq, reps, rows):
    group = n_seq * N_RWKV_HEADS
    for y_ref, o_ref in ((yf_ref, of_ref), (yb_ref, ob_ref)):
        def body(row, c, y_ref=y_ref):
            line0 = pl.multiple_of(row * TIME_BLOCK, TIME_BLOCK)
            mt = jnp.concatenate([y_ref[u, pl.ds(line0, TIME_BLOCK), :]
                                  for u in range(PACK_T // TIME_BLOCK)], axis=0)
            m = mt.T
            for rep in range(reps):
                part = m[rep * group:(rep + 1) * group].reshape(n_seq, N_RWKV_HEADS, PACK_T)
                scr[:, pl.ds(row * reps + rep, N_RWKV_HEADS, stride=RWKV_HEAD_DIM), :] = part
            return c

        lax.fori_loop(0, rows, body, 0, unroll=PACK_UNROLL)
        for b in range(n_seq):
            o_ref[b] = scr[b].T


def _unpack(y_f, y_b, n_seq, seq_len, reps):
    rows = y_f.shape[1] // TIME_BLOCK
    yspec = pl.BlockSpec((PACK_T // TIME_BLOCK, rows * TIME_BLOCK, LANES), lambda j: (j, 0, 0))
    ospec = pl.BlockSpec((n_seq, PACK_T, D_RWKV), lambda j: (0, j, 0))
    shape = jax.ShapeDtypeStruct((n_seq, seq_len, D_RWKV), f32)
    o_f, o_b = pl.pallas_call(
        functools.partial(_unpack_kernel, n_seq=n_seq, reps=reps, rows=rows),
        grid=(seq_len // PACK_T,),
        in_specs=[yspec, yspec],
        out_specs=[ospec, ospec],
        out_shape=[shape, shape],
        scratch_shapes=[pltpu.VMEM((n_seq, D_RWKV, PACK_T), f32)],
        compiler_params=_cparams(1),
        name="rwkv_unpack",
    )(y_f, y_b)
    return o_f.reshape(n_seq * seq_len, D_RWKV), o_b.reshape(n_seq * seq_len, D_RWKV)


def _rwkv_scan_kernel(wf_ref, wb_ref, kaf_ref, kab_ref, kdf_ref, kdb_ref, nkkf_ref, nkkb_ref,
                      rf_ref, rb_ref, vvf_ref, vvb_ref, s0_ref, yf_ref, yb_ref, s_ref, *,
                      time_blocks, v_rows, unroll):
    @pl.when(pl.program_id(0) == 0)
    def _():
        s_ref[...] = s0_ref[...]

    dirs = ((wf_ref, kaf_ref, kdf_ref, nkkf_ref, rf_ref, vvf_ref, yf_ref),
            (wb_ref, kab_ref, kdb_ref, nkkb_ref, rb_ref, vvb_ref, yb_ref))
    n_k = s_ref.shape[1]

    for u in range(time_blocks):
        def step(t, carry, u=u):
            for g, (w_ref, ka_ref, kd_ref, nkk_ref, r_ref, vv_ref, y_ref) in enumerate(dirs):
                ub = u if g == 0 else time_blocks - 1 - u
                tt = t if g == 0 else TIME_BLOCK - 1 - t
                lanes = slice(g * LANES, (g + 1) * LANES)
                at_t = pl.ds(tt, n_k, stride=TIME_BLOCK)
                w = w_ref[ub, at_t, :]
                ka = ka_ref[ub, at_t, :]
                kd = kd_ref[ub, at_t, :]
                nkk = nkk_ref[ub, at_t, :]
                r = r_ref[ub, at_t, :]

                def rows(vb, c):
                    v0 = pl.multiple_of(vb * SUBLANES, SUBLANES)
                    rows_at_t = pl.ds(v0 * TIME_BLOCK + tt, SUBLANES, stride=TIME_BLOCK)
                    vv = vv_ref[ub, rows_at_t, :]
                    ys = []
                    for j in range(SUBLANES):
                        s = s_ref[v0 + j, :, lanes]
                        sa = jnp.sum(s * nkk, axis=0, keepdims=True)
                        s_new = s * w + ka * sa + kd * vv[j:j + 1]
                        s_ref[v0 + j, :, lanes] = s_new
                        ys.append(jnp.sum(s_new * r, axis=0, keepdims=True))
                    y_ref[ub, rows_at_t, :] = jnp.concatenate(ys, axis=0)
                    return c

                lax.fori_loop(0, v_rows // SUBLANES, rows, 0, unroll=unroll)
            return carry

        lax.fori_loop(0, TIME_BLOCK, step, 0)


def _rwkv_scan_packed(kvecs, vv, s0, *, steps):
    K = kvecs.shape[2] // TIME_BLOCK
    L = kvecs.shape[1] * TIME_BLOCK
    VR = vv.shape[1] // TIME_BLOCK
    nblk = L // steps
    tbs = steps // TIME_BLOCK

    def kspec(name, mirrored):
        a = SCAN_IDX[name]
        if mirrored:
            return pl.BlockSpec((None, tbs, K * TIME_BLOCK, LANES), lambda i: (a, nblk - 1 - i, 0, 0))
        return pl.BlockSpec((None, tbs, K * TIME_BLOCK, LANES), lambda i: (a, i, 0, 0))

    vf = pl.BlockSpec((tbs, VR * TIME_BLOCK, LANES), lambda i: (i, 0, 0))
    vb = pl.BlockSpec((tbs, VR * TIME_BLOCK, LANES), lambda i: (nblk - 1 - i, 0, 0))
    sspec = pl.BlockSpec((VR, K, 2 * LANES), lambda i: (0, 0, 0))
    kern = functools.partial(_rwkv_scan_kernel, time_blocks=tbs, v_rows=VR,
                             unroll=min(2, VR // SUBLANES))
    return pl.pallas_call(
        kern,
        grid=(nblk,),
        in_specs=[kspec('w_f', False), kspec('w_b', True), kspec('ka_f', False), kspec('ka_b', True),
                  kspec('kd_f', False), kspec('kd_b', True), kspec('nkk', False), kspec('nkk', True),
                  kspec('r', False), kspec('r', True), vf, vb, sspec],
        out_specs=[vf, vb, sspec],
        out_shape=[jax.ShapeDtypeStruct((L // TIME_BLOCK, VR * TIME_BLOCK, LANES), f32),
                   jax.ShapeDtypeStruct((L // TIME_BLOCK, VR * TIME_BLOCK, LANES), f32),
                   jax.ShapeDtypeStruct((VR, K, 2 * LANES), f32)],
        compiler_params=_cparams(1),
        name="rwkv7_scan",
    )(*([kvecs] * 10), vv, vv, s0)


def _rwkv_scan_path(stacked, seq0, B, L, S0):
    H, K = N_RWKV_HEADS, RWKV_HEAD_DIM
    nd = 2
    reps = LANES // (B * H)
    VR = K // reps
    kvecs = _pack(stacked, 0, N_SCAN_ARRAYS - 1, B, L, seq0, reps, fold_rows=False)
    vv = _pack(stacked, SCAN_IDX['v'], 1, B, L, seq0, reps, fold_rows=True)[0]
    if S0 is None:
        s0 = jnp.zeros((VR, K, nd * LANES), f32)
    else:
        s0 = S0.astype(f32).reshape(B, nd, H, VR, reps, K)
        s0 = jnp.transpose(s0, (3, 5, 1, 4, 0, 2)).reshape(VR, K, nd * LANES)

    y_f, y_b, s_fin = _rwkv_scan_packed(kvecs, vv, s0, steps=8 if reps == 1 else 16)
    y_f, y_b = _unpack(y_f, y_b, B, L, reps)
    s_fin = s_fin.reshape(VR, K, nd, reps, B, H)
    s_fin = jnp.transpose(s_fin, (4, 2, 5, 0, 3, 1)).reshape(B, nd, H, K, K)
    return y_f, y_b, s_fin


ATT_TQ = 256
HEAD_W = 2 * DIFF_HEAD_DIM


def _rope(x, cos, sin_signed, lo_mask):
    n = x.shape[1]
    partner = jnp.where(lo_mask, pltpu.roll(x, n - DIFF_HEAD_DIM // 4, axis=1),
                        pltpu.roll(x, DIFF_HEAD_DIM // 4, axis=1))
    return x * cos + partner * sin_signed


def _attn_heads(q, k, v, lam, subln_g, out_scale):
    lane = lax.broadcasted_iota(jnp.int32, (1, HEAD_W), 1)
    first = lane < DIFF_HEAD_DIM
    outs = []
    for h in range(N_DIFF_HEADS):
        cols = slice(h * HEAD_W, (h + 1) * HEAD_W)
        qh = (q[:, cols] * (DIFF_HEAD_DIM ** -0.5)).astype(bf16)
        kh = k[:, cols]
        vh = v[:, cols].astype(bf16)
        probs = []
        for m in range(2):
            km = jnp.where(first if m == 0 else ~first, kh, 0.0).astype(bf16)
            s = lax.dot_general(qh, km, (((1,), (1,)), ((), ())), preferred_element_type=f32)
            s = s - jnp.max(s, axis=-1, keepdims=True)
            e = jnp.exp(s)
            probs.append(e / jnp.sum(e, axis=-1, keepdims=True))
        attn = (probs[0] - lam * probs[1]).astype(bf16)
        o = jnp.dot(attn, vh, preferred_element_type=f32)
        o = o * lax.rsqrt(jnp.mean(o * o, axis=-1, keepdims=True) + NORM_EPS) * subln_g * out_scale
        outs.append(o)
    return jnp.concatenate(outs, axis=1)


def _attn_kernel(q_ref, k_ref, v_ref, ck_ref, cv_ref, cos_ref, sin_ref, lv_ref, g_ref, o_ref,
                 kall_ref, vall_ref, *, lam_init):
    blk = pl.program_id(0)
    qt = pl.program_id(1)
    lv = lv_ref[...]
    lam = (jnp.exp(jnp.sum(lv[0:1] * lv[1:2], axis=-1, keepdims=True))
           - jnp.exp(jnp.sum(lv[2:3] * lv[3:4], axis=-1, keepdims=True)) + lam_init)
    subln_g = g_ref[...]
    out_scale = 1.0 - lam_init
    row0 = pl.multiple_of(qt * ATT_TQ, ATT_TQ)

    @pl.when(blk < N_CTX_BLOCKS)
    def _():
        q = q_ref[pl.ds(row0, ATT_TQ), :]
        k = k_ref[pl.ds(row0, ATT_TQ), :]
        v = v_ref[pl.ds(row0, ATT_TQ), :]
        o_ref[...] = _attn_heads(q, k, v, lam, subln_g, out_scale).astype(o_ref.dtype)

    @pl.when(blk >= N_CTX_BLOCKS)
    def _():
        lane = lax.broadcasted_iota(jnp.int32, (1, D_DIFF), 1)
        lo_mask = (lane % (DIFF_HEAD_DIM // 2)) < (DIFF_HEAD_DIM // 4)

        @pl.when(qt == 0)
        def _():
            kall_ref[0:LAT_LEN, :] = _rope(k_ref[...], cos_ref[...], sin_ref[...], lo_mask)
            kall_ref[LAT_LEN:LAT_LEN + PAST_LEN, :] = ck_ref[0]
            vall_ref[0:LAT_LEN, :] = v_ref[...]
            vall_ref[LAT_LEN:LAT_LEN + PAST_LEN, :] = cv_ref[0]

        q = _rope(q_ref[pl.ds(row0, ATT_TQ), :], cos_ref[pl.ds(row0, ATT_TQ), :],
                  sin_ref[pl.ds(row0, ATT_TQ), :], lo_mask)
        o_ref[...] = _attn_heads(q, kall_ref[...], vall_ref[...], lam, subln_g,
                                 out_scale).astype(o_ref.dtype)


def _attention(z, cache_k_l, cache_v_l, rope_cos, rope_sin, p, lam_init):
    zcol = lambda off: pl.BlockSpec((SEQ_BLOCK, D_DIFF), lambda i, j, off=off: (i, off // D_DIFF))
    cache = pl.BlockSpec((1, PAST_LEN, D_DIFF), lambda i, j: (jnp.maximum(i - N_CTX_BLOCKS, 0), 0, 0))
    full = lambda shape: pl.BlockSpec(shape, lambda i, j: (0,) * len(shape))
    return pl.pallas_call(
        functools.partial(_attn_kernel, lam_init=lam_init),
        grid=(N_SEQ_BLOCKS, SEQ_BLOCK // ATT_TQ),
        in_specs=[zcol(Z_Q), zcol(Z_K), zcol(Z_V), cache, cache,
                  full((LAT_LEN, D_DIFF)), full((LAT_LEN, D_DIFF)),
                  full((4, DIFF_HEAD_DIM)), full((1, HEAD_W))],
        out_specs=pl.BlockSpec((ATT_TQ, D_DIFF), lambda i, j: (i * (SEQ_BLOCK // ATT_TQ) + j, 0)),
        out_shape=jax.ShapeDtypeStruct((N_TOK, D_DIFF), bf16),
        scratch_shapes=[pltpu.VMEM((LAT_LEN + PAST_LEN, D_DIFF), f32),
                        pltpu.VMEM((LAT_LEN + PAST_LEN, D_DIFF), f32)],
        compiler_params=_cparams(2),
        name="diff_attention",
    )(z, z, z, cache_k_l, cache_v_l, rope_cos, rope_sin, p['diff_lambda'], p['diff_subln_g'])


def _rope_tables():
    t = jnp.arange(LAT_LEN)
    row = (t // GRID_W).astype(f32)
    col = (t % GRID_W).astype(f32)
    nf = DIFF_HEAD_DIM // 4
    freqs = ROPE_THETA ** (-jnp.arange(nf, dtype=f32) / nf)
    d = jnp.arange(DIFF_HEAD_DIM)
    pos = jnp.where((d < DIFF_HEAD_DIM // 2)[None, :], row[:, None], col[:, None])
    ang = pos * freqs[d % nf][None, :]
    sign = jnp.where((d % (2 * nf)) < nf, -1.0, 1.0)[None, :]
    cos = jnp.tile(jnp.cos(ang), (1, D_DIFF // DIFF_HEAD_DIM))
    sin = jnp.tile(jnp.sin(ang) * sign, (1, D_DIFF // DIFF_HEAD_DIM))
    return cos, sin


MERGE_TM = 512


def _merge_kernel(x_ref, lru_ref, yf_ref, yb_ref, bonus_ref, g_ref, diff_ref, gp0_ref, gp1_ref, gp2_ref,
                  mod_ref, lnp_ref, ones_ref, wlo_ref, wro_ref, wdo_ref, wout_ref, ng_ref, rhi_ref, rlo_ref,
                  xo_ref, h2_ref, logit_ref):
    y = yf_ref[...] + yb_ref[...]
    inv = 1.0 / RWKV_HEAD_DIM
    mu = _split_dot(y, ones_ref[...]) * inv
    dlt = y - mu
    var = _split_dot(dlt * dlt, ones_ref[...]) * inv
    yn = dlt * lax.rsqrt(var + RWKV_LN_EPS)
    rwkv_out = ((yn * lnp_ref[0:1, :] + lnp_ref[1:2, :] + bonus_ref[...]) * g_ref[...]).astype(bf16)

    merged = (jax.nn.sigmoid(gp0_ref[...]) * jnp.dot(lru_ref[...], wlo_ref[...], preferred_element_type=f32)
              + jax.nn.sigmoid(gp1_ref[...]) * jnp.dot(rwkv_out, wro_ref[...], preferred_element_type=f32)
              + jax.nn.sigmoid(gp2_ref[...]) * jnp.dot(diff_ref[...], wdo_ref[...], preferred_element_type=f32))
    mix = jnp.dot(merged.astype(bf16), wout_ref[...], preferred_element_type=f32)

    gate1 = mod_ref[0, :, 2 * D_MODEL:3 * D_MODEL]
    shift2 = mod_ref[0, :, 3 * D_MODEL:4 * D_MODEL]
    scale2 = mod_ref[0, :, 4 * D_MODEL:5 * D_MODEL]
    x = x_ref[...] + gate1 * mix
    xo_ref[...] = x
    yn2 = x * lax.rsqrt(jnp.mean(x * x, axis=-1, keepdims=True) + NORM_EPS) * ng_ref[...]
    h2 = yn2 * (1.0 + scale2) + shift2
    hi = h2.astype(bf16)
    lo = (h2 - hi.astype(f32)).astype(bf16)
    h2_ref[...] = hi
    logit_ref[...] = (jnp.dot(hi, rhi_ref[...], preferred_element_type=f32)
                      + jnp.dot(lo, rhi_ref[...], preferred_element_type=f32)
                      + jnp.dot(hi, rlo_ref[...], preferred_element_type=f32))


def _merge(x, z, lru_out, y_f, y_b, bonus, g, diff_out, mod_l, p):
    row = lambda w: pl.BlockSpec((MERGE_TM, w), lambda i: (i, 0))
    gate = lambda b: pl.BlockSpec((MERGE_TM, D_MODEL), lambda i, b=b: (i, Z_GATES // D_MODEL + b))
    full = lambda shape: pl.BlockSpec(shape, lambda i: (0,) * len(shape))
    return pl.pallas_call(
        _merge_kernel,
        grid=(N_TOK // MERGE_TM,),
        in_specs=[row(D_MODEL), row(D_LRU), row(D_RWKV), row(D_RWKV), row(D_RWKV), row(D_RWKV), row(D_DIFF),
                  gate(0), gate(1), gate(2),
                  pl.BlockSpec((1, 1, 6 * D_MODEL), lambda i: (_mod_group_of_rows(i, MERGE_TM), 0, 0)),
                  full((2, D_RWKV)), full((D_RWKV, D_RWKV)),
                  full((D_LRU, D_MODEL)), full((D_RWKV, D_MODEL)), full((D_DIFF, D_MODEL)),
                  full((D_MODEL, D_MODEL)), full((1, D_MODEL)),
                  full((D_MODEL, ROUTER_PAD)), full((D_MODEL, ROUTER_PAD))],
        out_specs=[row(D_MODEL), row(D_MODEL), row(ROUTER_PAD)],
        out_shape=[jax.ShapeDtypeStruct((N_TOK, D_MODEL), f32),
                   jax.ShapeDtypeStruct((N_TOK, D_MODEL), bf16),
                   jax.ShapeDtypeStruct((N_TOK, ROUTER_PAD), f32)],
        compiler_params=_cparams(1),
        name="merge",
    )(x, lru_out, y_f, y_b, bonus, g, diff_out, z, z, z, mod_l, p['rwkv_lnp'], p['head_ones'],
      p['w_lru_out'], p['w_rwkv_out'], p['w_diff_out'], p['w_out'], p['norm_g2'],
      p['router_hi'], p['router_lo'])


CAP_CTX = EC_CAPACITY * N_CTX_TOK // N_EXPERTS
CAP_LAT = EC_CAPACITY * (N_TOK - N_CTX_TOK) // N_EXPERTS
N_SLOTS = CAP_CTX + CAP_LAT
F32_MANTISSA_BITS = 23
F32_EXPONENT_BIAS = 127
LADDER_STEPS = 8
BISECT_STEPS = 60


def _cumsum_lanes(x01):
    rows, n = x01.shape
    i = lax.broadcasted_iota(jnp.int32, (LANES, LANES), 0)
    j = lax.broadcasted_iota(jnp.int32, (LANES, LANES), 1)
    upper = jnp.where(i <= j, 1.0, 0.0).astype(bf16)
    off = jnp.zeros((rows, 1), f32)
    outs = []
    for t in range(n // LANES):
        c = jnp.dot(x01[:, t * LANES:(t + 1) * LANES].astype(bf16), upper, preferred_element_type=f32) + off
        outs.append(c)
        off = c[:, LANES - 1:LANES]
    return jnp.concatenate(outs, axis=1)


def _route_kernel(logit_ref, pos_ref, gate_ref, *, cap, slot0):
    n = logit_ref.shape[0]
    lane = lax.broadcasted_iota(jnp.int32, (1, ROUTER_PAD), 1)
    lg = jnp.where(lane < N_EXPERTS, logit_ref[...], -jnp.inf)
    ex = jnp.exp(lg - jnp.max(lg, axis=-1, keepdims=True))
    aff = (ex / jnp.sum(ex, axis=-1, keepdims=True)).T[:N_EXPERTS]

    def n_above(thr):
        return jnp.sum(jnp.where(aff > thr, 1.0, 0.0), axis=1, keepdims=True)

    def rung(j):
        return pltpu.bitcast(jnp.left_shift(j, F32_MANTISSA_BITS), f32)

    def pick_rung(_, jj):
        j_lo, j_hi = jj
        j_mid = (j_lo + j_hi) >> 1
        enough = n_above(rung(j_mid)) >= cap
        return jnp.where(enough, j_mid, j_lo), jnp.where(enough, j_hi, j_mid)

    j_lo, j_hi = lax.fori_loop(0, LADDER_STEPS, pick_rung,
                               (jnp.full((N_EXPERTS, 1), -1, jnp.int32),
                                jnp.full((N_EXPERTS, 1), F32_EXPONENT_BIAS + 1, jnp.int32)))

    def bisect(_, lh):
        lo, hi = lh
        mid = 0.5 * (lo + hi)
        enough = n_above(mid) >= cap
        return jnp.where(enough, mid, lo), jnp.where(enough, hi, mid)

    lo, hi = lax.fori_loop(0, BISECT_STEPS, bisect, (jnp.where(j_lo < 0, -1.0, rung(j_lo)), rung(j_hi)))
    above = aff > hi
    tie = (aff > lo) & (aff <= hi)
    need = cap - n_above(hi)
    keep = above | (tie & (_cumsum_lanes(jnp.where(tie, 1.0, 0.0)) <= need))
    slot = _cumsum_lanes(jnp.where(keep, 1.0, 0.0)) - 1.0 + slot0
    pos_ref[...] = jnp.where(keep, slot, -1.0).astype(jnp.int32)
    gate_ref[...] = jnp.where(keep, aff, 0.0)


def _route(logits, block, n, cap, slot0):
    shape = (N_EXPERTS, n)
    pos, gate = pl.pallas_call(
        functools.partial(_route_kernel, cap=cap, slot0=slot0),
        grid=(1,),
        in_specs=[pl.BlockSpec((n, ROUTER_PAD), lambda i: (block, 0))],
        out_specs=[pl.BlockSpec(shape, lambda i: (0, 0))] * 2,
        out_shape=[jax.ShapeDtypeStruct(shape, jnp.int32), jax.ShapeDtypeStruct(shape, f32)],
        compiler_params=_cparams(1),
        name="route",
    )(logits)
    return pos.reshape(N_EXPERTS, 1, n), gate.reshape(N_EXPERTS, 1, n)


FFN_TF = 512
MOE_TT = 1024


def _one_hot(pos_row, slot0, n_slots):
    slots = lax.broadcasted_iota(jnp.int32, (n_slots, pos_row.shape[1]), 0) + slot0
    return pos_row == slots


def _slots_of_tile(tile):
    return (0, CAP_CTX) if tile * MOE_TT < N_CTX_TOK else (CAP_CTX, CAP_LAT)


def _ffn_kernel(pos_ref, gate_ref, h_ref, wg_ref, wu_ref, wd_ref, o_ref, xe_ref, gs_ref, acc_ref):
    f = pl.program_id(1)

    @pl.when(f == 0)
    def _():
        xe_ref[...] = jnp.zeros_like(xe_ref)
        gs_ref[...] = jnp.zeros_like(gs_ref)
        for tile in range(N_TOK // MOE_TT):
            slot0, n_slots = _slots_of_tile(tile)
            tok = slice(tile * MOE_TT, (tile + 1) * MOE_TT)
            sel = _one_hot(pos_ref[0, :, tok], slot0, n_slots)
            rows = slice(slot0, slot0 + n_slots)
            xe_ref[rows, :] += jnp.dot(jnp.where(sel, 1.0, 0.0).astype(bf16), h_ref[tok, :],
                                       preferred_element_type=f32)
            gs_ref[rows, :] += jnp.sum(jnp.where(sel, gate_ref[0, :, tok], 0.0), axis=1, keepdims=True)

    x = xe_ref[...].astype(bf16)
    gate = jnp.dot(x, wg_ref[0, 0].astype(bf16), preferred_element_type=f32)
    up = jnp.dot(x, wu_ref[0, 0].astype(bf16), preferred_element_type=f32)
    hid = (gate * jax.nn.sigmoid(gate) * up).astype(bf16)
    part = jnp.dot(hid, wd_ref[0, 0].astype(bf16), preferred_element_type=f32)

    @pl.when(f == 0)
    def _():
        acc_ref[...] = part

    @pl.when(f != 0)
    def _():
        acc_ref[...] += part

    @pl.when(f == pl.num_programs(1) - 1)
    def _():
        o_ref[0] = (acc_ref[...] * gs_ref[...]).astype(o_ref.dtype)


def _expert_ffn(pos, gate, h2, w_gate, w_up, w_down, layer):
    tok_row = pl.BlockSpec((1, 1, N_TOK), lambda e, f: (e, 0, 0))
    return pl.pallas_call(
        _ffn_kernel,
        grid=(N_EXPERTS, EXPERT_FF // FFN_TF),
        in_specs=[tok_row, tok_row,
                  pl.BlockSpec((N_TOK, D_MODEL), lambda e, f: (0, 0), pipeline_mode=pl.Buffered(1)),
                  pl.BlockSpec((1, 1, D_MODEL, FFN_TF), lambda e, f: (layer, e, 0, f)),
                  pl.BlockSpec((1, 1, D_MODEL, FFN_TF), lambda e, f: (layer, e, 0, f)),
                  pl.BlockSpec((1, 1, FFN_TF, D_MODEL), lambda e, f: (layer, e, f, 0))],
        out_specs=pl.BlockSpec((1, N_SLOTS, D_MODEL), lambda e, f: (e, 0, 0)),
        out_shape=jax.ShapeDtypeStruct((N_EXPERTS, N_SLOTS, D_MODEL), bf16),
        scratch_shapes=[pltpu.VMEM((N_SLOTS, D_MODEL), f32),
                        pltpu.VMEM((N_SLOTS, 1), f32),
                        pltpu.VMEM((N_SLOTS, D_MODEL), f32)],
        compiler_params=_cparams(2),
        name="expert_ffn",
    )(pos, gate, h2, w_gate, w_up, w_down)


def _combine_kernel(x_ref, pos_ref, ye_ref, mod_ref, g_ref, xo_ref, acc_ref, *, final_norm):
    tile = pl.program_id(0)
    e = pl.program_id(1)

    @pl.when(e == 0)
    def _():
        acc_ref[...] = jnp.zeros_like(acc_ref)

    def scatter(slot0, n_slots):
        sel = jnp.where(_one_hot(pos_ref[0], slot0, n_slots), 1.0, 0.0).astype(bf16)
        acc_ref[...] += lax.dot_general(sel, ye_ref[0, slot0:slot0 + n_slots, :], (((0,), (0,)), ((), ())),
                                        preferred_element_type=f32)

    @pl.when(tile * MOE_TT < N_CTX_TOK)
    def _():
        scatter(0, CAP_CTX)

    @pl.when(tile * MOE_TT >= N_CTX_TOK)
    def _():
        scatter(CAP_CTX, CAP_LAT)

    @pl.when(e == pl.num_programs(1) - 1)
    def _():
        gate2 = mod_ref[0, :, 5 * D_MODEL:6 * D_MODEL]
        x = x_ref[...] + gate2 * acc_ref[...]
        if final_norm:
            x = x * lax.rsqrt(jnp.mean(x * x, axis=-1, keepdims=True) + NORM_EPS) * g_ref[...]
        xo_ref[...] = x


def _combine(x, pos, ye, mod_l, final_g, final_norm):
    row = pl.BlockSpec((MOE_TT, D_MODEL), lambda i, e: (i, 0))
    return pl.pallas_call(
        functools.partial(_combine_kernel, final_norm=final_norm),
        grid=(N_TOK // MOE_TT, N_EXPERTS),
        in_specs=[row,
                  pl.BlockSpec((1, 1, MOE_TT), lambda i, e: (e, 0, i)),
                  pl.BlockSpec((1, N_SLOTS, D_MODEL), lambda i, e: (e, 0, 0)),
                  pl.BlockSpec((1, 1, 6 * D_MODEL), lambda i, e: (_mod_group_of_rows(i, MOE_TT), 0, 0)),
                  pl.BlockSpec((1, D_MODEL), lambda i, e: (0, 0))],
        out_specs=row,
        out_shape=jax.ShapeDtypeStruct((N_TOK, D_MODEL), f32),
        scratch_shapes=[pltpu.VMEM((MOE_TT, D_MODEL), f32)],
        compiler_params=_cparams(2),
        name="moe_combine",
    )(x, pos, ye, mod_l, final_g)


def _moe_residual(x, h2, logits, w_gate, w_up, w_down, layer, mod_l, final_g, final_norm):
    pos_c, gate_c = _route(logits, 0, N_CTX_TOK, CAP_CTX, 0)
    lat_n = N_TOK - N_CTX_TOK
    pos_l, gate_l = _route(logits, N_CTX_TOK // lat_n, lat_n, CAP_LAT, CAP_CTX)
    pos = jnp.concatenate([pos_c, pos_l], axis=2)
    gate = jnp.concatenate([gate_c, gate_l], axis=2)
    ye = _expert_ffn(pos, gate, h2, w_gate, w_up, w_down, layer)
    return _combine(x, pos, ye, mod_l, final_g, final_norm)


def _block_diag(blocks):
    n, a, b = blocks.shape
    eye = jnp.eye(n, dtype=blocks.dtype)
    return (eye[:, None, :, None] * blocks[:, :, None, :]).reshape(n * a, n * b)


def _prep_layer(l, w):
    p = {}
    w_in = w['w_in'][l]
    lru_x, lru_g, zr, q, k, v, gates = (
        w_in[:, 0:D_LRU], w_in[:, D_LRU:2 * D_LRU], w_in[:, 2 * D_LRU:2 * D_LRU + RWKV_COLS],
        w_in[:, 2 * D_LRU + RWKV_COLS:2 * D_LRU + RWKV_COLS + D_DIFF],
        w_in[:, 2 * D_LRU + RWKV_COLS + D_DIFF:2 * D_LRU + RWKV_COLS + 2 * D_DIFF],
        w_in[:, 2 * D_LRU + RWKV_COLS + 2 * D_DIFF:2 * D_LRU + RWKV_COLS + 3 * D_DIFF],
        w_in[:, 2 * D_LRU + RWKV_COLS + 3 * D_DIFF:])
    pad = jnp.zeros((D_MODEL, RWKV_COLS_PAD - RWKV_COLS), f32)
    p['w_in'] = jnp.concatenate([zr, pad, gates, lru_x, lru_g, q, k, v], axis=1).astype(bf16)
    p['norm_g1'] = w['norm_g'][l, 0].reshape(1, D_MODEL)
    p['norm_g2'] = w['norm_g'][l, 1].reshape(1, D_MODEL)

    p['lru_conv_w'] = w['lru_conv_w'][l]
    p['lru_conv_b'] = w['lru_conv_b'][l].reshape(1, D_LRU)
    p['lru_wg'] = jnp.concatenate(
        [_block_diag(w[name][l, d]) for d in range(2) for name in ('lru_wa', 'lru_wx')], axis=1).astype(bf16)
    p['lru_bg'] = jnp.concatenate(
        [w[name][l, d] for d in range(2) for name in ('lru_ba', 'lru_bx')]).reshape(1, 4 * D_LRU)
    p['lru_lambda'] = w['lru_lambda'][l]

    p['rwkv_conv_w'] = jnp.pad(w['rwkv_conv_w'][l], ((0, 0), (0, RWKV_COLS_PAD - RWKV_COLS)))

    def lowrank_pair(m):
        rank = m.shape[1]
        fwd, bwd = (m[d].reshape(rank, N_RWKV_CT, RWKV_CT).transpose(1, 0, 2) for d in range(2))
        zero = jnp.zeros_like(fwd)
        return jnp.concatenate([jnp.concatenate([fwd, zero], axis=2),
                                jnp.concatenate([zero, bwd], axis=2)], axis=1).astype(bf16)

    p['rwkv_w2'] = lowrank_pair(w['rwkv_w2'][l])
    p['rwkv_a2'] = lowrank_pair(w['rwkv_a2'][l])
    p['rwkv_g2'] = w['rwkv_g2'][l].astype(bf16)
    p['rwkv_pc'] = jnp.concatenate(
        [w['rwkv_w0'][l], w['rwkv_a0'][l], w['rwkv_k_k'][l][None], w['rwkv_k_a'][l][None],
         w['rwkv_r_k'][l].reshape(1, D_RWKV), jnp.zeros((1, D_RWKV), f32)], axis=0)
    p['rwkv_lnp'] = jnp.stack([w['rwkv_ln_g'][l], w['rwkv_ln_b'][l]], axis=0)
    p['head_ones'] = _block_diag(jnp.ones((N_RWKV_HEADS, RWKV_HEAD_DIM, RWKV_HEAD_DIM), bf16))

    p['diff_lambda'] = w['diff_lambda'][l]
    p['diff_subln_g'] = w['diff_subln_g'][l].reshape(1, HEAD_W)
    p['w_lru_out'] = w['w_lru_out'][l].astype(bf16)
    p['w_rwkv_out'] = w['w_rwkv_out'][l].astype(bf16)
    p['w_diff_out'] = w['w_diff_out'][l].astype(bf16)
    p['w_out'] = w['w_out'][l].astype(bf16)
    rw = jnp.pad(w['router_w'][l], ((0, 0), (0, ROUTER_PAD - N_EXPERTS)))
    p['router_hi'] = rw.astype(bf16)
    p['router_lo'] = (rw - p['router_hi'].astype(f32)).astype(bf16)
    return p


def kernel(x_prompt, x_sample, cache_k, cache_v, state_lru, state_rwkv, c, c_ctx, norm_g, final_norm_g, w_mod, b_mod, w_in, lru_conv_w, lru_conv_b, lru_wa, lru_ba, lru_wx, lru_bx, lru_lambda, rwkv_conv_w, rwkv_w0, rwkv_w2, rwkv_a0, rwkv_a2, rwkv_k_k, rwkv_k_a, rwkv_r_k, rwkv_g2, rwkv_ln_g, rwkv_ln_b, diff_lambda, diff_subln_g, w_lru_out, w_rwkv_out, w_diff_out, w_out, router_w, exp_w_gate, exp_w_up, exp_w_down):
    w = dict(norm_g=norm_g, w_in=w_in, lru_conv_w=lru_conv_w, lru_conv_b=lru_conv_b, lru_wa=lru_wa,
             lru_ba=lru_ba, lru_wx=lru_wx, lru_bx=lru_bx, lru_lambda=lru_lambda, rwkv_conv_w=rwkv_conv_w,
             rwkv_w0=rwkv_w0, rwkv_w2=rwkv_w2, rwkv_a0=rwkv_a0, rwkv_a2=rwkv_a2, rwkv_k_k=rwkv_k_k,
             rwkv_k_a=rwkv_k_a, rwkv_r_k=rwkv_r_k, rwkv_g2=rwkv_g2, rwkv_ln_g=rwkv_ln_g,
             rwkv_ln_b=rwkv_ln_b, diff_lambda=diff_lambda, diff_subln_g=diff_subln_g, w_lru_out=w_lru_out,
             w_rwkv_out=w_rwkv_out, w_diff_out=w_diff_out, w_out=w_out, router_w=router_w)

    x = jnp.concatenate([x_prompt.reshape(N_CTX_TOK, D_MODEL), x_sample.reshape(-1, D_MODEL)], axis=0)
    cvec = jnp.concatenate([c_ctx[None], c, jnp.zeros((SUBLANES - N_MOD_GROUPS, D_MODEL), f32)], axis=0)
    mod = _modulation(cvec, w_mod, b_mod)
    rope_cos, rope_sin = _rope_tables()
    final_g = final_norm_g.reshape(1, D_MODEL)
    ctx_rows = slice(0, N_CTX_TOK)
    lat_rows = slice(N_CTX_TOK, N_TOK)

    new_k, new_v, new_lru, new_rwkv = [], [], [], []
    for l in range(DEPTH):
        p = _prep_layer(l, w)
        lam_init = 0.8 - 0.6 * math.exp(-0.3 * l)
        mod_l = mod[l].reshape(SUBLANES, 1, 6 * D_MODEL)

        z = _in_projection(x, mod_l, p['norm_g1'], p['w_in'])
        new_k.append(z[ctx_rows, Z_K:Z_K + D_DIFF].reshape(N_CTX_SEQ, CTX_LEN, N_DIFF_HEADS, 2, DIFF_HEAD_DIM))
        new_v.append(z[ctx_rows, Z_V:Z_V + D_DIFF].reshape(N_CTX_SEQ, CTX_LEN, N_DIFF_HEADS, HEAD_W))

        h0_blocks = jnp.concatenate(
            [jnp.zeros((N_CTX_BLOCKS, 2, D_LRU), f32), state_lru[:, l].astype(f32)], axis=0)
        lru_out, lru_fin = _lru_branch(z, p, h0_blocks)
        n_per = SEQ_BLOCK // CTX_LEN
        fin = lru_fin[:N_CTX_BLOCKS].reshape(N_CTX_BLOCKS, 2, n_per, D_LRU)
        new_lru.append(jnp.transpose(fin, (0, 2, 1, 3)).reshape(N_CTX_SEQ, 2, D_LRU))

        scan_in, bonus, out_gate = _rwkv_pre(z, p)
        yf_c, yb_c, s_c = _rwkv_scan_path(scan_in, 0, N_CTX_SEQ, CTX_LEN, None)
        yf_l, yb_l, _ = _rwkv_scan_path(scan_in, N_CTX_TOK // LAT_LEN, N_LAT_SEQ, LAT_LEN, state_rwkv[:, l])
        new_rwkv.append(s_c)
        y_f = jnp.concatenate([yf_c, yf_l], axis=0)
        y_b = jnp.concatenate([yb_c, yb_l], axis=0)

        diff_out = _attention(z, cache_k[:, l].reshape(N_LAT_SEQ, PAST_LEN, D_DIFF),
                              cache_v[:, l].reshape(N_LAT_SEQ, PAST_LEN, D_DIFF),
                              rope_cos, rope_sin, p, lam_init)

        x, h2, logits = _merge(x, z, lru_out, y_f, y_b, bonus, out_gate, diff_out, mod_l, p)
        x = _moe_residual(x, h2, logits, exp_w_gate, exp_w_up, exp_w_down, l, mod_l, final_g,
                          final_norm=(l == DEPTH - 1))

    y_prompt = x[ctx_rows].reshape(x_prompt.shape)
    y_sample = x[lat_rows].reshape(x_sample.shape)
    return (y_prompt, y_sample, jnp.stack(new_k, axis=1), jnp.stack(new_v, axis=1),
            jnp.stack(new_lru, axis=1), jnp.stack(new_rwkv, axis=1))
```

```python
import functools
import math

import jax
import jax.numpy as jnp
from jax import lax
from jax.experimental import pallas as pl
from jax.experimental.pallas import tpu as pltpu

f32 = jnp.float32
bf16 = jnp.bfloat16

D_MODEL = 1024
DEPTH = 2
GRID_W = 64
NORM_EPS = 1e-6

N_CTX_SEQ = 16
CTX_LEN = 256
N_LAT_SEQ = 2
LAT_LEN = 1024
PAST_LEN = 256
N_CTX_TOK = N_CTX_SEQ * CTX_LEN
N_TOK = N_CTX_TOK + N_LAT_SEQ * LAT_LEN

D_LRU = 512
LRU_BLOCKS = 8
LRU_BLOCK_W = D_LRU // LRU_BLOCKS
LRU_C = 8.0

D_RWKV = 512
RWKV_HEAD_DIM = 64
N_RWKV_HEADS = D_RWKV // RWKV_HEAD_DIM
W_RANK = 64
A_RANK = 64
G_RANK = 128
RWKV_LN_EPS = 64e-5
RWKV_COLS = 3 * D_RWKV + 2 * W_RANK + 2 * A_RANK + G_RANK
RWKV_COLS_PAD = 2048

N_DIFF_HEADS = 4
DIFF_HEAD_DIM = 64
D_DIFF = N_DIFF_HEADS * 2 * DIFF_HEAD_DIM
ROPE_THETA = 10000.0

N_BRANCHES = 3
N_EXPERTS = 16
EXPERT_FF = 1024
EC_CAPACITY = 2
ROUTER_PAD = 128

LANES = 128
SUBLANES = 8

SEQ_BLOCK = 1024
N_SEQ_BLOCKS = N_TOK // SEQ_BLOCK
N_CTX_BLOCKS = N_CTX_TOK // SEQ_BLOCK
N_MOD_GROUPS = 1 + N_LAT_SEQ

Z_RWKV = 0
Z_GATES = RWKV_COLS_PAD
Z_LRU_X = Z_GATES + N_BRANCHES * D_MODEL
Z_LRU_G = Z_LRU_X + D_LRU
Z_Q = Z_LRU_G + D_LRU
Z_K = Z_Q + D_DIFF
Z_V = Z_K + D_DIFF
Z_COLS = Z_V + D_DIFF

VMEM_LIMIT = 56 * 1024 * 1024


def _cparams(n_axes):
    return pltpu.CompilerParams(dimension_semantics=("arbitrary",) * n_axes,
                                vmem_limit_bytes=VMEM_LIMIT)


def _seg_len(block_idx):
    return jnp.where(block_idx < N_CTX_BLOCKS, CTX_LEN, LAT_LEN)


def _mod_group_of_rows(row_block, rows_per_block):
    first_lat = N_CTX_TOK // rows_per_block
    per_seq = LAT_LEN // rows_per_block
    return jnp.where(row_block < first_lat, 0, (row_block - first_lat) // per_seq + 1)


def _shift_rows(v, d, tm, seg):
    n = v.shape[0]
    r = pltpu.roll(v, (-d) % n, axis=0)
    ok = (tm + d >= 0) & (tm + d < seg)
    return jnp.where(ok, r, 0.0)


def _split_dot(x, w_bf16):
    hi = x.astype(bf16)
    lo = (x - hi.astype(f32)).astype(bf16)
    return (jnp.dot(hi, w_bf16, preferred_element_type=f32)
            + jnp.dot(lo, w_bf16, preferred_element_type=f32))


MOD_TN = 1536


def _mod_kernel(c_ref, w_ref, b_ref, o_ref):
    c = c_ref[...]
    s = (c * jax.nn.sigmoid(c)).astype(bf16)
    o_ref[0] = jnp.dot(s, w_ref[0].astype(bf16), preferred_element_type=f32) + b_ref[0]


def _modulation(cvec, w_mod, b_mod):
    n_out = w_mod.shape[-1]
    return pl.pallas_call(
        _mod_kernel,
        grid=(DEPTH, n_out // MOD_TN),
        in_specs=[pl.BlockSpec((SUBLANES, D_MODEL), lambda l, j: (0, 0)),
                  pl.BlockSpec((1, D_MODEL, MOD_TN), lambda l, j: (l, 0, j)),
                  pl.BlockSpec((1, 1, MOD_TN), lambda l, j: (l, 0, j))],
        out_specs=pl.BlockSpec((1, SUBLANES, MOD_TN), lambda l, j: (l, 0, j)),
        out_shape=jax.ShapeDtypeStruct((DEPTH, SUBLANES, n_out), f32),
        compiler_params=_cparams(2),
        name="modulation",
    )(cvec, w_mod, b_mod.reshape(DEPTH, 1, n_out))


INPROJ_TM = 1024
INPROJ_TN = 768


def _inproj_kernel(x_ref, mod_ref, g_ref, w_ref, o_ref, h_ref):
    @pl.when(pl.program_id(1) == 0)
    def _():
        x = x_ref[...]
        y = x * lax.rsqrt(jnp.mean(x * x, axis=-1, keepdims=True) + NORM_EPS) * g_ref[...]
        shift = mod_ref[0, :, 0:D_MODEL]
        scale = mod_ref[0, :, D_MODEL:2 * D_MODEL]
        h_ref[...] = (y * (1.0 + scale) + shift).astype(bf16)

    o_ref[...] = jnp.dot(h_ref[...], w_ref[...], preferred_element_type=f32)


def _in_projection(x, mod_l, norm_g_row, w_in_p):
    return pl.pallas_call(
        _inproj_kernel,
        grid=(N_TOK // INPROJ_TM, Z_COLS // INPROJ_TN),
        in_specs=[pl.BlockSpec((INPROJ_TM, D_MODEL), lambda i, j: (i, 0)),
                  pl.BlockSpec((1, 1, 2 * D_MODEL),
                               lambda i, j: (_mod_group_of_rows(i, INPROJ_TM), 0, 0)),
                  pl.BlockSpec((1, D_MODEL), lambda i, j: (0, 0)),
                  pl.BlockSpec((D_MODEL, INPROJ_TN), lambda i, j: (0, j))],
        out_specs=pl.BlockSpec((INPROJ_TM, INPROJ_TN), lambda i, j: (i, j)),
        out_shape=jax.ShapeDtypeStruct((N_TOK, Z_COLS), f32),
        scratch_shapes=[pltpu.VMEM((INPROJ_TM, D_MODEL), bf16)],
        compiler_params=_cparams(2),
        name="in_projection",
    )(x, mod_l, norm_g_row, w_in_p)


LRU_TILE_UNROLL = 4


def _lru_kernel(x_ref, gate_ref, cw_ref, cb_ref, wg_ref, bg_ref, lam_ref, h0_ref, out_ref, hfin_ref,
                a_scr, b_scr, yf_scr, yb_scr):
    blk = pl.program_id(0)
    seg = _seg_len(blk)
    n = x_ref.shape[0]
    t = lax.broadcasted_iota(jnp.int32, (n, D_LRU), 0)
    tm = t & (seg - 1)

    x = x_ref[...]
    xc = (cb_ref[...]
          + cw_ref[0:1, :] * _shift_rows(x, -2, tm, seg)
          + cw_ref[1:2, :] * _shift_rows(x, -1, tm, seg)
          + cw_ref[2:3, :] * x
          + cw_ref[3:4, :] * _shift_rows(x, 1, tm, seg))
    pre = jnp.dot(xc.astype(bf16), wg_ref[...], preferred_element_type=f32) + bg_ref[...]

    n_tiles = n // SUBLANES
    tiles_per_seg = seg // SUBLANES
    row = lax.broadcasted_iota(jnp.int32, (SUBLANES, D_LRU), 0)
    for d, y_scr in enumerate((yf_scr, yb_scr)):
        r = jax.nn.sigmoid(pre[:, (2 * d) * D_LRU:(2 * d + 1) * D_LRU])
        i = jax.nn.sigmoid(pre[:, (2 * d + 1) * D_LRU:(2 * d + 2) * D_LRU])
        log_a = -LRU_C * r * jax.nn.softplus(-lam_ref[d:d + 1, :])
        a = jnp.exp(log_a)
        a_scr[...] = a
        b_scr[...] = jnp.sqrt(1.0 - a * a) * (i * xc)
        h0 = h0_ref[0, d:d + 1, :]

        def tile_step(k, h_prev, d=d, y_scr=y_scr, h0=h0):
            idx = k if d == 0 else n_tiles - 1 - k
            rows = pl.ds(pl.multiple_of(idx * SUBLANES, SUBLANES), SUBLANES)
            a_t = a_scr[rows, :]
            b_t = b_scr[rows, :]
            s = 1
            while s < SUBLANES:
                ok = (row >= s) if d == 0 else (row < SUBLANES - s)
                shift = s if d == 0 else SUBLANES - s
                a_sh = jnp.where(ok, pltpu.roll(a_t, shift, axis=0), 1.0)
                b_sh = jnp.where(ok, pltpu.roll(b_t, shift, axis=0), 0.0)
                b_t = a_t * b_sh + b_t
                a_t = a_t * a_sh
                s *= 2
            first_of_seg = (idx & (tiles_per_seg - 1)) == (0 if d == 0 else tiles_per_seg - 1)
            h_t = a_t * jnp.where(first_of_seg, h0, h_prev) + b_t
            y_scr[rows, :] = h_t
            return h_t[SUBLANES - 1:SUBLANES, :] if d == 0 else h_t[0:1, :]

        lax.fori_loop(0, n_tiles, tile_step, jnp.zeros((1, D_LRU), f32), unroll=LRU_TILE_UNROLL)

    out_ref[...] = ((yf_scr[...] + yb_scr[...]) * jax.nn.gelu(gate_ref[...])).astype(out_ref.dtype)
    fins = ([yf_scr[(j + 1) * CTX_LEN - 1:(j + 1) * CTX_LEN, :] for j in range(SEQ_BLOCK // CTX_LEN)]
            + [yb_scr[j * CTX_LEN:j * CTX_LEN + 1, :] for j in range(SEQ_BLOCK // CTX_LEN)])
    hfin_ref[0] = jnp.concatenate(fins, axis=0)


def _lru_branch(z, p, h0_blocks):
    full = lambda shape: pl.BlockSpec(shape, lambda i: (0,) * len(shape))
    return pl.pallas_call(
        _lru_kernel,
        grid=(N_SEQ_BLOCKS,),
        in_specs=[pl.BlockSpec((SEQ_BLOCK, D_LRU), lambda i: (i, Z_LRU_X // D_LRU)),
                  pl.BlockSpec((SEQ_BLOCK, D_LRU), lambda i: (i, Z_LRU_G // D_LRU)),
                  full((4, D_LRU)), full((1, D_LRU)), full((D_LRU, 4 * D_LRU)), full((1, 4 * D_LRU)),
                  full((2, D_LRU)),
                  pl.BlockSpec((1, 2, D_LRU), lambda i: (i, 0, 0))],
        out_specs=[pl.BlockSpec((SEQ_BLOCK, D_LRU), lambda i: (i, 0)),
                   pl.BlockSpec((1, SUBLANES, D_LRU), lambda i: (i, 0, 0))],
        out_shape=[jax.ShapeDtypeStruct((N_TOK, D_LRU), bf16),
                   jax.ShapeDtypeStruct((N_SEQ_BLOCKS, SUBLANES, D_LRU), f32)],
        scratch_shapes=[pltpu.VMEM((SEQ_BLOCK, D_LRU), f32)] * 4,
        compiler_params=_cparams(1),
        name="lru_branch",
    )(z, z, p['lru_conv_w'], p['lru_conv_b'], p['lru_wg'], p['lru_bg'], p['lru_lambda'], h0_blocks)


RWKV_CT = 256
N_RWKV_CT = D_RWKV // RWKV_CT
RWKV_LOWRANK_OFF = 3 * D_RWKV
RWKV_LOWRANK_W = RWKV_COLS_PAD - RWKV_LOWRANK_OFF
SCAN_ARRAYS = ('w_f', 'w_b', 'ka_f', 'ka_b', 'kd_f', 'kd_b', 'nkk', 'r', 'v')
N_SCAN_ARRAYS = len(SCAN_ARRAYS)
SCAN_IDX = {name: i for i, name in enumerate(SCAN_ARRAYS)}


def _rwkv_pre_kernel(r_ref, k_ref, v_ref, lr_ref, cwr_ref, cwk_ref, cwv_ref, cwl_ref,
                     w2_ref, a2_ref, g2_ref, pc_ref, ones_ref, sc_ref, bonus_ref, g_ref):
    blk = pl.program_id(0)
    seg = _seg_len(blk)
    n = r_ref.shape[0]

    def conv(x_ref, cw_ref):
        width = x_ref.shape[1]
        t = lax.broadcasted_iota(jnp.int32, (n, width), 0)
        tm = t & (seg - 1)
        x = x_ref[...]
        return (cw_ref[0:1, :] * _shift_rows(x, -1, tm, seg) + cw_ref[1:2, :] * x
                + cw_ref[2:3, :] * _shift_rows(x, 1, tm, seg))

    r = conv(r_ref, cwr_ref)
    k = conv(k_ref, cwk_ref)
    v = conv(v_ref, cwv_ref)
    lr = conv(lr_ref, cwl_ref)
    wl = lr[:, 0:2 * W_RANK]
    al = lr[:, 2 * W_RANK:2 * W_RANK + 2 * A_RANK]
    gl = lr[:, 2 * W_RANK + 2 * A_RANK:2 * W_RANK + 2 * A_RANK + G_RANK]

    w0 = [pc_ref[0:1, :], pc_ref[1:2, :]]
    a0 = [pc_ref[2:3, :], pc_ref[3:4, :]]
    k_k = pc_ref[4:5, :]
    k_a = pc_ref[5:6, :]
    r_k = pc_ref[6:7, :]

    w_pre = jnp.dot(jnp.tanh(wl).astype(bf16), w2_ref[0], preferred_element_type=f32)
    a_pre = jnp.dot(al.astype(bf16), a2_ref[0], preferred_element_type=f32)

    kk = k * k_k
    ss = _split_dot(kk * kk, ones_ref[...])
    kk = kk / jnp.maximum(jnp.sqrt(ss), 1e-12)

    kd_sum = None
    for d, sfx in enumerate(('_f', '_b')):
        cols = slice(d * RWKV_CT, (d + 1) * RWKV_CT)
        w_log = -jax.nn.softplus(-(w0[d] + w_pre[:, cols])) - 0.5
        sc_ref[SCAN_IDX['w' + sfx]] = jnp.exp(-jnp.exp(w_log))
        a = jax.nn.sigmoid(a0[d] + a_pre[:, cols])
        sc_ref[SCAN_IDX['ka' + sfx]] = kk * a
        kd = k * (1.0 + (a - 1.0) * k_a)
        sc_ref[SCAN_IDX['kd' + sfx]] = kd
        kd_sum = kd if kd_sum is None else kd_sum + kd

    sc_ref[SCAN_IDX['r']] = r
    sc_ref[SCAN_IDX['v']] = v
    sc_ref[SCAN_IDX['nkk']] = -kk
    bonus_ref[...] = _split_dot(r * kd_sum * r_k, ones_ref[...]) * v
    g_ref[...] = jnp.dot(jax.nn.sigmoid(gl).astype(bf16), g2_ref[...], preferred_element_type=f32)


def _rwkv_pre(z, p):
    nct = D_RWKV // RWKV_CT
    zcol = lambda base: pl.BlockSpec((SEQ_BLOCK, RWKV_CT), lambda i, c, base=base: (i, base + c))
    cwcol = lambda base: pl.BlockSpec((3, RWKV_CT), lambda i, c, base=base: (0, base + c))
    out_spec = pl.BlockSpec((SEQ_BLOCK, RWKV_CT), lambda i, c: (i, c))
    return pl.pallas_call(
        _rwkv_pre_kernel,
        grid=(N_SEQ_BLOCKS, nct),
        in_specs=[zcol(0), zcol(nct), zcol(2 * nct),
                  pl.BlockSpec((SEQ_BLOCK, RWKV_LOWRANK_W),
                               lambda i, c: (i, RWKV_LOWRANK_OFF // RWKV_LOWRANK_W)),
                  cwcol(0), cwcol(nct), cwcol(2 * nct),
                  pl.BlockSpec((3, RWKV_LOWRANK_W), lambda i, c: (0, RWKV_LOWRANK_OFF // RWKV_LOWRANK_W)),
                  pl.BlockSpec((1, 2 * W_RANK, 2 * RWKV_CT), lambda i, c: (c, 0, 0)),
                  pl.BlockSpec((1, 2 * A_RANK, 2 * RWKV_CT), lambda i, c: (c, 0, 0)),
                  pl.BlockSpec((G_RANK, RWKV_CT), lambda i, c: (0, c)),
                  pl.BlockSpec((SUBLANES, RWKV_CT), lambda i, c: (0, c)),
                  pl.BlockSpec((RWKV_CT, RWKV_CT), lambda i, c: (0, 0))],
        out_specs=[pl.BlockSpec((N_SCAN_ARRAYS, SEQ_BLOCK, RWKV_CT), lambda i, c: (0, i, c)),
                   out_spec, out_spec],
        out_shape=[jax.ShapeDtypeStruct((N_SCAN_ARRAYS, N_TOK, D_RWKV), f32),
                   jax.ShapeDtypeStruct((N_TOK, D_RWKV), f32),
                   jax.ShapeDtypeStruct((N_TOK, D_RWKV), f32)],
        compiler_params=_cparams(2),
        name="rwkv_pre",
    )(z, z, z, z, p['rwkv_conv_w'], p['rwkv_conv_w'], p['rwkv_conv_w'], p['rwkv_conv_w'],
      p['rwkv_w2'], p['rwkv_a2'], p['rwkv_g2'], p['rwkv_pc'], p['head_ones'][:RWKV_CT, :RWKV_CT])


PACK_T = 128
TIME_BLOCK = SUBLANES
PACK_UNROLL = 8


def _gather_states(scr, row, n_seq):
    return scr[:, pl.ds(row, N_RWKV_HEADS, stride=RWKV_HEAD_DIM), :].reshape(n_seq * N_RWKV_HEADS, PACK_T)


def _pack_kernel(x_ref, o_ref, scr, *, n_seq, reps, rows, fold_rows):
    for b in range(n_seq):
        scr[b] = x_ref[b].T

    def body(row, c):
        if fold_rows:
            m = jnp.concatenate([_gather_states(scr, row * reps + rep, n_seq) for rep in range(reps)], axis=0)
        else:
            m = _gather_states(scr, row, n_seq)
            if reps > 1:
                m = jnp.concatenate([m] * reps, axis=0)
        mt = m.T
        line0 = pl.multiple_of(row * TIME_BLOCK, TIME_BLOCK)
        for u in range(PACK_T // TIME_BLOCK):
            o_ref[u, pl.ds(line0, TIME_BLOCK), :] = mt[u * TIME_BLOCK:(u + 1) * TIME_BLOCK]
        return c

    lax.fori_loop(0, rows, body, 0, unroll=PACK_UNROLL)


def _pack(stacked, first, count, n_seq, seq_len, seq0, reps, fold_rows):
    rows = RWKV_HEAD_DIM // reps if fold_rows else RWKV_HEAD_DIM
    x = stacked.reshape(N_SCAN_ARRAYS, N_TOK // seq_len, seq_len, D_RWKV)
    tb = PACK_T // TIME_BLOCK
    return pl.pallas_call(
        functools.partial(_pack_kernel, n_seq=n_seq, reps=reps, rows=rows, fold_rows=fold_rows),
        grid=(count, seq_len // PACK_T),
        in_specs=[pl.BlockSpec((None, n_seq, PACK_T, D_RWKV),
                               lambda a, j: (first + a, seq0 // n_seq, j, 0))],
        out_specs=pl.BlockSpec((None, tb, rows * TIME_BLOCK, LANES), lambda a, j: (a, j, 0, 0)),
        out_shape=jax.ShapeDtypeStruct((count, seq_len // TIME_BLOCK, rows * TIME_BLOCK, LANES), f32),
        scratch_shapes=[pltpu.VMEM((n_seq, D_RWKV, PACK_T), f32)],
        compiler_params=_cparams(2),
        name="rwkv_pack",
    )(x)


def _unpack_kernel(yf_ref, yb_ref, of_ref, ob_ref, scr, *, n_seq, reps, rows):
    group = n_seq * N_RWKV_HEADS
    for y_ref, o_ref in ((yf_ref, of_ref), (yb_ref, ob_ref)):
        def body(row, c, y_ref=y_ref):
            line0 = pl.multiple_of(row * TIME_BLOCK, TIME_BLOCK)
            mt = jnp.concatenate([y_ref[u, pl.ds(line0, TIME_BLOCK), :]
                                  for u in range(PACK_T // TIME_BLOCK)], axis=0)
            m = mt.T
            for rep in range(reps):
                part = m[rep * group:(rep + 1) * group].reshape(n_seq, N_RWKV_HEADS, PACK_T)
                scr[:, pl.ds(row * reps + rep, N_RWKV_HEADS, stride=RWKV_HEAD_DIM), :] = part
            return c

        lax.fori_loop(0, rows, body, 0, unroll=PACK_UNROLL)
        for b in range(n_seq):
            o_ref[b] = scr[b].T


def _unpack(y_f, y_b, n_seq, seq_len, reps):
    rows = y_f.shape[1] // TIME_BLOCK
    yspec = pl.BlockSpec((PACK_T // TIME_BLOCK, rows * TIME_BLOCK, LANES), lambda j: (j, 0, 0))
    ospec = pl.BlockSpec((n_seq, PACK_T, D_RWKV), lambda j: (0, j, 0))
    shape = jax.ShapeDtypeStruct((n_seq, seq_len, D_RWKV), f32)
    o_f, o_b = pl.pallas_call(
        functools.partial(_unpack_kernel, n_seq=n_seq, reps=reps, rows=rows),
        grid=(seq_len // PACK_T,),
        in_specs=[yspec, yspec],
        out_specs=[ospec, ospec],
        out_shape=[shape, shape],
        scratch_shapes=[pltpu.VMEM((n_seq, D_RWKV, PACK_T), f32)],
        compiler_params=_cparams(1),
        name="rwkv_unpack",
    )(y_f, y_b)
    return o_f.reshape(n_seq * seq_len, D_RWKV), o_b.reshape(n_seq * seq_len, D_RWKV)


def _rwkv_scan_kernel(wf_ref, wb_ref, kaf_ref, kab_ref, kdf_ref, kdb_ref, nkkf_ref, nkkb_ref,
                      rf_ref, rb_ref, vvf_ref, vvb_ref, s0_ref, yf_ref, yb_ref, s_ref, *,
                      time_blocks, v_rows, unroll):
    @pl.when(pl.program_id(0) == 0)
    def _():
        s_ref[...] = s0_ref[...]

    dirs = ((wf_ref, kaf_ref, kdf_ref, nkkf_ref, rf_ref, vvf_ref, yf_ref),
            (wb_ref, kab_ref, kdb_ref, nkkb_ref, rb_ref, vvb_ref, yb_ref))
    n_k = s_ref.shape[1]

    for u in range(time_blocks):
        def step(t, carry, u=u):
            for g, (w_ref, ka_ref, kd_ref, nkk_ref, r_ref, vv_ref, y_ref) in enumerate(dirs):
                ub = u if g == 0 else time_blocks - 1 - u
                tt = t if g == 0 else TIME_BLOCK - 1 - t
                lanes = slice(g * LANES, (g + 1) * LANES)
                at_t = pl.ds(tt, n_k, stride=TIME_BLOCK)
                w = w_ref[ub, at_t, :]
                ka = ka_ref[ub, at_t, :]
                kd = kd_ref[ub, at_t, :]
                nkk = nkk_ref[ub, at_t, :]
                r = r_ref[ub, at_t, :]

                def rows(vb, c):
                    v0 = pl.multiple_of(vb * SUBLANES, SUBLANES)
                    rows_at_t = pl.ds(v0 * TIME_BLOCK + tt, SUBLANES, stride=TIME_BLOCK)
                    vv = vv_ref[ub, rows_at_t, :]
                    ys = []
                    for j in range(SUBLANES):
                        s = s_ref[v0 + j, :, lanes]
                        sa = jnp.sum(s * nkk, axis=0, keepdims=True)
                        s_new = s * w + ka * sa + kd * vv[j:j + 1]
                        s_ref[v0 + j, :, lanes] = s_new
                        ys.append(jnp.sum(s_new * r, axis=0, keepdims=True))
                    y_ref[ub, rows_at_t, :] = jnp.concatenate(ys, axis=0)
                    return c

                lax.fori_loop(0, v_rows // SUBLANES, rows, 0, unroll=unroll)
            return carry

        lax.fori_loop(0, TIME_BLOCK, step, 0)


SCAN_STEPS_PER_BLOCK = 16


def _rwkv_scan_packed(kvecs, vv, s0, *, steps):
    K = kvecs.shape[2] // TIME_BLOCK
    L = kvecs.shape[1] * TIME_BLOCK
    VR = vv.shape[1] // TIME_BLOCK
    nblk = L // steps
    tbs = steps // TIME_BLOCK

    def kspec(name, mirrored):
        a = SCAN_IDX[name]
        if mirrored:
            return pl.BlockSpec((None, tbs, K * TIME_BLOCK, LANES), lambda i: (a, nblk - 1 - i, 0, 0))
        return pl.BlockSpec((None, tbs, K * TIME_BLOCK, LANES), lambda i: (a, i, 0, 0))

    vf = pl.BlockSpec((tbs, VR * TIME_BLOCK, LANES), lambda i: (i, 0, 0))
    vb = pl.BlockSpec((tbs, VR * TIME_BLOCK, LANES), lambda i: (nblk - 1 - i, 0, 0))
    sspec = pl.BlockSpec((VR, K, 2 * LANES), lambda i: (0, 0, 0))
    kern = functools.partial(_rwkv_scan_kernel, time_blocks=tbs, v_rows=VR,
                             unroll=min(2, VR // SUBLANES))
    return pl.pallas_call(
        kern,
        grid=(nblk,),
        in_specs=[kspec('w_f', False), kspec('w_b', True), kspec('ka_f', False), kspec('ka_b', True),
                  kspec('kd_f', False), kspec('kd_b', True), kspec('nkk', False), kspec('nkk', True),
                  kspec('r', False), kspec('r', True), vf, vb, sspec],
        out_specs=[vf, vb, sspec],
        out_shape=[jax.ShapeDtypeStruct((L // TIME_BLOCK, VR * TIME_BLOCK, LANES), f32),
                   jax.ShapeDtypeStruct((L // TIME_BLOCK, VR * TIME_BLOCK, LANES), f32),
                   jax.ShapeDtypeStruct((VR, K, 2 * LANES), f32)],
        compiler_params=_cparams(1),
        name="rwkv7_scan",
    )(*([kvecs] * 10), vv, vv, s0)


def _rwkv_scan_path(stacked, seq0, B, L, S0):
    H, K = N_RWKV_HEADS, RWKV_HEAD_DIM
    nd = 2
    reps = LANES // (B * H)
    VR = K // reps
    kvecs = _pack(stacked, 0, N_SCAN_ARRAYS - 1, B, L, seq0, reps, fold_rows=False)
    vv = _pack(stacked, SCAN_IDX['v'], 1, B, L, seq0, reps, fold_rows=True)[0]
    if S0 is None:
        s0 = jnp.zeros((VR, K, nd * LANES), f32)
    else:
        s0 = S0.astype(f32).reshape(B, nd, H, VR, reps, K)
        s0 = jnp.transpose(s0, (3, 5, 1, 4, 0, 2)).reshape(VR, K, nd * LANES)

    y_f, y_b, s_fin = _rwkv_scan_packed(kvecs, vv, s0, steps=SCAN_STEPS_PER_BLOCK)
    y_f, y_b = _unpack(y_f, y_b, B, L, reps)
    s_fin = s_fin.reshape(VR, K, nd, reps, B, H)
    s_fin = jnp.transpose(s_fin, (4, 2, 5, 0, 3, 1)).reshape(B, nd, H, K, K)
    return y_f, y_b, s_fin


ATT_TQ = 256
HEAD_W = 2 * DIFF_HEAD_DIM


def _rope(x, cos, sin_signed, lo_mask):
    n = x.shape[1]
    partner = jnp.where(lo_mask, pltpu.roll(x, n - DIFF_HEAD_DIM // 4, axis=1),
                        pltpu.roll(x, DIFF_HEAD_DIM // 4, axis=1))
    return x * cos + partner * sin_signed


def _attn_heads(q, k, v, lam, subln_g, out_scale):
    lane = lax.broadcasted_iota(jnp.int32, (1, HEAD_W), 1)
    first = lane < DIFF_HEAD_DIM
    outs = []
    for h in range(N_DIFF_HEADS):
        cols = slice(h * HEAD_W, (h + 1) * HEAD_W)
        qh = (q[:, cols] * (DIFF_HEAD_DIM ** -0.5)).astype(bf16)
        kh = k[:, cols]
        vh = v[:, cols].astype(bf16)
        probs = []
        for m in range(2):
            km = jnp.where(first if m == 0 else ~first, kh, 0.0).astype(bf16)
            s = lax.dot_general(qh, km, (((1,), (1,)), ((), ())), preferred_element_type=f32)
            s = s - jnp.max(s, axis=-1, keepdims=True)
            e = jnp.exp(s)
            probs.append(e / jnp.sum(e, axis=-1, keepdims=True))
        attn = (probs[0] - lam * probs[1]).astype(bf16)
        o = jnp.dot(attn, vh, preferred_element_type=f32)
        o = o * lax.rsqrt(jnp.mean(o * o, axis=-1, keepdims=True) + NORM_EPS) * subln_g * out_scale
        outs.append(o)
    return jnp.concatenate(outs, axis=1)


def _attn_kernel(q_ref, k_ref, v_ref, ck_ref, cv_ref, cos_ref, sin_ref, lv_ref, g_ref, o_ref,
                 kall_ref, vall_ref, *, lam_init):
    blk = pl.program_id(0)
    qt = pl.program_id(1)
    lv = lv_ref[...]
    lam = (jnp.exp(jnp.sum(lv[0:1] * lv[1:2], axis=-1, keepdims=True))
           - jnp.exp(jnp.sum(lv[2:3] * lv[3:4], axis=-1, keepdims=True)) + lam_init)
    subln_g = g_ref[...]
    out_scale = 1.0 - lam_init
    row0 = pl.multiple_of(qt * ATT_TQ, ATT_TQ)

    @pl.when(blk < N_CTX_BLOCKS)
    def _():
        q = q_ref[pl.ds(row0, ATT_TQ), :]
        k = k_ref[pl.ds(row0, ATT_TQ), :]
        v = v_ref[pl.ds(row0, ATT_TQ), :]
        o_ref[...] = _attn_heads(q, k, v, lam, subln_g, out_scale).astype(o_ref.dtype)

    @pl.when(blk >= N_CTX_BLOCKS)
    def _():
        lane = lax.broadcasted_iota(jnp.int32, (1, D_DIFF), 1)
        lo_mask = (lane % (DIFF_HEAD_DIM // 2)) < (DIFF_HEAD_DIM // 4)

        @pl.when(qt == 0)
        def _():
            kall_ref[0:LAT_LEN, :] = _rope(k_ref[...], cos_ref[...], sin_ref[...], lo_mask)
            kall_ref[LAT_LEN:LAT_LEN + PAST_LEN, :] = ck_ref[0]
            vall_ref[0:LAT_LEN, :] = v_ref[...]
            vall_ref[LAT_LEN:LAT_LEN + PAST_LEN, :] = cv_ref[0]

        q = _rope(q_ref[pl.ds(row0, ATT_TQ), :], cos_ref[pl.ds(row0, ATT_TQ), :],
                  sin_ref[pl.ds(row0, ATT_TQ), :], lo_mask)
        o_ref[...] = _attn_heads(q, kall_ref[...], vall_ref[...], lam, subln_g,
                                 out_scale).astype(o_ref.dtype)


def _attention(z, cache_k_l, cache_v_l, rope_cos, rope_sin, p, lam_init):
    zcol = lambda off: pl.BlockSpec((SEQ_BLOCK, D_DIFF), lambda i, j, off=off: (i, off // D_DIFF))
    cache = pl.BlockSpec((1, PAST_LEN, D_DIFF), lambda i, j: (jnp.maximum(i - N_CTX_BLOCKS, 0), 0, 0))
    full = lambda shape: pl.BlockSpec(shape, lambda i, j: (0,) * len(shape))
    return pl.pallas_call(
        functools.partial(_attn_kernel, lam_init=lam_init),
        grid=(N_SEQ_BLOCKS, SEQ_BLOCK // ATT_TQ),
        in_specs=[zcol(Z_Q), zcol(Z_K), zcol(Z_V), cache, cache,
                  full((LAT_LEN, D_DIFF)), full((LAT_LEN, D_DIFF)),
                  full((4, DIFF_HEAD_DIM)), full((1, HEAD_W))],
        out_specs=pl.BlockSpec((ATT_TQ, D_DIFF), lambda i, j: (i * (SEQ_BLOCK // ATT_TQ) + j, 0)),
        out_shape=jax.ShapeDtypeStruct((N_TOK, D_DIFF), bf16),
        scratch_shapes=[pltpu.VMEM((LAT_LEN + PAST_LEN, D_DIFF), f32),
                        pltpu.VMEM((LAT_LEN + PAST_LEN, D_DIFF), f32)],
        compiler_params=_cparams(2),
        name="diff_attention",
    )(z, z, z, cache_k_l, cache_v_l, rope_cos, rope_sin, p['diff_lambda'], p['diff_subln_g'])


def _rope_tables():
    t = jnp.arange(LAT_LEN)
    row = (t // GRID_W).astype(f32)
    col = (t % GRID_W).astype(f32)
    nf = DIFF_HEAD_DIM // 4
    freqs = ROPE_THETA ** (-jnp.arange(nf, dtype=f32) / nf)
    d = jnp.arange(DIFF_HEAD_DIM)
    pos = jnp.where((d < DIFF_HEAD_DIM // 2)[None, :], row[:, None], col[:, None])
    ang = pos * freqs[d % nf][None, :]
    sign = jnp.where((d % (2 * nf)) < nf, -1.0, 1.0)[None, :]
    cos = jnp.tile(jnp.cos(ang), (1, D_DIFF // DIFF_HEAD_DIM))
    sin = jnp.tile(jnp.sin(ang) * sign, (1, D_DIFF // DIFF_HEAD_DIM))
    return cos, sin


MERGE_TM = 512


def _merge_kernel(x_ref, lru_ref, yfc_ref, ybc_ref, yfl_ref, ybl_ref, bonus_ref, g_ref, diff_ref,
                  gp0_ref, gp1_ref, gp2_ref,
                  mod_ref, lnp_ref, ones_ref, wlo_ref, wro_ref, wdo_ref, wout_ref, ng_ref, rhi_ref, rlo_ref,
                  xo_ref, h2_ref, logit_ref):
    is_ctx = pl.program_id(0) < N_CTX_TOK // MERGE_TM
    y = jnp.where(is_ctx, yfc_ref[...] + ybc_ref[...], yfl_ref[...] + ybl_ref[...])
    inv = 1.0 / RWKV_HEAD_DIM
    mu = _split_dot(y, ones_ref[...]) * inv
    dlt = y - mu
    var = _split_dot(dlt * dlt, ones_ref[...]) * inv
    yn = dlt * lax.rsqrt(var + RWKV_LN_EPS)
    rwkv_out = ((yn * lnp_ref[0:1, :] + lnp_ref[1:2, :] + bonus_ref[...]) * g_ref[...]).astype(bf16)

    merged = (jax.nn.sigmoid(gp0_ref[...]) * jnp.dot(lru_ref[...], wlo_ref[...], preferred_element_type=f32)
              + jax.nn.sigmoid(gp1_ref[...]) * jnp.dot(rwkv_out, wro_ref[...], preferred_element_type=f32)
              + jax.nn.sigmoid(gp2_ref[...]) * jnp.dot(diff_ref[...], wdo_ref[...], preferred_element_type=f32))
    mix = jnp.dot(merged.astype(bf16), wout_ref[...], preferred_element_type=f32)

    gate1 = mod_ref[0, :, 2 * D_MODEL:3 * D_MODEL]
    shift2 = mod_ref[0, :, 3 * D_MODEL:4 * D_MODEL]
    scale2 = mod_ref[0, :, 4 * D_MODEL:5 * D_MODEL]
    x = x_ref[...] + gate1 * mix
    xo_ref[...] = x
    yn2 = x * lax.rsqrt(jnp.mean(x * x, axis=-1, keepdims=True) + NORM_EPS) * ng_ref[...]
    h2 = yn2 * (1.0 + scale2) + shift2
    hi = h2.astype(bf16)
    lo = (h2 - hi.astype(f32)).astype(bf16)
    h2_ref[...] = hi
    logit_ref[...] = (jnp.dot(hi, rhi_ref[...], preferred_element_type=f32)
                      + jnp.dot(lo, rhi_ref[...], preferred_element_type=f32)
                      + jnp.dot(hi, rlo_ref[...], preferred_element_type=f32))


def _merge(x, z, lru_out, y_ctx, y_lat, bonus, g, diff_out, mod_l, p):
    n_ctx_tiles = N_CTX_TOK // MERGE_TM
    row = lambda w: pl.BlockSpec((MERGE_TM, w), lambda i: (i, 0))
    ctx_row = pl.BlockSpec((MERGE_TM, D_RWKV), lambda i: (jnp.minimum(i, n_ctx_tiles - 1), 0))
    lat_row = pl.BlockSpec((MERGE_TM, D_RWKV), lambda i: (jnp.maximum(i - n_ctx_tiles, 0), 0))
    gate = lambda b: pl.BlockSpec((MERGE_TM, D_MODEL), lambda i, b=b: (i, Z_GATES // D_MODEL + b))
    full = lambda shape: pl.BlockSpec(shape, lambda i: (0,) * len(shape))
    return pl.pallas_call(
        _merge_kernel,
        grid=(N_TOK // MERGE_TM,),
        in_specs=[row(D_MODEL), row(D_LRU), ctx_row, ctx_row, lat_row, lat_row,
                  row(D_RWKV), row(D_RWKV), row(D_DIFF),
                  gate(0), gate(1), gate(2),
                  pl.BlockSpec((1, 1, 6 * D_MODEL), lambda i: (_mod_group_of_rows(i, MERGE_TM), 0, 0)),
                  full((2, D_RWKV)), full((D_RWKV, D_RWKV)),
                  full((D_LRU, D_MODEL)), full((D_RWKV, D_MODEL)), full((D_DIFF, D_MODEL)),
                  full((D_MODEL, D_MODEL)), full((1, D_MODEL)),
                  full((D_MODEL, ROUTER_PAD)), full((D_MODEL, ROUTER_PAD))],
        out_specs=[row(D_MODEL), row(D_MODEL), row(ROUTER_PAD)],
        out_shape=[jax.ShapeDtypeStruct((N_TOK, D_MODEL), f32),
                   jax.ShapeDtypeStruct((N_TOK, D_MODEL), bf16),
                   jax.ShapeDtypeStruct((N_TOK, ROUTER_PAD), f32)],
        compiler_params=_cparams(1),
        name="merge",
    )(x, lru_out, *y_ctx, *y_lat, bonus, g, diff_out, z, z, z, mod_l, p['rwkv_lnp'], p['head_ones'],
      p['w_lru_out'], p['w_rwkv_out'], p['w_diff_out'], p['w_out'], p['norm_g2'],
      p['router_hi'], p['router_lo'])


CAP_CTX = EC_CAPACITY * N_CTX_TOK // N_EXPERTS
CAP_LAT = EC_CAPACITY * (N_TOK - N_CTX_TOK) // N_EXPERTS
N_SLOTS = CAP_CTX + CAP_LAT
F32_MANTISSA_BITS = 23
F32_EXPONENT_BIAS = 127
LADDER_STEPS = 8
BISECT_STEPS = 60


def _cumsum_lanes(x01):
    rows, n = x01.shape
    i = lax.broadcasted_iota(jnp.int32, (LANES, LANES), 0)
    j = lax.broadcasted_iota(jnp.int32, (LANES, LANES), 1)
    upper = jnp.where(i <= j, 1.0, 0.0).astype(bf16)
    off = jnp.zeros((rows, 1), f32)
    outs = []
    for t in range(n // LANES):
        c = jnp.dot(x01[:, t * LANES:(t + 1) * LANES].astype(bf16), upper, preferred_element_type=f32) + off
        outs.append(c)
        off = c[:, LANES - 1:LANES]
    return jnp.concatenate(outs, axis=1)


def _route_kernel(logit_ref, pos_ref, gate_ref, *, cap, slot0):
    n = logit_ref.shape[0]
    lane = lax.broadcasted_iota(jnp.int32, (1, ROUTER_PAD), 1)
    lg = jnp.where(lane < N_EXPERTS, logit_ref[...], -jnp.inf)
    ex = jnp.exp(lg - jnp.max(lg, axis=-1, keepdims=True))
    aff = (ex / jnp.sum(ex, axis=-1, keepdims=True)).T[:N_EXPERTS]

    def n_above(thr):
        return jnp.sum(jnp.where(aff > thr, 1.0, 0.0), axis=1, keepdims=True)

    def rung(j):
        return pltpu.bitcast(jnp.left_shift(j, F32_MANTISSA_BITS), f32)

    def pick_rung(_, jj):
        j_lo, j_hi = jj
        j_mid = (j_lo + j_hi) >> 1
        enough = n_above(rung(j_mid)) >= cap
        return jnp.where(enough, j_mid, j_lo), jnp.where(enough, j_hi, j_mid)

    j_lo, j_hi = lax.fori_loop(0, LADDER_STEPS, pick_rung,
                               (jnp.full((N_EXPERTS, 1), -1, jnp.int32),
                                jnp.full((N_EXPERTS, 1), F32_EXPONENT_BIAS + 1, jnp.int32)))

    def bisect(_, lh):
        lo, hi = lh
        mid = 0.5 * (lo + hi)
        enough = n_above(mid) >= cap
        return jnp.where(enough, mid, lo), jnp.where(enough, hi, mid)

    lo, hi = lax.fori_loop(0, BISECT_STEPS, bisect, (jnp.where(j_lo < 0, -1.0, rung(j_lo)), rung(j_hi)))
    above = aff > hi
    tie = (aff > lo) & (aff <= hi)
    need = cap - n_above(hi)
    keep = above | (tie & (_cumsum_lanes(jnp.where(tie, 1.0, 0.0)) <= need))
    slot = _cumsum_lanes(jnp.where(keep, 1.0, 0.0)) - 1.0 + slot0
    pos_ref[...] = jnp.where(keep, slot, -1.0).astype(jnp.int32)
    gate_ref[...] = jnp.where(keep, aff, 0.0)


def _route(logits, block, n, cap, slot0):
    shape = (N_EXPERTS, n)
    pos, gate = pl.pallas_call(
        functools.partial(_route_kernel, cap=cap, slot0=slot0),
        grid=(1,),
        in_specs=[pl.BlockSpec((n, ROUTER_PAD), lambda i: (block, 0))],
        out_specs=[pl.BlockSpec(shape, lambda i: (0, 0))] * 2,
        out_shape=[jax.ShapeDtypeStruct(shape, jnp.int32), jax.ShapeDtypeStruct(shape, f32)],
        compiler_params=_cparams(1),
        name="route",
    )(logits)
    return pos.reshape(N_EXPERTS, 1, n), gate.reshape(N_EXPERTS, 1, n)


FFN_TF = 512
MOE_TT = 1024


def _one_hot(pos_row, slot0, n_slots):
    slots = lax.broadcasted_iota(jnp.int32, (n_slots, pos_row.shape[1]), 0) + slot0
    return pos_row == slots


def _slots_of_tile(tile):
    return (0, CAP_CTX) if tile * MOE_TT < N_CTX_TOK else (CAP_CTX, CAP_LAT)


def _ffn_kernel(pos_ref, gate_ref, h_ref, wg_ref, wu_ref, wd_ref, o_ref, xe_ref, gs_ref, acc_ref):
    f = pl.program_id(1)

    @pl.when(f == 0)
    def _():
        xe_ref[...] = jnp.zeros_like(xe_ref)
        gs_ref[...] = jnp.zeros_like(gs_ref)
        for tile in range(N_TOK // MOE_TT):
            slot0, n_slots = _slots_of_tile(tile)
            tok = slice(tile * MOE_TT, (tile + 1) * MOE_TT)
            sel = _one_hot(pos_ref[0, :, tok], slot0, n_slots)
            rows = slice(slot0, slot0 + n_slots)
            xe_ref[rows, :] += jnp.dot(jnp.where(sel, 1.0, 0.0).astype(bf16), h_ref[tok, :],
                                       preferred_element_type=f32)
            gs_ref[rows, :] += jnp.sum(jnp.where(sel, gate_ref[0, :, tok], 0.0), axis=1, keepdims=True)

    x = xe_ref[...].astype(bf16)
    gate = jnp.dot(x, wg_ref[0, 0].astype(bf16), preferred_element_type=f32)
    up = jnp.dot(x, wu_ref[0, 0].astype(bf16), preferred_element_type=f32)
    hid = (gate * jax.nn.sigmoid(gate) * up).astype(bf16)
    part = jnp.dot(hid, wd_ref[0, 0].astype(bf16), preferred_element_type=f32)

    @pl.when(f == 0)
    def _():
        acc_ref[...] = part

    @pl.when(f != 0)
    def _():
        acc_ref[...] += part

    @pl.when(f == pl.num_programs(1) - 1)
    def _():
        o_ref[0] = (acc_ref[...] * gs_ref[...]).astype(o_ref.dtype)


def _expert_ffn(pos, gate, h2, w_gate, w_up, w_down, layer):
    tok_row = pl.BlockSpec((1, 1, N_TOK), lambda e, f: (e, 0, 0))
    return pl.pallas_call(
        _ffn_kernel,
        grid=(N_EXPERTS, EXPERT_FF // FFN_TF),
        in_specs=[tok_row, tok_row,
                  pl.BlockSpec((N_TOK, D_MODEL), lambda e, f: (0, 0), pipeline_mode=pl.Buffered(1)),
                  pl.BlockSpec((1, 1, D_MODEL, FFN_TF), lambda e, f: (layer, e, 0, f)),
                  pl.BlockSpec((1, 1, D_MODEL, FFN_TF), lambda e, f: (layer, e, 0, f)),
                  pl.BlockSpec((1, 1, FFN_TF, D_MODEL), lambda e, f: (layer, e, f, 0))],
        out_specs=pl.BlockSpec((1, N_SLOTS, D_MODEL), lambda e, f: (e, 0, 0)),
        out_shape=jax.ShapeDtypeStruct((N_EXPERTS, N_SLOTS, D_MODEL), bf16),
        scratch_shapes=[pltpu.VMEM((N_SLOTS, D_MODEL), f32),
                        pltpu.VMEM((N_SLOTS, 1), f32),
                        pltpu.VMEM((N_SLOTS, D_MODEL), f32)],
        compiler_params=_cparams(2),
        name="expert_ffn",
    )(pos, gate, h2, w_gate, w_up, w_down)


def _combine_kernel(x_ref, pos_ref, ye_ref, mod_ref, g_ref, xo_ref, acc_ref, *, final_norm):
    tile = pl.program_id(0)
    e = pl.program_id(1)

    @pl.when(e == 0)
    def _():
        acc_ref[...] = jnp.zeros_like(acc_ref)

    def scatter(slot0, n_slots):
        sel = jnp.where(_one_hot(pos_ref[0], slot0, n_slots), 1.0, 0.0).astype(bf16)
        acc_ref[...] += lax.dot_general(sel, ye_ref[0, slot0:slot0 + n_slots, :], (((0,), (0,)), ((), ())),
                                        preferred_element_type=f32)

    @pl.when(tile * MOE_TT < N_CTX_TOK)
    def _():
        scatter(0, CAP_CTX)

    @pl.when(tile * MOE_TT >= N_CTX_TOK)
    def _():
        scatter(CAP_CTX, CAP_LAT)

    @pl.when(e == pl.num_programs(1) - 1)
    def _():
        gate2 = mod_ref[0, :, 5 * D_MODEL:6 * D_MODEL]
        x = x_ref[...] + gate2 * acc_ref[...]
        if final_norm:
            x = x * lax.rsqrt(jnp.mean(x * x, axis=-1, keepdims=True) + NORM_EPS) * g_ref[...]
        xo_ref[...] = x


def _combine(x, pos, ye, mod_l, final_g, final_norm):
    row = pl.BlockSpec((MOE_TT, D_MODEL), lambda i, e: (i, 0))
    return pl.pallas_call(
        functools.partial(_combine_kernel, final_norm=final_norm),
        grid=(N_TOK // MOE_TT, N_EXPERTS),
        in_specs=[row,
                  pl.BlockSpec((1, 1, MOE_TT), lambda i, e: (e, 0, i)),
                  pl.BlockSpec((1, N_SLOTS, D_MODEL), lambda i, e: (e, 0, 0)),
                  pl.BlockSpec((1, 1, 6 * D_MODEL), lambda i, e: (_mod_group_of_rows(i, MOE_TT), 0, 0)),
                  pl.BlockSpec((1, D_MODEL), lambda i, e: (0, 0))],
        out_specs=row,
        out_shape=jax.ShapeDtypeStruct((N_TOK, D_MODEL), f32),
        scratch_shapes=[pltpu.VMEM((MOE_TT, D_MODEL), f32)],
        compiler_params=_cparams(2),
        name="moe_combine",
    )(x, pos, ye, mod_l, final_g)


def _moe_residual(x, h2, logits, w_gate, w_up, w_down, layer, mod_l, final_g, final_norm):
    pos_c, gate_c = _route(logits, 0, N_CTX_TOK, CAP_CTX, 0)
    lat_n = N_TOK - N_CTX_TOK
    pos_l, gate_l = _route(logits, N_CTX_TOK // lat_n, lat_n, CAP_LAT, CAP_CTX)
    pos = jnp.concatenate([pos_c, pos_l], axis=2)
    gate = jnp.concatenate([gate_c, gate_l], axis=2)
    ye = _expert_ffn(pos, gate, h2, w_gate, w_up, w_down, layer)
    return _combine(x, pos, ye, mod_l, final_g, final_norm)


def _block_diag(blocks):
    n, a, b = blocks.shape
    eye = jnp.eye(n, dtype=blocks.dtype)
    return (eye[:, None, :, None] * blocks[:, :, None, :]).reshape(n * a, n * b)


def _prep_layer(l, w):
    p = {}
    w_in = w['w_in'][l]
    lru_x, lru_g, zr, q, k, v, gates = (
        w_in[:, 0:D_LRU], w_in[:, D_LRU:2 * D_LRU], w_in[:, 2 * D_LRU:2 * D_LRU + RWKV_COLS],
        w_in[:, 2 * D_LRU + RWKV_COLS:2 * D_LRU + RWKV_COLS + D_DIFF],
        w_in[:, 2 * D_LRU + RWKV_COLS + D_DIFF:2 * D_LRU + RWKV_COLS + 2 * D_DIFF],
        w_in[:, 2 * D_LRU + RWKV_COLS + 2 * D_DIFF:2 * D_LRU + RWKV_COLS + 3 * D_DIFF],
        w_in[:, 2 * D_LRU + RWKV_COLS + 3 * D_DIFF:])
    pad = jnp.zeros((D_MODEL, RWKV_COLS_PAD - RWKV_COLS), f32)
    p['w_in'] = jnp.concatenate([zr, pad, gates, lru_x, lru_g, q, k, v], axis=1).astype(bf16)
    p['norm_g1'] = w['norm_g'][l, 0].reshape(1, D_MODEL)
    p['norm_g2'] = w['norm_g'][l, 1].reshape(1, D_MODEL)

    p['lru_conv_w'] = w['lru_conv_w'][l]
    p['lru_conv_b'] = w['lru_conv_b'][l].reshape(1, D_LRU)
    p['lru_wg'] = jnp.concatenate(
        [_block_diag(w[name][l, d]) for d in range(2) for name in ('lru_wa', 'lru_wx')], axis=1).astype(bf16)
    p['lru_bg'] = jnp.concatenate(
        [w[name][l, d] for d in range(2) for name in ('lru_ba', 'lru_bx')]).reshape(1, 4 * D_LRU)
    p['lru_lambda'] = w['lru_lambda'][l]

    p['rwkv_conv_w'] = jnp.pad(w['rwkv_conv_w'][l], ((0, 0), (0, RWKV_COLS_PAD - RWKV_COLS)))

    def lowrank_pair(m):
        rank = m.shape[1]
        fwd, bwd = (m[d].reshape(rank, N_RWKV_CT, RWKV_CT).transpose(1, 0, 2) for d in range(2))
        zero = jnp.zeros_like(fwd)
        return jnp.concatenate([jnp.concatenate([fwd, zero], axis=2),
                                jnp.concatenate([zero, bwd], axis=2)], axis=1).astype(bf16)

    p['rwkv_w2'] = lowrank_pair(w['rwkv_w2'][l])
    p['rwkv_a2'] = lowrank_pair(w['rwkv_a2'][l])
    p['rwkv_g2'] = w['rwkv_g2'][l].astype(bf16)
    p['rwkv_pc'] = jnp.concatenate(
        [w['rwkv_w0'][l], w['rwkv_a0'][l], w['rwkv_k_k'][l][None], w['rwkv_k_a'][l][None],
         w['rwkv_r_k'][l].reshape(1, D_RWKV), jnp.zeros((1, D_RWKV), f32)], axis=0)
    p['rwkv_lnp'] = jnp.stack([w['rwkv_ln_g'][l], w['rwkv_ln_b'][l]], axis=0)
    p['head_ones'] = _block_diag(jnp.ones((N_RWKV_HEADS, RWKV_HEAD_DIM, RWKV_HEAD_DIM), bf16))

    p['diff_lambda'] = w['diff_lambda'][l]
    p['diff_subln_g'] = w['diff_subln_g'][l].reshape(1, HEAD_W)
    p['w_lru_out'] = w['w_lru_out'][l].astype(bf16)
    p['w_rwkv_out'] = w['w_rwkv_out'][l].astype(bf16)
    p['w_diff_out'] = w['w_diff_out'][l].astype(bf16)
    p['w_out'] = w['w_out'][l].astype(bf16)
    rw = jnp.pad(w['router_w'][l], ((0, 0), (0, ROUTER_PAD - N_EXPERTS)))
    p['router_hi'] = rw.astype(bf16)
    p['router_lo'] = (rw - p['router_hi'].astype(f32)).astype(bf16)
    return p


def kernel(x_prompt, x_sample, cache_k, cache_v, state_lru, state_rwkv, c, c_ctx, norm_g, final_norm_g, w_mod, b_mod, w_in, lru_conv_w, lru_conv_b, lru_wa, lru_ba, lru_wx, lru_bx, lru_lambda, rwkv_conv_w, rwkv_w0, rwkv_w2, rwkv_a0, rwkv_a2, rwkv_k_k, rwkv_k_a, rwkv_r_k, rwkv_g2, rwkv_ln_g, rwkv_ln_b, diff_lambda, diff_subln_g, w_lru_out, w_rwkv_out, w_diff_out, w_out, router_w, exp_w_gate, exp_w_up, exp_w_down):
    w = dict(norm_g=norm_g, w_in=w_in, lru_conv_w=lru_conv_w, lru_conv_b=lru_conv_b, lru_wa=lru_wa,
             lru_ba=lru_ba, lru_wx=lru_wx, lru_bx=lru_bx, lru_lambda=lru_lambda, rwkv_conv_w=rwkv_conv_w,
             rwkv_w0=rwkv_w0, rwkv_w2=rwkv_w2, rwkv_a0=rwkv_a0, rwkv_a2=rwkv_a2, rwkv_k_k=rwkv_k_k,
             rwkv_k_a=rwkv_k_a, rwkv_r_k=rwkv_r_k, rwkv_g2=rwkv_g2, rwkv_ln_g=rwkv_ln_g,
             rwkv_ln_b=rwkv_ln_b, diff_lambda=diff_lambda, diff_subln_g=diff_subln_g, w_lru_out=w_lru_out,
             w_rwkv_out=w_rwkv_out, w_diff_out=w_diff_out, w_out=w_out, router_w=router_w)

    x = jnp.concatenate([x_prompt.reshape(N_CTX_TOK, D_MODEL), x_sample.reshape(-1, D_MODEL)], axis=0)
    cvec = jnp.concatenate([c_ctx[None], c, jnp.zeros((SUBLANES - N_MOD_GROUPS, D_MODEL), f32)], axis=0)
    mod = _modulation(cvec, w_mod, b_mod)
    rope_cos, rope_sin = _rope_tables()
    final_g = final_norm_g.reshape(1, D_MODEL)
    ctx_rows = slice(0, N_CTX_TOK)
    lat_rows = slice(N_CTX_TOK, N_TOK)

    new_k, new_v, new_lru, new_rwkv = [], [], [], []
    for l in range(DEPTH):
        p = _prep_layer(l, w)
        lam_init = 0.8 - 0.6 * math.exp(-0.3 * l)
        mod_l = mod[l].reshape(SUBLANES, 1, 6 * D_MODEL)

        z = _in_projection(x, mod_l, p['norm_g1'], p['w_in'])
        new_k.append(z[ctx_rows, Z_K:Z_K + D_DIFF].reshape(N_CTX_SEQ, CTX_LEN, N_DIFF_HEADS, 2, DIFF_HEAD_DIM))
        new_v.append(z[ctx_rows, Z_V:Z_V + D_DIFF].reshape(N_CTX_SEQ, CTX_LEN, N_DIFF_HEADS, HEAD_W))

        h0_blocks = jnp.concatenate(
            [jnp.zeros((N_CTX_BLOCKS, 2, D_LRU), f32), state_lru[:, l].astype(f32)], axis=0)
        lru_out, lru_fin = _lru_branch(z, p, h0_blocks)
        n_per = SEQ_BLOCK // CTX_LEN
        fin = lru_fin[:N_CTX_BLOCKS].reshape(N_CTX_BLOCKS, 2, n_per, D_LRU)
        new_lru.append(jnp.transpose(fin, (0, 2, 1, 3)).reshape(N_CTX_SEQ, 2, D_LRU))

        scan_in, bonus, out_gate = _rwkv_pre(z, p)
        yf_c, yb_c, s_c = _rwkv_scan_path(scan_in, 0, N_CTX_SEQ, CTX_LEN, None)
        yf_l, yb_l, _ = _rwkv_scan_path(scan_in, N_CTX_TOK // LAT_LEN, N_LAT_SEQ, LAT_LEN, state_rwkv[:, l])
        new_rwkv.append(s_c)

        diff_out = _attention(z, cache_k[:, l].reshape(N_LAT_SEQ, PAST_LEN, D_DIFF),
                              cache_v[:, l].reshape(N_LAT_SEQ, PAST_LEN, D_DIFF),
                              rope_cos, rope_sin, p, lam_init)

        x, h2, logits = _merge(x, z, lru_out, (yf_c, yb_c), (yf_l, yb_l), bonus, out_gate, diff_out, mod_l, p)
        x = _moe_residual(x, h2, logits, exp_w_gate, exp_w_up, exp_w_down, l, mod_l, final_g,
                          final_norm=(l == DEPTH - 1))

    y_prompt = x[ctx_rows].reshape(x_prompt.shape)
    y_sample = x[lat_rows].reshape(x_sample.shape)
    return (y_prompt, y_sample, jnp.stack(new_k, axis=1), jnp.stack(new_v, axis=1),
            jnp.stack(new_lru, axis=1), jnp.stack(new_rwkv, axis=1))
```

```python
import functools
import math

import jax
import jax.numpy as jnp
from jax import lax
from jax.experimental import pallas as pl
from jax.experimental.pallas import tpu as pltpu

f32 = jnp.float32
bf16 = jnp.bfloat16

D_MODEL = 1024
DEPTH = 2
GRID_W = 64
NORM_EPS = 1e-6

N_CTX_SEQ = 16
CTX_LEN = 256
N_LAT_SEQ = 2
LAT_LEN = 1024
PAST_LEN = 256
N_CTX_TOK = N_CTX_SEQ * CTX_LEN
N_TOK = N_CTX_TOK + N_LAT_SEQ * LAT_LEN

D_LRU = 512
LRU_BLOCKS = 8
LRU_BLOCK_W = D_LRU // LRU_BLOCKS
LRU_C = 8.0

D_RWKV = 512
RWKV_HEAD_DIM = 64
N_RWKV_HEADS = D_RWKV // RWKV_HEAD_DIM
W_RANK = 64
A_RANK = 64
G_RANK = 128
RWKV_LN_EPS = 64e-5
RWKV_COLS = 3 * D_RWKV + 2 * W_RANK + 2 * A_RANK + G_RANK
RWKV_COLS_PAD = 2048

N_DIFF_HEADS = 4
DIFF_HEAD_DIM = 64
D_DIFF = N_DIFF_HEADS * 2 * DIFF_HEAD_DIM
ROPE_THETA = 10000.0

N_BRANCHES = 3
N_EXPERTS = 16
EXPERT_FF = 1024
EC_CAPACITY = 2
ROUTER_PAD = 128

LANES = 128
SUBLANES = 8

SEQ_BLOCK = 1024
N_SEQ_BLOCKS = N_TOK // SEQ_BLOCK
N_CTX_BLOCKS = N_CTX_TOK // SEQ_BLOCK
N_MOD_GROUPS = 1 + N_LAT_SEQ

Z_RWKV = 0
Z_GATES = RWKV_COLS_PAD
Z_LRU_X = Z_GATES + N_BRANCHES * D_MODEL
Z_LRU_G = Z_LRU_X + D_LRU
Z_Q = Z_LRU_G + D_LRU
Z_K = Z_Q + D_DIFF
Z_V = Z_K + D_DIFF
Z_COLS = Z_V + D_DIFF

VMEM_LIMIT = 56 * 1024 * 1024


def _cparams(n_axes):
    return pltpu.CompilerParams(dimension_semantics=("arbitrary",) * n_axes,
                                vmem_limit_bytes=VMEM_LIMIT)


def _seg_len(block_idx):
    return jnp.where(block_idx < N_CTX_BLOCKS, CTX_LEN, LAT_LEN)


def _mod_group_of_rows(row_block, rows_per_block):
    first_lat = N_CTX_TOK // rows_per_block
    per_seq = LAT_LEN // rows_per_block
    return jnp.where(row_block < first_lat, 0, (row_block - first_lat) // per_seq + 1)


def _shift_rows(v, d, tm, seg):
    n = v.shape[0]
    r = pltpu.roll(v, (-d) % n, axis=0)
    ok = (tm + d >= 0) & (tm + d < seg)
    return jnp.where(ok, r, 0.0)


def _split_dot(x, w_bf16):
    hi = x.astype(bf16)
    lo = (x - hi.astype(f32)).astype(bf16)
    return (jnp.dot(hi, w_bf16, preferred_element_type=f32)
            + jnp.dot(lo, w_bf16, preferred_element_type=f32))


MOD_TN = 1536


def _mod_kernel(c_ref, w_ref, b_ref, o_ref):
    c = c_ref[...]
    s = (c * jax.nn.sigmoid(c)).astype(bf16)
    o_ref[0] = jnp.dot(s, w_ref[0].astype(bf16), preferred_element_type=f32) + b_ref[0]


def _modulation(cvec, w_mod, b_mod):
    n_out = w_mod.shape[-1]
    return pl.pallas_call(
        _mod_kernel,
        grid=(DEPTH, n_out // MOD_TN),
        in_specs=[pl.BlockSpec((SUBLANES, D_MODEL), lambda l, j: (0, 0)),
                  pl.BlockSpec((1, D_MODEL, MOD_TN), lambda l, j: (l, 0, j)),
                  pl.BlockSpec((1, 1, MOD_TN), lambda l, j: (l, 0, j))],
        out_specs=pl.BlockSpec((1, SUBLANES, MOD_TN), lambda l, j: (l, 0, j)),
        out_shape=jax.ShapeDtypeStruct((DEPTH, SUBLANES, n_out), f32),
        compiler_params=_cparams(2),
        name="modulation",
    )(cvec, w_mod, b_mod.reshape(DEPTH, 1, n_out))


INPROJ_TM = 1024
INPROJ_TN = 768


def _inproj_kernel(x_ref, mod_ref, g_ref, w_ref, o_ref, h_ref):
    @pl.when(pl.program_id(1) == 0)
    def _():
        x = x_ref[...]
        y = x * lax.rsqrt(jnp.mean(x * x, axis=-1, keepdims=True) + NORM_EPS) * g_ref[...]
        shift = mod_ref[0, :, 0:D_MODEL]
        scale = mod_ref[0, :, D_MODEL:2 * D_MODEL]
        h_ref[...] = (y * (1.0 + scale) + shift).astype(bf16)

    o_ref[...] = jnp.dot(h_ref[...], w_ref[...], preferred_element_type=f32)


def _in_projection(x, mod_l, norm_g_row, w_in_p):
    return pl.pallas_call(
        _inproj_kernel,
        grid=(N_TOK // INPROJ_TM, Z_COLS // INPROJ_TN),
        in_specs=[pl.BlockSpec((INPROJ_TM, D_MODEL), lambda i, j: (i, 0)),
                  pl.BlockSpec((1, 1, 2 * D_MODEL),
                               lambda i, j: (_mod_group_of_rows(i, INPROJ_TM), 0, 0)),
                  pl.BlockSpec((1, D_MODEL), lambda i, j: (0, 0)),
                  pl.BlockSpec((D_MODEL, INPROJ_TN), lambda i, j: (0, j))],
        out_specs=pl.BlockSpec((INPROJ_TM, INPROJ_TN), lambda i, j: (i, j)),
        out_shape=jax.ShapeDtypeStruct((N_TOK, Z_COLS), f32),
        scratch_shapes=[pltpu.VMEM((INPROJ_TM, D_MODEL), bf16)],
        compiler_params=_cparams(2),
        name="in_projection",
    )(x, mod_l, norm_g_row, w_in_p)


LRU_TILE_UNROLL = 4


def _lru_kernel(x_ref, gate_ref, cw_ref, cb_ref, wg_ref, bg_ref, lam_ref, h0_ref, out_ref, hfin_ref,
                a_scr, b_scr, yf_scr, yb_scr):
    blk = pl.program_id(0)
    seg = _seg_len(blk)
    n = x_ref.shape[0]
    t = lax.broadcasted_iota(jnp.int32, (n, D_LRU), 0)
    tm = t & (seg - 1)

    x = x_ref[...]
    xc = (cb_ref[...]
          + cw_ref[0:1, :] * _shift_rows(x, -2, tm, seg)
          + cw_ref[1:2, :] * _shift_rows(x, -1, tm, seg)
          + cw_ref[2:3, :] * x
          + cw_ref[3:4, :] * _shift_rows(x, 1, tm, seg))
    pre = jnp.dot(xc.astype(bf16), wg_ref[...], preferred_element_type=f32) + bg_ref[...]

    n_tiles = n // SUBLANES
    tiles_per_seg = seg // SUBLANES
    row = lax.broadcasted_iota(jnp.int32, (SUBLANES, D_LRU), 0)
    for d, y_scr in enumerate((yf_scr, yb_scr)):
        r = jax.nn.sigmoid(pre[:, (2 * d) * D_LRU:(2 * d + 1) * D_LRU])
        i = jax.nn.sigmoid(pre[:, (2 * d + 1) * D_LRU:(2 * d + 2) * D_LRU])
        log_a = -LRU_C * r * jax.nn.softplus(-lam_ref[d:d + 1, :])
        a = jnp.exp(log_a)
        a_scr[...] = a
        b_scr[...] = jnp.sqrt(1.0 - a * a) * (i * xc)
        h0 = h0_ref[0, d:d + 1, :]

        def tile_step(k, h_prev, d=d, y_scr=y_scr, h0=h0):
            idx = k if d == 0 else n_tiles - 1 - k
            rows = pl.ds(pl.multiple_of(idx * SUBLANES, SUBLANES), SUBLANES)
            a_t = a_scr[rows, :]
            b_t = b_scr[rows, :]
            s = 1
            while s < SUBLANES:
                ok = (row >= s) if d == 0 else (row < SUBLANES - s)
                shift = s if d == 0 else SUBLANES - s
                a_sh = jnp.where(ok, pltpu.roll(a_t, shift, axis=0), 1.0)
                b_sh = jnp.where(ok, pltpu.roll(b_t, shift, axis=0), 0.0)
                b_t = a_t * b_sh + b_t
                a_t = a_t * a_sh
                s *= 2
            first_of_seg = (idx & (tiles_per_seg - 1)) == (0 if d == 0 else tiles_per_seg - 1)
            h_t = a_t * jnp.where(first_of_seg, h0, h_prev) + b_t
            y_scr[rows, :] = h_t
            return h_t[SUBLANES - 1:SUBLANES, :] if d == 0 else h_t[0:1, :]

        lax.fori_loop(0, n_tiles, tile_step, jnp.zeros((1, D_LRU), f32), unroll=LRU_TILE_UNROLL)

    out_ref[...] = ((yf_scr[...] + yb_scr[...]) * jax.nn.gelu(gate_ref[...])).astype(out_ref.dtype)
    fins = ([yf_scr[(j + 1) * CTX_LEN - 1:(j + 1) * CTX_LEN, :] for j in range(SEQ_BLOCK // CTX_LEN)]
            + [yb_scr[j * CTX_LEN:j * CTX_LEN + 1, :] for j in range(SEQ_BLOCK // CTX_LEN)])
    hfin_ref[0] = jnp.concatenate(fins, axis=0)


def _lru_branch(z, p, h0_blocks):
    full = lambda shape: pl.BlockSpec(shape, lambda i: (0,) * len(shape))
    return pl.pallas_call(
        _lru_kernel,
        grid=(N_SEQ_BLOCKS,),
        in_specs=[pl.BlockSpec((SEQ_BLOCK, D_LRU), lambda i: (i, Z_LRU_X // D_LRU)),
                  pl.BlockSpec((SEQ_BLOCK, D_LRU), lambda i: (i, Z_LRU_G // D_LRU)),
                  full((4, D_LRU)), full((1, D_LRU)), full((D_LRU, 4 * D_LRU)), full((1, 4 * D_LRU)),
                  full((2, D_LRU)),
                  pl.BlockSpec((1, 2, D_LRU), lambda i: (i, 0, 0))],
        out_specs=[pl.BlockSpec((SEQ_BLOCK, D_LRU), lambda i: (i, 0)),
                   pl.BlockSpec((1, SUBLANES, D_LRU), lambda i: (i, 0, 0))],
        out_shape=[jax.ShapeDtypeStruct((N_TOK, D_LRU), bf16),
                   jax.ShapeDtypeStruct((N_SEQ_BLOCKS, SUBLANES, D_LRU), f32)],
        scratch_shapes=[pltpu.VMEM((SEQ_BLOCK, D_LRU), f32)] * 4,
        compiler_params=_cparams(1),
        name="lru_branch",
    )(z, z, p['lru_conv_w'], p['lru_conv_b'], p['lru_wg'], p['lru_bg'], p['lru_lambda'], h0_blocks)


RWKV_CT = 256
N_RWKV_CT = D_RWKV // RWKV_CT
RWKV_LOWRANK_OFF = 3 * D_RWKV
RWKV_LOWRANK_W = RWKV_COLS_PAD - RWKV_LOWRANK_OFF
SCAN_ARRAYS = ('w_f', 'w_b', 'ka_f', 'ka_b', 'kd_f', 'kd_b', 'nkk', 'r', 'v')
N_SCAN_ARRAYS = len(SCAN_ARRAYS)
SCAN_IDX = {name: i for i, name in enumerate(SCAN_ARRAYS)}


def _rwkv_pre_kernel(r_ref, k_ref, v_ref, lr_ref, cwr_ref, cwk_ref, cwv_ref, cwl_ref,
                     w2_ref, a2_ref, g2_ref, pc_ref, ones_ref, sc_ref, bonus_ref, g_ref):
    blk = pl.program_id(0)
    seg = _seg_len(blk)
    n = r_ref.shape[0]

    def conv(x_ref, cw_ref):
        width = x_ref.shape[1]
        t = lax.broadcasted_iota(jnp.int32, (n, width), 0)
        tm = t & (seg - 1)
        x = x_ref[...]
        return (cw_ref[0:1, :] * _shift_rows(x, -1, tm, seg) + cw_ref[1:2, :] * x
                + cw_ref[2:3, :] * _shift_rows(x, 1, tm, seg))

    r = conv(r_ref, cwr_ref)
    k = conv(k_ref, cwk_ref)
    v = conv(v_ref, cwv_ref)
    lr = conv(lr_ref, cwl_ref)
    wl = lr[:, 0:2 * W_RANK]
    al = lr[:, 2 * W_RANK:2 * W_RANK + 2 * A_RANK]
    gl = lr[:, 2 * W_RANK + 2 * A_RANK:2 * W_RANK + 2 * A_RANK + G_RANK]

    w0 = [pc_ref[0:1, :], pc_ref[1:2, :]]
    a0 = [pc_ref[2:3, :], pc_ref[3:4, :]]
    k_k = pc_ref[4:5, :]
    k_a = pc_ref[5:6, :]
    r_k = pc_ref[6:7, :]

    w_pre = jnp.dot(jnp.tanh(wl).astype(bf16), w2_ref[0], preferred_element_type=f32)
    a_pre = jnp.dot(al.astype(bf16), a2_ref[0], preferred_element_type=f32)

    kk = k * k_k
    ss = _split_dot(kk * kk, ones_ref[...])
    kk = kk / jnp.maximum(jnp.sqrt(ss), 1e-12)

    kd_sum = None
    for d, sfx in enumerate(('_f', '_b')):
        cols = slice(d * RWKV_CT, (d + 1) * RWKV_CT)
        w_log = -jax.nn.softplus(-(w0[d] + w_pre[:, cols])) - 0.5
        sc_ref[SCAN_IDX['w' + sfx]] = jnp.exp(-jnp.exp(w_log))
        a = jax.nn.sigmoid(a0[d] + a_pre[:, cols])
        sc_ref[SCAN_IDX['ka' + sfx]] = kk * a
        kd = k * (1.0 + (a - 1.0) * k_a)
        sc_ref[SCAN_IDX['kd' + sfx]] = kd
        kd_sum = kd if kd_sum is None else kd_sum + kd

    sc_ref[SCAN_IDX['r']] = r
    sc_ref[SCAN_IDX['v']] = v
    sc_ref[SCAN_IDX['nkk']] = -kk
    bonus_ref[...] = _split_dot(r * kd_sum * r_k, ones_ref[...]) * v
    g_ref[...] = jnp.dot(jax.nn.sigmoid(gl).astype(bf16), g2_ref[...], preferred_element_type=f32)


def _rwkv_pre(z, p):
    nct = D_RWKV // RWKV_CT
    zcol = lambda base: pl.BlockSpec((SEQ_BLOCK, RWKV_CT), lambda i, c, base=base: (i, base + c))
    cwcol = lambda base: pl.BlockSpec((3, RWKV_CT), lambda i, c, base=base: (0, base + c))
    out_spec = pl.BlockSpec((SEQ_BLOCK, RWKV_CT), lambda i, c: (i, c))
    return pl.pallas_call(
        _rwkv_pre_kernel,
        grid=(N_SEQ_BLOCKS, nct),
        in_specs=[zcol(0), zcol(nct), zcol(2 * nct),
                  pl.BlockSpec((SEQ_BLOCK, RWKV_LOWRANK_W),
                               lambda i, c: (i, RWKV_LOWRANK_OFF // RWKV_LOWRANK_W)),
                  cwcol(0), cwcol(nct), cwcol(2 * nct),
                  pl.BlockSpec((3, RWKV_LOWRANK_W), lambda i, c: (0, RWKV_LOWRANK_OFF // RWKV_LOWRANK_W)),
                  pl.BlockSpec((1, 2 * W_RANK, 2 * RWKV_CT), lambda i, c: (c, 0, 0)),
                  pl.BlockSpec((1, 2 * A_RANK, 2 * RWKV_CT), lambda i, c: (c, 0, 0)),
                  pl.BlockSpec((G_RANK, RWKV_CT), lambda i, c: (0, c)),
                  pl.BlockSpec((SUBLANES, RWKV_CT), lambda i, c: (0, c)),
                  pl.BlockSpec((RWKV_CT, RWKV_CT), lambda i, c: (0, 0))],
        out_specs=[pl.BlockSpec((N_SCAN_ARRAYS, SEQ_BLOCK, RWKV_CT), lambda i, c: (0, i, c)),
                   out_spec, out_spec],
        out_shape=[jax.ShapeDtypeStruct((N_SCAN_ARRAYS, N_TOK, D_RWKV), f32),
                   jax.ShapeDtypeStruct((N_TOK, D_RWKV), f32),
                   jax.ShapeDtypeStruct((N_TOK, D_RWKV), f32)],
        compiler_params=_cparams(2),
        name="rwkv_pre",
    )(z, z, z, z, p['rwkv_conv_w'], p['rwkv_conv_w'], p['rwkv_conv_w'], p['rwkv_conv_w'],
      p['rwkv_w2'], p['rwkv_a2'], p['rwkv_g2'], p['rwkv_pc'], p['head_ones'][:RWKV_CT, :RWKV_CT])


PACK_T = 128
TIME_BLOCK = SUBLANES
PACK_UNROLL = 8


def _gather_states(scr, row, n_seq):
    return scr[:, pl.ds(row, N_RWKV_HEADS, stride=RWKV_HEAD_DIM), :].reshape(n_seq * N_RWKV_HEADS, PACK_T)


def _pack_kernel(x_ref, o_ref, scr, *, n_seq, reps, rows, fold_rows):
    for b in range(n_seq):
        scr[b] = x_ref[b].T

    def body(row, c):
        if fold_rows:
            m = jnp.concatenate([_gather_states(scr, row * reps + rep, n_seq) for rep in range(reps)], axis=0)
        else:
            m = _gather_states(scr, row, n_seq)
            if reps > 1:
                m = jnp.concatenate([m] * reps, axis=0)
        mt = m.T
        line0 = pl.multiple_of(row * TIME_BLOCK, TIME_BLOCK)
        for u in range(PACK_T // TIME_BLOCK):
            o_ref[u, pl.ds(line0, TIME_BLOCK), :] = mt[u * TIME_BLOCK:(u + 1) * TIME_BLOCK]
        return c

    lax.fori_loop(0, rows, body, 0, unroll=PACK_UNROLL)


def _pack(stacked, first, count, n_seq, seq_len, seq0, reps, fold_rows):
    rows = RWKV_HEAD_DIM // reps if fold_rows else RWKV_HEAD_DIM
    x = stacked.reshape(N_SCAN_ARRAYS, N_TOK // seq_len, seq_len, D_RWKV)
    tb = PACK_T // TIME_BLOCK
    return pl.pallas_call(
        functools.partial(_pack_kernel, n_seq=n_seq, reps=reps, rows=rows, fold_rows=fold_rows),
        grid=(count, seq_len // PACK_T),
        in_specs=[pl.BlockSpec((None, n_seq, PACK_T, D_RWKV),
                               lambda a, j: (first + a, seq0 // n_seq, j, 0))],
        out_specs=pl.BlockSpec((None, tb, rows * TIME_BLOCK, LANES), lambda a, j: (a, j, 0, 0)),
        out_shape=jax.ShapeDtypeStruct((count, seq_len // TIME_BLOCK, rows * TIME_BLOCK, LANES), f32),
        scratch_shapes=[pltpu.VMEM((n_seq, D_RWKV, PACK_T), f32)],
        compiler_params=_cparams(2),
        name="rwkv_pack",
    )(x)


def _unpack_kernel(yf_ref, yb_ref, of_ref, ob_ref, scr, *, n_seq, reps, rows):
    group = n_seq * N_RWKV_HEADS
    for y_ref, o_ref in ((yf_ref, of_ref), (yb_ref, ob_ref)):
        def body(row, c, y_ref=y_ref):
            line0 = pl.multiple_of(row * TIME_BLOCK, TIME_BLOCK)
            mt = jnp.concatenate([y_ref[u, pl.ds(line0, TIME_BLOCK), :]
                                  for u in range(PACK_T // TIME_BLOCK)], axis=0)
            m = mt.T
            for rep in range(reps):
                part = m[rep * group:(rep + 1) * group].reshape(n_seq, N_RWKV_HEADS, PACK_T)
                scr[:, pl.ds(row * reps + rep, N_RWKV_HEADS, stride=RWKV_HEAD_DIM), :] = part
            return c

        lax.fori_loop(0, rows, body, 0, unroll=PACK_UNROLL)
        for b in range(n_seq):
            o_ref[b] = scr[b].T


def _unpack(y_f, y_b, n_seq, seq_len, reps):
    rows = y_f.shape[1] // TIME_BLOCK
    yspec = pl.BlockSpec((PACK_T // TIME_BLOCK, rows * TIME_BLOCK, LANES), lambda j: (j, 0, 0))
    ospec = pl.BlockSpec((n_seq, PACK_T, D_RWKV), lambda j: (0, j, 0))
    shape = jax.ShapeDtypeStruct((n_seq, seq_len, D_RWKV), f32)
    o_f, o_b = pl.pallas_call(
        functools.partial(_unpack_kernel, n_seq=n_seq, reps=reps, rows=rows),
        grid=(seq_len // PACK_T,),
        in_specs=[yspec, yspec],
        out_specs=[ospec, ospec],
        out_shape=[shape, shape],
        scratch_shapes=[pltpu.VMEM((n_seq, D_RWKV, PACK_T), f32)],
        compiler_params=_cparams(1),
        name="rwkv_unpack",
    )(y_f, y_b)
    return o_f.reshape(n_seq * seq_len, D_RWKV), o_b.reshape(n_seq * seq_len, D_RWKV)


def _rwkv_scan_kernel(wf_ref, wb_ref, kaf_ref, kab_ref, kdf_ref, kdb_ref, nkkf_ref, nkkb_ref,
                      rf_ref, rb_ref, vvf_ref, vvb_ref, s0_ref, yf_ref, yb_ref, s_ref, *,
                      time_blocks, v_rows, unroll):
    @pl.when(pl.program_id(0) == 0)
    def _():
        s_ref[...] = s0_ref[...]

    dirs = ((wf_ref, kaf_ref, kdf_ref, nkkf_ref, rf_ref, vvf_ref, yf_ref),
            (wb_ref, kab_ref, kdb_ref, nkkb_ref, rb_ref, vvb_ref, yb_ref))
    n_k = s_ref.shape[1]

    for u in range(time_blocks):
        def step(t, carry, u=u):
            for g, (w_ref, ka_ref, kd_ref, nkk_ref, r_ref, vv_ref, y_ref) in enumerate(dirs):
                ub = u if g == 0 else time_blocks - 1 - u
                tt = t if g == 0 else TIME_BLOCK - 1 - t
                lanes = slice(g * LANES, (g + 1) * LANES)
                at_t = pl.ds(tt, n_k, stride=TIME_BLOCK)
                w = w_ref[ub, at_t, :]
                ka = ka_ref[ub, at_t, :]
                kd = kd_ref[ub, at_t, :]
                nkk = nkk_ref[ub, at_t, :]
                r = r_ref[ub, at_t, :]

                def rows(vb, c):
                    v0 = pl.multiple_of(vb * SUBLANES, SUBLANES)
                    rows_at_t = pl.ds(v0 * TIME_BLOCK + tt, SUBLANES, stride=TIME_BLOCK)
                    vv = vv_ref[ub, rows_at_t, :]
                    ys = []
                    for j in range(SUBLANES):
                        s = s_ref[v0 + j, :, lanes]
                        sa = jnp.sum(s * nkk, axis=0, keepdims=True)
                        s_new = s * w + ka * sa + kd * vv[j:j + 1]
                        s_ref[v0 + j, :, lanes] = s_new
                        ys.append(jnp.sum(s_new * r, axis=0, keepdims=True))
                    y_ref[ub, rows_at_t, :] = jnp.concatenate(ys, axis=0)
                    return c

                lax.fori_loop(0, v_rows // SUBLANES, rows, 0, unroll=unroll)
            return carry

        lax.fori_loop(0, TIME_BLOCK, step, 0, unroll=2 if v_rows == SUBLANES else 1)


SCAN_STEPS_PER_BLOCK = 16


def _rwkv_scan_packed(kvecs, vv, s0, *, steps):
    K = kvecs.shape[2] // TIME_BLOCK
    L = kvecs.shape[1] * TIME_BLOCK
    VR = vv.shape[1] // TIME_BLOCK
    nblk = L // steps
    tbs = steps // TIME_BLOCK

    def kspec(name, mirrored):
        a = SCAN_IDX[name]
        if mirrored:
            return pl.BlockSpec((None, tbs, K * TIME_BLOCK, LANES), lambda i: (a, nblk - 1 - i, 0, 0))
        return pl.BlockSpec((None, tbs, K * TIME_BLOCK, LANES), lambda i: (a, i, 0, 0))

    vf = pl.BlockSpec((tbs, VR * TIME_BLOCK, LANES), lambda i: (i, 0, 0))
    vb = pl.BlockSpec((tbs, VR * TIME_BLOCK, LANES), lambda i: (nblk - 1 - i, 0, 0))
    sspec = pl.BlockSpec((VR, K, 2 * LANES), lambda i: (0, 0, 0))
    kern = functools.partial(_rwkv_scan_kernel, time_blocks=tbs, v_rows=VR,
                             unroll=min(8, VR // SUBLANES))
    return pl.pallas_call(
        kern,
        grid=(nblk,),
        in_specs=[kspec('w_f', False), kspec('w_b', True), kspec('ka_f', False), kspec('ka_b', True),
                  kspec('kd_f', False), kspec('kd_b', True), kspec('nkk', False), kspec('nkk', True),
                  kspec('r', False), kspec('r', True), vf, vb, sspec],
        out_specs=[vf, vb, sspec],
        out_shape=[jax.ShapeDtypeStruct((L // TIME_BLOCK, VR * TIME_BLOCK, LANES), f32),
                   jax.ShapeDtypeStruct((L // TIME_BLOCK, VR * TIME_BLOCK, LANES), f32),
                   jax.ShapeDtypeStruct((VR, K, 2 * LANES), f32)],
        compiler_params=_cparams(1),
        name="rwkv7_scan",
    )(*([kvecs] * 10), vv, vv, s0)


def _rwkv_scan_path(stacked, seq0, B, L, S0):
    H, K = N_RWKV_HEADS, RWKV_HEAD_DIM
    nd = 2
    reps = LANES // (B * H)
    VR = K // reps
    kvecs = _pack(stacked, 0, N_SCAN_ARRAYS - 1, B, L, seq0, reps, fold_rows=False)
    vv = _pack(stacked, SCAN_IDX['v'], 1, B, L, seq0, reps, fold_rows=True)[0]
    if S0 is None:
        s0 = jnp.zeros((VR, K, nd * LANES), f32)
    else:
        s0 = S0.astype(f32).reshape(B, nd, H, VR, reps, K)
        s0 = jnp.transpose(s0, (3, 5, 1, 4, 0, 2)).reshape(VR, K, nd * LANES)

    y_f, y_b, s_fin = _rwkv_scan_packed(kvecs, vv, s0, steps=SCAN_STEPS_PER_BLOCK)
    y_f, y_b = _unpack(y_f, y_b, B, L, reps)
    s_fin = s_fin.reshape(VR, K, nd, reps, B, H)
    s_fin = jnp.transpose(s_fin, (4, 2, 5, 0, 3, 1)).reshape(B, nd, H, K, K)
    return y_f, y_b, s_fin


ATT_TQ = 256
HEAD_W = 2 * DIFF_HEAD_DIM


def _rope(x, cos, sin_signed, lo_mask):
    n = x.shape[1]
    partner = jnp.where(lo_mask, pltpu.roll(x, n - DIFF_HEAD_DIM // 4, axis=1),
                        pltpu.roll(x, DIFF_HEAD_DIM // 4, axis=1))
    return x * cos + partner * sin_signed


def _attn_heads(q, k, v, lam, subln_g, out_scale):
    lane = lax.broadcasted_iota(jnp.int32, (1, HEAD_W), 1)
    first = lane < DIFF_HEAD_DIM
    outs = []
    for h in range(N_DIFF_HEADS):
        cols = slice(h * HEAD_W, (h + 1) * HEAD_W)
        qh = (q[:, cols] * (DIFF_HEAD_DIM ** -0.5)).astype(bf16)
        kh = k[:, cols]
        vh = v[:, cols].astype(bf16)
        probs = []
        for m in range(2):
            km = jnp.where(first if m == 0 else ~first, kh, 0.0).astype(bf16)
            s = lax.dot_general(qh, km, (((1,), (1,)), ((), ())), preferred_element_type=f32)
            s = s - jnp.max(s, axis=-1, keepdims=True)
            e = jnp.exp(s)
            probs.append(e / jnp.sum(e, axis=-1, keepdims=True))
        attn = (probs[0] - lam * probs[1]).astype(bf16)
        o = jnp.dot(attn, vh, preferred_element_type=f32)
        o = o * lax.rsqrt(jnp.mean(o * o, axis=-1, keepdims=True) + NORM_EPS) * subln_g * out_scale
        outs.append(o)
    return jnp.concatenate(outs, axis=1)


def _attn_kernel(q_ref, k_ref, v_ref, ck_ref, cv_ref, cos_ref, sin_ref, lv_ref, g_ref, o_ref,
                 kall_ref, vall_ref, *, lam_init):
    blk = pl.program_id(0)
    qt = pl.program_id(1)
    lv = lv_ref[...]
    lam = (jnp.exp(jnp.sum(lv[0:1] * lv[1:2], axis=-1, keepdims=True))
           - jnp.exp(jnp.sum(lv[2:3] * lv[3:4], axis=-1, keepdims=True)) + lam_init)
    subln_g = g_ref[...]
    out_scale = 1.0 - lam_init
    row0 = pl.multiple_of(qt * ATT_TQ, ATT_TQ)

    @pl.when(blk < N_CTX_BLOCKS)
    def _():
        q = q_ref[pl.ds(row0, ATT_TQ), :]
        k = k_ref[pl.ds(row0, ATT_TQ), :]
        v = v_ref[pl.ds(row0, ATT_TQ), :]
        o_ref[...] = _attn_heads(q, k, v, lam, subln_g, out_scale).astype(o_ref.dtype)

    @pl.when(blk >= N_CTX_BLOCKS)
    def _():
        lane = lax.broadcasted_iota(jnp.int32, (1, D_DIFF), 1)
        lo_mask = (lane % (DIFF_HEAD_DIM // 2)) < (DIFF_HEAD_DIM // 4)

        @pl.when(qt == 0)
        def _():
            kall_ref[0:LAT_LEN, :] = _rope(k_ref[...], cos_ref[...], sin_ref[...], lo_mask)
            kall_ref[LAT_LEN:LAT_LEN + PAST_LEN, :] = ck_ref[0]
            vall_ref[0:LAT_LEN, :] = v_ref[...]
            vall_ref[LAT_LEN:LAT_LEN + PAST_LEN, :] = cv_ref[0]

        q = _rope(q_ref[pl.ds(row0, ATT_TQ), :], cos_ref[pl.ds(row0, ATT_TQ), :],
                  sin_ref[pl.ds(row0, ATT_TQ), :], lo_mask)
        o_ref[...] = _attn_heads(q, kall_ref[...], vall_ref[...], lam, subln_g,
                                 out_scale).astype(o_ref.dtype)


def _attention(z, cache_k_l, cache_v_l, rope_cos, rope_sin, p, lam_init):
    zcol = lambda off: pl.BlockSpec((SEQ_BLOCK, D_DIFF), lambda i, j, off=off: (i, off // D_DIFF))
    cache = pl.BlockSpec((1, PAST_LEN, D_DIFF), lambda i, j: (jnp.maximum(i - N_CTX_BLOCKS, 0), 0, 0))
    full = lambda shape: pl.BlockSpec(shape, lambda i, j: (0,) * len(shape))
    return pl.pallas_call(
        functools.partial(_attn_kernel, lam_init=lam_init),
        grid=(N_SEQ_BLOCKS, SEQ_BLOCK // ATT_TQ),
        in_specs=[zcol(Z_Q), zcol(Z_K), zcol(Z_V), cache, cache,
                  full((LAT_LEN, D_DIFF)), full((LAT_LEN, D_DIFF)),
                  full((4, DIFF_HEAD_DIM)), full((1, HEAD_W))],
        out_specs=pl.BlockSpec((ATT_TQ, D_DIFF), lambda i, j: (i * (SEQ_BLOCK // ATT_TQ) + j, 0)),
        out_shape=jax.ShapeDtypeStruct((N_TOK, D_DIFF), bf16),
        scratch_shapes=[pltpu.VMEM((LAT_LEN + PAST_LEN, D_DIFF), f32),
                        pltpu.VMEM((LAT_LEN + PAST_LEN, D_DIFF), f32)],
        compiler_params=_cparams(2),
        name="diff_attention",
    )(z, z, z, cache_k_l, cache_v_l, rope_cos, rope_sin, p['diff_lambda'], p['diff_subln_g'])


def _rope_tables():
    t = jnp.arange(LAT_LEN)
    row = (t // GRID_W).astype(f32)
    col = (t % GRID_W).astype(f32)
    nf = DIFF_HEAD_DIM // 4
    freqs = ROPE_THETA ** (-jnp.arange(nf, dtype=f32) / nf)
    d = jnp.arange(DIFF_HEAD_DIM)
    pos = jnp.where((d < DIFF_HEAD_DIM // 2)[None, :], row[:, None], col[:, None])
    ang = pos * freqs[d % nf][None, :]
    sign = jnp.where((d % (2 * nf)) < nf, -1.0, 1.0)[None, :]
    cos = jnp.tile(jnp.cos(ang), (1, D_DIFF // DIFF_HEAD_DIM))
    sin = jnp.tile(jnp.sin(ang) * sign, (1, D_DIFF // DIFF_HEAD_DIM))
    return cos, sin


MERGE_TM = 512


def _merge_kernel(x_ref, lru_ref, yfc_ref, ybc_ref, yfl_ref, ybl_ref, bonus_ref, g_ref, diff_ref,
                  gp0_ref, gp1_ref, gp2_ref,
                  mod_ref, lnp_ref, ones_ref, wlo_ref, wro_ref, wdo_ref, wout_ref, ng_ref, rhi_ref, rlo_ref,
                  xo_ref, h2_ref, logit_ref):
    is_ctx = pl.program_id(0) < N_CTX_TOK // MERGE_TM
    y = jnp.where(is_ctx, yfc_ref[...] + ybc_ref[...], yfl_ref[...] + ybl_ref[...])
    inv = 1.0 / RWKV_HEAD_DIM
    mu = _split_dot(y, ones_ref[...]) * inv
    dlt = y - mu
    var = _split_dot(dlt * dlt, ones_ref[...]) * inv
    yn = dlt * lax.rsqrt(var + RWKV_LN_EPS)
    rwkv_out = ((yn * lnp_ref[0:1, :] + lnp_ref[1:2, :] + bonus_ref[...]) * g_ref[...]).astype(bf16)

    merged = (jax.nn.sigmoid(gp0_ref[...]) * jnp.dot(lru_ref[...], wlo_ref[...], preferred_element_type=f32)
              + jax.nn.sigmoid(gp1_ref[...]) * jnp.dot(rwkv_out, wro_ref[...], preferred_element_type=f32)
              + jax.nn.sigmoid(gp2_ref[...]) * jnp.dot(diff_ref[...], wdo_ref[...], preferred_element_type=f32))
    mix = jnp.dot(merged.astype(bf16), wout_ref[...], preferred_element_type=f32)

    gate1 = mod_ref[0, :, 2 * D_MODEL:3 * D_MODEL]
    shift2 = mod_ref[0, :, 3 * D_MODEL:4 * D_MODEL]
    scale2 = mod_ref[0, :, 4 * D_MODEL:5 * D_MODEL]
    x = x_ref[...] + gate1 * mix
    xo_ref[...] = x
    yn2 = x * lax.rsqrt(jnp.mean(x * x, axis=-1, keepdims=True) + NORM_EPS) * ng_ref[...]
    h2 = yn2 * (1.0 + scale2) + shift2
    hi = h2.astype(bf16)
    lo = (h2 - hi.astype(f32)).astype(bf16)
    h2_ref[...] = hi
    logit_ref[...] = (jnp.dot(hi, rhi_ref[...], preferred_element_type=f32)
                      + jnp.dot(lo, rhi_ref[...], preferred_element_type=f32)
                      + jnp.dot(hi, rlo_ref[...], preferred_element_type=f32))


def _merge(x, z, lru_out, y_ctx, y_lat, bonus, g, diff_out, mod_l, p):
    n_ctx_tiles = N_CTX_TOK // MERGE_TM
    row = lambda w: pl.BlockSpec((MERGE_TM, w), lambda i: (i, 0))
    ctx_row = pl.BlockSpec((MERGE_TM, D_RWKV), lambda i: (jnp.minimum(i, n_ctx_tiles - 1), 0))
    lat_row = pl.BlockSpec((MERGE_TM, D_RWKV), lambda i: (jnp.maximum(i - n_ctx_tiles, 0), 0))
    gate = lambda b: pl.BlockSpec((MERGE_TM, D_MODEL), lambda i, b=b: (i, Z_GATES // D_MODEL + b))
    full = lambda shape: pl.BlockSpec(shape, lambda i: (0,) * len(shape))
    return pl.pallas_call(
        _merge_kernel,
        grid=(N_TOK // MERGE_TM,),
        in_specs=[row(D_MODEL), row(D_LRU), ctx_row, ctx_row, lat_row, lat_row,
                  row(D_RWKV), row(D_RWKV), row(D_DIFF),
                  gate(0), gate(1), gate(2),
                  pl.BlockSpec((1, 1, 6 * D_MODEL), lambda i: (_mod_group_of_rows(i, MERGE_TM), 0, 0)),
                  full((2, D_RWKV)), full((D_RWKV, D_RWKV)),
                  full((D_LRU, D_MODEL)), full((D_RWKV, D_MODEL)), full((D_DIFF, D_MODEL)),
                  full((D_MODEL, D_MODEL)), full((1, D_MODEL)),
                  full((D_MODEL, ROUTER_PAD)), full((D_MODEL, ROUTER_PAD))],
        out_specs=[row(D_MODEL), row(D_MODEL), row(ROUTER_PAD)],
        out_shape=[jax.ShapeDtypeStruct((N_TOK, D_MODEL), f32),
                   jax.ShapeDtypeStruct((N_TOK, D_MODEL), bf16),
                   jax.ShapeDtypeStruct((N_TOK, ROUTER_PAD), f32)],
        compiler_params=_cparams(1),
        name="merge",
    )(x, lru_out, *y_ctx, *y_lat, bonus, g, diff_out, z, z, z, mod_l, p['rwkv_lnp'], p['head_ones'],
      p['w_lru_out'], p['w_rwkv_out'], p['w_diff_out'], p['w_out'], p['norm_g2'],
      p['router_hi'], p['router_lo'])


CAP_CTX = EC_CAPACITY * N_CTX_TOK // N_EXPERTS
CAP_LAT = EC_CAPACITY * (N_TOK - N_CTX_TOK) // N_EXPERTS
N_SLOTS = CAP_CTX + CAP_LAT
F32_MANTISSA_BITS = 23
F32_EXPONENT_BIAS = 127
LADDER_STEPS = 8
BISECT_STEPS = 60


def _cumsum_lanes(x01):
    rows, n = x01.shape
    i = lax.broadcasted_iota(jnp.int32, (LANES, LANES), 0)
    j = lax.broadcasted_iota(jnp.int32, (LANES, LANES), 1)
    upper = jnp.where(i <= j, 1.0, 0.0).astype(bf16)
    off = jnp.zeros((rows, 1), f32)
    outs = []
    for t in range(n // LANES):
        c = jnp.dot(x01[:, t * LANES:(t + 1) * LANES].astype(bf16), upper, preferred_element_type=f32) + off
        outs.append(c)
        off = c[:, LANES - 1:LANES]
    return jnp.concatenate(outs, axis=1)


def _route_kernel(logit_ref, pos_ref, gate_ref, *, cap, slot0):
    n = logit_ref.shape[0]
    lane = lax.broadcasted_iota(jnp.int32, (1, ROUTER_PAD), 1)
    lg = jnp.where(lane < N_EXPERTS, logit_ref[...], -jnp.inf)
    ex = jnp.exp(lg - jnp.max(lg, axis=-1, keepdims=True))
    aff = (ex / jnp.sum(ex, axis=-1, keepdims=True)).T[:N_EXPERTS]

    def n_above(thr):
        return jnp.sum(jnp.where(aff > thr, 1.0, 0.0), axis=1, keepdims=True)

    def rung(j):
        return pltpu.bitcast(jnp.left_shift(j, F32_MANTISSA_BITS), f32)

    def pick_rung(_, jj):
        j_lo, j_hi = jj
        j_mid = (j_lo + j_hi) >> 1
        enough = n_above(rung(j_mid)) >= cap
        return jnp.where(enough, j_mid, j_lo), jnp.where(enough, j_hi, j_mid)

    j_lo, j_hi = lax.fori_loop(0, LADDER_STEPS, pick_rung,
                               (jnp.full((N_EXPERTS, 1), -1, jnp.int32),
                                jnp.full((N_EXPERTS, 1), F32_EXPONENT_BIAS + 1, jnp.int32)))

    def bisect(_, lh):
        lo, hi = lh
        mid = 0.5 * (lo + hi)
        enough = n_above(mid) >= cap
        return jnp.where(enough, mid, lo), jnp.where(enough, hi, mid)

    lo, hi = lax.fori_loop(0, BISECT_STEPS, bisect, (jnp.where(j_lo < 0, -1.0, rung(j_lo)), rung(j_hi)))
    above = aff > hi
    tie = (aff > lo) & (aff <= hi)
    need = cap - n_above(hi)
    keep = above | (tie & (_cumsum_lanes(jnp.where(tie, 1.0, 0.0)) <= need))
    slot = _cumsum_lanes(jnp.where(keep, 1.0, 0.0)) - 1.0 + slot0
    pos_ref[...] = jnp.where(keep, slot, -1.0).astype(jnp.int32)
    gate_ref[...] = jnp.where(keep, aff, 0.0)


def _route(logits, block, n, cap, slot0):
    shape = (N_EXPERTS, n)
    pos, gate = pl.pallas_call(
        functools.partial(_route_kernel, cap=cap, slot0=slot0),
        grid=(1,),
        in_specs=[pl.BlockSpec((n, ROUTER_PAD), lambda i: (block, 0))],
        out_specs=[pl.BlockSpec(shape, lambda i: (0, 0))] * 2,
        out_shape=[jax.ShapeDtypeStruct(shape, jnp.int32), jax.ShapeDtypeStruct(shape, f32)],
        compiler_params=_cparams(1),
        name="route",
    )(logits)
    return pos.reshape(N_EXPERTS, 1, n), gate.reshape(N_EXPERTS, 1, n)


FFN_TF = 512
MOE_TT = 1024


def _one_hot(pos_row, slot0, n_slots):
    slots = lax.broadcasted_iota(jnp.int32, (n_slots, pos_row.shape[1]), 0) + slot0
    return pos_row == slots


def _slots_of_tile(tile):
    return (0, CAP_CTX) if tile * MOE_TT < N_CTX_TOK else (CAP_CTX, CAP_LAT)


def _ffn_kernel(pos_ref, gate_ref, h_ref, wg_ref, wu_ref, wd_ref, o_ref, xe_ref, gs_ref, acc_ref):
    f = pl.program_id(1)

    @pl.when(f == 0)
    def _():
        xe_ref[...] = jnp.zeros_like(xe_ref)
        gs_ref[...] = jnp.zeros_like(gs_ref)
        for tile in range(N_TOK // MOE_TT):
            slot0, n_slots = _slots_of_tile(tile)
            tok = slice(tile * MOE_TT, (tile + 1) * MOE_TT)
            sel = _one_hot(pos_ref[0, :, tok], slot0, n_slots)
            rows = slice(slot0, slot0 + n_slots)
            xe_ref[rows, :] += jnp.dot(jnp.where(sel, 1.0, 0.0).astype(bf16), h_ref[tok, :],
                                       preferred_element_type=f32)
            gs_ref[rows, :] += jnp.sum(jnp.where(sel, gate_ref[0, :, tok], 0.0), axis=1, keepdims=True)

    x = xe_ref[...].astype(bf16)
    gate = jnp.dot(x, wg_ref[0, 0].astype(bf16), preferred_element_type=f32)
    up = jnp.dot(x, wu_ref[0, 0].astype(bf16), preferred_element_type=f32)
    hid = (gate * jax.nn.sigmoid(gate) * up).astype(bf16)
    part = jnp.dot(hid, wd_ref[0, 0].astype(bf16), preferred_element_type=f32)

    @pl.when(f == 0)
    def _():
        acc_ref[...] = part

    @pl.when(f != 0)
    def _():
        acc_ref[...] += part

    @pl.when(f == pl.num_programs(1) - 1)
    def _():
        o_ref[0] = (acc_ref[...] * gs_ref[...]).astype(o_ref.dtype)


def _expert_ffn(pos, gate, h2, w_gate, w_up, w_down, layer):
    tok_row = pl.BlockSpec((1, 1, N_TOK), lambda e, f: (e, 0, 0))
    return pl.pallas_call(
        _ffn_kernel,
        grid=(N_EXPERTS, EXPERT_FF // FFN_TF),
        in_specs=[tok_row, tok_row,
                  pl.BlockSpec((N_TOK, D_MODEL), lambda e, f: (0, 0), pipeline_mode=pl.Buffered(1)),
                  pl.BlockSpec((1, 1, D_MODEL, FFN_TF), lambda e, f: (layer, e, 0, f)),
                  pl.BlockSpec((1, 1, D_MODEL, FFN_TF), lambda e, f: (layer, e, 0, f)),
                  pl.BlockSpec((1, 1, FFN_TF, D_MODEL), lambda e, f: (layer, e, f, 0))],
        out_specs=pl.BlockSpec((1, N_SLOTS, D_MODEL), lambda e, f: (e, 0, 0)),
        out_shape=jax.ShapeDtypeStruct((N_EXPERTS, N_SLOTS, D_MODEL), bf16),
        scratch_shapes=[pltpu.VMEM((N_SLOTS, D_MODEL), f32),
                        pltpu.VMEM((N_SLOTS, 1), f32),
                        pltpu.VMEM((N_SLOTS, D_MODEL), f32)],
        compiler_params=_cparams(2),
        name="expert_ffn",
    )(pos, gate, h2, w_gate, w_up, w_down)


def _combine_kernel(x_ref, pos_ref, ye_ref, mod_ref, g_ref, xo_ref, acc_ref, *, final_norm):
    tile = pl.program_id(0)
    e = pl.program_id(1)

    @pl.when(e == 0)
    def _():
        acc_ref[...] = jnp.zeros_like(acc_ref)

    def scatter(slot0, n_slots):
        sel = jnp.where(_one_hot(pos_ref[0], slot0, n_slots), 1.0, 0.0).astype(bf16)
        acc_ref[...] += lax.dot_general(sel, ye_ref[0, slot0:slot0 + n_slots, :], (((0,), (0,)), ((), ())),
                                        preferred_element_type=f32)

    @pl.when(tile * MOE_TT < N_CTX_TOK)
    def _():
        scatter(0, CAP_CTX)

    @pl.when(tile * MOE_TT >= N_CTX_TOK)
    def _():
        scatter(CAP_CTX, CAP_LAT)

    @pl.when(e == pl.num_programs(1) - 1)
    def _():
        gate2 = mod_ref[0, :, 5 * D_MODEL:6 * D_MODEL]
        x = x_ref[...] + gate2 * acc_ref[...]
        if final_norm:
            x = x * lax.rsqrt(jnp.mean(x * x, axis=-1, keepdims=True) + NORM_EPS) * g_ref[...]
        xo_ref[...] = x


def _combine(x, pos, ye, mod_l, final_g, final_norm):
    row = pl.BlockSpec((MOE_TT, D_MODEL), lambda i, e: (i, 0))
    return pl.pallas_call(
        functools.partial(_combine_kernel, final_norm=final_norm),
        grid=(N_TOK // MOE_TT, N_EXPERTS),
        in_specs=[row,
                  pl.BlockSpec((1, 1, MOE_TT), lambda i, e: (e, 0, i)),
                  pl.BlockSpec((1, N_SLOTS, D_MODEL), lambda i, e: (e, 0, 0)),
                  pl.BlockSpec((1, 1, 6 * D_MODEL), lambda i, e: (_mod_group_of_rows(i, MOE_TT), 0, 0)),
                  pl.BlockSpec((1, D_MODEL), lambda i, e: (0, 0))],
        out_specs=row,
        out_shape=jax.ShapeDtypeStruct((N_TOK, D_MODEL), f32),
        scratch_shapes=[pltpu.VMEM((MOE_TT, D_MODEL), f32)],
        compiler_params=_cparams(2),
        name="moe_combine",
    )(x, pos, ye, mod_l, final_g)


def _moe_residual(x, h2, logits, w_gate, w_up, w_down, layer, mod_l, final_g, final_norm):
    pos_c, gate_c = _route(logits, 0, N_CTX_TOK, CAP_CTX, 0)
    lat_n = N_TOK - N_CTX_TOK
    pos_l, gate_l = _route(logits, N_CTX_TOK // lat_n, lat_n, CAP_LAT, CAP_CTX)
    pos = jnp.concatenate([pos_c, pos_l], axis=2)
    gate = jnp.concatenate([gate_c, gate_l], axis=2)
    ye = _expert_ffn(pos, gate, h2, w_gate, w_up, w_down, layer)
    return _combine(x, pos, ye, mod_l, final_g, final_norm)


def _block_diag(blocks):
    n, a, b = blocks.shape
    eye = jnp.eye(n, dtype=blocks.dtype)
    return (eye[:, None, :, None] * blocks[:, :, None, :]).reshape(n * a, n * b)


def _prep_layer(l, w):
    p = {}
    w_in = w['w_in'][l]
    lru_x, lru_g, zr, q, k, v, gates = (
        w_in[:, 0:D_LRU], w_in[:, D_LRU:2 * D_LRU], w_in[:, 2 * D_LRU:2 * D_LRU + RWKV_COLS],
        w_in[:, 2 * D_LRU + RWKV_COLS:2 * D_LRU + RWKV_COLS + D_DIFF],
        w_in[:, 2 * D_LRU + RWKV_COLS + D_DIFF:2 * D_LRU + RWKV_COLS + 2 * D_DIFF],
        w_in[:, 2 * D_LRU + RWKV_COLS + 2 * D_DIFF:2 * D_LRU + RWKV_COLS + 3 * D_DIFF],
        w_in[:, 2 * D_LRU + RWKV_COLS + 3 * D_DIFF:])
    pad = jnp.zeros((D_MODEL, RWKV_COLS_PAD - RWKV_COLS), f32)
    p['w_in'] = jnp.concatenate([zr, pad, gates, lru_x, lru_g, q, k, v], axis=1).astype(bf16)
    p['norm_g1'] = w['norm_g'][l, 0].reshape(1, D_MODEL)
    p['norm_g2'] = w['norm_g'][l, 1].reshape(1, D_MODEL)

    p['lru_conv_w'] = w['lru_conv_w'][l]
    p['lru_conv_b'] = w['lru_conv_b'][l].reshape(1, D_LRU)
    p['lru_wg'] = jnp.concatenate(
        [_block_diag(w[name][l, d]) for d in range(2) for name in ('lru_wa', 'lru_wx')], axis=1).astype(bf16)
    p['lru_bg'] = jnp.concatenate(
        [w[name][l, d] for d in range(2) for name in ('lru_ba', 'lru_bx')]).reshape(1, 4 * D_LRU)
    p['lru_lambda'] = w['lru_lambda'][l]

    p['rwkv_conv_w'] = jnp.pad(w['rwkv_conv_w'][l], ((0, 0), (0, RWKV_COLS_PAD - RWKV_COLS)))

    def lowrank_pair(m):
        rank = m.shape[1]
        fwd, bwd = (m[d].reshape(rank, N_RWKV_CT, RWKV_CT).transpose(1, 0, 2) for d in range(2))
        zero = jnp.zeros_like(fwd)
        return jnp.concatenate([jnp.concatenate([fwd, zero], axis=2),
                                jnp.concatenate([zero, bwd], axis=2)], axis=1).astype(bf16)

    p['rwkv_w2'] = lowrank_pair(w['rwkv_w2'][l])
    p['rwkv_a2'] = lowrank_pair(w['rwkv_a2'][l])
    p['rwkv_g2'] = w['rwkv_g2'][l].astype(bf16)
    p['rwkv_pc'] = jnp.concatenate(
        [w['rwkv_w0'][l], w['rwkv_a0'][l], w['rwkv_k_k'][l][None], w['rwkv_k_a'][l][None],
         w['rwkv_r_k'][l].reshape(1, D_RWKV), jnp.zeros((1, D_RWKV), f32)], axis=0)
    p['rwkv_lnp'] = jnp.stack([w['rwkv_ln_g'][l], w['rwkv_ln_b'][l]], axis=0)
    p['head_ones'] = _block_diag(jnp.ones((N_RWKV_HEADS, RWKV_HEAD_DIM, RWKV_HEAD_DIM), bf16))

    p['diff_lambda'] = w['diff_lambda'][l]
    p['diff_subln_g'] = w['diff_subln_g'][l].reshape(1, HEAD_W)
    p['w_lru_out'] = w['w_lru_out'][l].astype(bf16)
    p['w_rwkv_out'] = w['w_rwkv_out'][l].astype(bf16)
    p['w_diff_out'] = w['w_diff_out'][l].astype(bf16)
    p['w_out'] = w['w_out'][l].astype(bf16)
    rw = jnp.pad(w['router_w'][l], ((0, 0), (0, ROUTER_PAD - N_EXPERTS)))
    p['router_hi'] = rw.astype(bf16)
    p['router_lo'] = (rw - p['router_hi'].astype(f32)).astype(bf16)
    return p


def kernel(x_prompt, x_sample, cache_k, cache_v, state_lru, state_rwkv, c, c_ctx, norm_g, final_norm_g, w_mod, b_mod, w_in, lru_conv_w, lru_conv_b, lru_wa, lru_ba, lru_wx, lru_bx, lru_lambda, rwkv_conv_w, rwkv_w0, rwkv_w2, rwkv_a0, rwkv_a2, rwkv_k_k, rwkv_k_a, rwkv_r_k, rwkv_g2, rwkv_ln_g, rwkv_ln_b, diff_lambda, diff_subln_g, w_lru_out, w_rwkv_out, w_diff_out, w_out, router_w, exp_w_gate, exp_w_up, exp_w_down):
    w = dict(norm_g=norm_g, w_in=w_in, lru_conv_w=lru_conv_w, lru_conv_b=lru_conv_b, lru_wa=lru_wa,
             lru_ba=lru_ba, lru_wx=lru_wx, lru_bx=lru_bx, lru_lambda=lru_lambda, rwkv_conv_w=rwkv_conv_w,
             rwkv_w0=rwkv_w0, rwkv_w2=rwkv_w2, rwkv_a0=rwkv_a0, rwkv_a2=rwkv_a2, rwkv_k_k=rwkv_k_k,
             rwkv_k_a=rwkv_k_a, rwkv_r_k=rwkv_r_k, rwkv_g2=rwkv_g2, rwkv_ln_g=rwkv_ln_g,
             rwkv_ln_b=rwkv_ln_b, diff_lambda=diff_lambda, diff_subln_g=diff_subln_g, w_lru_out=w_lru_out,
             w_rwkv_out=w_rwkv_out, w_diff_out=w_diff_out, w_out=w_out, router_w=router_w)

    x = jnp.concatenate([x_prompt.reshape(N_CTX_TOK, D_MODEL), x_sample.reshape(-1, D_MODEL)], axis=0)
    cvec = jnp.concatenate([c_ctx[None], c, jnp.zeros((SUBLANES - N_MOD_GROUPS, D_MODEL), f32)], axis=0)
    mod = _modulation(cvec, w_mod, b_mod)
    rope_cos, rope_sin = _rope_tables()
    final_g = final_norm_g.reshape(1, D_MODEL)
    ctx_rows = slice(0, N_CTX_TOK)
    lat_rows = slice(N_CTX_TOK, N_TOK)

    new_k, new_v, new_lru, new_rwkv = [], [], [], []
    for l in range(DEPTH):
        p = _prep_layer(l, w)
        lam_init = 0.8 - 0.6 * math.exp(-0.3 * l)
        mod_l = mod[l].reshape(SUBLANES, 1, 6 * D_MODEL)

        z = _in_projection(x, mod_l, p['norm_g1'], p['w_in'])
        new_k.append(z[ctx_rows, Z_K:Z_K + D_DIFF].reshape(N_CTX_SEQ, CTX_LEN, N_DIFF_HEADS, 2, DIFF_HEAD_DIM))
        new_v.append(z[ctx_rows, Z_V:Z_V + D_DIFF].reshape(N_CTX_SEQ, CTX_LEN, N_DIFF_HEADS, HEAD_W))

        h0_blocks = jnp.concatenate(
            [jnp.zeros((N_CTX_BLOCKS, 2, D_LRU), f32), state_lru[:, l].astype(f32)], axis=0)
        lru_out, lru_fin = _lru_branch(z, p, h0_blocks)
        n_per = SEQ_BLOCK // CTX_LEN
        fin = lru_fin[:N_CTX_BLOCKS].reshape(N_CTX_BLOCKS, 2, n_per, D_LRU)
        new_lru.append(jnp.transpose(fin, (0, 2, 1, 3)).reshape(N_CTX_SEQ, 2, D_LRU))

        scan_in, bonus, out_gate = _rwkv_pre(z, p)
        yf_c, yb_c, s_c = _rwkv_scan_path(scan_in, 0, N_CTX_SEQ, CTX_LEN, None)
        yf_l, yb_l, _ = _rwkv_scan_path(scan_in, N_CTX_TOK // LAT_LEN, N_LAT_SEQ, LAT_LEN, state_rwkv[:, l])
        new_rwkv.append(s_c)

        diff_out = _attention(z, cache_k[:, l].reshape(N_LAT_SEQ, PAST_LEN, D_DIFF),
                              cache_v[:, l].reshape(N_LAT_SEQ, PAST_LEN, D_DIFF),
                              rope_cos, rope_sin, p, lam_init)

        x, h2, logits = _merge(x, z, lru_out, (yf_c, yb_c), (yf_l, yb_l), bonus, out_gate, diff_out, mod_l, p)
        x = _moe_residual(x, h2, logits, exp_w_gate, exp_w_up, exp_w_down, l, mod_l, final_g,
                          final_norm=(l == DEPTH - 1))

    y_prompt = x[ctx_rows].reshape(x_prompt.shape)
    y_sample = x[lat_rows].reshape(x_sample.shape)
    return (y_prompt, y_sample, jnp.stack(new_k, axis=1), jnp.stack(new_v, axis=1),
            jnp.stack(new_lru, axis=1), jnp.stack(new_rwkv, axis=1))
```

```python
import functools
import math

import jax
import jax.numpy as jnp
from jax import lax
from jax.experimental import pallas as pl
from jax.experimental.pallas import tpu as pltpu

f32 = jnp.float32
bf16 = jnp.bfloat16

D_MODEL = 1024
DEPTH = 2
GRID_W = 64
NORM_EPS = 1e-6

N_CTX_SEQ = 16
CTX_LEN = 256
N_LAT_SEQ = 2
LAT_LEN = 1024
PAST_LEN = 256
N_CTX_TOK = N_CTX_SEQ * CTX_LEN
N_TOK = N_CTX_TOK + N_LAT_SEQ * LAT_LEN

D_LRU = 512
LRU_BLOCKS = 8
LRU_BLOCK_W = D_LRU // LRU_BLOCKS
LRU_C = 8.0

D_RWKV = 512
RWKV_HEAD_DIM = 64
N_RWKV_HEADS = D_RWKV // RWKV_HEAD_DIM
W_RANK = 64
A_RANK = 64
G_RANK = 128
RWKV_LN_EPS = 64e-5
RWKV_COLS = 3 * D_RWKV + 2 * W_RANK + 2 * A_RANK + G_RANK
RWKV_COLS_PAD = 2048

N_DIFF_HEADS = 4
DIFF_HEAD_DIM = 64
D_DIFF = N_DIFF_HEADS * 2 * DIFF_HEAD_DIM
ROPE_THETA = 10000.0

N_BRANCHES = 3
N_EXPERTS = 16
EXPERT_FF = 1024
EC_CAPACITY = 2
ROUTER_PAD = 128

LANES = 128
SUBLANES = 8

SEQ_BLOCK = 1024
N_SEQ_BLOCKS = N_TOK // SEQ_BLOCK
N_CTX_BLOCKS = N_CTX_TOK // SEQ_BLOCK
N_MOD_GROUPS = 1 + N_LAT_SEQ

Z_RWKV = 0
Z_GATES = RWKV_COLS_PAD
Z_LRU_X = Z_GATES + N_BRANCHES * D_MODEL
Z_LRU_G = Z_LRU_X + D_LRU
Z_Q = Z_LRU_G + D_LRU
Z_K = Z_Q + D_DIFF
Z_V = Z_K + D_DIFF
Z_COLS = Z_V + D_DIFF

VMEM_LIMIT = 56 * 1024 * 1024


def _cparams(n_axes):
    return pltpu.CompilerParams(dimension_semantics=("arbitrary",) * n_axes,
                                vmem_limit_bytes=VMEM_LIMIT)


def _seg_len(block_idx):
    return jnp.where(block_idx < N_CTX_BLOCKS, CTX_LEN, LAT_LEN)


def _mod_group_of_rows(row_block, rows_per_block):
    first_lat = N_CTX_TOK // rows_per_block
    per_seq = LAT_LEN // rows_per_block
    return jnp.where(row_block < first_lat, 0, (row_block - first_lat) // per_seq + 1)


def _shift_rows(v, d, tm, seg):
    n = v.shape[0]
    r = pltpu.roll(v, (-d) % n, axis=0)
    ok = (tm + d >= 0) & (tm + d < seg)
    return jnp.where(ok, r, 0.0)


def _split_dot(x, w_bf16):
    hi = x.astype(bf16)
    lo = (x - hi.astype(f32)).astype(bf16)
    return (jnp.dot(hi, w_bf16, preferred_element_type=f32)
            + jnp.dot(lo, w_bf16, preferred_element_type=f32))


MOD_TN = 1536


def _mod_kernel(c_ref, w_ref, b_ref, o_ref):
    c = c_ref[...]
    s = (c * jax.nn.sigmoid(c)).astype(bf16)
    o_ref[0] = jnp.dot(s, w_ref[0].astype(bf16), preferred_element_type=f32) + b_ref[0]


def _modulation(cvec, w_mod, b_mod):
    n_out = w_mod.shape[-1]
    return pl.pallas_call(
        _mod_kernel,
        grid=(DEPTH, n_out // MOD_TN),
        in_specs=[pl.BlockSpec((SUBLANES, D_MODEL), lambda l, j: (0, 0)),
                  pl.BlockSpec((1, D_MODEL, MOD_TN), lambda l, j: (l, 0, j)),
                  pl.BlockSpec((1, 1, MOD_TN), lambda l, j: (l, 0, j))],
        out_specs=pl.BlockSpec((1, SUBLANES, MOD_TN), lambda l, j: (l, 0, j)),
        out_shape=jax.ShapeDtypeStruct((DEPTH, SUBLANES, n_out), f32),
        compiler_params=_cparams(2),
        name="modulation",
    )(cvec, w_mod, b_mod.reshape(DEPTH, 1, n_out))


INPROJ_TM = 1024
INPROJ_TN = 768


def _inproj_kernel(x_ref, mod_ref, g_ref, w_ref, o_ref, h_ref):
    @pl.when(pl.program_id(1) == 0)
    def _():
        x = x_ref[...]
        y = x * lax.rsqrt(jnp.mean(x * x, axis=-1, keepdims=True) + NORM_EPS) * g_ref[...]
        shift = mod_ref[0, :, 0:D_MODEL]
        scale = mod_ref[0, :, D_MODEL:2 * D_MODEL]
        h_ref[...] = (y * (1.0 + scale) + shift).astype(bf16)

    o_ref[...] = jnp.dot(h_ref[...], w_ref[...], preferred_element_type=f32)


def _in_projection(x, mod_l, norm_g_row, w_in_p):
    return pl.pallas_call(
        _inproj_kernel,
        grid=(N_TOK // INPROJ_TM, Z_COLS // INPROJ_TN),
        in_specs=[pl.BlockSpec((INPROJ_TM, D_MODEL), lambda i, j: (i, 0)),
                  pl.BlockSpec((1, 1, 2 * D_MODEL),
                               lambda i, j: (_mod_group_of_rows(i, INPROJ_TM), 0, 0)),
                  pl.BlockSpec((1, D_MODEL), lambda i, j: (0, 0)),
                  pl.BlockSpec((D_MODEL, INPROJ_TN), lambda i, j: (0, j))],
        out_specs=pl.BlockSpec((INPROJ_TM, INPROJ_TN), lambda i, j: (i, j)),
        out_shape=jax.ShapeDtypeStruct((N_TOK, Z_COLS), f32),
        scratch_shapes=[pltpu.VMEM((INPROJ_TM, D_MODEL), bf16)],
        compiler_params=_cparams(2),
        name="in_projection",
    )(x, mod_l, norm_g_row, w_in_p)


LRU_TILE_UNROLL = 4


def _lru_kernel(x_ref, gate_ref, cw_ref, cb_ref, wg_ref, bg_ref, lam_ref, h0_ref, out_ref, hfin_ref,
                a_scr, b_scr, yf_scr, yb_scr):
    blk = pl.program_id(0)
    seg = _seg_len(blk)
    n = x_ref.shape[0]
    t = lax.broadcasted_iota(jnp.int32, (n, D_LRU), 0)
    tm = t & (seg - 1)

    x = x_ref[...]
    xc = (cb_ref[...]
          + cw_ref[0:1, :] * _shift_rows(x, -2, tm, seg)
          + cw_ref[1:2, :] * _shift_rows(x, -1, tm, seg)
          + cw_ref[2:3, :] * x
          + cw_ref[3:4, :] * _shift_rows(x, 1, tm, seg))
    pre = jnp.dot(xc.astype(bf16), wg_ref[...], preferred_element_type=f32) + bg_ref[...]

    n_tiles = n // SUBLANES
    tiles_per_seg = seg // SUBLANES
    row = lax.broadcasted_iota(jnp.int32, (SUBLANES, D_LRU), 0)
    for d, y_scr in enumerate((yf_scr, yb_scr)):
        r = jax.nn.sigmoid(pre[:, (2 * d) * D_LRU:(2 * d + 1) * D_LRU])
        i = jax.nn.sigmoid(pre[:, (2 * d + 1) * D_LRU:(2 * d + 2) * D_LRU])
        log_a = -LRU_C * r * jax.nn.softplus(-lam_ref[d:d + 1, :])
        a = jnp.exp(log_a)
        a_scr[...] = a
        b_scr[...] = jnp.sqrt(1.0 - a * a) * (i * xc)
        h0 = h0_ref[0, d:d + 1, :]

        def tile_step(k, h_prev, d=d, y_scr=y_scr, h0=h0):
            idx = k if d == 0 else n_tiles - 1 - k
            rows = pl.ds(pl.multiple_of(idx * SUBLANES, SUBLANES), SUBLANES)
            a_t = a_scr[rows, :]
            b_t = b_scr[rows, :]
            s = 1
            while s < SUBLANES:
                ok = (row >= s) if d == 0 else (row < SUBLANES - s)
                shift = s if d == 0 else SUBLANES - s
                a_sh = jnp.where(ok, pltpu.roll(a_t, shift, axis=0), 1.0)
                b_sh = jnp.where(ok, pltpu.roll(b_t, shift, axis=0), 0.0)
                b_t = a_t * b_sh + b_t
                a_t = a_t * a_sh
                s *= 2
            first_of_seg = (idx & (tiles_per_seg - 1)) == (0 if d == 0 else tiles_per_seg - 1)
            h_t = a_t * jnp.where(first_of_seg, h0, h_prev) + b_t
            y_scr[rows, :] = h_t
            return h_t[SUBLANES - 1:SUBLANES, :] if d == 0 else h_t[0:1, :]

        lax.fori_loop(0, n_tiles, tile_step, jnp.zeros((1, D_LRU), f32), unroll=LRU_TILE_UNROLL)

    out_ref[...] = ((yf_scr[...] + yb_scr[...]) * jax.nn.gelu(gate_ref[...])).astype(out_ref.dtype)
    fins = ([yf_scr[(j + 1) * CTX_LEN - 1:(j + 1) * CTX_LEN, :] for j in range(SEQ_BLOCK // CTX_LEN)]
            + [yb_scr[j * CTX_LEN:j * CTX_LEN + 1, :] for j in range(SEQ_BLOCK // CTX_LEN)])
    hfin_ref[0] = jnp.concatenate(fins, axis=0)


def _lru_branch(z, p, h0_blocks):
    full = lambda shape: pl.BlockSpec(shape, lambda i: (0,) * len(shape))
    return pl.pallas_call(
        _lru_kernel,
        grid=(N_SEQ_BLOCKS,),
        in_specs=[pl.BlockSpec((SEQ_BLOCK, D_LRU), lambda i: (i, Z_LRU_X // D_LRU)),
                  pl.BlockSpec((SEQ_BLOCK, D_LRU), lambda i: (i, Z_LRU_G // D_LRU)),
                  full((4, D_LRU)), full((1, D_LRU)), full((D_LRU, 4 * D_LRU)), full((1, 4 * D_LRU)),
                  full((2, D_LRU)),
                  pl.BlockSpec((1, 2, D_LRU), lambda i: (i, 0, 0))],
        out_specs=[pl.BlockSpec((SEQ_BLOCK, D_LRU), lambda i: (i, 0)),
                   pl.BlockSpec((1, SUBLANES, D_LRU), lambda i: (i, 0, 0))],
        out_shape=[jax.ShapeDtypeStruct((N_TOK, D_LRU), bf16),
                   jax.ShapeDtypeStruct((N_SEQ_BLOCKS, SUBLANES, D_LRU), f32)],
        scratch_shapes=[pltpu.VMEM((SEQ_BLOCK, D_LRU), f32)] * 4,
        compiler_params=_cparams(1),
        name="lru_branch",
    )(z, z, p['lru_conv_w'], p['lru_conv_b'], p['lru_wg'], p['lru_bg'], p['lru_lambda'], h0_blocks)


RWKV_CT = 256
N_RWKV_CT = D_RWKV // RWKV_CT
RWKV_LOWRANK_OFF = 3 * D_RWKV
RWKV_LOWRANK_W = RWKV_COLS_PAD - RWKV_LOWRANK_OFF
SCAN_ARRAYS = ('w_f', 'w_b', 'ka_f', 'ka_b', 'kd_f', 'kd_b', 'nkk', 'r', 'v')
N_SCAN_ARRAYS = len(SCAN_ARRAYS)
SCAN_IDX = {name: i for i, name in enumerate(SCAN_ARRAYS)}


def _rwkv_pre_kernel(r_ref, k_ref, v_ref, lr_ref, cwr_ref, cwk_ref, cwv_ref, cwl_ref,
                     w2_ref, a2_ref, g2_ref, pc_ref, ones_ref, sc_ref, bonus_ref, g_ref):
    blk = pl.program_id(0)
    seg = _seg_len(blk)
    n = r_ref.shape[0]

    def conv(x_ref, cw_ref):
        width = x_ref.shape[1]
        t = lax.broadcasted_iota(jnp.int32, (n, width), 0)
        tm = t & (seg - 1)
        x = x_ref[...]
        return (cw_ref[0:1, :] * _shift_rows(x, -1, tm, seg) + cw_ref[1:2, :] * x
                + cw_ref[2:3, :] * _shift_rows(x, 1, tm, seg))

    r = conv(r_ref, cwr_ref)
    k = conv(k_ref, cwk_ref)
    v = conv(v_ref, cwv_ref)
    lr = conv(lr_ref, cwl_ref)
    wl = lr[:, 0:2 * W_RANK]
    al = lr[:, 2 * W_RANK:2 * W_RANK + 2 * A_RANK]
    gl = lr[:, 2 * W_RANK + 2 * A_RANK:2 * W_RANK + 2 * A_RANK + G_RANK]

    w0 = [pc_ref[0:1, :], pc_ref[1:2, :]]
    a0 = [pc_ref[2:3, :], pc_ref[3:4, :]]
    k_k = pc_ref[4:5, :]
    k_a = pc_ref[5:6, :]
    r_k = pc_ref[6:7, :]

    w_pre = jnp.dot(jnp.tanh(wl).astype(bf16), w2_ref[0], preferred_element_type=f32)
    a_pre = jnp.dot(al.astype(bf16), a2_ref[0], preferred_element_type=f32)

    kk = k * k_k
    ss = _split_dot(kk * kk, ones_ref[...])
    kk = kk / jnp.maximum(jnp.sqrt(ss), 1e-12)

    kd_sum = None
    for d, sfx in enumerate(('_f', '_b')):
        cols = slice(d * RWKV_CT, (d + 1) * RWKV_CT)
        w_log = -jax.nn.softplus(-(w0[d] + w_pre[:, cols])) - 0.5
        sc_ref[SCAN_IDX['w' + sfx]] = jnp.exp(-jnp.exp(w_log))
        a = jax.nn.sigmoid(a0[d] + a_pre[:, cols])
        sc_ref[SCAN_IDX['ka' + sfx]] = kk * a
        kd = k * (1.0 + (a - 1.0) * k_a)
        sc_ref[SCAN_IDX['kd' + sfx]] = kd
        kd_sum = kd if kd_sum is None else kd_sum + kd

    sc_ref[SCAN_IDX['r']] = r
    sc_ref[SCAN_IDX['v']] = v
    sc_ref[SCAN_IDX['nkk']] = -kk
    bonus_ref[...] = _split_dot(r * kd_sum * r_k, ones_ref[...]) * v
    g_ref[...] = jnp.dot(jax.nn.sigmoid(gl).astype(bf16), g2_ref[...], preferred_element_type=f32)


def _rwkv_pre(z, p):
    nct = D_RWKV // RWKV_CT
    zcol = lambda base: pl.BlockSpec((SEQ_BLOCK, RWKV_CT), lambda i, c, base=base: (i, base + c))
    cwcol = lambda base: pl.BlockSpec((3, RWKV_CT), lambda i, c, base=base: (0, base + c))
    out_spec = pl.BlockSpec((SEQ_BLOCK, RWKV_CT), lambda i, c: (i, c))
    return pl.pallas_call(
        _rwkv_pre_kernel,
        grid=(N_SEQ_BLOCKS, nct),
        in_specs=[zcol(0), zcol(nct), zcol(2 * nct),
                  pl.BlockSpec((SEQ_BLOCK, RWKV_LOWRANK_W),
                               lambda i, c: (i, RWKV_LOWRANK_OFF // RWKV_LOWRANK_W)),
                  cwcol(0), cwcol(nct), cwcol(2 * nct),
                  pl.BlockSpec((3, RWKV_LOWRANK_W), lambda i, c: (0, RWKV_LOWRANK_OFF // RWKV_LOWRANK_W)),
                  pl.BlockSpec((1, 2 * W_RANK, 2 * RWKV_CT), lambda i, c: (c, 0, 0)),
                  pl.BlockSpec((1, 2 * A_RANK, 2 * RWKV_CT), lambda i, c: (c, 0, 0)),
                  pl.BlockSpec((G_RANK, RWKV_CT), lambda i, c: (0, c)),
                  pl.BlockSpec((SUBLANES, RWKV_CT), lambda i, c: (0, c)),
                  pl.BlockSpec((RWKV_CT, RWKV_CT), lambda i, c: (0, 0))],
        out_specs=[pl.BlockSpec((N_SCAN_ARRAYS, SEQ_BLOCK, RWKV_CT), lambda i, c: (0, i, c)),
                   out_spec, out_spec],
        out_shape=[jax.ShapeDtypeStruct((N_SCAN_ARRAYS, N_TOK, D_RWKV), f32),
                   jax.ShapeDtypeStruct((N_TOK, D_RWKV), f32),
                   jax.ShapeDtypeStruct((N_TOK, D_RWKV), f32)],
        compiler_params=_cparams(2),
        name="rwkv_pre",
    )(z, z, z, z, p['rwkv_conv_w'], p['rwkv_conv_w'], p['rwkv_conv_w'], p['rwkv_conv_w'],
      p['rwkv_w2'], p['rwkv_a2'], p['rwkv_g2'], p['rwkv_pc'], p['head_ones'][:RWKV_CT, :RWKV_CT])


PACK_T = 128
TIME_BLOCK = SUBLANES
PACK_UNROLL = 8


def _gather_states(scr, row, n_seq):
    return scr[:, pl.ds(row, N_RWKV_HEADS, stride=RWKV_HEAD_DIM), :].reshape(n_seq * N_RWKV_HEADS, PACK_T)


def _pack_kernel(x_ref, o_ref, scr, *, n_seq, reps, rows, fold_rows):
    for b in range(n_seq):
        scr[b] = x_ref[b].T

    def body(row, c):
        if fold_rows:
            m = jnp.concatenate([_gather_states(scr, row * reps + rep, n_seq) for rep in range(reps)], axis=0)
        else:
            m = _gather_states(scr, row, n_seq)
            if reps > 1:
                m = jnp.concatenate([m] * reps, axis=0)
        mt = m.T
        line0 = pl.multiple_of(row * TIME_BLOCK, TIME_BLOCK)
        for u in range(PACK_T // TIME_BLOCK):
            o_ref[u, pl.ds(line0, TIME_BLOCK), :] = mt[u * TIME_BLOCK:(u + 1) * TIME_BLOCK]
        return c

    lax.fori_loop(0, rows, body, 0, unroll=PACK_UNROLL)


def _pack(stacked, first, count, n_seq, seq_len, seq0, reps, fold_rows):
    rows = RWKV_HEAD_DIM // reps if fold_rows else RWKV_HEAD_DIM
    x = stacked.reshape(N_SCAN_ARRAYS, N_TOK // seq_len, seq_len, D_RWKV)
    tb = PACK_T // TIME_BLOCK
    return pl.pallas_call(
        functools.partial(_pack_kernel, n_seq=n_seq, reps=reps, rows=rows, fold_rows=fold_rows),
        grid=(count, seq_len // PACK_T),
        in_specs=[pl.BlockSpec((None, n_seq, PACK_T, D_RWKV),
                               lambda a, j: (first + a, seq0 // n_seq, j, 0))],
        out_specs=pl.BlockSpec((None, tb, rows * TIME_BLOCK, LANES), lambda a, j: (a, j, 0, 0)),
        out_shape=jax.ShapeDtypeStruct((count, seq_len // TIME_BLOCK, rows * TIME_BLOCK, LANES), f32),
        scratch_shapes=[pltpu.VMEM((n_seq, D_RWKV, PACK_T), f32)],
        compiler_params=_cparams(2),
        name="rwkv_pack",
    )(x)


def _unpack_kernel(yf_ref, yb_ref, of_ref, ob_ref, scr, *, n_seq, reps, rows):
    group = n_seq * N_RWKV_HEADS
    for y_ref, o_ref in ((yf_ref, of_ref), (yb_ref, ob_ref)):
        def body(row, c, y_ref=y_ref):
            line0 = pl.multiple_of(row * TIME_BLOCK, TIME_BLOCK)
            mt = jnp.concatenate([y_ref[u, pl.ds(line0, TIME_BLOCK), :]
                                  for u in range(PACK_T // TIME_BLOCK)], axis=0)
            m = mt.T
            for rep in range(reps):
                part = m[rep * group:(rep + 1) * group].reshape(n_seq, N_RWKV_HEADS, PACK_T)
                scr[:, pl.ds(row * reps + rep, N_RWKV_HEADS, stride=RWKV_HEAD_DIM), :] = part
            return c

        lax.fori_loop(0, rows, body, 0, unroll=PACK_UNROLL)
        for b in range(n_seq):
            o_ref[b] = scr[b].T


def _unpack(y_f, y_b, n_seq, seq_len, reps):
    rows = y_f.shape[1] // TIME_BLOCK
    yspec = pl.BlockSpec((PACK_T // TIME_BLOCK, rows * TIME_BLOCK, LANES), lambda j: (j, 0, 0))
    ospec = pl.BlockSpec((n_seq, PACK_T, D_RWKV), lambda j: (0, j, 0))
    shape = jax.ShapeDtypeStruct((n_seq, seq_len, D_RWKV), f32)
    o_f, o_b = pl.pallas_call(
        functools.partial(_unpack_kernel, n_seq=n_seq, reps=reps, rows=rows),
        grid=(seq_len // PACK_T,),
        in_specs=[yspec, yspec],
        out_specs=[ospec, ospec],
        out_shape=[shape, shape],
        scratch_shapes=[pltpu.VMEM((n_seq, D_RWKV, PACK_T), f32)],
        compiler_params=_cparams(1),
        name="rwkv_unpack",
    )(y_f, y_b)
    return o_f.reshape(n_seq * seq_len, D_RWKV), o_b.reshape(n_seq * seq_len, D_RWKV)


def _rwkv_scan_kernel(wf_ref, wb_ref, kaf_ref, kab_ref, kdf_ref, kdb_ref, nkkf_ref, nkkb_ref,
                      rf_ref, rb_ref, vvf_ref, vvb_ref, s0_ref, yf_ref, yb_ref, s_ref, *,
                      time_blocks, v_rows, unroll):
    @pl.when(pl.program_id(0) == 0)
    def _():
        s_ref[...] = s0_ref[...]

    dirs = ((wf_ref, kaf_ref, kdf_ref, nkkf_ref, rf_ref, vvf_ref, yf_ref),
            (wb_ref, kab_ref, kdb_ref, nkkb_ref, rb_ref, vvb_ref, yb_ref))
    n_k = s_ref.shape[1]

    for u in range(time_blocks):
        def step(t, carry, u=u):
            for g, (w_ref, ka_ref, kd_ref, nkk_ref, r_ref, vv_ref, y_ref) in enumerate(dirs):
                ub = u if g == 0 else time_blocks - 1 - u
                tt = t if g == 0 else TIME_BLOCK - 1 - t
                lanes = slice(g * LANES, (g + 1) * LANES)
                at_t = pl.ds(tt, n_k, stride=TIME_BLOCK)
                w = w_ref[ub, at_t, :]
                ka = ka_ref[ub, at_t, :]
                kd = kd_ref[ub, at_t, :]
                nkk = nkk_ref[ub, at_t, :]
                r = r_ref[ub, at_t, :]

                def rows(vb, c):
                    v0 = pl.multiple_of(vb * SUBLANES, SUBLANES)
                    rows_at_t = pl.ds(v0 * TIME_BLOCK + tt, SUBLANES, stride=TIME_BLOCK)
                    vv = vv_ref[ub, rows_at_t, :]
                    ys = []
                    for j in range(SUBLANES):
                        s = s_ref[v0 + j, :, lanes]
                        sa = jnp.sum(s * nkk, axis=0, keepdims=True)
                        s_new = s * w + ka * sa + kd * vv[j:j + 1]
                        s_ref[v0 + j, :, lanes] = s_new
                        ys.append(jnp.sum(s_new * r, axis=0, keepdims=True))
                    y_ref[ub, rows_at_t, :] = jnp.concatenate(ys, axis=0)
                    return c

                lax.fori_loop(0, v_rows // SUBLANES, rows, 0, unroll=unroll)
            return carry

        lax.fori_loop(0, TIME_BLOCK, step, 0, unroll=2 if v_rows == SUBLANES else 1)


SCAN_STEPS_PER_BLOCK = 16


def _rwkv_scan_packed(kvecs, vv, s0, *, steps):
    K = kvecs.shape[2] // TIME_BLOCK
    L = kvecs.shape[1] * TIME_BLOCK
    VR = vv.shape[1] // TIME_BLOCK
    nblk = L // steps
    tbs = steps // TIME_BLOCK

    def kspec(name, mirrored):
        a = SCAN_IDX[name]
        if mirrored:
            return pl.BlockSpec((None, tbs, K * TIME_BLOCK, LANES), lambda i: (a, nblk - 1 - i, 0, 0))
        return pl.BlockSpec((None, tbs, K * TIME_BLOCK, LANES), lambda i: (a, i, 0, 0))

    vf = pl.BlockSpec((tbs, VR * TIME_BLOCK, LANES), lambda i: (i, 0, 0))
    vb = pl.BlockSpec((tbs, VR * TIME_BLOCK, LANES), lambda i: (nblk - 1 - i, 0, 0))
    sspec = pl.BlockSpec((VR, K, 2 * LANES), lambda i: (0, 0, 0))
    kern = functools.partial(_rwkv_scan_kernel, time_blocks=tbs, v_rows=VR,
                             unroll=min(8, VR // SUBLANES))
    return pl.pallas_call(
        kern,
        grid=(nblk,),
        in_specs=[kspec('w_f', False), kspec('w_b', True), kspec('ka_f', False), kspec('ka_b', True),
                  kspec('kd_f', False), kspec('kd_b', True), kspec('nkk', False), kspec('nkk', True),
                  kspec('r', False), kspec('r', True), vf, vb, sspec],
        out_specs=[vf, vb, sspec],
        out_shape=[jax.ShapeDtypeStruct((L // TIME_BLOCK, VR * TIME_BLOCK, LANES), f32),
                   jax.ShapeDtypeStruct((L // TIME_BLOCK, VR * TIME_BLOCK, LANES), f32),
                   jax.ShapeDtypeStruct((VR, K, 2 * LANES), f32)],
        compiler_params=_cparams(1),
        name="rwkv7_scan",
    )(*([kvecs] * 10), vv, vv, s0)


def _rwkv_scan_path(stacked, seq0, B, L, S0):
    H, K = N_RWKV_HEADS, RWKV_HEAD_DIM
    nd = 2
    reps = LANES // (B * H)
    VR = K // reps
    kvecs = _pack(stacked, 0, N_SCAN_ARRAYS - 1, B, L, seq0, reps, fold_rows=False)
    vv = _pack(stacked, SCAN_IDX['v'], 1, B, L, seq0, reps, fold_rows=True)[0]
    if S0 is None:
        s0 = jnp.zeros((VR, K, nd * LANES), f32)
    else:
        s0 = S0.astype(f32).reshape(B, nd, H, VR, reps, K)
        s0 = jnp.transpose(s0, (3, 5, 1, 4, 0, 2)).reshape(VR, K, nd * LANES)

    y_f, y_b, s_fin = _rwkv_scan_packed(kvecs, vv, s0, steps=SCAN_STEPS_PER_BLOCK)
    y_f, y_b = _unpack(y_f, y_b, B, L, reps)
    s_fin = s_fin.reshape(VR, K, nd, reps, B, H)
    s_fin = jnp.transpose(s_fin, (4, 2, 5, 0, 3, 1)).reshape(B, nd, H, K, K)
    return y_f, y_b, s_fin


ATT_TQ = 256
HEAD_W = 2 * DIFF_HEAD_DIM


def _rope(x, cos, sin_signed, lo_mask):
    n = x.shape[1]
    partner = jnp.where(lo_mask, pltpu.roll(x, n - DIFF_HEAD_DIM // 4, axis=1),
                        pltpu.roll(x, DIFF_HEAD_DIM // 4, axis=1))
    return x * cos + partner * sin_signed


def _attn_heads(q, k, v, lam, subln_g, out_scale):
    lane = lax.broadcasted_iota(jnp.int32, (1, HEAD_W), 1)
    first = lane < DIFF_HEAD_DIM
    outs = []
    for h in range(N_DIFF_HEADS):
        cols = slice(h * HEAD_W, (h + 1) * HEAD_W)
        qh = (q[:, cols] * (DIFF_HEAD_DIM ** -0.5)).astype(bf16)
        kh = k[:, cols]
        vh = v[:, cols].astype(bf16)
        probs = []
        for m in range(2):
            km = jnp.where(first if m == 0 else ~first, kh, 0.0).astype(bf16)
            s = lax.dot_general(qh, km, (((1,), (1,)), ((), ())), preferred_element_type=f32)
            s = s - jnp.max(s, axis=-1, keepdims=True)
            e = jnp.exp(s)
            probs.append(e / jnp.sum(e, axis=-1, keepdims=True))
        attn = (probs[0] - lam * probs[1]).astype(bf16)
        o = jnp.dot(attn, vh, preferred_element_type=f32)
        o = o * lax.rsqrt(jnp.mean(o * o, axis=-1, keepdims=True) + NORM_EPS) * subln_g * out_scale
        outs.append(o)
    return jnp.concatenate(outs, axis=1)


def _attn_kernel(q_ref, k_ref, v_ref, ck_ref, cv_ref, cos_ref, sin_ref, lv_ref, g_ref, o_ref,
                 kall_ref, vall_ref, *, lam_init):
    blk = pl.program_id(0)
    qt = pl.program_id(1)
    lv = lv_ref[...]
    lam = (jnp.exp(jnp.sum(lv[0:1] * lv[1:2], axis=-1, keepdims=True))
           - jnp.exp(jnp.sum(lv[2:3] * lv[3:4], axis=-1, keepdims=True)) + lam_init)
    subln_g = g_ref[...]
    out_scale = 1.0 - lam_init
    row0 = pl.multiple_of(qt * ATT_TQ, ATT_TQ)

    @pl.when(blk < N_CTX_BLOCKS)
    def _():
        q = q_ref[pl.ds(row0, ATT_TQ), :]
        k = k_ref[pl.ds(row0, ATT_TQ), :]
        v = v_ref[pl.ds(row0, ATT_TQ), :]
        o_ref[...] = _attn_heads(q, k, v, lam, subln_g, out_scale).astype(o_ref.dtype)

    @pl.when(blk >= N_CTX_BLOCKS)
    def _():
        lane = lax.broadcasted_iota(jnp.int32, (1, D_DIFF), 1)
        lo_mask = (lane % (DIFF_HEAD_DIM // 2)) < (DIFF_HEAD_DIM // 4)

        @pl.when(qt == 0)
        def _():
            kall_ref[0:LAT_LEN, :] = _rope(k_ref[...], cos_ref[...], sin_ref[...], lo_mask)
            kall_ref[LAT_LEN:LAT_LEN + PAST_LEN, :] = ck_ref[0]
            vall_ref[0:LAT_LEN, :] = v_ref[...]
            vall_ref[LAT_LEN:LAT_LEN + PAST_LEN, :] = cv_ref[0]

        q = _rope(q_ref[pl.ds(row0, ATT_TQ), :], cos_ref[pl.ds(row0, ATT_TQ), :],
                  sin_ref[pl.ds(row0, ATT_TQ), :], lo_mask)
        o_ref[...] = _attn_heads(q, kall_ref[...], vall_ref[...], lam, subln_g,
                                 out_scale).astype(o_ref.dtype)


def _attention(z, cache_k_l, cache_v_l, rope_cos, rope_sin, p, lam_init):
    zcol = lambda off: pl.BlockSpec((SEQ_BLOCK, D_DIFF), lambda i, j, off=off: (i, off // D_DIFF))
    cache = pl.BlockSpec((1, PAST_LEN, D_DIFF), lambda i, j: (jnp.maximum(i - N_CTX_BLOCKS, 0), 0, 0))
    full = lambda shape: pl.BlockSpec(shape, lambda i, j: (0,) * len(shape))
    return pl.pallas_call(
        functools.partial(_attn_kernel, lam_init=lam_init),
        grid=(N_SEQ_BLOCKS, SEQ_BLOCK // ATT_TQ),
        in_specs=[zcol(Z_Q), zcol(Z_K), zcol(Z_V), cache, cache,
                  full((LAT_LEN, D_DIFF)), full((LAT_LEN, D_DIFF)),
                  full((4, DIFF_HEAD_DIM)), full((1, HEAD_W))],
        out_specs=pl.BlockSpec((ATT_TQ, D_DIFF), lambda i, j: (i * (SEQ_BLOCK // ATT_TQ) + j, 0)),
        out_shape=jax.ShapeDtypeStruct((N_TOK, D_DIFF), bf16),
        scratch_shapes=[pltpu.VMEM((LAT_LEN + PAST_LEN, D_DIFF), f32),
                        pltpu.VMEM((LAT_LEN + PAST_LEN, D_DIFF), f32)],
        compiler_params=_cparams(2),
        name="diff_attention",
    )(z, z, z, cache_k_l, cache_v_l, rope_cos, rope_sin, p['diff_lambda'], p['diff_subln_g'])


def _rope_tables():
    t = jnp.arange(LAT_LEN)
    row = (t // GRID_W).astype(f32)
    col = (t % GRID_W).astype(f32)
    nf = DIFF_HEAD_DIM // 4
    freqs = ROPE_THETA ** (-jnp.arange(nf, dtype=f32) / nf)
    d = jnp.arange(DIFF_HEAD_DIM)
    pos = jnp.where((d < DIFF_HEAD_DIM // 2)[None, :], row[:, None], col[:, None])
    ang = pos * freqs[d % nf][None, :]
    sign = jnp.where((d % (2 * nf)) < nf, -1.0, 1.0)[None, :]
    cos = jnp.tile(jnp.cos(ang), (1, D_DIFF // DIFF_HEAD_DIM))
    sin = jnp.tile(jnp.sin(ang) * sign, (1, D_DIFF // DIFF_HEAD_DIM))
    return cos, sin


MERGE_TM = 512


def _merge_kernel(x_ref, lru_ref, yfc_ref, ybc_ref, yfl_ref, ybl_ref, bonus_ref, g_ref, diff_ref,
                  gp0_ref, gp1_ref, gp2_ref,
                  mod_ref, lnp_ref, ones_ref, wlo_ref, wro_ref, wdo_ref, wout_ref, ng_ref, rhi_ref, rlo_ref,
                  xo_ref, h2_ref, logit_ref):
    is_ctx = pl.program_id(0) < N_CTX_TOK // MERGE_TM
    y = jnp.where(is_ctx, yfc_ref[...] + ybc_ref[...], yfl_ref[...] + ybl_ref[...])
    inv = 1.0 / RWKV_HEAD_DIM
    mu = _split_dot(y, ones_ref[...]) * inv
    dlt = y - mu
    var = _split_dot(dlt * dlt, ones_ref[...]) * inv
    yn = dlt * lax.rsqrt(var + RWKV_LN_EPS)
    rwkv_out = ((yn * lnp_ref[0:1, :] + lnp_ref[1:2, :] + bonus_ref[...]) * g_ref[...]).astype(bf16)

    merged = (jax.nn.sigmoid(gp0_ref[...]) * jnp.dot(lru_ref[...], wlo_ref[...], preferred_element_type=f32)
              + jax.nn.sigmoid(gp1_ref[...]) * jnp.dot(rwkv_out, wro_ref[...], preferred_element_type=f32)
              + jax.nn.sigmoid(gp2_ref[...]) * jnp.dot(diff_ref[...], wdo_ref[...], preferred_element_type=f32))
    mix = jnp.dot(merged.astype(bf16), wout_ref[...], preferred_element_type=f32)

    gate1 = mod_ref[0, :, 2 * D_MODEL:3 * D_MODEL]
    shift2 = mod_ref[0, :, 3 * D_MODEL:4 * D_MODEL]
    scale2 = mod_ref[0, :, 4 * D_MODEL:5 * D_MODEL]
    x = x_ref[...] + gate1 * mix
    xo_ref[...] = x
    yn2 = x * lax.rsqrt(jnp.mean(x * x, axis=-1, keepdims=True) + NORM_EPS) * ng_ref[...]
    h2 = yn2 * (1.0 + scale2) + shift2
    hi = h2.astype(bf16)
    lo = (h2 - hi.astype(f32)).astype(bf16)
    h2_ref[...] = hi
    logit_ref[...] = (jnp.dot(hi, rhi_ref[...], preferred_element_type=f32)
                      + jnp.dot(lo, rhi_ref[...], preferred_element_type=f32)
                      + jnp.dot(hi, rlo_ref[...], preferred_element_type=f32))


def _merge(x, z, lru_out, y_ctx, y_lat, bonus, g, diff_out, mod_l, p):
    n_ctx_tiles = N_CTX_TOK // MERGE_TM
    row = lambda w: pl.BlockSpec((MERGE_TM, w), lambda i: (i, 0))
    ctx_row = pl.BlockSpec((MERGE_TM, D_RWKV), lambda i: (jnp.minimum(i, n_ctx_tiles - 1), 0))
    lat_row = pl.BlockSpec((MERGE_TM, D_RWKV), lambda i: (jnp.maximum(i - n_ctx_tiles, 0), 0))
    gate = lambda b: pl.BlockSpec((MERGE_TM, D_MODEL), lambda i, b=b: (i, Z_GATES // D_MODEL + b))
    full = lambda shape: pl.BlockSpec(shape, lambda i: (0,) * len(shape))
    return pl.pallas_call(
        _merge_kernel,
        grid=(N_TOK // MERGE_TM,),
        in_specs=[row(D_MODEL), row(D_LRU), ctx_row, ctx_row, lat_row, lat_row,
                  row(D_RWKV), row(D_RWKV), row(D_DIFF),
                  gate(0), gate(1), gate(2),
                  pl.BlockSpec((1, 1, 6 * D_MODEL), lambda i: (_mod_group_of_rows(i, MERGE_TM), 0, 0)),
                  full((2, D_RWKV)), full((D_RWKV, D_RWKV)),
                  full((D_LRU, D_MODEL)), full((D_RWKV, D_MODEL)), full((D_DIFF, D_MODEL)),
                  full((D_MODEL, D_MODEL)), full((1, D_MODEL)),
                  full((D_MODEL, ROUTER_PAD)), full((D_MODEL, ROUTER_PAD))],
        out_specs=[row(D_MODEL), row(D_MODEL), row(ROUTER_PAD)],
        out_shape=[jax.ShapeDtypeStruct((N_TOK, D_MODEL), f32),
                   jax.ShapeDtypeStruct((N_TOK, D_MODEL), bf16),
                   jax.ShapeDtypeStruct((N_TOK, ROUTER_PAD), f32)],
        compiler_params=_cparams(1),
        name="merge",
    )(x, lru_out, *y_ctx, *y_lat, bonus, g, diff_out, z, z, z, mod_l, p['rwkv_lnp'], p['head_ones'],
      p['w_lru_out'], p['w_rwkv_out'], p['w_diff_out'], p['w_out'], p['norm_g2'],
      p['router_hi'], p['router_lo'])


CAP_CTX = EC_CAPACITY * N_CTX_TOK // N_EXPERTS
CAP_LAT = EC_CAPACITY * (N_TOK - N_CTX_TOK) // N_EXPERTS
N_SLOTS = CAP_CTX + CAP_LAT
F32_MANTISSA_BITS = 23
F32_EXPONENT_BIAS = 127
LADDER_STEPS = 8
BISECT_STEPS = 60


def _cumsum_lanes(x01):
    rows, n = x01.shape
    i = lax.broadcasted_iota(jnp.int32, (LANES, LANES), 0)
    j = lax.broadcasted_iota(jnp.int32, (LANES, LANES), 1)
    upper = jnp.where(i <= j, 1.0, 0.0).astype(bf16)
    off = jnp.zeros((rows, 1), f32)
    outs = []
    for t in range(n // LANES):
        c = jnp.dot(x01[:, t * LANES:(t + 1) * LANES].astype(bf16), upper, preferred_element_type=f32) + off
        outs.append(c)
        off = c[:, LANES - 1:LANES]
    return jnp.concatenate(outs, axis=1)


def _route_kernel(logit_ref, pos_ref, gate_ref, *, cap, slot0):
    n = logit_ref.shape[0]
    lane = lax.broadcasted_iota(jnp.int32, (1, ROUTER_PAD), 1)
    lg = jnp.where(lane < N_EXPERTS, logit_ref[...], -jnp.inf)
    ex = jnp.exp(lg - jnp.max(lg, axis=-1, keepdims=True))
    aff = (ex / jnp.sum(ex, axis=-1, keepdims=True)).T[:N_EXPERTS]

    def n_above(thr):
        return jnp.sum(jnp.where(aff > thr, 1.0, 0.0), axis=1, keepdims=True)

    def rung(j):
        return pltpu.bitcast(jnp.left_shift(j, F32_MANTISSA_BITS), f32)

    def pick_rung(_, jj):
        j_lo, j_hi = jj
        j_mid = (j_lo + j_hi) >> 1
        enough = n_above(rung(j_mid)) >= cap
        return jnp.where(enough, j_mid, j_lo), jnp.where(enough, j_hi, j_mid)

    j_lo, j_hi = lax.fori_loop(0, LADDER_STEPS, pick_rung,
                               (jnp.full((N_EXPERTS, 1), -1, jnp.int32),
                                jnp.full((N_EXPERTS, 1), F32_EXPONENT_BIAS + 1, jnp.int32)))

    def bisect(_, lh):
        lo, hi = lh
        mid = 0.5 * (lo + hi)
        enough = n_above(mid) >= cap
        return jnp.where(enough, mid, lo), jnp.where(enough, hi, mid)

    lo, hi = lax.fori_loop(0, BISECT_STEPS, bisect, (jnp.where(j_lo < 0, -1.0, rung(j_lo)), rung(j_hi)))
    above = aff > hi
    tie = (aff > lo) & (aff <= hi)
    need = cap - n_above(hi)
    keep = above | (tie & (_cumsum_lanes(jnp.where(tie, 1.0, 0.0)) <= need))
    slot = _cumsum_lanes(jnp.where(keep, 1.0, 0.0)) - 1.0 + slot0
    pos_ref[...] = jnp.where(keep, slot, -1.0).astype(jnp.int32)
    gate_ref[...] = jnp.where(keep, aff, 0.0)


def _route(logits, block, n, cap, slot0):
    shape = (N_EXPERTS, n)
    pos, gate = pl.pallas_call(
        functools.partial(_route_kernel, cap=cap, slot0=slot0),
        grid=(1,),
        in_specs=[pl.BlockSpec((n, ROUTER_PAD), lambda i: (block, 0))],
        out_specs=[pl.BlockSpec(shape, lambda i: (0, 0))] * 2,
        out_shape=[jax.ShapeDtypeStruct(shape, jnp.int32), jax.ShapeDtypeStruct(shape, f32)],
        compiler_params=_cparams(1),
        name="route",
    )(logits)
    return pos.reshape(N_EXPERTS, 1, n), gate.reshape(N_EXPERTS, 1, n)


FFN_TF = 512
MOE_TT = 1024


def _one_hot(pos_row, slot0, n_slots):
    slots = lax.broadcasted_iota(jnp.int32, (n_slots, pos_row.shape[1]), 0) + slot0
    return pos_row == slots


N_MOE_TILES = N_TOK // MOE_TT
SLOT_CHUNK_CTX = CAP_CTX // 2
SLOT_CHUNK_LAT = CAP_LAT // 2


def _slot_chunks(is_ctx):
    slot0, n_slots, size = (0, CAP_CTX, SLOT_CHUNK_CTX) if is_ctx else (CAP_CTX, CAP_LAT, SLOT_CHUNK_LAT)
    return [(slot0 + c * size, size) for c in range(n_slots // size)]


def _tile_slot_ranges(pos):
    p = pos.reshape(N_EXPERTS, N_MOE_TILES, MOE_TT)
    lo = jnp.min(jnp.where(p >= 0, p, N_SLOTS), axis=2)
    hi = jnp.max(p, axis=2) + 1
    return jnp.stack([lo, hi]).astype(jnp.int32)


def _ffn_kernel(rng_ref, pos_ref, gate_ref, h_ref, wg_ref, wu_ref, wd_ref, o_ref, xe_ref, gs_ref, acc_ref):
    e = pl.program_id(0)
    f = pl.program_id(1)

    @pl.when(f == 0)
    def _():
        xe_ref[...] = jnp.zeros_like(xe_ref)
        gs_ref[...] = jnp.zeros_like(gs_ref)
        for tile in range(N_MOE_TILES):
            lo = rng_ref[0, e, tile]
            hi = rng_ref[1, e, tile]
            tok = slice(tile * MOE_TT, (tile + 1) * MOE_TT)
            for c0, size in _slot_chunks(tile * MOE_TT < N_CTX_TOK):
                @pl.when((lo < c0 + size) & (hi > c0))
                def _(c0=c0, size=size, tok=tok):
                    sel = _one_hot(pos_ref[0, :, tok], c0, size)
                    rows = slice(c0, c0 + size)
                    xe_ref[rows, :] += jnp.dot(jnp.where(sel, 1.0, 0.0).astype(bf16), h_ref[tok, :],
                                               preferred_element_type=f32)
                    gs_ref[rows, :] += jnp.sum(jnp.where(sel, gate_ref[0, :, tok], 0.0), axis=1,
                                               keepdims=True)

    x = xe_ref[...].astype(bf16)
    gate = jnp.dot(x, wg_ref[0, 0].astype(bf16), preferred_element_type=f32)
    up = jnp.dot(x, wu_ref[0, 0].astype(bf16), preferred_element_type=f32)
    hid = (gate * jax.nn.sigmoid(gate) * up).astype(bf16)
    part = jnp.dot(hid, wd_ref[0, 0].astype(bf16), preferred_element_type=f32)

    @pl.when(f == 0)
    def _():
        acc_ref[...] = part

    @pl.when(f != 0)
    def _():
        acc_ref[...] += part

    @pl.when(f == pl.num_programs(1) - 1)
    def _():
        o_ref[0] = (acc_ref[...] * gs_ref[...]).astype(o_ref.dtype)


def _expert_ffn(rng, pos, gate, h2, w_gate, w_up, w_down, layer):
    tok_row = pl.BlockSpec((1, 1, N_TOK), lambda e, f, rng: (e, 0, 0))
    return pl.pallas_call(
        _ffn_kernel,
        grid_spec=pltpu.PrefetchScalarGridSpec(
            num_scalar_prefetch=1,
            grid=(N_EXPERTS, EXPERT_FF // FFN_TF),
            in_specs=[tok_row, tok_row,
                      pl.BlockSpec((N_TOK, D_MODEL), lambda e, f, rng: (0, 0), pipeline_mode=pl.Buffered(1)),
                      pl.BlockSpec((1, 1, D_MODEL, FFN_TF), lambda e, f, rng: (layer, e, 0, f)),
                      pl.BlockSpec((1, 1, D_MODEL, FFN_TF), lambda e, f, rng: (layer, e, 0, f)),
                      pl.BlockSpec((1, 1, FFN_TF, D_MODEL), lambda e, f, rng: (layer, e, f, 0))],
            out_specs=pl.BlockSpec((1, N_SLOTS, D_MODEL), lambda e, f, rng: (e, 0, 0)),
            scratch_shapes=[pltpu.VMEM((N_SLOTS, D_MODEL), f32),
                            pltpu.VMEM((N_SLOTS, 1), f32),
                            pltpu.VMEM((N_SLOTS, D_MODEL), f32)]),
        out_shape=jax.ShapeDtypeStruct((N_EXPERTS, N_SLOTS, D_MODEL), bf16),
        compiler_params=_cparams(2),
        name="expert_ffn",
    )(rng, pos, gate, h2, w_gate, w_up, w_down)


def _combine_kernel(rng_ref, x_ref, pos_ref, ye_ref, mod_ref, g_ref, xo_ref, acc_ref, *, final_norm):
    tile = pl.program_id(0)
    e = pl.program_id(1)
    lo = rng_ref[0, e, tile]
    hi = rng_ref[1, e, tile]

    @pl.when(e == 0)
    def _():
        acc_ref[...] = jnp.zeros_like(acc_ref)

    def scatter(is_ctx):
        for c0, size in _slot_chunks(is_ctx):
            @pl.when((lo < c0 + size) & (hi > c0))
            def _(c0=c0, size=size):
                sel = jnp.where(_one_hot(pos_ref[0], c0, size), 1.0, 0.0).astype(bf16)
                acc_ref[...] += lax.dot_general(sel, ye_ref[0, c0:c0 + size, :], (((0,), (0,)), ((), ())),
                                                preferred_element_type=f32)

    @pl.when(tile * MOE_TT < N_CTX_TOK)
    def _():
        scatter(True)

    @pl.when(tile * MOE_TT >= N_CTX_TOK)
    def _():
        scatter(False)

    @pl.when(e == pl.num_programs(1) - 1)
    def _():
        gate2 = mod_ref[0, :, 5 * D_MODEL:6 * D_MODEL]
        x = x_ref[...] + gate2 * acc_ref[...]
        if final_norm:
            x = x * lax.rsqrt(jnp.mean(x * x, axis=-1, keepdims=True) + NORM_EPS) * g_ref[...]
        xo_ref[...] = x


def _combine(rng, x, pos, ye, mod_l, final_g, final_norm):
    row = pl.BlockSpec((MOE_TT, D_MODEL), lambda i, e, rng: (i, 0))
    return pl.pallas_call(
        functools.partial(_combine_kernel, final_norm=final_norm),
        grid_spec=pltpu.PrefetchScalarGridSpec(
            num_scalar_prefetch=1,
            grid=(N_MOE_TILES, N_EXPERTS),
            in_specs=[row,
                      pl.BlockSpec((1, 1, MOE_TT), lambda i, e, rng: (e, 0, i)),
                      pl.BlockSpec((1, N_SLOTS, D_MODEL), lambda i, e, rng: (e, 0, 0)),
                      pl.BlockSpec((1, 1, 6 * D_MODEL),
                                   lambda i, e, rng: (_mod_group_of_rows(i, MOE_TT), 0, 0)),
                      pl.BlockSpec((1, D_MODEL), lambda i, e, rng: (0, 0))],
            out_specs=row,
            scratch_shapes=[pltpu.VMEM((MOE_TT, D_MODEL), f32)]),
        out_shape=jax.ShapeDtypeStruct((N_TOK, D_MODEL), f32),
        compiler_params=_cparams(2),
        name="moe_combine",
    )(rng, x, pos, ye, mod_l, final_g)


def _moe_residual(x, h2, logits, w_gate, w_up, w_down, layer, mod_l, final_g, final_norm):
    pos_c, gate_c = _route(logits, 0, N_CTX_TOK, CAP_CTX, 0)
    lat_n = N_TOK - N_CTX_TOK
    pos_l, gate_l = _route(logits, N_CTX_TOK // lat_n, lat_n, CAP_LAT, CAP_CTX)
    pos = jnp.concatenate([pos_c, pos_l], axis=2)
    gate = jnp.concatenate([gate_c, gate_l], axis=2)
    rng = _tile_slot_ranges(pos)
    ye = _expert_ffn(rng, pos, gate, h2, w_gate, w_up, w_down, layer)
    return _combine(rng, x, pos, ye, mod_l, final_g, final_norm)


def _block_diag(blocks):
    n, a, b = blocks.shape
    eye = jnp.eye(n, dtype=blocks.dtype)
    return (eye[:, None, :, None] * blocks[:, :, None, :]).reshape(n * a, n * b)


def _prep_layer(l, w):
    p = {}
    w_in = w['w_in'][l]
    lru_x, lru_g, zr, q, k, v, gates = (
        w_in[:, 0:D_LRU], w_in[:, D_LRU:2 * D_LRU], w_in[:, 2 * D_LRU:2 * D_LRU + RWKV_COLS],
        w_in[:, 2 * D_LRU + RWKV_COLS:2 * D_LRU + RWKV_COLS + D_DIFF],
        w_in[:, 2 * D_LRU + RWKV_COLS + D_DIFF:2 * D_LRU + RWKV_COLS + 2 * D_DIFF],
        w_in[:, 2 * D_LRU + RWKV_COLS + 2 * D_DIFF:2 * D_LRU + RWKV_COLS + 3 * D_DIFF],
        w_in[:, 2 * D_LRU + RWKV_COLS + 3 * D_DIFF:])
    pad = jnp.zeros((D_MODEL, RWKV_COLS_PAD - RWKV_COLS), f32)
    p['w_in'] = jnp.concatenate([zr, pad, gates, lru_x, lru_g, q, k, v], axis=1).astype(bf16)
    p['norm_g1'] = w['norm_g'][l, 0].reshape(1, D_MODEL)
    p['norm_g2'] = w['norm_g'][l, 1].reshape(1, D_MODEL)

    p['lru_conv_w'] = w['lru_conv_w'][l]
    p['lru_conv_b'] = w['lru_conv_b'][l].reshape(1, D_LRU)
    p['lru_wg'] = jnp.concatenate(
        [_block_diag(w[name][l, d]) for d in range(2) for name in ('lru_wa', 'lru_wx')], axis=1).astype(bf16)
    p['lru_bg'] = jnp.concatenate(
        [w[name][l, d] for d in range(2) for name in ('lru_ba', 'lru_bx')]).reshape(1, 4 * D_LRU)
    p['lru_lambda'] = w['lru_lambda'][l]

    p['rwkv_conv_w'] = jnp.pad(w['rwkv_conv_w'][l], ((0, 0), (0, RWKV_COLS_PAD - RWKV_COLS)))

    def lowrank_pair(m):
        rank = m.shape[1]
        fwd, bwd = (m[d].reshape(rank, N_RWKV_CT, RWKV_CT).transpose(1, 0, 2) for d in range(2))
        zero = jnp.zeros_like(fwd)
        return jnp.concatenate([jnp.concatenate([fwd, zero], axis=2),
                                jnp.concatenate([zero, bwd], axis=2)], axis=1).astype(bf16)

    p['rwkv_w2'] = lowrank_pair(w['rwkv_w2'][l])
    p['rwkv_a2'] = lowrank_pair(w['rwkv_a2'][l])
    p['rwkv_g2'] = w['rwkv_g2'][l].astype(bf16)
    p['rwkv_pc'] = jnp.concatenate(
        [w['rwkv_w0'][l], w['rwkv_a0'][l], w['rwkv_k_k'][l][None], w['rwkv_k_a'][l][None],
         w['rwkv_r_k'][l].reshape(1, D_RWKV), jnp.zeros((1, D_RWKV), f32)], axis=0)
    p['rwkv_lnp'] = jnp.stack([w['rwkv_ln_g'][l], w['rwkv_ln_b'][l]], axis=0)
    p['head_ones'] = _block_diag(jnp.ones((N_RWKV_HEADS, RWKV_HEAD_DIM, RWKV_HEAD_DIM), bf16))

    p['diff_lambda'] = w['diff_lambda'][l]
    p['diff_subln_g'] = w['diff_subln_g'][l].reshape(1, HEAD_W)
    p['w_lru_out'] = w['w_lru_out'][l].astype(bf16)
    p['w_rwkv_out'] = w['w_rwkv_out'][l].astype(bf16)
    p['w_diff_out'] = w['w_diff_out'][l].astype(bf16)
    p['w_out'] = w['w_out'][l].astype(bf16)
    rw = jnp.pad(w['router_w'][l], ((0, 0), (0, ROUTER_PAD - N_EXPERTS)))
    p['router_hi'] = rw.astype(bf16)
    p['router_lo'] = (rw - p['router_hi'].astype(f32)).astype(bf16)
    return p


def kernel(x_prompt, x_sample, cache_k, cache_v, state_lru, state_rwkv, c, c_ctx, norm_g, final_norm_g, w_mod, b_mod, w_in, lru_conv_w, lru_conv_b, lru_wa, lru_ba, lru_wx, lru_bx, lru_lambda, rwkv_conv_w, rwkv_w0, rwkv_w2, rwkv_a0, rwkv_a2, rwkv_k_k, rwkv_k_a, rwkv_r_k, rwkv_g2, rwkv_ln_g, rwkv_ln_b, diff_lambda, diff_subln_g, w_lru_out, w_rwkv_out, w_diff_out, w_out, router_w, exp_w_gate, exp_w_up, exp_w_down):
    w = dict(norm_g=norm_g, w_in=w_in, lru_conv_w=lru_conv_w, lru_conv_b=lru_conv_b, lru_wa=lru_wa,
             lru_ba=lru_ba, lru_wx=lru_wx, lru_bx=lru_bx, lru_lambda=lru_lambda, rwkv_conv_w=rwkv_conv_w,
             rwkv_w0=rwkv_w0, rwkv_w2=rwkv_w2, rwkv_a0=rwkv_a0, rwkv_a2=rwkv_a2, rwkv_k_k=rwkv_k_k,
             rwkv_k_a=rwkv_k_a, rwkv_r_k=rwkv_r_k, rwkv_g2=rwkv_g2, rwkv_ln_g=rwkv_ln_g,
             rwkv_ln_b=rwkv_ln_b, diff_lambda=diff_lambda, diff_subln_g=diff_subln_g, w_lru_out=w_lru_out,
             w_rwkv_out=w_rwkv_out, w_diff_out=w_diff_out, w_out=w_out, router_w=router_w)

    x = jnp.concatenate([x_prompt.reshape(N_CTX_TOK, D_MODEL), x_sample.reshape(-1, D_MODEL)], axis=0)
    cvec = jnp.concatenate([c_ctx[None], c, jnp.zeros((SUBLANES - N_MOD_GROUPS, D_MODEL), f32)], axis=0)
    mod = _modulation(cvec, w_mod, b_mod)
    rope_cos, rope_sin = _rope_tables()
    final_g = final_norm_g.reshape(1, D_MODEL)
    ctx_rows = slice(0, N_CTX_TOK)
    lat_rows = slice(N_CTX_TOK, N_TOK)

    new_k, new_v, new_lru, new_rwkv = [], [], [], []
    for l in range(DEPTH):
        p = _prep_layer(l, w)
        lam_init = 0.8 - 0.6 * math.exp(-0.3 * l)
        mod_l = mod[l].reshape(SUBLANES, 1, 6 * D_MODEL)

        z = _in_projection(x, mod_l, p['norm_g1'], p['w_in'])
        new_k.append(z[ctx_rows, Z_K:Z_K + D_DIFF].reshape(N_CTX_SEQ, CTX_LEN, N_DIFF_HEADS, 2, DIFF_HEAD_DIM))
        new_v.append(z[ctx_rows, Z_V:Z_V + D_DIFF].reshape(N_CTX_SEQ, CTX_LEN, N_DIFF_HEADS, HEAD_W))

        h0_blocks = jnp.concatenate(
            [jnp.zeros((N_CTX_BLOCKS, 2, D_LRU), f32), state_lru[:, l].astype(f32)], axis=0)
        lru_out, lru_fin = _lru_branch(z, p, h0_blocks)
        n_per = SEQ_BLOCK // CTX_LEN
        fin = lru_fin[:N_CTX_BLOCKS].reshape(N_CTX_BLOCKS, 2, n_per, D_LRU)
        new_lru.append(jnp.transpose(fin, (0, 2, 1, 3)).reshape(N_CTX_SEQ, 2, D_LRU))

        scan_in, bonus, out_gate = _rwkv_pre(z, p)
        yf_c, yb_c, s_c = _rwkv_scan_path(scan_in, 0, N_CTX_SEQ, CTX_LEN, None)
        yf_l, yb_l, _ = _rwkv_scan_path(scan_in, N_CTX_TOK // LAT_LEN, N_LAT_SEQ, LAT_LEN, state_rwkv[:, l])
        new_rwkv.append(s_c)

        diff_out = _attention(z, cache_k[:, l].reshape(N_LAT_SEQ, PAST_LEN, D_DIFF),
                              cache_v[:, l].reshape(N_LAT_SEQ, PAST_LEN, D_DIFF),
                              rope_cos, rope_sin, p, lam_init)

        x, h2, logits = _merge(x, z, lru_out, (yf_c, yb_c), (yf_l, yb_l), bonus, out_gate, diff_out, mod_l, p)
        x = _moe_residual(x, h2, logits, exp_w_gate, exp_w_up, exp_w_down, l, mod_l, final_g,
                          final_norm=(l == DEPTH - 1))

    y_prompt = x[ctx_rows].reshape(x_prompt.shape)
    y_sample = x[lat_rows].reshape(x_sample.shape)
    return (y_prompt, y_sample, jnp.stack(new_k, axis=1), jnp.stack(new_v, axis=1),
            jnp.stack(new_lru, axis=1), jnp.stack(new_rwkv, axis=1))
```

```python
import functools
import math

import jax
import jax.numpy as jnp
from jax import lax
from jax.experimental import pallas as pl
from jax.experimental.pallas import tpu as pltpu

f32 = jnp.float32
bf16 = jnp.bfloat16

D_MODEL = 1024
DEPTH = 2
GRID_W = 64
NORM_EPS = 1e-6

N_CTX_SEQ = 16
CTX_LEN = 256
N_LAT_SEQ = 2
LAT_LEN = 1024
PAST_LEN = 256
N_CTX_TOK = N_CTX_SEQ * CTX_LEN
N_TOK = N_CTX_TOK + N_LAT_SEQ * LAT_LEN

D_LRU = 512
LRU_BLOCKS = 8
LRU_BLOCK_W = D_LRU // LRU_BLOCKS
LRU_C = 8.0

D_RWKV = 512
RWKV_HEAD_DIM = 64
N_RWKV_HEADS = D_RWKV // RWKV_HEAD_DIM
W_RANK = 64
A_RANK = 64
G_RANK = 128
RWKV_LN_EPS = 64e-5
RWKV_COLS = 3 * D_RWKV + 2 * W_RANK + 2 * A_RANK + G_RANK
RWKV_COLS_PAD = 2048

N_DIFF_HEADS = 4
DIFF_HEAD_DIM = 64
D_DIFF = N_DIFF_HEADS * 2 * DIFF_HEAD_DIM
ROPE_THETA = 10000.0

N_BRANCHES = 3
N_EXPERTS = 16
EXPERT_FF = 1024
EC_CAPACITY = 2
ROUTER_PAD = 128

LANES = 128
SUBLANES = 8

SEQ_BLOCK = 1024
N_SEQ_BLOCKS = N_TOK // SEQ_BLOCK
N_CTX_BLOCKS = N_CTX_TOK // SEQ_BLOCK
N_MOD_GROUPS = 1 + N_LAT_SEQ

Z_RWKV = 0
Z_GATES = RWKV_COLS_PAD
Z_LRU_X = Z_GATES + N_BRANCHES * D_MODEL
Z_LRU_G = Z_LRU_X + D_LRU
Z_Q = Z_LRU_G + D_LRU
Z_K = Z_Q + D_DIFF
Z_V = Z_K + D_DIFF
Z_COLS = Z_V + D_DIFF

VMEM_LIMIT = 56 * 1024 * 1024


def _cparams(n_axes):
    return pltpu.CompilerParams(dimension_semantics=("arbitrary",) * n_axes,
                                vmem_limit_bytes=VMEM_LIMIT)


def _seg_len(block_idx):
    return jnp.where(block_idx < N_CTX_BLOCKS, CTX_LEN, LAT_LEN)


def _mod_group_of_rows(row_block, rows_per_block):
    first_lat = N_CTX_TOK // rows_per_block
    per_seq = LAT_LEN // rows_per_block
    return jnp.where(row_block < first_lat, 0, (row_block - first_lat) // per_seq + 1)


def _shift_rows(v, d, tm, seg):
    n = v.shape[0]
    r = pltpu.roll(v, (-d) % n, axis=0)
    ok = (tm + d >= 0) & (tm + d < seg)
    return jnp.where(ok, r, 0.0)


def _split_dot(x, w_bf16):
    hi = x.astype(bf16)
    lo = (x - hi.astype(f32)).astype(bf16)
    return (jnp.dot(hi, w_bf16, preferred_element_type=f32)
            + jnp.dot(lo, w_bf16, preferred_element_type=f32))


MOD_TN = 1536


def _mod_kernel(c_ref, w_ref, b_ref, o_ref):
    c = c_ref[...]
    s = (c * jax.nn.sigmoid(c)).astype(bf16)
    o_ref[0] = jnp.dot(s, w_ref[0].astype(bf16), preferred_element_type=f32) + b_ref[0]


def _modulation(cvec, w_mod, b_mod):
    n_out = w_mod.shape[-1]
    return pl.pallas_call(
        _mod_kernel,
        grid=(DEPTH, n_out // MOD_TN),
        in_specs=[pl.BlockSpec((SUBLANES, D_MODEL), lambda l, j: (0, 0)),
                  pl.BlockSpec((1, D_MODEL, MOD_TN), lambda l, j: (l, 0, j)),
                  pl.BlockSpec((1, 1, MOD_TN), lambda l, j: (l, 0, j))],
        out_specs=pl.BlockSpec((1, SUBLANES, MOD_TN), lambda l, j: (l, 0, j)),
        out_shape=jax.ShapeDtypeStruct((DEPTH, SUBLANES, n_out), f32),
        compiler_params=_cparams(2),
        name="modulation",
    )(cvec, w_mod, b_mod.reshape(DEPTH, 1, n_out))


INPROJ_TM = 1024
INPROJ_TN = 768


def _inproj_kernel(x_ref, mod_ref, g_ref, w_ref, o_ref, h_ref):
    @pl.when(pl.program_id(1) == 0)
    def _():
        x = x_ref[...]
        y = x * lax.rsqrt(jnp.mean(x * x, axis=-1, keepdims=True) + NORM_EPS) * g_ref[...]
        shift = mod_ref[0, :, 0:D_MODEL]
        scale = mod_ref[0, :, D_MODEL:2 * D_MODEL]
        h_ref[...] = (y * (1.0 + scale) + shift).astype(bf16)

    o_ref[...] = jnp.dot(h_ref[...], w_ref[...], preferred_element_type=f32)


def _in_projection(x, mod_l, norm_g_row, w_in_p):
    return pl.pallas_call(
        _inproj_kernel,
        grid=(N_TOK // INPROJ_TM, Z_COLS // INPROJ_TN),
        in_specs=[pl.BlockSpec((INPROJ_TM, D_MODEL), lambda i, j: (i, 0)),
                  pl.BlockSpec((1, 1, 2 * D_MODEL),
                               lambda i, j: (_mod_group_of_rows(i, INPROJ_TM), 0, 0)),
                  pl.BlockSpec((1, D_MODEL), lambda i, j: (0, 0)),
                  pl.BlockSpec((D_MODEL, INPROJ_TN), lambda i, j: (0, j))],
        out_specs=pl.BlockSpec((INPROJ_TM, INPROJ_TN), lambda i, j: (i, j)),
        out_shape=jax.ShapeDtypeStruct((N_TOK, Z_COLS), f32),
        scratch_shapes=[pltpu.VMEM((INPROJ_TM, D_MODEL), bf16)],
        compiler_params=_cparams(2),
        name="in_projection",
    )(x, mod_l, norm_g_row, w_in_p)


LRU_TILE_UNROLL = 4


def _lru_kernel(x_ref, gate_ref, cw_ref, cb_ref, wg_ref, bg_ref, lam_ref, h0_ref, out_ref, hfin_ref,
                a_scr, b_scr, yf_scr, yb_scr):
    blk = pl.program_id(0)
    seg = _seg_len(blk)
    n = x_ref.shape[0]
    t = lax.broadcasted_iota(jnp.int32, (n, D_LRU), 0)
    tm = t & (seg - 1)

    x = x_ref[...]
    xc = (cb_ref[...]
          + cw_ref[0:1, :] * _shift_rows(x, -2, tm, seg)
          + cw_ref[1:2, :] * _shift_rows(x, -1, tm, seg)
          + cw_ref[2:3, :] * x
          + cw_ref[3:4, :] * _shift_rows(x, 1, tm, seg))
    pre = jnp.dot(xc.astype(bf16), wg_ref[...], preferred_element_type=f32) + bg_ref[...]

    n_tiles = n // SUBLANES
    tiles_per_seg = seg // SUBLANES
    row = lax.broadcasted_iota(jnp.int32, (SUBLANES, D_LRU), 0)
    for d, y_scr in enumerate((yf_scr, yb_scr)):
        r = jax.nn.sigmoid(pre[:, (2 * d) * D_LRU:(2 * d + 1) * D_LRU])
        i = jax.nn.sigmoid(pre[:, (2 * d + 1) * D_LRU:(2 * d + 2) * D_LRU])
        log_a = -LRU_C * r * jax.nn.softplus(-lam_ref[d:d + 1, :])
        a = jnp.exp(log_a)
        a_scr[...] = a
        b_scr[...] = jnp.sqrt(1.0 - a * a) * (i * xc)
        h0 = h0_ref[0, d:d + 1, :]

        def tile_step(k, h_prev, d=d, y_scr=y_scr, h0=h0):
            idx = k if d == 0 else n_tiles - 1 - k
            rows = pl.ds(pl.multiple_of(idx * SUBLANES, SUBLANES), SUBLANES)
            a_t = a_scr[rows, :]
            b_t = b_scr[rows, :]
            s = 1
            while s < SUBLANES:
                ok = (row >= s) if d == 0 else (row < SUBLANES - s)
                shift = s if d == 0 else SUBLANES - s
                a_sh = jnp.where(ok, pltpu.roll(a_t, shift, axis=0), 1.0)
                b_sh = jnp.where(ok, pltpu.roll(b_t, shift, axis=0), 0.0)
                b_t = a_t * b_sh + b_t
                a_t = a_t * a_sh
                s *= 2
            first_of_seg = (idx & (tiles_per_seg - 1)) == (0 if d == 0 else tiles_per_seg - 1)
            h_t = a_t * jnp.where(first_of_seg, h0, h_prev) + b_t
            y_scr[rows, :] = h_t
            return h_t[SUBLANES - 1:SUBLANES, :] if d == 0 else h_t[0:1, :]

        lax.fori_loop(0, n_tiles, tile_step, jnp.zeros((1, D_LRU), f32), unroll=LRU_TILE_UNROLL)

    out_ref[...] = ((yf_scr[...] + yb_scr[...]) * jax.nn.gelu(gate_ref[...])).astype(out_ref.dtype)
    fins = ([yf_scr[(j + 1) * CTX_LEN - 1:(j + 1) * CTX_LEN, :] for j in range(SEQ_BLOCK // CTX_LEN)]
            + [yb_scr[j * CTX_LEN:j * CTX_LEN + 1, :] for j in range(SEQ_BLOCK // CTX_LEN)])
    hfin_ref[0] = jnp.concatenate(fins, axis=0)


def _lru_branch(z, p, h0_blocks):
    full = lambda shape: pl.BlockSpec(shape, lambda i: (0,) * len(shape))
    return pl.pallas_call(
        _lru_kernel,
        grid=(N_SEQ_BLOCKS,),
        in_specs=[pl.BlockSpec((SEQ_BLOCK, D_LRU), lambda i: (i, Z_LRU_X // D_LRU)),
                  pl.BlockSpec((SEQ_BLOCK, D_LRU), lambda i: (i, Z_LRU_G // D_LRU)),
                  full((4, D_LRU)), full((1, D_LRU)), full((D_LRU, 4 * D_LRU)), full((1, 4 * D_LRU)),
                  full((2, D_LRU)),
                  pl.BlockSpec((1, 2, D_LRU), lambda i: (i, 0, 0))],
        out_specs=[pl.BlockSpec((SEQ_BLOCK, D_LRU), lambda i: (i, 0)),
                   pl.BlockSpec((1, SUBLANES, D_LRU), lambda i: (i, 0, 0))],
        out_shape=[jax.ShapeDtypeStruct((N_TOK, D_LRU), bf16),
                   jax.ShapeDtypeStruct((N_SEQ_BLOCKS, SUBLANES, D_LRU), f32)],
        scratch_shapes=[pltpu.VMEM((SEQ_BLOCK, D_LRU), f32)] * 4,
        compiler_params=_cparams(1),
        name="lru_branch",
    )(z, z, p['lru_conv_w'], p['lru_conv_b'], p['lru_wg'], p['lru_bg'], p['lru_lambda'], h0_blocks)


RWKV_CT = 256
N_RWKV_CT = D_RWKV // RWKV_CT
RWKV_LOWRANK_OFF = 3 * D_RWKV
RWKV_LOWRANK_W = RWKV_COLS_PAD - RWKV_LOWRANK_OFF
SCAN_ARRAYS = ('w_f', 'w_b', 'ka_f', 'ka_b', 'kd_f', 'kd_b', 'nkk', 'r', 'v')
N_SCAN_ARRAYS = len(SCAN_ARRAYS)
SCAN_IDX = {name: i for i, name in enumerate(SCAN_ARRAYS)}


def _rwkv_pre_kernel(r_ref, k_ref, v_ref, lr_ref, cwr_ref, cwk_ref, cwv_ref, cwl_ref,
                     w2_ref, a2_ref, g2_ref, pc_ref, ones_ref, sc_ref, bonus_ref, g_ref):
    blk = pl.program_id(0)
    seg = _seg_len(blk)
    n = r_ref.shape[0]

    def conv(x_ref, cw_ref):
        width = x_ref.shape[1]
        t = lax.broadcasted_iota(jnp.int32, (n, width), 0)
        tm = t & (seg - 1)
        x = x_ref[...]
        return (cw_ref[0:1, :] * _shift_rows(x, -1, tm, seg) + cw_ref[1:2, :] * x
                + cw_ref[2:3, :] * _shift_rows(x, 1, tm, seg))

    r = conv(r_ref, cwr_ref)
    k = conv(k_ref, cwk_ref)
    v = conv(v_ref, cwv_ref)
    lr = conv(lr_ref, cwl_ref)
    wl = lr[:, 0:2 * W_RANK]
    al = lr[:, 2 * W_RANK:2 * W_RANK + 2 * A_RANK]
    gl = lr[:, 2 * W_RANK + 2 * A_RANK:2 * W_RANK + 2 * A_RANK + G_RANK]

    w0 = [pc_ref[0:1, :], pc_ref[1:2, :]]
    a0 = [pc_ref[2:3, :], pc_ref[3:4, :]]
    k_k = pc_ref[4:5, :]
    k_a = pc_ref[5:6, :]
    r_k = pc_ref[6:7, :]

    w_pre = jnp.dot(jnp.tanh(wl).astype(bf16), w2_ref[0], preferred_element_type=f32)
    a_pre = jnp.dot(al.astype(bf16), a2_ref[0], preferred_element_type=f32)

    kk = k * k_k
    ss = _split_dot(kk * kk, ones_ref[...])
    kk = kk / jnp.maximum(jnp.sqrt(ss), 1e-12)

    kd_sum = None
    for d, sfx in enumerate(('_f', '_b')):
        cols = slice(d * RWKV_CT, (d + 1) * RWKV_CT)
        w_log = -jax.nn.softplus(-(w0[d] + w_pre[:, cols])) - 0.5
        sc_ref[SCAN_IDX['w' + sfx]] = jnp.exp(-jnp.exp(w_log))
        a = jax.nn.sigmoid(a0[d] + a_pre[:, cols])
        sc_ref[SCAN_IDX['ka' + sfx]] = kk * a
        kd = k * (1.0 + (a - 1.0) * k_a)
        sc_ref[SCAN_IDX['kd' + sfx]] = kd
        kd_sum = kd if kd_sum is None else kd_sum + kd

    sc_ref[SCAN_IDX['r']] = r
    sc_ref[SCAN_IDX['v']] = v
    sc_ref[SCAN_IDX['nkk']] = -kk
    bonus_ref[...] = _split_dot(r * kd_sum * r_k, ones_ref[...]) * v
    g_ref[...] = jnp.dot(jax.nn.sigmoid(gl).astype(bf16), g2_ref[...], preferred_element_type=f32)


def _rwkv_pre(z, p):
    nct = D_RWKV // RWKV_CT
    zcol = lambda base: pl.BlockSpec((SEQ_BLOCK, RWKV_CT), lambda i, c, base=base: (i, base + c))
    cwcol = lambda base: pl.BlockSpec((3, RWKV_CT), lambda i, c, base=base: (0, base + c))
    out_spec = pl.BlockSpec((SEQ_BLOCK, RWKV_CT), lambda i, c: (i, c))
    return pl.pallas_call(
        _rwkv_pre_kernel,
        grid=(N_SEQ_BLOCKS, nct),
        in_specs=[zcol(0), zcol(nct), zcol(2 * nct),
                  pl.BlockSpec((SEQ_BLOCK, RWKV_LOWRANK_W),
                               lambda i, c: (i, RWKV_LOWRANK_OFF // RWKV_LOWRANK_W)),
                  cwcol(0), cwcol(nct), cwcol(2 * nct),
                  pl.BlockSpec((3, RWKV_LOWRANK_W), lambda i, c: (0, RWKV_LOWRANK_OFF // RWKV_LOWRANK_W)),
                  pl.BlockSpec((1, 2 * W_RANK, 2 * RWKV_CT), lambda i, c: (c, 0, 0)),
                  pl.BlockSpec((1, 2 * A_RANK, 2 * RWKV_CT), lambda i, c: (c, 0, 0)),
                  pl.BlockSpec((G_RANK, RWKV_CT), lambda i, c: (0, c)),
                  pl.BlockSpec((SUBLANES, RWKV_CT), lambda i, c: (0, c)),
                  pl.BlockSpec((RWKV_CT, RWKV_CT), lambda i, c: (0, 0))],
        out_specs=[pl.BlockSpec((N_SCAN_ARRAYS, SEQ_BLOCK, RWKV_CT), lambda i, c: (0, i, c)),
                   out_spec, out_spec],
        out_shape=[jax.ShapeDtypeStruct((N_SCAN_ARRAYS, N_TOK, D_RWKV), f32),
                   jax.ShapeDtypeStruct((N_TOK, D_RWKV), f32),
                   jax.ShapeDtypeStruct((N_TOK, D_RWKV), f32)],
        compiler_params=_cparams(2),
        name="rwkv_pre",
    )(z, z, z, z, p['rwkv_conv_w'], p['rwkv_conv_w'], p['rwkv_conv_w'], p['rwkv_conv_w'],
      p['rwkv_w2'], p['rwkv_a2'], p['rwkv_g2'], p['rwkv_pc'], p['head_ones'][:RWKV_CT, :RWKV_CT])


PACK_T = 128
TIME_BLOCK = SUBLANES
PACK_UNROLL = 8


def _gather_states(scr, row, n_seq):
    return scr[:, pl.ds(row, N_RWKV_HEADS, stride=RWKV_HEAD_DIM), :].reshape(n_seq * N_RWKV_HEADS, PACK_T)


def _pack_kernel(x_ref, o_ref, scr, *, n_seq, reps, rows, fold_rows):
    for b in range(n_seq):
        scr[b] = x_ref[b].T

    def body(row, c):
        if fold_rows:
            m = jnp.concatenate([_gather_states(scr, row * reps + rep, n_seq) for rep in range(reps)], axis=0)
        else:
            m = _gather_states(scr, row, n_seq)
            if reps > 1:
                m = jnp.concatenate([m] * reps, axis=0)
        mt = m.T
        line0 = pl.multiple_of(row * TIME_BLOCK, TIME_BLOCK)
        for u in range(PACK_T // TIME_BLOCK):
            o_ref[u, pl.ds(line0, TIME_BLOCK), :] = mt[u * TIME_BLOCK:(u + 1) * TIME_BLOCK]
        return c

    lax.fori_loop(0, rows, body, 0, unroll=PACK_UNROLL)


def _pack(stacked, first, count, n_seq, seq_len, seq0, reps, fold_rows):
    rows = RWKV_HEAD_DIM // reps if fold_rows else RWKV_HEAD_DIM
    x = stacked.reshape(N_SCAN_ARRAYS, N_TOK // seq_len, seq_len, D_RWKV)
    tb = PACK_T // TIME_BLOCK
    return pl.pallas_call(
        functools.partial(_pack_kernel, n_seq=n_seq, reps=reps, rows=rows, fold_rows=fold_rows),
        grid=(count, seq_len // PACK_T),
        in_specs=[pl.BlockSpec((None, n_seq, PACK_T, D_RWKV),
                               lambda a, j: (first + a, seq0 // n_seq, j, 0))],
        out_specs=pl.BlockSpec((None, tb, rows * TIME_BLOCK, LANES), lambda a, j: (a, j, 0, 0)),
        out_shape=jax.ShapeDtypeStruct((count, seq_len // TIME_BLOCK, rows * TIME_BLOCK, LANES), f32),
        scratch_shapes=[pltpu.VMEM((n_seq, D_RWKV, PACK_T), f32)],
        compiler_params=_cparams(2),
        name="rwkv_pack",
    )(x)


def _unpack_kernel(yf_ref, yb_ref, of_ref, ob_ref, scr, *, n_seq, reps, rows):
    group = n_seq * N_RWKV_HEADS
    for y_ref, o_ref in ((yf_ref, of_ref), (yb_ref, ob_ref)):
        def body(row, c, y_ref=y_ref):
            line0 = pl.multiple_of(row * TIME_BLOCK, TIME_BLOCK)
            mt = jnp.concatenate([y_ref[u, pl.ds(line0, TIME_BLOCK), :]
                                  for u in range(PACK_T // TIME_BLOCK)], axis=0)
            m = mt.T
            for rep in range(reps):
                part = m[rep * group:(rep + 1) * group].reshape(n_seq, N_RWKV_HEADS, PACK_T)
                scr[:, pl.ds(row * reps + rep, N_RWKV_HEADS, stride=RWKV_HEAD_DIM), :] = part
            return c

        lax.fori_loop(0, rows, body, 0, unroll=PACK_UNROLL)
        for b in range(n_seq):
            o_ref[b] = scr[b].T


def _unpack(y_f, y_b, n_seq, seq_len, reps):
    rows = y_f.shape[1] // TIME_BLOCK
    yspec = pl.BlockSpec((PACK_T // TIME_BLOCK, rows * TIME_BLOCK, LANES), lambda j: (j, 0, 0))
    ospec = pl.BlockSpec((n_seq, PACK_T, D_RWKV), lambda j: (0, j, 0))
    shape = jax.ShapeDtypeStruct((n_seq, seq_len, D_RWKV), f32)
    o_f, o_b = pl.pallas_call(
        functools.partial(_unpack_kernel, n_seq=n_seq, reps=reps, rows=rows),
        grid=(seq_len // PACK_T,),
        in_specs=[yspec, yspec],
        out_specs=[ospec, ospec],
        out_shape=[shape, shape],
        scratch_shapes=[pltpu.VMEM((n_seq, D_RWKV, PACK_T), f32)],
        compiler_params=_cparams(1),
        name="rwkv_unpack",
    )(y_f, y_b)
    return o_f.reshape(n_seq * seq_len, D_RWKV), o_b.reshape(n_seq * seq_len, D_RWKV)


def _rwkv_scan_kernel(wf_ref, wb_ref, kaf_ref, kab_ref, kdf_ref, kdb_ref, nkkf_ref, nkkb_ref,
                      rf_ref, rb_ref, vvf_ref, vvb_ref, s0_ref, yf_ref, yb_ref, s_ref, *,
                      time_blocks, v_rows, unroll):
    @pl.when(pl.program_id(0) == 0)
    def _():
        s_ref[...] = s0_ref[...]

    dirs = ((wf_ref, kaf_ref, kdf_ref, nkkf_ref, rf_ref, vvf_ref, yf_ref),
            (wb_ref, kab_ref, kdb_ref, nkkb_ref, rb_ref, vvb_ref, yb_ref))
    n_k = s_ref.shape[1]

    for u in range(time_blocks):
        def step(t, carry, u=u):
            for g, (w_ref, ka_ref, kd_ref, nkk_ref, r_ref, vv_ref, y_ref) in enumerate(dirs):
                ub = u if g == 0 else time_blocks - 1 - u
                tt = t if g == 0 else TIME_BLOCK - 1 - t
                lanes = slice(g * LANES, (g + 1) * LANES)
                at_t = pl.ds(tt, n_k, stride=TIME_BLOCK)
                w = w_ref[ub, at_t, :]
                ka = ka_ref[ub, at_t, :]
                kd = kd_ref[ub, at_t, :]
                nkk = nkk_ref[ub, at_t, :]
                r = r_ref[ub, at_t, :]

                def rows(vb, c):
                    v0 = pl.multiple_of(vb * SUBLANES, SUBLANES)
                    rows_at_t = pl.ds(v0 * TIME_BLOCK + tt, SUBLANES, stride=TIME_BLOCK)
                    vv = vv_ref[ub, rows_at_t, :]
                    ys = []
                    for j in range(SUBLANES):
                        s = s_ref[v0 + j, :, lanes]
                        sa = jnp.sum(s * nkk, axis=0, keepdims=True)
                        s_new = s * w + ka * sa + kd * vv[j:j + 1]
                        s_ref[v0 + j, :, lanes] = s_new
                        ys.append(jnp.sum(s_new * r, axis=0, keepdims=True))
                    y_ref[ub, rows_at_t, :] = jnp.concatenate(ys, axis=0)
                    return c

                lax.fori_loop(0, v_rows // SUBLANES, rows, 0, unroll=unroll)
            return carry

        lax.fori_loop(0, TIME_BLOCK, step, 0, unroll=2 if v_rows == SUBLANES else 1)


SCAN_STEPS_PER_BLOCK = 16


def _rwkv_scan_packed(kvecs, vv, s0, *, steps):
    K = kvecs.shape[2] // TIME_BLOCK
    L = kvecs.shape[1] * TIME_BLOCK
    VR = vv.shape[1] // TIME_BLOCK
    nblk = L // steps
    tbs = steps // TIME_BLOCK

    def kspec(name, mirrored):
        a = SCAN_IDX[name]
        if mirrored:
            return pl.BlockSpec((None, tbs, K * TIME_BLOCK, LANES), lambda i: (a, nblk - 1 - i, 0, 0))
        return pl.BlockSpec((None, tbs, K * TIME_BLOCK, LANES), lambda i: (a, i, 0, 0))

    vf = pl.BlockSpec((tbs, VR * TIME_BLOCK, LANES), lambda i: (i, 0, 0))
    vb = pl.BlockSpec((tbs, VR * TIME_BLOCK, LANES), lambda i: (nblk - 1 - i, 0, 0))
    sspec = pl.BlockSpec((VR, K, 2 * LANES), lambda i: (0, 0, 0))
    kern = functools.partial(_rwkv_scan_kernel, time_blocks=tbs, v_rows=VR,
                             unroll=min(8, VR // SUBLANES))
    return pl.pallas_call(
        kern,
        grid=(nblk,),
        in_specs=[kspec('w_f', False), kspec('w_b', True), kspec('ka_f', False), kspec('ka_b', True),
                  kspec('kd_f', False), kspec('kd_b', True), kspec('nkk', False), kspec('nkk', True),
                  kspec('r', False), kspec('r', True), vf, vb, sspec],
        out_specs=[vf, vb, sspec],
        out_shape=[jax.ShapeDtypeStruct((L // TIME_BLOCK, VR * TIME_BLOCK, LANES), f32),
                   jax.ShapeDtypeStruct((L // TIME_BLOCK, VR * TIME_BLOCK, LANES), f32),
                   jax.ShapeDtypeStruct((VR, K, 2 * LANES), f32)],
        compiler_params=_cparams(1),
        name="rwkv7_scan",
    )(*([kvecs] * 10), vv, vv, s0)


def _rwkv_scan_path(stacked, seq0, B, L, S0):
    H, K = N_RWKV_HEADS, RWKV_HEAD_DIM
    nd = 2
    reps = LANES // (B * H)
    VR = K // reps
    kvecs = _pack(stacked, 0, N_SCAN_ARRAYS - 1, B, L, seq0, reps, fold_rows=False)
    vv = _pack(stacked, SCAN_IDX['v'], 1, B, L, seq0, reps, fold_rows=True)[0]
    if S0 is None:
        s0 = jnp.zeros((VR, K, nd * LANES), f32)
    else:
        s0 = S0.astype(f32).reshape(B, nd, H, VR, reps, K)
        s0 = jnp.transpose(s0, (3, 5, 1, 4, 0, 2)).reshape(VR, K, nd * LANES)

    y_f, y_b, s_fin = _rwkv_scan_packed(kvecs, vv, s0, steps=SCAN_STEPS_PER_BLOCK)
    y_f, y_b = _unpack(y_f, y_b, B, L, reps)
    s_fin = s_fin.reshape(VR, K, nd, reps, B, H)
    s_fin = jnp.transpose(s_fin, (4, 2, 5, 0, 3, 1)).reshape(B, nd, H, K, K)
    return y_f, y_b, s_fin


ATT_TQ = 256
HEAD_W = 2 * DIFF_HEAD_DIM


def _rope(x, cos, sin_signed, lo_mask):
    n = x.shape[1]
    partner = jnp.where(lo_mask, pltpu.roll(x, n - DIFF_HEAD_DIM // 4, axis=1),
                        pltpu.roll(x, DIFF_HEAD_DIM // 4, axis=1))
    return x * cos + partner * sin_signed


def _attn_heads(q, k, v, lam, subln_g, out_scale):
    lane = lax.broadcasted_iota(jnp.int32, (1, HEAD_W), 1)
    first = lane < DIFF_HEAD_DIM
    outs = []
    for h in range(N_DIFF_HEADS):
        cols = slice(h * HEAD_W, (h + 1) * HEAD_W)
        qh = (q[:, cols] * (DIFF_HEAD_DIM ** -0.5)).astype(bf16)
        kh = k[:, cols]
        vh = v[:, cols].astype(bf16)
        probs = []
        for m in range(2):
            km = jnp.where(first if m == 0 else ~first, kh, 0.0).astype(bf16)
            s = lax.dot_general(qh, km, (((1,), (1,)), ((), ())), preferred_element_type=f32)
            s = s - jnp.max(s, axis=-1, keepdims=True)
            e = jnp.exp(s)
            probs.append(e / jnp.sum(e, axis=-1, keepdims=True))
        attn = (probs[0] - lam * probs[1]).astype(bf16)
        o = jnp.dot(attn, vh, preferred_element_type=f32)
        o = o * lax.rsqrt(jnp.mean(o * o, axis=-1, keepdims=True) + NORM_EPS) * subln_g * out_scale
        outs.append(o)
    return jnp.concatenate(outs, axis=1)


def _attn_kernel(q_ref, k_ref, v_ref, ck_ref, cv_ref, cos_ref, sin_ref, lv_ref, g_ref, o_ref,
                 kall_ref, vall_ref, *, lam_init):
    blk = pl.program_id(0)
    qt = pl.program_id(1)
    lv = lv_ref[...]
    lam = (jnp.exp(jnp.sum(lv[0:1] * lv[1:2], axis=-1, keepdims=True))
           - jnp.exp(jnp.sum(lv[2:3] * lv[3:4], axis=-1, keepdims=True)) + lam_init)
    subln_g = g_ref[...]
    out_scale = 1.0 - lam_init
    row0 = pl.multiple_of(qt * ATT_TQ, ATT_TQ)

    @pl.when(blk < N_CTX_BLOCKS)
    def _():
        q = q_ref[pl.ds(row0, ATT_TQ), :]
        k = k_ref[pl.ds(row0, ATT_TQ), :]
        v = v_ref[pl.ds(row0, ATT_TQ), :]
        o_ref[...] = _attn_heads(q, k, v, lam, subln_g, out_scale).astype(o_ref.dtype)

    @pl.when(blk >= N_CTX_BLOCKS)
    def _():
        lane = lax.broadcasted_iota(jnp.int32, (1, D_DIFF), 1)
        lo_mask = (lane % (DIFF_HEAD_DIM // 2)) < (DIFF_HEAD_DIM // 4)

        @pl.when(qt == 0)
        def _():
            kall_ref[0:LAT_LEN, :] = _rope(k_ref[...], cos_ref[...], sin_ref[...], lo_mask)
            kall_ref[LAT_LEN:LAT_LEN + PAST_LEN, :] = ck_ref[0]
            vall_ref[0:LAT_LEN, :] = v_ref[...]
            vall_ref[LAT_LEN:LAT_LEN + PAST_LEN, :] = cv_ref[0]

        q = _rope(q_ref[pl.ds(row0, ATT_TQ), :], cos_ref[pl.ds(row0, ATT_TQ), :],
                  sin_ref[pl.ds(row0, ATT_TQ), :], lo_mask)
        o_ref[...] = _attn_heads(q, kall_ref[...], vall_ref[...], lam, subln_g,
                                 out_scale).astype(o_ref.dtype)


def _attention(z, cache_k_l, cache_v_l, rope_cos, rope_sin, p, lam_init):
    zcol = lambda off: pl.BlockSpec((SEQ_BLOCK, D_DIFF), lambda i, j, off=off: (i, off // D_DIFF))
    cache = pl.BlockSpec((1, PAST_LEN, D_DIFF), lambda i, j: (jnp.maximum(i - N_CTX_BLOCKS, 0), 0, 0))
    full = lambda shape: pl.BlockSpec(shape, lambda i, j: (0,) * len(shape))
    return pl.pallas_call(
        functools.partial(_attn_kernel, lam_init=lam_init),
        grid=(N_SEQ_BLOCKS, SEQ_BLOCK // ATT_TQ),
        in_specs=[zcol(Z_Q), zcol(Z_K), zcol(Z_V), cache, cache,
                  full((LAT_LEN, D_DIFF)), full((LAT_LEN, D_DIFF)),
                  full((4, DIFF_HEAD_DIM)), full((1, HEAD_W))],
        out_specs=pl.BlockSpec((ATT_TQ, D_DIFF), lambda i, j: (i * (SEQ_BLOCK // ATT_TQ) + j, 0)),
        out_shape=jax.ShapeDtypeStruct((N_TOK, D_DIFF), bf16),
        scratch_shapes=[pltpu.VMEM((LAT_LEN + PAST_LEN, D_DIFF), f32),
                        pltpu.VMEM((LAT_LEN + PAST_LEN, D_DIFF), f32)],
        compiler_params=_cparams(2),
        name="diff_attention",
    )(z, z, z, cache_k_l, cache_v_l, rope_cos, rope_sin, p['diff_lambda'], p['diff_subln_g'])


def _rope_tables():
    t = jnp.arange(LAT_LEN)
    row = (t // GRID_W).astype(f32)
    col = (t % GRID_W).astype(f32)
    nf = DIFF_HEAD_DIM // 4
    freqs = ROPE_THETA ** (-jnp.arange(nf, dtype=f32) / nf)
    d = jnp.arange(DIFF_HEAD_DIM)
    pos = jnp.where((d < DIFF_HEAD_DIM // 2)[None, :], row[:, None], col[:, None])
    ang = pos * freqs[d % nf][None, :]
    sign = jnp.where((d % (2 * nf)) < nf, -1.0, 1.0)[None, :]
    cos = jnp.tile(jnp.cos(ang), (1, D_DIFF // DIFF_HEAD_DIM))
    sin = jnp.tile(jnp.sin(ang) * sign, (1, D_DIFF // DIFF_HEAD_DIM))
    return cos, sin


MERGE_TM = 512


def _merge_kernel(x_ref, lru_ref, yfc_ref, ybc_ref, yfl_ref, ybl_ref, bonus_ref, g_ref, diff_ref,
                  gp0_ref, gp1_ref, gp2_ref,
                  mod_ref, lnp_ref, ones_ref, wlo_ref, wro_ref, wdo_ref, wout_ref, ng_ref, rhi_ref, rlo_ref,
                  xo_ref, h2_ref, logit_ref):
    is_ctx = pl.program_id(0) < N_CTX_TOK // MERGE_TM
    y = jnp.where(is_ctx, yfc_ref[...] + ybc_ref[...], yfl_ref[...] + ybl_ref[...])
    inv = 1.0 / RWKV_HEAD_DIM
    mu = _split_dot(y, ones_ref[...]) * inv
    dlt = y - mu
    var = _split_dot(dlt * dlt, ones_ref[...]) * inv
    yn = dlt * lax.rsqrt(var + RWKV_LN_EPS)
    rwkv_out = ((yn * lnp_ref[0:1, :] + lnp_ref[1:2, :] + bonus_ref[...]) * g_ref[...]).astype(bf16)

    merged = (jax.nn.sigmoid(gp0_ref[...]) * jnp.dot(lru_ref[...], wlo_ref[...], preferred_element_type=f32)
              + jax.nn.sigmoid(gp1_ref[...]) * jnp.dot(rwkv_out, wro_ref[...], preferred_element_type=f32)
              + jax.nn.sigmoid(gp2_ref[...]) * jnp.dot(diff_ref[...], wdo_ref[...], preferred_element_type=f32))
    mix = jnp.dot(merged.astype(bf16), wout_ref[...], preferred_element_type=f32)

    gate1 = mod_ref[0, :, 2 * D_MODEL:3 * D_MODEL]
    shift2 = mod_ref[0, :, 3 * D_MODEL:4 * D_MODEL]
    scale2 = mod_ref[0, :, 4 * D_MODEL:5 * D_MODEL]
    x = x_ref[...] + gate1 * mix
    xo_ref[...] = x
    yn2 = x * lax.rsqrt(jnp.mean(x * x, axis=-1, keepdims=True) + NORM_EPS) * ng_ref[...]
    h2 = yn2 * (1.0 + scale2) + shift2
    hi = h2.astype(bf16)
    lo = (h2 - hi.astype(f32)).astype(bf16)
    h2_ref[...] = hi
    logit_ref[...] = (jnp.dot(hi, rhi_ref[...], preferred_element_type=f32)
                      + jnp.dot(lo, rhi_ref[...], preferred_element_type=f32)
                      + jnp.dot(hi, rlo_ref[...], preferred_element_type=f32))


def _merge(x, z, lru_out, y_ctx, y_lat, bonus, g, diff_out, mod_l, p):
    n_ctx_tiles = N_CTX_TOK // MERGE_TM
    row = lambda w: pl.BlockSpec((MERGE_TM, w), lambda i: (i, 0))
    ctx_row = pl.BlockSpec((MERGE_TM, D_RWKV), lambda i: (jnp.minimum(i, n_ctx_tiles - 1), 0))
    lat_row = pl.BlockSpec((MERGE_TM, D_RWKV), lambda i: (jnp.maximum(i - n_ctx_tiles, 0), 0))
    gate = lambda b: pl.BlockSpec((MERGE_TM, D_MODEL), lambda i, b=b: (i, Z_GATES // D_MODEL + b))
    full = lambda shape: pl.BlockSpec(shape, lambda i: (0,) * len(shape))
    return pl.pallas_call(
        _merge_kernel,
        grid=(N_TOK // MERGE_TM,),
        in_specs=[row(D_MODEL), row(D_LRU), ctx_row, ctx_row, lat_row, lat_row,
                  row(D_RWKV), row(D_RWKV), row(D_DIFF),
                  gate(0), gate(1), gate(2),
                  pl.BlockSpec((1, 1, 6 * D_MODEL), lambda i: (_mod_group_of_rows(i, MERGE_TM), 0, 0)),
                  full((2, D_RWKV)), full((D_RWKV, D_RWKV)),
                  full((D_LRU, D_MODEL)), full((D_RWKV, D_MODEL)), full((D_DIFF, D_MODEL)),
                  full((D_MODEL, D_MODEL)), full((1, D_MODEL)),
                  full((D_MODEL, ROUTER_PAD)), full((D_MODEL, ROUTER_PAD))],
        out_specs=[row(D_MODEL), row(D_MODEL), row(ROUTER_PAD)],
        out_shape=[jax.ShapeDtypeStruct((N_TOK, D_MODEL), f32),
                   jax.ShapeDtypeStruct((N_TOK, D_MODEL), bf16),
                   jax.ShapeDtypeStruct((N_TOK, ROUTER_PAD), f32)],
        compiler_params=_cparams(1),
        name="merge",
    )(x, lru_out, *y_ctx, *y_lat, bonus, g, diff_out, z, z, z, mod_l, p['rwkv_lnp'], p['head_ones'],
      p['w_lru_out'], p['w_rwkv_out'], p['w_diff_out'], p['w_out'], p['norm_g2'],
      p['router_hi'], p['router_lo'])


CAP_CTX = EC_CAPACITY * N_CTX_TOK // N_EXPERTS
CAP_LAT = EC_CAPACITY * (N_TOK - N_CTX_TOK) // N_EXPERTS
N_SLOTS = CAP_CTX + CAP_LAT
F32_MANTISSA_BITS = 23
F32_EXPONENT_BIAS = 127
LADDER_STEPS = 8
BISECT_STEPS = 60


def _cumsum_lanes(x01):
    rows, n = x01.shape
    i = lax.broadcasted_iota(jnp.int32, (LANES, LANES), 0)
    j = lax.broadcasted_iota(jnp.int32, (LANES, LANES), 1)
    upper = jnp.where(i <= j, 1.0, 0.0).astype(bf16)
    off = jnp.zeros((rows, 1), f32)
    outs = []
    for t in range(n // LANES):
        c = jnp.dot(x01[:, t * LANES:(t + 1) * LANES].astype(bf16), upper, preferred_element_type=f32) + off
        outs.append(c)
        off = c[:, LANES - 1:LANES]
    return jnp.concatenate(outs, axis=1)


def _route_kernel(logit_ref, pos_ref, gate_ref, *, cap, slot0):
    n = logit_ref.shape[0]
    lane = lax.broadcasted_iota(jnp.int32, (1, ROUTER_PAD), 1)
    lg = jnp.where(lane < N_EXPERTS, logit_ref[...], -jnp.inf)
    ex = jnp.exp(lg - jnp.max(lg, axis=-1, keepdims=True))
    aff = (ex / jnp.sum(ex, axis=-1, keepdims=True)).T[:N_EXPERTS]

    def n_above(thr):
        return jnp.sum(jnp.where(aff > thr, 1.0, 0.0), axis=1, keepdims=True)

    def rung(j):
        return pltpu.bitcast(jnp.left_shift(j, F32_MANTISSA_BITS), f32)

    def pick_rung(_, jj):
        j_lo, j_hi = jj
        j_mid = (j_lo + j_hi) >> 1
        enough = n_above(rung(j_mid)) >= cap
        return jnp.where(enough, j_mid, j_lo), jnp.where(enough, j_hi, j_mid)

    j_lo, j_hi = lax.fori_loop(0, LADDER_STEPS, pick_rung,
                               (jnp.full((N_EXPERTS, 1), -1, jnp.int32),
                                jnp.full((N_EXPERTS, 1), F32_EXPONENT_BIAS + 1, jnp.int32)))

    def bisect(_, lh):
        lo, hi = lh
        mid = 0.5 * (lo + hi)
        enough = n_above(mid) >= cap
        return jnp.where(enough, mid, lo), jnp.where(enough, hi, mid)

    lo, hi = lax.fori_loop(0, BISECT_STEPS, bisect, (jnp.where(j_lo < 0, -1.0, rung(j_lo)), rung(j_hi)))
    above = aff > hi
    tie = (aff > lo) & (aff <= hi)
    need = cap - n_above(hi)
    keep = above | (tie & (_cumsum_lanes(jnp.where(tie, 1.0, 0.0)) <= need))
    slot = _cumsum_lanes(jnp.where(keep, 1.0, 0.0)) - 1.0 + slot0
    pos_ref[...] = jnp.where(keep, slot, -1.0).astype(jnp.int32)
    gate_ref[...] = jnp.where(keep, aff, 0.0)


def _route(logits, block, n, cap, slot0):
    shape = (N_EXPERTS, n)
    pos, gate = pl.pallas_call(
        functools.partial(_route_kernel, cap=cap, slot0=slot0),
        grid=(1,),
        in_specs=[pl.BlockSpec((n, ROUTER_PAD), lambda i: (block, 0))],
        out_specs=[pl.BlockSpec(shape, lambda i: (0, 0))] * 2,
        out_shape=[jax.ShapeDtypeStruct(shape, jnp.int32), jax.ShapeDtypeStruct(shape, f32)],
        compiler_params=_cparams(1),
        name="route",
    )(logits)
    return pos.reshape(N_EXPERTS, 1, n), gate.reshape(N_EXPERTS, 1, n)


FFN_TF = 512
MOE_TT = 1024


def _one_hot(pos_row, slot0, n_slots):
    slots = lax.broadcasted_iota(jnp.int32, (n_slots, pos_row.shape[1]), 0) + slot0
    return pos_row == slots


N_MOE_TILES = N_TOK // MOE_TT
SLOT_CHUNK_CTX = CAP_CTX // 2
SLOT_CHUNK_LAT = CAP_LAT // 2


def _slot_chunks(is_ctx):
    slot0, n_slots, size = (0, CAP_CTX, SLOT_CHUNK_CTX) if is_ctx else (CAP_CTX, CAP_LAT, SLOT_CHUNK_LAT)
    return [(slot0 + c * size, size) for c in range(n_slots // size)]


def _tile_slot_ranges(pos):
    p = pos.reshape(N_EXPERTS, N_MOE_TILES, MOE_TT)
    lo = jnp.min(jnp.where(p >= 0, p, N_SLOTS), axis=2)
    hi = jnp.max(p, axis=2) + 1
    return jnp.stack([lo, hi]).astype(jnp.int32)


def _ffn_kernel(rng_ref, pos_ref, gate_ref, h_ref, wg_ref, wu_ref, wd_ref, o_ref, xe_ref, gs_ref, acc_ref):
    e = pl.program_id(0)
    f = pl.program_id(1)

    @pl.when(f == 0)
    def _():
        xe_ref[...] = jnp.zeros_like(xe_ref)
        gs_ref[...] = jnp.zeros_like(gs_ref)
        for tile in range(N_MOE_TILES):
            lo = rng_ref[0, e, tile]
            hi = rng_ref[1, e, tile]
            tok = slice(tile * MOE_TT, (tile + 1) * MOE_TT)
            for c0, size in _slot_chunks(tile * MOE_TT < N_CTX_TOK):
                @pl.when((lo < c0 + size) & (hi > c0))
                def _(c0=c0, size=size, tok=tok):
                    sel = _one_hot(pos_ref[0, :, tok], c0, size)
                    rows = slice(c0, c0 + size)
                    xe_ref[rows, :] += jnp.dot(jnp.where(sel, 1.0, 0.0).astype(bf16), h_ref[tok, :],
                                               preferred_element_type=f32)
                    gs_ref[rows, :] += jnp.sum(jnp.where(sel, gate_ref[0, :, tok], 0.0), axis=1,
                                               keepdims=True)

    x = xe_ref[...].astype(bf16)
    gate = jnp.dot(x, wg_ref[0, 0].astype(bf16), preferred_element_type=f32)
    up = jnp.dot(x, wu_ref[0, 0].astype(bf16), preferred_element_type=f32)
    hid = (gate * jax.nn.sigmoid(gate) * up).astype(bf16)
    part = jnp.dot(hid, wd_ref[0, 0].astype(bf16), preferred_element_type=f32)

    @pl.when(f == 0)
    def _():
        acc_ref[...] = part

    @pl.when(f != 0)
    def _():
        acc_ref[...] += part

    @pl.when(f == pl.num_programs(1) - 1)
    def _():
        o_ref[0] = (acc_ref[...] * gs_ref[...]).astype(o_ref.dtype)


def _expert_ffn(rng, pos, gate, h2, w_gate, w_up, w_down, layer):
    tok_row = pl.BlockSpec((1, 1, N_TOK), lambda e, f, rng: (e, 0, 0))
    return pl.pallas_call(
        _ffn_kernel,
        grid_spec=pltpu.PrefetchScalarGridSpec(
            num_scalar_prefetch=1,
            grid=(N_EXPERTS, EXPERT_FF // FFN_TF),
            in_specs=[tok_row, tok_row,
                      pl.BlockSpec((N_TOK, D_MODEL), lambda e, f, rng: (0, 0), pipeline_mode=pl.Buffered(1)),
                      pl.BlockSpec((1, 1, D_MODEL, FFN_TF), lambda e, f, rng: (layer, e, 0, f)),
                      pl.BlockSpec((1, 1, D_MODEL, FFN_TF), lambda e, f, rng: (layer, e, 0, f)),
                      pl.BlockSpec((1, 1, FFN_TF, D_MODEL), lambda e, f, rng: (layer, e, f, 0))],
            out_specs=pl.BlockSpec((1, N_SLOTS, D_MODEL), lambda e, f, rng: (e, 0, 0)),
            scratch_shapes=[pltpu.VMEM((N_SLOTS, D_MODEL), f32),
                            pltpu.VMEM((N_SLOTS, 1), f32),
                            pltpu.VMEM((N_SLOTS, D_MODEL), f32)]),
        out_shape=jax.ShapeDtypeStruct((N_EXPERTS, N_SLOTS, D_MODEL), bf16),
        compiler_params=_cparams(2),
        name="expert_ffn",
    )(rng, pos, gate, h2, w_gate, w_up, w_down)


COMBINE_GROUP = 4


def _combine_kernel(x_ref, pos_ref, ye_ref, mod_ref, g_ref, xo_ref, acc_ref, *, final_norm):
    tile = pl.program_id(0)
    e = pl.program_id(1)

    @pl.when(e == 0)
    def _():
        acc_ref[...] = jnp.zeros_like(acc_ref)

    def scatter(slot0, n_slots):
        sel = jnp.concatenate([jnp.where(_one_hot(pos_ref[g], slot0, n_slots), 1.0, 0.0).astype(bf16)
                               for g in range(COMBINE_GROUP)], axis=0)
        ye = ye_ref[:, slot0:slot0 + n_slots, :].reshape(COMBINE_GROUP * n_slots, D_MODEL)
        acc_ref[...] += lax.dot_general(sel, ye, (((0,), (0,)), ((), ())), preferred_element_type=f32)

    @pl.when(tile * MOE_TT < N_CTX_TOK)
    def _():
        scatter(0, CAP_CTX)

    @pl.when(tile * MOE_TT >= N_CTX_TOK)
    def _():
        scatter(CAP_CTX, CAP_LAT)

    @pl.when(e == pl.num_programs(1) - 1)
    def _():
        gate2 = mod_ref[0, :, 5 * D_MODEL:6 * D_MODEL]
        x = x_ref[...] + gate2 * acc_ref[...]
        if final_norm:
            x = x * lax.rsqrt(jnp.mean(x * x, axis=-1, keepdims=True) + NORM_EPS) * g_ref[...]
        xo_ref[...] = x


def _combine(x, pos, ye, mod_l, final_g, final_norm):
    row = pl.BlockSpec((MOE_TT, D_MODEL), lambda i, e: (i, 0))
    return pl.pallas_call(
        functools.partial(_combine_kernel, final_norm=final_norm),
        grid=(N_MOE_TILES, N_EXPERTS // COMBINE_GROUP),
        in_specs=[row,
                  pl.BlockSpec((COMBINE_GROUP, 1, MOE_TT), lambda i, e: (e, 0, i)),
                  pl.BlockSpec((COMBINE_GROUP, N_SLOTS, D_MODEL), lambda i, e: (e, 0, 0)),
                  pl.BlockSpec((1, 1, 6 * D_MODEL), lambda i, e: (_mod_group_of_rows(i, MOE_TT), 0, 0)),
                  pl.BlockSpec((1, D_MODEL), lambda i, e: (0, 0))],
        out_specs=row,
        out_shape=jax.ShapeDtypeStruct((N_TOK, D_MODEL), f32),
        scratch_shapes=[pltpu.VMEM((MOE_TT, D_MODEL), f32)],
        compiler_params=_cparams(2),
        name="moe_combine",
    )(x, pos, ye, mod_l, final_g)


def _moe_residual(x, h2, logits, w_gate, w_up, w_down, layer, mod_l, final_g, final_norm):
    pos_c, gate_c = _route(logits, 0, N_CTX_TOK, CAP_CTX, 0)
    lat_n = N_TOK - N_CTX_TOK
    pos_l, gate_l = _route(logits, N_CTX_TOK // lat_n, lat_n, CAP_LAT, CAP_CTX)
    pos = jnp.concatenate([pos_c, pos_l], axis=2)
    gate = jnp.concatenate([gate_c, gate_l], axis=2)
    rng = _tile_slot_ranges(pos)
    ye = _expert_ffn(rng, pos, gate, h2, w_gate, w_up, w_down, layer)
    return _combine(x, pos, ye, mod_l, final_g, final_norm)


def _block_diag(blocks):
    n, a, b = blocks.shape
    eye = jnp.eye(n, dtype=blocks.dtype)
    return (eye[:, None, :, None] * blocks[:, :, None, :]).reshape(n * a, n * b)


def _prep_layer(l, w):
    p = {}
    w_in = w['w_in'][l]
    lru_x, lru_g, zr, q, k, v, gates = (
        w_in[:, 0:D_LRU], w_in[:, D_LRU:2 * D_LRU], w_in[:, 2 * D_LRU:2 * D_LRU + RWKV_COLS],
        w_in[:, 2 * D_LRU + RWKV_COLS:2 * D_LRU + RWKV_COLS + D_DIFF],
        w_in[:, 2 * D_LRU + RWKV_COLS + D_DIFF:2 * D_LRU + RWKV_COLS + 2 * D_DIFF],
        w_in[:, 2 * D_LRU + RWKV_COLS + 2 * D_DIFF:2 * D_LRU + RWKV_COLS + 3 * D_DIFF],
        w_in[:, 2 * D_LRU + RWKV_COLS + 3 * D_DIFF:])
    pad = jnp.zeros((D_MODEL, RWKV_COLS_PAD - RWKV_COLS), f32)
    p['w_in'] = jnp.concatenate([zr, pad, gates, lru_x, lru_g, q, k, v], axis=1).astype(bf16)
    p['norm_g1'] = w['norm_g'][l, 0].reshape(1, D_MODEL)
    p['norm_g2'] = w['norm_g'][l, 1].reshape(1, D_MODEL)

    p['lru_conv_w'] = w['lru_conv_w'][l]
    p['lru_conv_b'] = w['lru_conv_b'][l].reshape(1, D_LRU)
    p['lru_wg'] = jnp.concatenate(
        [_block_diag(w[name][l, d]) for d in range(2) for name in ('lru_wa', 'lru_wx')], axis=1).astype(bf16)
    p['lru_bg'] = jnp.concatenate(
        [w[name][l, d] for d in range(2) for name in ('lru_ba', 'lru_bx')]).reshape(1, 4 * D_LRU)
    p['lru_lambda'] = w['lru_lambda'][l]

    p['rwkv_conv_w'] = jnp.pad(w['rwkv_conv_w'][l], ((0, 0), (0, RWKV_COLS_PAD - RWKV_COLS)))

    def lowrank_pair(m):
        rank = m.shape[1]
        fwd, bwd = (m[d].reshape(rank, N_RWKV_CT, RWKV_CT).transpose(1, 0, 2) for d in range(2))
        zero = jnp.zeros_like(fwd)
        return jnp.concatenate([jnp.concatenate([fwd, zero], axis=2),
                                jnp.concatenate([zero, bwd], axis=2)], axis=1).astype(bf16)

    p['rwkv_w2'] = lowrank_pair(w['rwkv_w2'][l])
    p['rwkv_a2'] = lowrank_pair(w['rwkv_a2'][l])
    p['rwkv_g2'] = w['rwkv_g2'][l].astype(bf16)
    p['rwkv_pc'] = jnp.concatenate(
        [w['rwkv_w0'][l], w['rwkv_a0'][l], w['rwkv_k_k'][l][None], w['rwkv_k_a'][l][None],
         w['rwkv_r_k'][l].reshape(1, D_RWKV), jnp.zeros((1, D_RWKV), f32)], axis=0)
    p['rwkv_lnp'] = jnp.stack([w['rwkv_ln_g'][l], w['rwkv_ln_b'][l]], axis=0)
    p['head_ones'] = _block_diag(jnp.ones((N_RWKV_HEADS, RWKV_HEAD_DIM, RWKV_HEAD_DIM), bf16))

    p['diff_lambda'] = w['diff_lambda'][l]
    p['diff_subln_g'] = w['diff_subln_g'][l].reshape(1, HEAD_W)
    p['w_lru_out'] = w['w_lru_out'][l].astype(bf16)
    p['w_rwkv_out'] = w['w_rwkv_out'][l].astype(bf16)
    p['w_diff_out'] = w['w_diff_out'][l].astype(bf16)
    p['w_out'] = w['w_out'][l].astype(bf16)
    rw = jnp.pad(w['router_w'][l], ((0, 0), (0, ROUTER_PAD - N_EXPERTS)))
    p['router_hi'] = rw.astype(bf16)
    p['router_lo'] = (rw - p['router_hi'].astype(f32)).astype(bf16)
    return p


def kernel(x_prompt, x_sample, cache_k, cache_v, state_lru, state_rwkv, c, c_ctx, norm_g, final_norm_g, w_mod, b_mod, w_in, lru_conv_w, lru_conv_b, lru_wa, lru_ba, lru_wx, lru_bx, lru_lambda, rwkv_conv_w, rwkv_w0, rwkv_w2, rwkv_a0, rwkv_a2, rwkv_k_k, rwkv_k_a, rwkv_r_k, rwkv_g2, rwkv_ln_g, rwkv_ln_b, diff_lambda, diff_subln_g, w_lru_out, w_rwkv_out, w_diff_out, w_out, router_w, exp_w_gate, exp_w_up, exp_w_down):
    w = dict(norm_g=norm_g, w_in=w_in, lru_conv_w=lru_conv_w, lru_conv_b=lru_conv_b, lru_wa=lru_wa,
             lru_ba=lru_ba, lru_wx=lru_wx, lru_bx=lru_bx, lru_lambda=lru_lambda, rwkv_conv_w=rwkv_conv_w,
             rwkv_w0=rwkv_w0, rwkv_w2=rwkv_w2, rwkv_a0=rwkv_a0, rwkv_a2=rwkv_a2, rwkv_k_k=rwkv_k_k,
             rwkv_k_a=rwkv_k_a, rwkv_r_k=rwkv_r_k, rwkv_g2=rwkv_g2, rwkv_ln_g=rwkv_ln_g,
             rwkv_ln_b=rwkv_ln_b, diff_lambda=diff_lambda, diff_subln_g=diff_subln_g, w_lru_out=w_lru_out,
             w_rwkv_out=w_rwkv_out, w_diff_out=w_diff_out, w_out=w_out, router_w=router_w)

    x = jnp.concatenate([x_prompt.reshape(N_CTX_TOK, D_MODEL), x_sample.reshape(-1, D_MODEL)], axis=0)
    cvec = jnp.concatenate([c_ctx[None], c, jnp.zeros((SUBLANES - N_MOD_GROUPS, D_MODEL), f32)], axis=0)
    mod = _modulation(cvec, w_mod, b_mod)
    rope_cos, rope_sin = _rope_tables()
    final_g = final_norm_g.reshape(1, D_MODEL)
    ctx_rows = slice(0, N_CTX_TOK)
    lat_rows = slice(N_CTX_TOK, N_TOK)

    new_k, new_v, new_lru, new_rwkv = [], [], [], []
    for l in range(DEPTH):
        p = _prep_layer(l, w)
        lam_init = 0.8 - 0.6 * math.exp(-0.3 * l)
        mod_l = mod[l].reshape(SUBLANES, 1, 6 * D_MODEL)

        z = _in_projection(x, mod_l, p['norm_g1'], p['w_in'])
        new_k.append(z[ctx_rows, Z_K:Z_K + D_DIFF].reshape(N_CTX_SEQ, CTX_LEN, N_DIFF_HEADS, 2, DIFF_HEAD_DIM))
        new_v.append(z[ctx_rows, Z_V:Z_V + D_DIFF].reshape(N_CTX_SEQ, CTX_LEN, N_DIFF_HEADS, HEAD_W))

        h0_blocks = jnp.concatenate(
            [jnp.zeros((N_CTX_BLOCKS, 2, D_LRU), f32), state_lru[:, l].astype(f32)], axis=0)
        lru_out, lru_fin = _lru_branch(z, p, h0_blocks)
        n_per = SEQ_BLOCK // CTX_LEN
        fin = lru_fin[:N_CTX_BLOCKS].reshape(N_CTX_BLOCKS, 2, n_per, D_LRU)
        new_lru.append(jnp.transpose(fin, (0, 2, 1, 3)).reshape(N_CTX_SEQ, 2, D_LRU))

        scan_in, bonus, out_gate = _rwkv_pre(z, p)
        yf_c, yb_c, s_c = _rwkv_scan_path(scan_in, 0, N_CTX_SEQ, CTX_LEN, None)
        yf_l, yb_l, _ = _rwkv_scan_path(scan_in, N_CTX_TOK // LAT_LEN, N_LAT_SEQ, LAT_LEN, state_rwkv[:, l])
        new_rwkv.append(s_c)

        diff_out = _attention(z, cache_k[:, l].reshape(N_LAT_SEQ, PAST_LEN, D_DIFF),
                              cache_v[:, l].reshape(N_LAT_SEQ, PAST_LEN, D_DIFF),
                              rope_cos, rope_sin, p, lam_init)

        x, h2, logits = _merge(x, z, lru_out, (yf_c, yb_c), (yf_l, yb_l), bonus, out_gate, diff_out, mod_l, p)
        x = _moe_residual(x, h2, logits, exp_w_gate, exp_w_up, exp_w_down, l, mod_l, final_g,
                          final_norm=(l == DEPTH - 1))

    y_prompt = x[ctx_rows].reshape(x_prompt.shape)
    y_sample = x[lat_rows].reshape(x_sample.shape)
    return (y_prompt, y_sample, jnp.stack(new_k, axis=1), jnp.stack(new_v, axis=1),
            jnp.stack(new_lru, axis=1), jnp.stack(new_rwkv, axis=1))
```

```python
import functools
import math

import jax
import jax.numpy as jnp
from jax import lax
from jax.experimental import pallas as pl
from jax.experimental.pallas import tpu as pltpu

f32 = jnp.float32
bf16 = jnp.bfloat16

D_MODEL = 1024
DEPTH = 2
GRID_W = 64
NORM_EPS = 1e-6

N_CTX_SEQ = 16
CTX_LEN = 256
N_LAT_SEQ = 2
LAT_LEN = 1024
PAST_LEN = 256
N_CTX_TOK = N_CTX_SEQ * CTX_LEN
N_TOK = N_CTX_TOK + N_LAT_SEQ * LAT_LEN

D_LRU = 512
LRU_BLOCKS = 8
LRU_BLOCK_W = D_LRU // LRU_BLOCKS
LRU_C = 8.0

D_RWKV = 512
RWKV_HEAD_DIM = 64
N_RWKV_HEADS = D_RWKV // RWKV_HEAD_DIM
W_RANK = 64
A_RANK = 64
G_RANK = 128
RWKV_LN_EPS = 64e-5
RWKV_COLS = 3 * D_RWKV + 2 * W_RANK + 2 * A_RANK + G_RANK
RWKV_COLS_PAD = 2048

N_DIFF_HEADS = 4
DIFF_HEAD_DIM = 64
D_DIFF = N_DIFF_HEADS * 2 * DIFF_HEAD_DIM
ROPE_THETA = 10000.0

N_BRANCHES = 3
N_EXPERTS = 16
EXPERT_FF = 1024
EC_CAPACITY = 2
ROUTER_PAD = 128

LANES = 128
SUBLANES = 8

SEQ_BLOCK = 1024
N_SEQ_BLOCKS = N_TOK // SEQ_BLOCK
N_CTX_BLOCKS = N_CTX_TOK // SEQ_BLOCK
N_MOD_GROUPS = 1 + N_LAT_SEQ

Z_RWKV = 0
Z_GATES = RWKV_COLS_PAD
Z_LRU_X = Z_GATES + N_BRANCHES * D_MODEL
Z_LRU_G = Z_LRU_X + D_LRU
Z_Q = Z_LRU_G + D_LRU
Z_K = Z_Q + D_DIFF
Z_V = Z_K + D_DIFF
Z_COLS = Z_V + D_DIFF

VMEM_LIMIT = 56 * 1024 * 1024


def _cparams(n_axes):
    return pltpu.CompilerParams(dimension_semantics=("arbitrary",) * n_axes,
                                vmem_limit_bytes=VMEM_LIMIT)


def _seg_len(block_idx):
    return jnp.where(block_idx < N_CTX_BLOCKS, CTX_LEN, LAT_LEN)


def _mod_group_of_rows(row_block, rows_per_block):
    first_lat = N_CTX_TOK // rows_per_block
    per_seq = LAT_LEN // rows_per_block
    return jnp.where(row_block < first_lat, 0, (row_block - first_lat) // per_seq + 1)


def _shift_rows(v, d, tm, seg):
    n = v.shape[0]
    r = pltpu.roll(v, (-d) % n, axis=0)
    ok = (tm + d >= 0) & (tm + d < seg)
    return jnp.where(ok, r, 0.0)


def _split_dot(x, w_bf16):
    hi = x.astype(bf16)
    lo = (x - hi.astype(f32)).astype(bf16)
    return (jnp.dot(hi, w_bf16, preferred_element_type=f32)
            + jnp.dot(lo, w_bf16, preferred_element_type=f32))


MOD_TN = 1536


def _mod_kernel(c_ref, w_ref, b_ref, o_ref):
    c = c_ref[...]
    s = (c * jax.nn.sigmoid(c)).astype(bf16)
    o_ref[0] = jnp.dot(s, w_ref[0].astype(bf16), preferred_element_type=f32) + b_ref[0]


def _modulation(cvec, w_mod, b_mod):
    n_out = w_mod.shape[-1]
    return pl.pallas_call(
        _mod_kernel,
        grid=(DEPTH, n_out // MOD_TN),
        in_specs=[pl.BlockSpec((SUBLANES, D_MODEL), lambda l, j: (0, 0)),
                  pl.BlockSpec((1, D_MODEL, MOD_TN), lambda l, j: (l, 0, j)),
                  pl.BlockSpec((1, 1, MOD_TN), lambda l, j: (l, 0, j))],
        out_specs=pl.BlockSpec((1, SUBLANES, MOD_TN), lambda l, j: (l, 0, j)),
        out_shape=jax.ShapeDtypeStruct((DEPTH, SUBLANES, n_out), f32),
        compiler_params=_cparams(2),
        name="modulation",
    )(cvec, w_mod, b_mod.reshape(DEPTH, 1, n_out))


INPROJ_TM = 1024
INPROJ_TN = 768


def _inproj_kernel(x_ref, mod_ref, g_ref, w_ref, o_ref, h_ref):
    @pl.when(pl.program_id(1) == 0)
    def _():
        x = x_ref[...]
        y = x * lax.rsqrt(jnp.mean(x * x, axis=-1, keepdims=True) + NORM_EPS) * g_ref[...]
        shift = mod_ref[0, :, 0:D_MODEL]
        scale = mod_ref[0, :, D_MODEL:2 * D_MODEL]
        h_ref[...] = (y * (1.0 + scale) + shift).astype(bf16)

    o_ref[...] = jnp.dot(h_ref[...], w_ref[...], preferred_element_type=f32)


def _in_projection(x, mod_l, norm_g_row, w_in_p):
    return pl.pallas_call(
        _inproj_kernel,
        grid=(N_TOK // INPROJ_TM, Z_COLS // INPROJ_TN),
        in_specs=[pl.BlockSpec((INPROJ_TM, D_MODEL), lambda i, j: (i, 0)),
                  pl.BlockSpec((1, 1, 2 * D_MODEL),
                               lambda i, j: (_mod_group_of_rows(i, INPROJ_TM), 0, 0)),
                  pl.BlockSpec((1, D_MODEL), lambda i, j: (0, 0)),
                  pl.BlockSpec((D_MODEL, INPROJ_TN), lambda i, j: (0, j))],
        out_specs=pl.BlockSpec((INPROJ_TM, INPROJ_TN), lambda i, j: (i, j)),
        out_shape=jax.ShapeDtypeStruct((N_TOK, Z_COLS), f32),
        scratch_shapes=[pltpu.VMEM((INPROJ_TM, D_MODEL), bf16)],
        compiler_params=_cparams(2),
        name="in_projection",
    )(x, mod_l, norm_g_row, w_in_p)


LRU_TILE_UNROLL = 4


def _lru_kernel(x_ref, gate_ref, cw_ref, cb_ref, wg_ref, bg_ref, lam_ref, h0_ref, out_ref, hfin_ref,
                a_scr, b_scr, yf_scr, yb_scr):
    blk = pl.program_id(0)
    seg = _seg_len(blk)
    n = x_ref.shape[0]
    t = lax.broadcasted_iota(jnp.int32, (n, D_LRU), 0)
    tm = t & (seg - 1)

    x = x_ref[...]
    xc = (cb_ref[...]
          + cw_ref[0:1, :] * _shift_rows(x, -2, tm, seg)
          + cw_ref[1:2, :] * _shift_rows(x, -1, tm, seg)
          + cw_ref[2:3, :] * x
          + cw_ref[3:4, :] * _shift_rows(x, 1, tm, seg))
    pre = jnp.dot(xc.astype(bf16), wg_ref[...], preferred_element_type=f32) + bg_ref[...]

    n_tiles = n // SUBLANES
    tiles_per_seg = seg // SUBLANES
    row = lax.broadcasted_iota(jnp.int32, (SUBLANES, D_LRU), 0)
    for d, y_scr in enumerate((yf_scr, yb_scr)):
        r = jax.nn.sigmoid(pre[:, (2 * d) * D_LRU:(2 * d + 1) * D_LRU])
        i = jax.nn.sigmoid(pre[:, (2 * d + 1) * D_LRU:(2 * d + 2) * D_LRU])
        log_a = -LRU_C * r * jax.nn.softplus(-lam_ref[d:d + 1, :])
        a = jnp.exp(log_a)
        a_scr[...] = a
        b_scr[...] = jnp.sqrt(1.0 - a * a) * (i * xc)
        h0 = h0_ref[0, d:d + 1, :]

        def tile_step(k, h_prev, d=d, y_scr=y_scr, h0=h0):
            idx = k if d == 0 else n_tiles - 1 - k
            rows = pl.ds(pl.multiple_of(idx * SUBLANES, SUBLANES), SUBLANES)
            a_t = a_scr[rows, :]
            b_t = b_scr[rows, :]
            s = 1
            while s < SUBLANES:
                ok = (row >= s) if d == 0 else (row < SUBLANES - s)
                shift = s if d == 0 else SUBLANES - s
                a_sh = jnp.where(ok, pltpu.roll(a_t, shift, axis=0), 1.0)
                b_sh = jnp.where(ok, pltpu.roll(b_t, shift, axis=0), 0.0)
                b_t = a_t * b_sh + b_t
                a_t = a_t * a_sh
                s *= 2
            first_of_seg = (idx & (tiles_per_seg - 1)) == (0 if d == 0 else tiles_per_seg - 1)
            h_t = a_t * jnp.where(first_of_seg, h0, h_prev) + b_t
            y_scr[rows, :] = h_t
            return h_t[SUBLANES - 1:SUBLANES, :] if d == 0 else h_t[0:1, :]

        lax.fori_loop(0, n_tiles, tile_step, jnp.zeros((1, D_LRU), f32), unroll=LRU_TILE_UNROLL)

    out_ref[...] = ((yf_scr[...] + yb_scr[...]) * jax.nn.gelu(gate_ref[...])).astype(out_ref.dtype)
    fins = ([yf_scr[(j + 1) * CTX_LEN - 1:(j + 1) * CTX_LEN, :] for j in range(SEQ_BLOCK // CTX_LEN)]
            + [yb_scr[j * CTX_LEN:j * CTX_LEN + 1, :] for j in range(SEQ_BLOCK // CTX_LEN)])
    hfin_ref[0] = jnp.concatenate(fins, axis=0)


def _lru_branch(z, p, h0_blocks):
    full = lambda shape: pl.BlockSpec(shape, lambda i: (0,) * len(shape))
    return pl.pallas_call(
        _lru_kernel,
        grid=(N_SEQ_BLOCKS,),
        in_specs=[pl.BlockSpec((SEQ_BLOCK, D_LRU), lambda i: (i, Z_LRU_X // D_LRU)),
                  pl.BlockSpec((SEQ_BLOCK, D_LRU), lambda i: (i, Z_LRU_G // D_LRU)),
                  full((4, D_LRU)), full((1, D_LRU)), full((D_LRU, 4 * D_LRU)), full((1, 4 * D_LRU)),
                  full((2, D_LRU)),
                  pl.BlockSpec((1, 2, D_LRU), lambda i: (i, 0, 0))],
        out_specs=[pl.BlockSpec((SEQ_BLOCK, D_LRU), lambda i: (i, 0)),
                   pl.BlockSpec((1, SUBLANES, D_LRU), lambda i: (i, 0, 0))],
        out_shape=[jax.ShapeDtypeStruct((N_TOK, D_LRU), bf16),
                   jax.ShapeDtypeStruct((N_SEQ_BLOCKS, SUBLANES, D_LRU), f32)],
        scratch_shapes=[pltpu.VMEM((SEQ_BLOCK, D_LRU), f32)] * 4,
        compiler_params=_cparams(1),
        name="lru_branch",
    )(z, z, p['lru_conv_w'], p['lru_conv_b'], p['lru_wg'], p['lru_bg'], p['lru_lambda'], h0_blocks)


RWKV_CT = 256
N_RWKV_CT = D_RWKV // RWKV_CT
RWKV_LOWRANK_OFF = 3 * D_RWKV
RWKV_LOWRANK_W = RWKV_COLS_PAD - RWKV_LOWRANK_OFF
SCAN_ARRAYS = ('w_f', 'w_b', 'ka_f', 'ka_b', 'kd_f', 'kd_b', 'nkk', 'r', 'v')
N_SCAN_ARRAYS = len(SCAN_ARRAYS)
SCAN_IDX = {name: i for i, name in enumerate(SCAN_ARRAYS)}


def _rwkv_pre_kernel(r_ref, k_ref, v_ref, lr_ref, cwr_ref, cwk_ref, cwv_ref, cwl_ref,
                     w2_ref, a2_ref, g2_ref, pc_ref, ones_ref, sc_ref, bonus_ref, g_ref):
    blk = pl.program_id(0)
    seg = _seg_len(blk)
    n = r_ref.shape[0]

    def conv(x_ref, cw_ref):
        width = x_ref.shape[1]
        t = lax.broadcasted_iota(jnp.int32, (n, width), 0)
        tm = t & (seg - 1)
        x = x_ref[...]
        return (cw_ref[0:1, :] * _shift_rows(x, -1, tm, seg) + cw_ref[1:2, :] * x
                + cw_ref[2:3, :] * _shift_rows(x, 1, tm, seg))

    r = conv(r_ref, cwr_ref)
    k = conv(k_ref, cwk_ref)
    v = conv(v_ref, cwv_ref)
    lr = conv(lr_ref, cwl_ref)
    wl = lr[:, 0:2 * W_RANK]
    al = lr[:, 2 * W_RANK:2 * W_RANK + 2 * A_RANK]
    gl = lr[:, 2 * W_RANK + 2 * A_RANK:2 * W_RANK + 2 * A_RANK + G_RANK]

    w0 = [pc_ref[0:1, :], pc_ref[1:2, :]]
    a0 = [pc_ref[2:3, :], pc_ref[3:4, :]]
    k_k = pc_ref[4:5, :]
    k_a = pc_ref[5:6, :]
    r_k = pc_ref[6:7, :]

    w_pre = jnp.dot(jnp.tanh(wl).astype(bf16), w2_ref[0], preferred_element_type=f32)
    a_pre = jnp.dot(al.astype(bf16), a2_ref[0], preferred_element_type=f32)

    kk = k * k_k
    ss = _split_dot(kk * kk, ones_ref[...])
    kk = kk / jnp.maximum(jnp.sqrt(ss), 1e-12)

    kd_sum = None
    for d, sfx in enumerate(('_f', '_b')):
        cols = slice(d * RWKV_CT, (d + 1) * RWKV_CT)
        w_log = -jax.nn.softplus(-(w0[d] + w_pre[:, cols])) - 0.5
        sc_ref[SCAN_IDX['w' + sfx]] = jnp.exp(-jnp.exp(w_log))
        a = jax.nn.sigmoid(a0[d] + a_pre[:, cols])
        sc_ref[SCAN_IDX['ka' + sfx]] = kk * a
        kd = k * (1.0 + (a - 1.0) * k_a)
        sc_ref[SCAN_IDX['kd' + sfx]] = kd
        kd_sum = kd if kd_sum is None else kd_sum + kd

    sc_ref[SCAN_IDX['r']] = r
    sc_ref[SCAN_IDX['v']] = v
    sc_ref[SCAN_IDX['nkk']] = -kk
    bonus_ref[...] = _split_dot(r * kd_sum * r_k, ones_ref[...]) * v
    g_ref[...] = jnp.dot(jax.nn.sigmoid(gl).astype(bf16), g2_ref[...], preferred_element_type=f32)


def _rwkv_pre(z, p):
    nct = D_RWKV // RWKV_CT
    zcol = lambda base: pl.BlockSpec((SEQ_BLOCK, RWKV_CT), lambda i, c, base=base: (i, base + c))
    cwcol = lambda base: pl.BlockSpec((3, RWKV_CT), lambda i, c, base=base: (0, base + c))
    out_spec = pl.BlockSpec((SEQ_BLOCK, RWKV_CT), lambda i, c: (i, c))
    return pl.pallas_call(
        _rwkv_pre_kernel,
        grid=(N_SEQ_BLOCKS, nct),
        in_specs=[zcol(0), zcol(nct), zcol(2 * nct),
                  pl.BlockSpec((SEQ_BLOCK, RWKV_LOWRANK_W),
                               lambda i, c: (i, RWKV_LOWRANK_OFF // RWKV_LOWRANK_W)),
                  cwcol(0), cwcol(nct), cwcol(2 * nct),
                  pl.BlockSpec((3, RWKV_LOWRANK_W), lambda i, c: (0, RWKV_LOWRANK_OFF // RWKV_LOWRANK_W)),
                  pl.BlockSpec((1, 2 * W_RANK, 2 * RWKV_CT), lambda i, c: (c, 0, 0)),
                  pl.BlockSpec((1, 2 * A_RANK, 2 * RWKV_CT), lambda i, c: (c, 0, 0)),
                  pl.BlockSpec((G_RANK, RWKV_CT), lambda i, c: (0, c)),
                  pl.BlockSpec((SUBLANES, RWKV_CT), lambda i, c: (0, c)),
                  pl.BlockSpec((RWKV_CT, RWKV_CT), lambda i, c: (0, 0))],
        out_specs=[pl.BlockSpec((N_SCAN_ARRAYS, SEQ_BLOCK, RWKV_CT), lambda i, c: (0, i, c)),
                   out_spec, out_spec],
        out_shape=[jax.ShapeDtypeStruct((N_SCAN_ARRAYS, N_TOK, D_RWKV), f32),
                   jax.ShapeDtypeStruct((N_TOK, D_RWKV), f32),
                   jax.ShapeDtypeStruct((N_TOK, D_RWKV), f32)],
        compiler_params=_cparams(2),
        name="rwkv_pre",
    )(z, z, z, z, p['rwkv_conv_w'], p['rwkv_conv_w'], p['rwkv_conv_w'], p['rwkv_conv_w'],
      p['rwkv_w2'], p['rwkv_a2'], p['rwkv_g2'], p['rwkv_pc'], p['head_ones'][:RWKV_CT, :RWKV_CT])


PACK_T = 128
TIME_BLOCK = SUBLANES
PACK_UNROLL = 8


def _gather_states(scr, row, n_seq):
    return scr[:, pl.ds(row, N_RWKV_HEADS, stride=RWKV_HEAD_DIM), :].reshape(n_seq * N_RWKV_HEADS, PACK_T)


def _pack_kernel(x_ref, o_ref, scr, *, n_seq, reps, rows, fold_rows):
    for b in range(n_seq):
        scr[b] = x_ref[b].T

    def body(row, c):
        if fold_rows:
            m = jnp.concatenate([_gather_states(scr, row * reps + rep, n_seq) for rep in range(reps)], axis=0)
        else:
            m = _gather_states(scr, row, n_seq)
            if reps > 1:
                m = jnp.concatenate([m] * reps, axis=0)
        mt = m.T
        line0 = pl.multiple_of(row * TIME_BLOCK, TIME_BLOCK)
        for u in range(PACK_T // TIME_BLOCK):
            o_ref[u, pl.ds(line0, TIME_BLOCK), :] = mt[u * TIME_BLOCK:(u + 1) * TIME_BLOCK]
        return c

    lax.fori_loop(0, rows, body, 0, unroll=PACK_UNROLL)


def _pack(stacked, first, count, n_seq, seq_len, seq0, reps, fold_rows):
    rows = RWKV_HEAD_DIM // reps if fold_rows else RWKV_HEAD_DIM
    x = stacked.reshape(N_SCAN_ARRAYS, N_TOK // seq_len, seq_len, D_RWKV)
    tb = PACK_T // TIME_BLOCK
    return pl.pallas_call(
        functools.partial(_pack_kernel, n_seq=n_seq, reps=reps, rows=rows, fold_rows=fold_rows),
        grid=(count, seq_len // PACK_T),
        in_specs=[pl.BlockSpec((None, n_seq, PACK_T, D_RWKV),
                               lambda a, j: (first + a, seq0 // n_seq, j, 0))],
        out_specs=pl.BlockSpec((None, tb, rows * TIME_BLOCK, LANES), lambda a, j: (a, j, 0, 0)),
        out_shape=jax.ShapeDtypeStruct((count, seq_len // TIME_BLOCK, rows * TIME_BLOCK, LANES), f32),
        scratch_shapes=[pltpu.VMEM((n_seq, D_RWKV, PACK_T), f32)],
        compiler_params=_cparams(2),
        name="rwkv_pack",
    )(x)


def _unpack_kernel(yf_ref, yb_ref, of_ref, ob_ref, scr, *, n_seq, reps, rows):
    group = n_seq * N_RWKV_HEADS
    for y_ref, o_ref in ((yf_ref, of_ref), (yb_ref, ob_ref)):
        def body(row, c, y_ref=y_ref):
            line0 = pl.multiple_of(row * TIME_BLOCK, TIME_BLOCK)
            mt = jnp.concatenate([y_ref[u, pl.ds(line0, TIME_BLOCK), :]
                                  for u in range(PACK_T // TIME_BLOCK)], axis=0)
            m = mt.T
            for rep in range(reps):
                part = m[rep * group:(rep + 1) * group].reshape(n_seq, N_RWKV_HEADS, PACK_T)
                scr[:, pl.ds(row * reps + rep, N_RWKV_HEADS, stride=RWKV_HEAD_DIM), :] = part
            return c

        lax.fori_loop(0, rows, body, 0, unroll=PACK_UNROLL)
        for b in range(n_seq):
            o_ref[b] = scr[b].T


def _unpack(y_f, y_b, n_seq, seq_len, reps):
    rows = y_f.shape[1] // TIME_BLOCK
    yspec = pl.BlockSpec((PACK_T // TIME_BLOCK, rows * TIME_BLOCK, LANES), lambda j: (j, 0, 0))
    ospec = pl.BlockSpec((n_seq, PACK_T, D_RWKV), lambda j: (0, j, 0))
    shape = jax.ShapeDtypeStruct((n_seq, seq_len, D_RWKV), f32)
    o_f, o_b = pl.pallas_call(
        functools.partial(_unpack_kernel, n_seq=n_seq, reps=reps, rows=rows),
        grid=(seq_len // PACK_T,),
        in_specs=[yspec, yspec],
        out_specs=[ospec, ospec],
        out_shape=[shape, shape],
        scratch_shapes=[pltpu.VMEM((n_seq, D_RWKV, PACK_T), f32)],
        compiler_params=_cparams(1),
        name="rwkv_unpack",
    )(y_f, y_b)
    return o_f.reshape(n_seq * seq_len, D_RWKV), o_b.reshape(n_seq * seq_len, D_RWKV)


def _rwkv_scan_kernel(wf_ref, wb_ref, kaf_ref, kab_ref, kdf_ref, kdb_ref, nkkf_ref, nkkb_ref,
                      rf_ref, rb_ref, vvf_ref, vvb_ref, s0_ref, yf_ref, yb_ref, s_ref, *,
                      time_blocks, v_rows, unroll):
    @pl.when(pl.program_id(0) == 0)
    def _():
        s_ref[...] = s0_ref[...]

    dirs = ((wf_ref, kaf_ref, kdf_ref, nkkf_ref, rf_ref, vvf_ref, yf_ref),
            (wb_ref, kab_ref, kdb_ref, nkkb_ref, rb_ref, vvb_ref, yb_ref))
    n_k = s_ref.shape[1]

    for u in range(time_blocks):
        def step(t, carry, u=u):
            for g, (w_ref, ka_ref, kd_ref, nkk_ref, r_ref, vv_ref, y_ref) in enumerate(dirs):
                ub = u if g == 0 else time_blocks - 1 - u
                tt = t if g == 0 else TIME_BLOCK - 1 - t
                lanes = slice(g * LANES, (g + 1) * LANES)
                at_t = pl.ds(tt, n_k, stride=TIME_BLOCK)
                w = w_ref[ub, at_t, :]
                ka = ka_ref[ub, at_t, :]
                kd = kd_ref[ub, at_t, :]
                nkk = nkk_ref[ub, at_t, :]
                r = r_ref[ub, at_t, :]

                def rows(vb, c):
                    v0 = pl.multiple_of(vb * SUBLANES, SUBLANES)
                    rows_at_t = pl.ds(v0 * TIME_BLOCK + tt, SUBLANES, stride=TIME_BLOCK)
                    vv = vv_ref[ub, rows_at_t, :]
                    ys = []
                    for j in range(SUBLANES):
                        s = s_ref[v0 + j, :, lanes]
                        sa = jnp.sum(s * nkk, axis=0, keepdims=True)
                        s_new = s * w + ka * sa + kd * vv[j:j + 1]
                        s_ref[v0 + j, :, lanes] = s_new
                        ys.append(jnp.sum(s_new * r, axis=0, keepdims=True))
                    y_ref[ub, rows_at_t, :] = jnp.concatenate(ys, axis=0)
                    return c

                lax.fori_loop(0, v_rows // SUBLANES, rows, 0, unroll=unroll)
            return carry

        lax.fori_loop(0, TIME_BLOCK, step, 0, unroll=2 if v_rows == SUBLANES else 1)


SCAN_STEPS_PER_BLOCK = 16


def _rwkv_scan_packed(kvecs, vv, s0, *, steps):
    K = kvecs.shape[2] // TIME_BLOCK
    L = kvecs.shape[1] * TIME_BLOCK
    VR = vv.shape[1] // TIME_BLOCK
    nblk = L // steps
    tbs = steps // TIME_BLOCK

    def kspec(name, mirrored):
        a = SCAN_IDX[name]
        if mirrored:
            return pl.BlockSpec((None, tbs, K * TIME_BLOCK, LANES), lambda i: (a, nblk - 1 - i, 0, 0))
        return pl.BlockSpec((None, tbs, K * TIME_BLOCK, LANES), lambda i: (a, i, 0, 0))

    vf = pl.BlockSpec((tbs, VR * TIME_BLOCK, LANES), lambda i: (i, 0, 0))
    vb = pl.BlockSpec((tbs, VR * TIME_BLOCK, LANES), lambda i: (nblk - 1 - i, 0, 0))
    sspec = pl.BlockSpec((VR, K, 2 * LANES), lambda i: (0, 0, 0))
    kern = functools.partial(_rwkv_scan_kernel, time_blocks=tbs, v_rows=VR,
                             unroll=min(8, VR // SUBLANES))
    return pl.pallas_call(
        kern,
        grid=(nblk,),
        in_specs=[kspec('w_f', False), kspec('w_b', True), kspec('ka_f', False), kspec('ka_b', True),
                  kspec('kd_f', False), kspec('kd_b', True), kspec('nkk', False), kspec('nkk', True),
                  kspec('r', False), kspec('r', True), vf, vb, sspec],
        out_specs=[vf, vb, sspec],
        out_shape=[jax.ShapeDtypeStruct((L // TIME_BLOCK, VR * TIME_BLOCK, LANES), f32),
                   jax.ShapeDtypeStruct((L // TIME_BLOCK, VR * TIME_BLOCK, LANES), f32),
                   jax.ShapeDtypeStruct((VR, K, 2 * LANES), f32)],
        compiler_params=_cparams(1),
        name="rwkv7_scan",
    )(*([kvecs] * 10), vv, vv, s0)


def _rwkv_scan_path(stacked, seq0, B, L, S0):
    H, K = N_RWKV_HEADS, RWKV_HEAD_DIM
    nd = 2
    reps = LANES // (B * H)
    VR = K // reps
    kvecs = _pack(stacked, 0, N_SCAN_ARRAYS - 1, B, L, seq0, reps, fold_rows=False)
    vv = _pack(stacked, SCAN_IDX['v'], 1, B, L, seq0, reps, fold_rows=True)[0]
    if S0 is None:
        s0 = jnp.zeros((VR, K, nd * LANES), f32)
    else:
        s0 = S0.astype(f32).reshape(B, nd, H, VR, reps, K)
        s0 = jnp.transpose(s0, (3, 5, 1, 4, 0, 2)).reshape(VR, K, nd * LANES)

    y_f, y_b, s_fin = _rwkv_scan_packed(kvecs, vv, s0, steps=SCAN_STEPS_PER_BLOCK)
    y_f, y_b = _unpack(y_f, y_b, B, L, reps)
    s_fin = s_fin.reshape(VR, K, nd, reps, B, H)
    s_fin = jnp.transpose(s_fin, (4, 2, 5, 0, 3, 1)).reshape(B, nd, H, K, K)
    return y_f, y_b, s_fin


ATT_TQ = 256
HEAD_W = 2 * DIFF_HEAD_DIM


def _rope(x, cos, sin_signed, lo_mask):
    n = x.shape[1]
    partner = jnp.where(lo_mask, pltpu.roll(x, n - DIFF_HEAD_DIM // 4, axis=1),
                        pltpu.roll(x, DIFF_HEAD_DIM // 4, axis=1))
    return x * cos + partner * sin_signed


def _attn_heads(q, k, v, lam, subln_g, out_scale):
    lane = lax.broadcasted_iota(jnp.int32, (1, HEAD_W), 1)
    first = lane < DIFF_HEAD_DIM
    outs = []
    for h in range(N_DIFF_HEADS):
        cols = slice(h * HEAD_W, (h + 1) * HEAD_W)
        qh = (q[:, cols] * (DIFF_HEAD_DIM ** -0.5)).astype(bf16)
        kh = k[:, cols]
        vh = v[:, cols].astype(bf16)
        probs = []
        for m in range(2):
            km = jnp.where(first if m == 0 else ~first, kh, 0.0).astype(bf16)
            s = lax.dot_general(qh, km, (((1,), (1,)), ((), ())), preferred_element_type=f32)
            s = s - jnp.max(s, axis=-1, keepdims=True)
            e = jnp.exp(s)
            probs.append(e / jnp.sum(e, axis=-1, keepdims=True))
        attn = (probs[0] - lam * probs[1]).astype(bf16)
        o = jnp.dot(attn, vh, preferred_element_type=f32)
        o = o * lax.rsqrt(jnp.mean(o * o, axis=-1, keepdims=True) + NORM_EPS) * subln_g * out_scale
        outs.append(o)
    return jnp.concatenate(outs, axis=1)


def _attn_kernel(q_ref, k_ref, v_ref, ck_ref, cv_ref, cos_ref, sin_ref, lv_ref, g_ref, o_ref,
                 kall_ref, vall_ref, *, lam_init):
    blk = pl.program_id(0)
    qt = pl.program_id(1)
    lv = lv_ref[...]
    lam = (jnp.exp(jnp.sum(lv[0:1] * lv[1:2], axis=-1, keepdims=True))
           - jnp.exp(jnp.sum(lv[2:3] * lv[3:4], axis=-1, keepdims=True)) + lam_init)
    subln_g = g_ref[...]
    out_scale = 1.0 - lam_init
    row0 = pl.multiple_of(qt * ATT_TQ, ATT_TQ)

    @pl.when(blk < N_CTX_BLOCKS)
    def _():
        q = q_ref[pl.ds(row0, ATT_TQ), :]
        k = k_ref[pl.ds(row0, ATT_TQ), :]
        v = v_ref[pl.ds(row0, ATT_TQ), :]
        o_ref[...] = _attn_heads(q, k, v, lam, subln_g, out_scale).astype(o_ref.dtype)

    @pl.when(blk >= N_CTX_BLOCKS)
    def _():
        lane = lax.broadcasted_iota(jnp.int32, (1, D_DIFF), 1)
        lo_mask = (lane % (DIFF_HEAD_DIM // 2)) < (DIFF_HEAD_DIM // 4)

        @pl.when(qt == 0)
        def _():
            kall_ref[0:LAT_LEN, :] = _rope(k_ref[...], cos_ref[...], sin_ref[...], lo_mask)
            kall_ref[LAT_LEN:LAT_LEN + PAST_LEN, :] = ck_ref[0]
            vall_ref[0:LAT_LEN, :] = v_ref[...]
            vall_ref[LAT_LEN:LAT_LEN + PAST_LEN, :] = cv_ref[0]

        q = _rope(q_ref[pl.ds(row0, ATT_TQ), :], cos_ref[pl.ds(row0, ATT_TQ), :],
                  sin_ref[pl.ds(row0, ATT_TQ), :], lo_mask)
        o_ref[...] = _attn_heads(q, kall_ref[...], vall_ref[...], lam, subln_g,
                                 out_scale).astype(o_ref.dtype)


def _attention(z, cache_k_l, cache_v_l, rope_cos, rope_sin, p, lam_init):
    zcol = lambda off: pl.BlockSpec((SEQ_BLOCK, D_DIFF), lambda i, j, off=off: (i, off // D_DIFF))
    cache = pl.BlockSpec((1, PAST_LEN, D_DIFF), lambda i, j: (jnp.maximum(i - N_CTX_BLOCKS, 0), 0, 0))
    full = lambda shape: pl.BlockSpec(shape, lambda i, j: (0,) * len(shape))
    return pl.pallas_call(
        functools.partial(_attn_kernel, lam_init=lam_init),
        grid=(N_SEQ_BLOCKS, SEQ_BLOCK // ATT_TQ),
        in_specs=[zcol(Z_Q), zcol(Z_K), zcol(Z_V), cache, cache,
                  full((LAT_LEN, D_DIFF)), full((LAT_LEN, D_DIFF)),
                  full((4, DIFF_HEAD_DIM)), full((1, HEAD_W))],
        out_specs=pl.BlockSpec((ATT_TQ, D_DIFF), lambda i, j: (i * (SEQ_BLOCK // ATT_TQ) + j, 0)),
        out_shape=jax.ShapeDtypeStruct((N_TOK, D_DIFF), bf16),
        scratch_shapes=[pltpu.VMEM((LAT_LEN + PAST_LEN, D_DIFF), f32),
                        pltpu.VMEM((LAT_LEN + PAST_LEN, D_DIFF), f32)],
        compiler_params=_cparams(2),
        name="diff_attention",
    )(z, z, z, cache_k_l, cache_v_l, rope_cos, rope_sin, p['diff_lambda'], p['diff_subln_g'])


def _rope_tables():
    t = jnp.arange(LAT_LEN)
    row = (t // GRID_W).astype(f32)
    col = (t % GRID_W).astype(f32)
    nf = DIFF_HEAD_DIM // 4
    freqs = ROPE_THETA ** (-jnp.arange(nf, dtype=f32) / nf)
    d = jnp.arange(DIFF_HEAD_DIM)
    pos = jnp.where((d < DIFF_HEAD_DIM // 2)[None, :], row[:, None], col[:, None])
    ang = pos * freqs[d % nf][None, :]
    sign = jnp.where((d % (2 * nf)) < nf, -1.0, 1.0)[None, :]
    cos = jnp.tile(jnp.cos(ang), (1, D_DIFF // DIFF_HEAD_DIM))
    sin = jnp.tile(jnp.sin(ang) * sign, (1, D_DIFF // DIFF_HEAD_DIM))
    return cos, sin


MERGE_TM = 512


def _merge_kernel(x_ref, lru_ref, yfc_ref, ybc_ref, yfl_ref, ybl_ref, bonus_ref, g_ref, diff_ref,
                  gp0_ref, gp1_ref, gp2_ref,
                  mod_ref, lnp_ref, ones_ref, wlo_ref, wro_ref, wdo_ref, wout_ref, ng_ref, rhi_ref, rlo_ref,
                  xo_ref, h2_ref, logit_ref):
    is_ctx = pl.program_id(0) < N_CTX_TOK // MERGE_TM
    y = jnp.where(is_ctx, yfc_ref[...] + ybc_ref[...], yfl_ref[...] + ybl_ref[...])
    inv = 1.0 / RWKV_HEAD_DIM
    mu = _split_dot(y, ones_ref[...]) * inv
    dlt = y - mu
    var = _split_dot(dlt * dlt, ones_ref[...]) * inv
    yn = dlt * lax.rsqrt(var + RWKV_LN_EPS)
    rwkv_out = ((yn * lnp_ref[0:1, :] + lnp_ref[1:2, :] + bonus_ref[...]) * g_ref[...]).astype(bf16)

    merged = (jax.nn.sigmoid(gp0_ref[...]) * jnp.dot(lru_ref[...], wlo_ref[...], preferred_element_type=f32)
              + jax.nn.sigmoid(gp1_ref[...]) * jnp.dot(rwkv_out, wro_ref[...], preferred_element_type=f32)
              + jax.nn.sigmoid(gp2_ref[...]) * jnp.dot(diff_ref[...], wdo_ref[...], preferred_element_type=f32))
    mix = jnp.dot(merged.astype(bf16), wout_ref[...], preferred_element_type=f32)

    gate1 = mod_ref[0, :, 2 * D_MODEL:3 * D_MODEL]
    shift2 = mod_ref[0, :, 3 * D_MODEL:4 * D_MODEL]
    scale2 = mod_ref[0, :, 4 * D_MODEL:5 * D_MODEL]
    x = x_ref[...] + gate1 * mix
    xo_ref[...] = x
    yn2 = x * lax.rsqrt(jnp.mean(x * x, axis=-1, keepdims=True) + NORM_EPS) * ng_ref[...]
    h2 = yn2 * (1.0 + scale2) + shift2
    hi = h2.astype(bf16)
    lo = (h2 - hi.astype(f32)).astype(bf16)
    h2_ref[...] = hi
    logit_ref[...] = (jnp.dot(hi, rhi_ref[...], preferred_element_type=f32)
                      + jnp.dot(lo, rhi_ref[...], preferred_element_type=f32)
                      + jnp.dot(hi, rlo_ref[...], preferred_element_type=f32))


def _merge(x, z, lru_out, y_ctx, y_lat, bonus, g, diff_out, mod_l, p):
    n_ctx_tiles = N_CTX_TOK // MERGE_TM
    row = lambda w: pl.BlockSpec((MERGE_TM, w), lambda i: (i, 0))
    ctx_row = pl.BlockSpec((MERGE_TM, D_RWKV), lambda i: (jnp.minimum(i, n_ctx_tiles - 1), 0))
    lat_row = pl.BlockSpec((MERGE_TM, D_RWKV), lambda i: (jnp.maximum(i - n_ctx_tiles, 0), 0))
    gate = lambda b: pl.BlockSpec((MERGE_TM, D_MODEL), lambda i, b=b: (i, Z_GATES // D_MODEL + b))
    full = lambda shape: pl.BlockSpec(shape, lambda i: (0,) * len(shape))
    return pl.pallas_call(
        _merge_kernel,
        grid=(N_TOK // MERGE_TM,),
        in_specs=[row(D_MODEL), row(D_LRU), ctx_row, ctx_row, lat_row, lat_row,
                  row(D_RWKV), row(D_RWKV), row(D_DIFF),
                  gate(0), gate(1), gate(2),
                  pl.BlockSpec((1, 1, 6 * D_MODEL), lambda i: (_mod_group_of_rows(i, MERGE_TM), 0, 0)),
                  full((2, D_RWKV)), full((D_RWKV, D_RWKV)),
                  full((D_LRU, D_MODEL)), full((D_RWKV, D_MODEL)), full((D_DIFF, D_MODEL)),
                  full((D_MODEL, D_MODEL)), full((1, D_MODEL)),
                  full((D_MODEL, ROUTER_PAD)), full((D_MODEL, ROUTER_PAD))],
        out_specs=[row(D_MODEL), row(D_MODEL), row(ROUTER_PAD)],
        out_shape=[jax.ShapeDtypeStruct((N_TOK, D_MODEL), f32),
                   jax.ShapeDtypeStruct((N_TOK, D_MODEL), bf16),
                   jax.ShapeDtypeStruct((N_TOK, ROUTER_PAD), f32)],
        compiler_params=_cparams(1),
        name="merge",
    )(x, lru_out, *y_ctx, *y_lat, bonus, g, diff_out, z, z, z, mod_l, p['rwkv_lnp'], p['head_ones'],
      p['w_lru_out'], p['w_rwkv_out'], p['w_diff_out'], p['w_out'], p['norm_g2'],
      p['router_hi'], p['router_lo'])


CAP_CTX = EC_CAPACITY * N_CTX_TOK // N_EXPERTS
CAP_LAT = EC_CAPACITY * (N_TOK - N_CTX_TOK) // N_EXPERTS
N_SLOTS = CAP_CTX + CAP_LAT
F32_MANTISSA_BITS = 23
F32_EXPONENT_BIAS = 127
LADDER_STEPS = 8
BISECT_STEPS = 60


def _cumsum_lanes(x01):
    rows, n = x01.shape
    i = lax.broadcasted_iota(jnp.int32, (LANES, LANES), 0)
    j = lax.broadcasted_iota(jnp.int32, (LANES, LANES), 1)
    upper = jnp.where(i <= j, 1.0, 0.0).astype(bf16)
    off = jnp.zeros((rows, 1), f32)
    outs = []
    for t in range(n // LANES):
        c = jnp.dot(x01[:, t * LANES:(t + 1) * LANES].astype(bf16), upper, preferred_element_type=f32) + off
        outs.append(c)
        off = c[:, LANES - 1:LANES]
    return jnp.concatenate(outs, axis=1)


def _route_kernel(logit_ref, pos_ref, gate_ref, *, cap, slot0):
    n = logit_ref.shape[0]
    lane = lax.broadcasted_iota(jnp.int32, (1, ROUTER_PAD), 1)
    lg = jnp.where(lane < N_EXPERTS, logit_ref[...], -jnp.inf)
    ex = jnp.exp(lg - jnp.max(lg, axis=-1, keepdims=True))
    aff = (ex / jnp.sum(ex, axis=-1, keepdims=True)).T[:N_EXPERTS]

    def n_above(thr):
        return jnp.sum(jnp.where(aff > thr, 1.0, 0.0), axis=1, keepdims=True)

    def rung(j):
        return pltpu.bitcast(jnp.left_shift(j, F32_MANTISSA_BITS), f32)

    def pick_rung(_, jj):
        j_lo, j_hi = jj
        j_mid = (j_lo + j_hi) >> 1
        enough = n_above(rung(j_mid)) >= cap
        return jnp.where(enough, j_mid, j_lo), jnp.where(enough, j_hi, j_mid)

    j_lo, j_hi = lax.fori_loop(0, LADDER_STEPS, pick_rung,
                               (jnp.full((N_EXPERTS, 1), -1, jnp.int32),
                                jnp.full((N_EXPERTS, 1), F32_EXPONENT_BIAS + 1, jnp.int32)))

    def bisect(_, lh):
        lo, hi = lh
        mid = 0.5 * (lo + hi)
        enough = n_above(mid) >= cap
        return jnp.where(enough, mid, lo), jnp.where(enough, hi, mid)

    lo, hi = lax.fori_loop(0, BISECT_STEPS, bisect, (jnp.where(j_lo < 0, -1.0, rung(j_lo)), rung(j_hi)))
    above = aff > hi
    tie = (aff > lo) & (aff <= hi)
    need = cap - n_above(hi)
    keep = above | (tie & (_cumsum_lanes(jnp.where(tie, 1.0, 0.0)) <= need))
    slot = _cumsum_lanes(jnp.where(keep, 1.0, 0.0)) - 1.0 + slot0
    pos_ref[...] = jnp.where(keep, slot, -1.0).astype(jnp.int32)
    gate_ref[...] = jnp.where(keep, aff, 0.0)


def _route(logits, block, n, cap, slot0):
    shape = (N_EXPERTS, n)
    pos, gate = pl.pallas_call(
        functools.partial(_route_kernel, cap=cap, slot0=slot0),
        grid=(1,),
        in_specs=[pl.BlockSpec((n, ROUTER_PAD), lambda i: (block, 0))],
        out_specs=[pl.BlockSpec(shape, lambda i: (0, 0))] * 2,
        out_shape=[jax.ShapeDtypeStruct(shape, jnp.int32), jax.ShapeDtypeStruct(shape, f32)],
        compiler_params=_cparams(1),
        name="route",
    )(logits)
    return pos.reshape(N_EXPERTS, 1, n), gate.reshape(N_EXPERTS, 1, n)


FFN_TF = 512
MOE_TT = 1024


def _one_hot(pos_row, slot0, n_slots):
    slots = lax.broadcasted_iota(jnp.int32, (n_slots, pos_row.shape[1]), 0) + slot0
    return pos_row == slots


N_MOE_TILES = N_TOK // MOE_TT
SLOT_CHUNK_CTX = CAP_CTX // 2
SLOT_CHUNK_LAT = CAP_LAT // 2


def _slot_chunks(is_ctx):
    slot0, n_slots, size = (0, CAP_CTX, SLOT_CHUNK_CTX) if is_ctx else (CAP_CTX, CAP_LAT, SLOT_CHUNK_LAT)
    return [(slot0 + c * size, size) for c in range(n_slots // size)]


def _tile_slot_ranges(pos):
    p = pos.reshape(N_EXPERTS, N_MOE_TILES, MOE_TT)
    lo = jnp.min(jnp.where(p >= 0, p, N_SLOTS), axis=2)
    hi = jnp.max(p, axis=2) + 1
    return jnp.stack([lo, hi]).astype(jnp.int32)


def _ffn_kernel(rng_ref, pos_ref, gate_ref, h_ref, wg_ref, wu_ref, wd_ref, o_ref, xe_ref, gs_ref, hid_ref):
    e = pl.program_id(0)
    f = pl.program_id(1)

    @pl.when(f == 0)
    def _():
        xe_ref[...] = jnp.zeros_like(xe_ref)
        gs_ref[...] = jnp.zeros_like(gs_ref)
        for tile in range(N_MOE_TILES):
            lo = rng_ref[0, e, tile]
            hi = rng_ref[1, e, tile]
            tok = slice(tile * MOE_TT, (tile + 1) * MOE_TT)
            for c0, size in _slot_chunks(tile * MOE_TT < N_CTX_TOK):
                @pl.when((lo < c0 + size) & (hi > c0))
                def _(c0=c0, size=size, tok=tok):
                    sel = _one_hot(pos_ref[0, :, tok], c0, size)
                    rows = slice(c0, c0 + size)
                    xe_ref[rows, :] += jnp.dot(jnp.where(sel, 1.0, 0.0).astype(bf16), h_ref[tok, :],
                                               preferred_element_type=f32)
                    gs_ref[rows, :] += jnp.sum(jnp.where(sel, gate_ref[0, :, tok], 0.0), axis=1,
                                               keepdims=True)

    x = xe_ref[...].astype(bf16)
    gate = jnp.dot(x, wg_ref[0, 0].astype(bf16), preferred_element_type=f32)
    up = jnp.dot(x, wu_ref[0, 0].astype(bf16), preferred_element_type=f32)
    hid = (gate * jax.nn.sigmoid(gate) * up).astype(bf16)
    for half in range(EXPERT_FF // FFN_TF):
        @pl.when(f == half)
        def _(half=half):
            hid_ref[:, half * FFN_TF:(half + 1) * FFN_TF] = hid

    @pl.when(f == pl.num_programs(1) - 1)
    def _():
        out = jnp.dot(hid_ref[...], wd_ref[0, 0].astype(bf16), preferred_element_type=f32)
        o_ref[0] = (out * gs_ref[...]).astype(o_ref.dtype)


def _expert_ffn(rng, pos, gate, h2, w_gate, w_up, w_down, layer):
    tok_row = pl.BlockSpec((1, 1, N_TOK), lambda e, f, rng: (e, 0, 0))
    return pl.pallas_call(
        _ffn_kernel,
        grid_spec=pltpu.PrefetchScalarGridSpec(
            num_scalar_prefetch=1,
            grid=(N_EXPERTS, EXPERT_FF // FFN_TF),
            in_specs=[tok_row, tok_row,
                      pl.BlockSpec((N_TOK, D_MODEL), lambda e, f, rng: (0, 0), pipeline_mode=pl.Buffered(1)),
                      pl.BlockSpec((1, 1, D_MODEL, FFN_TF), lambda e, f, rng: (layer, e, 0, f)),
                      pl.BlockSpec((1, 1, D_MODEL, FFN_TF), lambda e, f, rng: (layer, e, 0, f)),
                      pl.BlockSpec((1, 1, EXPERT_FF, D_MODEL), lambda e, f, rng: (layer, e, 0, 0))],
            out_specs=pl.BlockSpec((1, N_SLOTS, D_MODEL), lambda e, f, rng: (e, 0, 0)),
            scratch_shapes=[pltpu.VMEM((N_SLOTS, D_MODEL), f32),
                            pltpu.VMEM((N_SLOTS, 1), f32),
                            pltpu.VMEM((N_SLOTS, EXPERT_FF), bf16)]),
        out_shape=jax.ShapeDtypeStruct((N_EXPERTS, N_SLOTS, D_MODEL), bf16),
        compiler_params=_cparams(2),
        name="expert_ffn",
    )(rng, pos, gate, h2, w_gate, w_up, w_down)


COMBINE_GROUP = 4


def _combine_kernel(x_ref, pos_ref, ye_ref, mod_ref, g_ref, xo_ref, acc_ref, *, final_norm):
    tile = pl.program_id(0)
    e = pl.program_id(1)

    @pl.when(e == 0)
    def _():
        acc_ref[...] = jnp.zeros_like(acc_ref)

    def scatter(slot0, n_slots):
        sel = jnp.concatenate([jnp.where(_one_hot(pos_ref[g], slot0, n_slots), 1.0, 0.0).astype(bf16)
                               for g in range(COMBINE_GROUP)], axis=0)
        ye = ye_ref[:, slot0:slot0 + n_slots, :].reshape(COMBINE_GROUP * n_slots, D_MODEL)
        acc_ref[...] += lax.dot_general(sel, ye, (((0,), (0,)), ((), ())), preferred_element_type=f32)

    @pl.when(tile * MOE_TT < N_CTX_TOK)
    def _():
        scatter(0, CAP_CTX)

    @pl.when(tile * MOE_TT >= N_CTX_TOK)
    def _():
        scatter(CAP_CTX, CAP_LAT)

    @pl.when(e == pl.num_programs(1) - 1)
    def _():
        gate2 = mod_ref[0, :, 5 * D_MODEL:6 * D_MODEL]
        x = x_ref[...] + gate2 * acc_ref[...]
        if final_norm:
            x = x * lax.rsqrt(jnp.mean(x * x, axis=-1, keepdims=True) + NORM_EPS) * g_ref[...]
        xo_ref[...] = x


def _combine(x, pos, ye, mod_l, final_g, final_norm):
    row = pl.BlockSpec((MOE_TT, D_MODEL), lambda i, e: (i, 0))
    return pl.pallas_call(
        functools.partial(_combine_kernel, final_norm=final_norm),
        grid=(N_MOE_TILES, N_EXPERTS // COMBINE_GROUP),
        in_specs=[row,
                  pl.BlockSpec((COMBINE_GROUP, 1, MOE_TT), lambda i, e: (e, 0, i)),
                  pl.BlockSpec((COMBINE_GROUP, N_SLOTS, D_MODEL), lambda i, e: (e, 0, 0)),
                  pl.BlockSpec((1, 1, 6 * D_MODEL), lambda i, e: (_mod_group_of_rows(i, MOE_TT), 0, 0)),
                  pl.BlockSpec((1, D_MODEL), lambda i, e: (0, 0))],
        out_specs=row,
        out_shape=jax.ShapeDtypeStruct((N_TOK, D_MODEL), f32),
        scratch_shapes=[pltpu.VMEM((MOE_TT, D_MODEL), f32)],
        compiler_params=_cparams(2),
        name="moe_combine",
    )(x, pos, ye, mod_l, final_g)


def _moe_residual(x, h2, logits, w_gate, w_up, w_down, layer, mod_l, final_g, final_norm):
    pos_c, gate_c = _route(logits, 0, N_CTX_TOK, CAP_CTX, 0)
    lat_n = N_TOK - N_CTX_TOK
    pos_l, gate_l = _route(logits, N_CTX_TOK // lat_n, lat_n, CAP_LAT, CAP_CTX)
    pos = jnp.concatenate([pos_c, pos_l], axis=2)
    gate = jnp.concatenate([gate_c, gate_l], axis=2)
    rng = _tile_slot_ranges(pos)
    ye = _expert_ffn(rng, pos, gate, h2, w_gate, w_up, w_down, layer)
    return _combine(x, pos, ye, mod_l, final_g, final_norm)


def _block_diag(blocks):
    n, a, b = blocks.shape
    eye = jnp.eye(n, dtype=blocks.dtype)
    return (eye[:, None, :, None] * blocks[:, :, None, :]).reshape(n * a, n * b)


def _prep_layer(l, w):
    p = {}
    w_in = w['w_in'][l]
    lru_x, lru_g, zr, q, k, v, gates = (
        w_in[:, 0:D_LRU], w_in[:, D_LRU:2 * D_LRU], w_in[:, 2 * D_LRU:2 * D_LRU + RWKV_COLS],
        w_in[:, 2 * D_LRU + RWKV_COLS:2 * D_LRU + RWKV_COLS + D_DIFF],
        w_in[:, 2 * D_LRU + RWKV_COLS + D_DIFF:2 * D_LRU + RWKV_COLS + 2 * D_DIFF],
        w_in[:, 2 * D_LRU + RWKV_COLS + 2 * D_DIFF:2 * D_LRU + RWKV_COLS + 3 * D_DIFF],
        w_in[:, 2 * D_LRU + RWKV_COLS + 3 * D_DIFF:])
    pad = jnp.zeros((D_MODEL, RWKV_COLS_PAD - RWKV_COLS), f32)
    p['w_in'] = jnp.concatenate([zr, pad, gates, lru_x, lru_g, q, k, v], axis=1).astype(bf16)
    p['norm_g1'] = w['norm_g'][l, 0].reshape(1, D_MODEL)
    p['norm_g2'] = w['norm_g'][l, 1].reshape(1, D_MODEL)

    p['lru_conv_w'] = w['lru_conv_w'][l]
    p['lru_conv_b'] = w['lru_conv_b'][l].reshape(1, D_LRU)
    p['lru_wg'] = jnp.concatenate(
        [_block_diag(w[name][l, d]) for d in range(2) for name in ('lru_wa', 'lru_wx')], axis=1).astype(bf16)
    p['lru_bg'] = jnp.concatenate(
        [w[name][l, d] for d in range(2) for name in ('lru_ba', 'lru_bx')]).reshape(1, 4 * D_LRU)
    p['lru_lambda'] = w['lru_lambda'][l]

    p['rwkv_conv_w'] = jnp.pad(w['rwkv_conv_w'][l], ((0, 0), (0, RWKV_COLS_PAD - RWKV_COLS)))

    def lowrank_pair(m):
        rank = m.shape[1]
        fwd, bwd = (m[d].reshape(rank, N_RWKV_CT, RWKV_CT).transpose(1, 0, 2) for d in range(2))
        zero = jnp.zeros_like(fwd)
        return jnp.concatenate([jnp.concatenate([fwd, zero], axis=2),
                                jnp.concatenate([zero, bwd], axis=2)], axis=1).astype(bf16)

    p['rwkv_w2'] = lowrank_pair(w['rwkv_w2'][l])
    p['rwkv_a2'] = lowrank_pair(w['rwkv_a2'][l])
    p['rwkv_g2'] = w['rwkv_g2'][l].astype(bf16)
    p['rwkv_pc'] = jnp.concatenate(
        [w['rwkv_w0'][l], w['rwkv_a0'][l], w['rwkv_k_k'][l][None], w['rwkv_k_a'][l][None],
         w['rwkv_r_k'][l].reshape(1, D_RWKV), jnp.zeros((1, D_RWKV), f32)], axis=0)
    p['rwkv_lnp'] = jnp.stack([w['rwkv_ln_g'][l], w['rwkv_ln_b'][l]], axis=0)
    p['head_ones'] = _block_diag(jnp.ones((N_RWKV_HEADS, RWKV_HEAD_DIM, RWKV_HEAD_DIM), bf16))

    p['diff_lambda'] = w['diff_lambda'][l]
    p['diff_subln_g'] = w['diff_subln_g'][l].reshape(1, HEAD_W)
    p['w_lru_out'] = w['w_lru_out'][l].astype(bf16)
    p['w_rwkv_out'] = w['w_rwkv_out'][l].astype(bf16)
    p['w_diff_out'] = w['w_diff_out'][l].astype(bf16)
    p['w_out'] = w['w_out'][l].astype(bf16)
    rw = jnp.pad(w['router_w'][l], ((0, 0), (0, ROUTER_PAD - N_EXPERTS)))
    p['router_hi'] = rw.astype(bf16)
    p['router_lo'] = (rw - p['router_hi'].astype(f32)).astype(bf16)
    return p


def kernel(x_prompt, x_sample, cache_k, cache_v, state_lru, state_rwkv, c, c_ctx, norm_g, final_norm_g, w_mod, b_mod, w_in, lru_conv_w, lru_conv_b, lru_wa, lru_ba, lru_wx, lru_bx, lru_lambda, rwkv_conv_w, rwkv_w0, rwkv_w2, rwkv_a0, rwkv_a2, rwkv_k_k, rwkv_k_a, rwkv_r_k, rwkv_g2, rwkv_ln_g, rwkv_ln_b, diff_lambda, diff_subln_g, w_lru_out, w_rwkv_out, w_diff_out, w_out, router_w, exp_w_gate, exp_w_up, exp_w_down):
    w = dict(norm_g=norm_g, w_in=w_in, lru_conv_w=lru_conv_w, lru_conv_b=lru_conv_b, lru_wa=lru_wa,
             lru_ba=lru_ba, lru_wx=lru_wx, lru_bx=lru_bx, lru_lambda=lru_lambda, rwkv_conv_w=rwkv_conv_w,
             rwkv_w0=rwkv_w0, rwkv_w2=rwkv_w2, rwkv_a0=rwkv_a0, rwkv_a2=rwkv_a2, rwkv_k_k=rwkv_k_k,
             rwkv_k_a=rwkv_k_a, rwkv_r_k=rwkv_r_k, rwkv_g2=rwkv_g2, rwkv_ln_g=rwkv_ln_g,
             rwkv_ln_b=rwkv_ln_b, diff_lambda=diff_lambda, diff_subln_g=diff_subln_g, w_lru_out=w_lru_out,
             w_rwkv_out=w_rwkv_out, w_diff_out=w_diff_out, w_out=w_out, router_w=router_w)

    x = jnp.concatenate([x_prompt.reshape(N_CTX_TOK, D_MODEL), x_sample.reshape(-1, D_MODEL)], axis=0)
    cvec = jnp.concatenate([c_ctx[None], c, jnp.zeros((SUBLANES - N_MOD_GROUPS, D_MODEL), f32)], axis=0)
    mod = _modulation(cvec, w_mod, b_mod)
    rope_cos, rope_sin = _rope_tables()
    final_g = final_norm_g.reshape(1, D_MODEL)
    ctx_rows = slice(0, N_CTX_TOK)
    lat_rows = slice(N_CTX_TOK, N_TOK)

    new_k, new_v, new_lru, new_rwkv = [], [], [], []
    for l in range(DEPTH):
        p = _prep_layer(l, w)
        lam_init = 0.8 - 0.6 * math.exp(-0.3 * l)
        mod_l = mod[l].reshape(SUBLANES, 1, 6 * D_MODEL)

        z = _in_projection(x, mod_l, p['norm_g1'], p['w_in'])
        new_k.append(z[ctx_rows, Z_K:Z_K + D_DIFF].reshape(N_CTX_SEQ, CTX_LEN, N_DIFF_HEADS, 2, DIFF_HEAD_DIM))
        new_v.append(z[ctx_rows, Z_V:Z_V + D_DIFF].reshape(N_CTX_SEQ, CTX_LEN, N_DIFF_HEADS, HEAD_W))

        h0_blocks = jnp.concatenate(
            [jnp.zeros((N_CTX_BLOCKS, 2, D_LRU), f32), state_lru[:, l].astype(f32)], axis=0)
        lru_out, lru_fin = _lru_branch(z, p, h0_blocks)
        n_per = SEQ_BLOCK // CTX_LEN
        fin = lru_fin[:N_CTX_BLOCKS].reshape(N_CTX_BLOCKS, 2, n_per, D_LRU)
        new_lru.append(jnp.transpose(fin, (0, 2, 1, 3)).reshape(N_CTX_SEQ, 2, D_LRU))

        scan_in, bonus, out_gate = _rwkv_pre(z, p)
        yf_c, yb_c, s_c = _rwkv_scan_path(scan_in, 0, N_CTX_SEQ, CTX_LEN, None)
        yf_l, yb_l, _ = _rwkv_scan_path(scan_in, N_CTX_TOK // LAT_LEN, N_LAT_SEQ, LAT_LEN, state_rwkv[:, l])
        new_rwkv.append(s_c)

        diff_out = _attention(z, cache_k[:, l].reshape(N_LAT_SEQ, PAST_LEN, D_DIFF),
                              cache_v[:, l].reshape(N_LAT_SEQ, PAST_LEN, D_DIFF),
                              rope_cos, rope_sin, p, lam_init)

        x, h2, logits = _merge(x, z, lru_out, (yf_c, yb_c), (yf_l, yb_l), bonus, out_gate, diff_out, mod_l, p)
        x = _moe_residual(x, h2, logits, exp_w_gate, exp_w_up, exp_w_down, l, mod_l, final_g,
                          final_norm=(l == DEPTH - 1))

    y_prompt = x[ctx_rows].reshape(x_prompt.shape)
    y_sample = x[lat_rows].reshape(x_sample.shape)
    return (y_prompt, y_sample, jnp.stack(new_k, axis=1), jnp.stack(new_v, axis=1),
            jnp.stack(new_lru, axis=1), jnp.stack(new_rwkv, axis=1))
```

```python
import functools
import math

import jax
import jax.numpy as jnp
from jax import lax
from jax.experimental import pallas as pl
from jax.experimental.pallas import tpu as pltpu

f32 = jnp.float32
bf16 = jnp.bfloat16

D_MODEL = 1024
DEPTH = 2
GRID_W = 64
NORM_EPS = 1e-6

N_CTX_SEQ = 16
CTX_LEN = 256
N_LAT_SEQ = 2
LAT_LEN = 1024
PAST_LEN = 256
N_CTX_TOK = N_CTX_SEQ * CTX_LEN
N_TOK = N_CTX_TOK + N_LAT_SEQ * LAT_LEN

D_LRU = 512
LRU_BLOCKS = 8
LRU_BLOCK_W = D_LRU // LRU_BLOCKS
LRU_C = 8.0

D_RWKV = 512
RWKV_HEAD_DIM = 64
N_RWKV_HEADS = D_RWKV // RWKV_HEAD_DIM
W_RANK = 64
A_RANK = 64
G_RANK = 128
RWKV_LN_EPS = 64e-5
RWKV_COLS = 3 * D_RWKV + 2 * W_RANK + 2 * A_RANK + G_RANK
RWKV_COLS_PAD = 2048

N_DIFF_HEADS = 4
DIFF_HEAD_DIM = 64
D_DIFF = N_DIFF_HEADS * 2 * DIFF_HEAD_DIM
ROPE_THETA = 10000.0

N_BRANCHES = 3
N_EXPERTS = 16
EXPERT_FF = 1024
EC_CAPACITY = 2
ROUTER_PAD = 128

LANES = 128
SUBLANES = 8

SEQ_BLOCK = 1024
N_SEQ_BLOCKS = N_TOK // SEQ_BLOCK
N_CTX_BLOCKS = N_CTX_TOK // SEQ_BLOCK
N_MOD_GROUPS = 1 + N_LAT_SEQ

Z_RWKV = 0
Z_GATES = RWKV_COLS_PAD
Z_LRU_X = Z_GATES + N_BRANCHES * D_MODEL
Z_LRU_G = Z_LRU_X + D_LRU
Z_Q = Z_LRU_G + D_LRU
Z_K = Z_Q + D_DIFF
Z_V = Z_K + D_DIFF
Z_COLS = Z_V + D_DIFF

VMEM_LIMIT = 56 * 1024 * 1024


def _cparams(n_axes):
    return pltpu.CompilerParams(dimension_semantics=("arbitrary",) * n_axes,
                                vmem_limit_bytes=VMEM_LIMIT)


def _seg_len(block_idx):
    return jnp.where(block_idx < N_CTX_BLOCKS, CTX_LEN, LAT_LEN)


def _mod_group_of_rows(row_block, rows_per_block):
    first_lat = N_CTX_TOK // rows_per_block
    per_seq = LAT_LEN // rows_per_block
    return jnp.where(row_block < first_lat, 0, (row_block - first_lat) // per_seq + 1)


def _shift_rows(v, d, tm, seg):
    n = v.shape[0]
    r = pltpu.roll(v, (-d) % n, axis=0)
    ok = (tm + d >= 0) & (tm + d < seg)
    return jnp.where(ok, r, 0.0)


def _split_dot(x, w_bf16):
    hi = x.astype(bf16)
    lo = (x - hi.astype(f32)).astype(bf16)
    return (jnp.dot(hi, w_bf16, preferred_element_type=f32)
            + jnp.dot(lo, w_bf16, preferred_element_type=f32))


MOD_TN = 1536


def _mod_kernel(c_ref, w_ref, b_ref, o_ref):
    c = c_ref[...]
    s = (c * jax.nn.sigmoid(c)).astype(bf16)
    o_ref[0] = jnp.dot(s, w_ref[0].astype(bf16), preferred_element_type=f32) + b_ref[0]


def _modulation(cvec, w_mod, b_mod):
    n_out = w_mod.shape[-1]
    return pl.pallas_call(
        _mod_kernel,
        grid=(DEPTH, n_out // MOD_TN),
        in_specs=[pl.BlockSpec((SUBLANES, D_MODEL), lambda l, j: (0, 0)),
                  pl.BlockSpec((1, D_MODEL, MOD_TN), lambda l, j: (l, 0, j)),
                  pl.BlockSpec((1, 1, MOD_TN), lambda l, j: (l, 0, j))],
        out_specs=pl.BlockSpec((1, SUBLANES, MOD_TN), lambda l, j: (l, 0, j)),
        out_shape=jax.ShapeDtypeStruct((DEPTH, SUBLANES, n_out), f32),
        compiler_params=_cparams(2),
        name="modulation",
    )(cvec, w_mod, b_mod.reshape(DEPTH, 1, n_out))


INPROJ_TM = 1024
INPROJ_TN = 768


def _inproj_kernel(x_ref, mod_ref, g_ref, w_ref, o_ref, h_ref):
    rows = pl.ds(pl.multiple_of(pl.program_id(1) * INPROJ_TM, INPROJ_TM), INPROJ_TM)

    @pl.when(pl.program_id(0) == 0)
    def _():
        x = x_ref[...]
        y = x * lax.rsqrt(jnp.mean(x * x, axis=-1, keepdims=True) + NORM_EPS) * g_ref[...]
        shift = mod_ref[0, :, 0:D_MODEL]
        scale = mod_ref[0, :, D_MODEL:2 * D_MODEL]
        h_ref[rows, :] = (y * (1.0 + scale) + shift).astype(bf16)

    o_ref[...] = jnp.dot(h_ref[rows, :], w_ref[...], preferred_element_type=f32)


def _in_projection(x, mod_l, norm_g_row, w_in_p):
    n_row_tiles = N_TOK // INPROJ_TM
    x_tile = lambda j, i: jnp.where(j == 0, i, n_row_tiles - 1)
    return pl.pallas_call(
        _inproj_kernel,
        grid=(Z_COLS // INPROJ_TN, n_row_tiles),
        in_specs=[pl.BlockSpec((INPROJ_TM, D_MODEL), lambda j, i: (x_tile(j, i), 0)),
                  pl.BlockSpec((1, 1, 2 * D_MODEL),
                               lambda j, i: (_mod_group_of_rows(x_tile(j, i), INPROJ_TM), 0, 0)),
                  pl.BlockSpec((1, D_MODEL), lambda j, i: (0, 0)),
                  pl.BlockSpec((D_MODEL, INPROJ_TN), lambda j, i: (0, j))],
        out_specs=pl.BlockSpec((INPROJ_TM, INPROJ_TN), lambda j, i: (i, j)),
        out_shape=jax.ShapeDtypeStruct((N_TOK, Z_COLS), f32),
        scratch_shapes=[pltpu.VMEM((N_TOK, D_MODEL), bf16)],
        compiler_params=_cparams(2),
        name="in_projection",
    )(x, mod_l, norm_g_row, w_in_p)


LRU_TILE_UNROLL = 4


def _lru_kernel(x_ref, gate_ref, cw_ref, cb_ref, wg_ref, bg_ref, lam_ref, h0_ref, out_ref, hfin_ref,
                a_scr, b_scr, yf_scr, yb_scr):
    blk = pl.program_id(0)
    seg = _seg_len(blk)
    n = x_ref.shape[0]
    t = lax.broadcasted_iota(jnp.int32, (n, D_LRU), 0)
    tm = t & (seg - 1)

    x = x_ref[...]
    xc = (cb_ref[...]
          + cw_ref[0:1, :] * _shift_rows(x, -2, tm, seg)
          + cw_ref[1:2, :] * _shift_rows(x, -1, tm, seg)
          + cw_ref[2:3, :] * x
          + cw_ref[3:4, :] * _shift_rows(x, 1, tm, seg))
    pre = jnp.dot(xc.astype(bf16), wg_ref[...], preferred_element_type=f32) + bg_ref[...]

    n_tiles = n // SUBLANES
    tiles_per_seg = seg // SUBLANES
    row = lax.broadcasted_iota(jnp.int32, (SUBLANES, D_LRU), 0)
    for d, y_scr in enumerate((yf_scr, yb_scr)):
        r = jax.nn.sigmoid(pre[:, (2 * d) * D_LRU:(2 * d + 1) * D_LRU])
        i = jax.nn.sigmoid(pre[:, (2 * d + 1) * D_LRU:(2 * d + 2) * D_LRU])
        log_a = -LRU_C * r * jax.nn.softplus(-lam_ref[d:d + 1, :])
        a = jnp.exp(log_a)
        a_scr[...] = a
        b_scr[...] = jnp.sqrt(1.0 - a * a) * (i * xc)
        h0 = h0_ref[0, d:d + 1, :]

        def tile_step(k, h_prev, d=d, y_scr=y_scr, h0=h0):
            idx = k if d == 0 else n_tiles - 1 - k
            rows = pl.ds(pl.multiple_of(idx * SUBLANES, SUBLANES), SUBLANES)
            a_t = a_scr[rows, :]
            b_t = b_scr[rows, :]
            s = 1
            while s < SUBLANES:
                ok = (row >= s) if d == 0 else (row < SUBLANES - s)
                shift = s if d == 0 else SUBLANES - s
                a_sh = jnp.where(ok, pltpu.roll(a_t, shift, axis=0), 1.0)
                b_sh = jnp.where(ok, pltpu.roll(b_t, shift, axis=0), 0.0)
                b_t = a_t * b_sh + b_t
                a_t = a_t * a_sh
                s *= 2
            first_of_seg = (idx & (tiles_per_seg - 1)) == (0 if d == 0 else tiles_per_seg - 1)
            h_t = a_t * jnp.where(first_of_seg, h0, h_prev) + b_t
            y_scr[rows, :] = h_t
            return h_t[SUBLANES - 1:SUBLANES, :] if d == 0 else h_t[0:1, :]

        lax.fori_loop(0, n_tiles, tile_step, jnp.zeros((1, D_LRU), f32), unroll=LRU_TILE_UNROLL)

    out_ref[...] = ((yf_scr[...] + yb_scr[...]) * jax.nn.gelu(gate_ref[...])).astype(out_ref.dtype)
    fins = ([yf_scr[(j + 1) * CTX_LEN - 1:(j + 1) * CTX_LEN, :] for j in range(SEQ_BLOCK // CTX_LEN)]
            + [yb_scr[j * CTX_LEN:j * CTX_LEN + 1, :] for j in range(SEQ_BLOCK // CTX_LEN)])
    hfin_ref[0] = jnp.concatenate(fins, axis=0)


def _lru_branch(z, p, h0_blocks):
    full = lambda shape: pl.BlockSpec(shape, lambda i: (0,) * len(shape))
    return pl.pallas_call(
        _lru_kernel,
        grid=(N_SEQ_BLOCKS,),
        in_specs=[pl.BlockSpec((SEQ_BLOCK, D_LRU), lambda i: (i, Z_LRU_X // D_LRU)),
                  pl.BlockSpec((SEQ_BLOCK, D_LRU), lambda i: (i, Z_LRU_G // D_LRU)),
                  full((4, D_LRU)), full((1, D_LRU)), full((D_LRU, 4 * D_LRU)), full((1, 4 * D_LRU)),
                  full((2, D_LRU)),
                  pl.BlockSpec((1, 2, D_LRU), lambda i: (i, 0, 0))],
        out_specs=[pl.BlockSpec((SEQ_BLOCK, D_LRU), lambda i: (i, 0)),
                   pl.BlockSpec((1, SUBLANES, D_LRU), lambda i: (i, 0, 0))],
        out_shape=[jax.ShapeDtypeStruct((N_TOK, D_LRU), bf16),
                   jax.ShapeDtypeStruct((N_SEQ_BLOCKS, SUBLANES, D_LRU), f32)],
        scratch_shapes=[pltpu.VMEM((SEQ_BLOCK, D_LRU), f32)] * 4,
        compiler_params=_cparams(1),
        name="lru_branch",
    )(z, z, p['lru_conv_w'], p['lru_conv_b'], p['lru_wg'], p['lru_bg'], p['lru_lambda'], h0_blocks)


RWKV_CT = 256
N_RWKV_CT = D_RWKV // RWKV_CT
RWKV_LOWRANK_OFF = 3 * D_RWKV
RWKV_LOWRANK_W = RWKV_COLS_PAD - RWKV_LOWRANK_OFF
SCAN_ARRAYS = ('w_f', 'w_b', 'ka_f', 'ka_b', 'kd_f', 'kd_b', 'nkk', 'r', 'v')
N_SCAN_ARRAYS = len(SCAN_ARRAYS)
SCAN_IDX = {name: i for i, name in enumerate(SCAN_ARRAYS)}


def _rwkv_pre_kernel(r_ref, k_ref, v_ref, lr_ref, cwr_ref, cwk_ref, cwv_ref, cwl_ref,
                     w2_ref, a2_ref, g2_ref, pc_ref, ones_ref, sc_ref, bonus_ref, g_ref):
    blk = pl.program_id(0)
    seg = _seg_len(blk)
    n = r_ref.shape[0]

    def conv(x_ref, cw_ref):
        width = x_ref.shape[1]
        t = lax.broadcasted_iota(jnp.int32, (n, width), 0)
        tm = t & (seg - 1)
        x = x_ref[...]
        return (cw_ref[0:1, :] * _shift_rows(x, -1, tm, seg) + cw_ref[1:2, :] * x
                + cw_ref[2:3, :] * _shift_rows(x, 1, tm, seg))

    r = conv(r_ref, cwr_ref)
    k = conv(k_ref, cwk_ref)
    v = conv(v_ref, cwv_ref)
    lr = conv(lr_ref, cwl_ref)
    wl = lr[:, 0:2 * W_RANK]
    al = lr[:, 2 * W_RANK:2 * W_RANK + 2 * A_RANK]
    gl = lr[:, 2 * W_RANK + 2 * A_RANK:2 * W_RANK + 2 * A_RANK + G_RANK]

    w0 = [pc_ref[0:1, :], pc_ref[1:2, :]]
    a0 = [pc_ref[2:3, :], pc_ref[3:4, :]]
    k_k = pc_ref[4:5, :]
    k_a = pc_ref[5:6, :]
    r_k = pc_ref[6:7, :]

    w_pre = jnp.dot(jnp.tanh(wl).astype(bf16), w2_ref[0], preferred_element_type=f32)
    a_pre = jnp.dot(al.astype(bf16), a2_ref[0], preferred_element_type=f32)

    kk = k * k_k
    ss = _split_dot(kk * kk, ones_ref[...])
    kk = kk / jnp.maximum(jnp.sqrt(ss), 1e-12)

    kd_sum = None
    for d, sfx in enumerate(('_f', '_b')):
        cols = slice(d * RWKV_CT, (d + 1) * RWKV_CT)
        w_log = -jax.nn.softplus(-(w0[d] + w_pre[:, cols])) - 0.5
        sc_ref[SCAN_IDX['w' + sfx]] = jnp.exp(-jnp.exp(w_log))
        a = jax.nn.sigmoid(a0[d] + a_pre[:, cols])
        sc_ref[SCAN_IDX['ka' + sfx]] = kk * a
        kd = k * (1.0 + (a - 1.0) * k_a)
        sc_ref[SCAN_IDX['kd' + sfx]] = kd
        kd_sum = kd if kd_sum is None else kd_sum + kd

    sc_ref[SCAN_IDX['r']] = r
    sc_ref[SCAN_IDX['v']] = v
    sc_ref[SCAN_IDX['nkk']] = -kk
    bonus_ref[...] = _split_dot(r * kd_sum * r_k, ones_ref[...]) * v
    g_ref[...] = jnp.dot(jax.nn.sigmoid(gl).astype(bf16), g2_ref[...], preferred_element_type=f32)


def _rwkv_pre(z, p):
    nct = D_RWKV // RWKV_CT
    zcol = lambda base: pl.BlockSpec((SEQ_BLOCK, RWKV_CT), lambda i, c, base=base: (i, base + c))
    cwcol = lambda base: pl.BlockSpec((3, RWKV_CT), lambda i, c, base=base: (0, base + c))
    out_spec = pl.BlockSpec((SEQ_BLOCK, RWKV_CT), lambda i, c: (i, c))
    return pl.pallas_call(
        _rwkv_pre_kernel,
        grid=(N_SEQ_BLOCKS, nct),
        in_specs=[zcol(0), zcol(nct), zcol(2 * nct),
                  pl.BlockSpec((SEQ_BLOCK, RWKV_LOWRANK_W),
                               lambda i, c: (i, RWKV_LOWRANK_OFF // RWKV_LOWRANK_W)),
                  cwcol(0), cwcol(nct), cwcol(2 * nct),
                  pl.BlockSpec((3, RWKV_LOWRANK_W), lambda i, c: (0, RWKV_LOWRANK_OFF // RWKV_LOWRANK_W)),
                  pl.BlockSpec((1, 2 * W_RANK, 2 * RWKV_CT), lambda i, c: (c, 0, 0)),
                  pl.BlockSpec((1, 2 * A_RANK, 2 * RWKV_CT), lambda i, c: (c, 0, 0)),
                  pl.BlockSpec((G_RANK, RWKV_CT), lambda i, c: (0, c)),
                  pl.BlockSpec((SUBLANES, RWKV_CT), lambda i, c: (0, c)),
                  pl.BlockSpec((RWKV_CT, RWKV_CT), lambda i, c: (0, 0))],
        out_specs=[pl.BlockSpec((N_SCAN_ARRAYS, SEQ_BLOCK, RWKV_CT), lambda i, c: (0, i, c)),
                   out_spec, out_spec],
        out_shape=[jax.ShapeDtypeStruct((N_SCAN_ARRAYS, N_TOK, D_RWKV), f32),
                   jax.ShapeDtypeStruct((N_TOK, D_RWKV), f32),
                   jax.ShapeDtypeStruct((N_TOK, D_RWKV), f32)],
        compiler_params=_cparams(2),
        name="rwkv_pre",
    )(z, z, z, z, p['rwkv_conv_w'], p['rwkv_conv_w'], p['rwkv_conv_w'], p['rwkv_conv_w'],
      p['rwkv_w2'], p['rwkv_a2'], p['rwkv_g2'], p['rwkv_pc'], p['head_ones'][:RWKV_CT, :RWKV_CT])


PACK_T = 128
TIME_BLOCK = SUBLANES
PACK_UNROLL = 8


def _gather_states(scr, row, n_seq):
    return scr[:, pl.ds(row, N_RWKV_HEADS, stride=RWKV_HEAD_DIM), :].reshape(n_seq * N_RWKV_HEADS, PACK_T)


def _pack_kernel(x_ref, o_ref, scr, *, n_seq, reps, rows, fold_rows):
    for b in range(n_seq):
        scr[b] = x_ref[b].T

    def body(row, c):
        if fold_rows:
            m = jnp.concatenate([_gather_states(scr, row * reps + rep, n_seq) for rep in range(reps)], axis=0)
        else:
            m = _gather_states(scr, row, n_seq)
            if reps > 1:
                m = jnp.concatenate([m] * reps, axis=0)
        mt = m.T
        line0 = pl.multiple_of(row * TIME_BLOCK, TIME_BLOCK)
        for u in range(PACK_T // TIME_BLOCK):
            o_ref[u, pl.ds(line0, TIME_BLOCK), :] = mt[u * TIME_BLOCK:(u + 1) * TIME_BLOCK]
        return c

    lax.fori_loop(0, rows, body, 0, unroll=PACK_UNROLL)


def _pack(stacked, first, count, n_seq, seq_len, seq0, reps, fold_rows):
    rows = RWKV_HEAD_DIM // reps if fold_rows else RWKV_HEAD_DIM
    x = stacked.reshape(N_SCAN_ARRAYS, N_TOK // seq_len, seq_len, D_RWKV)
    tb = PACK_T // TIME_BLOCK
    return pl.pallas_call(
        functools.partial(_pack_kernel, n_seq=n_seq, reps=reps, rows=rows, fold_rows=fold_rows),
        grid=(count, seq_len // PACK_T),
        in_specs=[pl.BlockSpec((None, n_seq, PACK_T, D_RWKV),
                               lambda a, j: (first + a, seq0 // n_seq, j, 0))],
        out_specs=pl.BlockSpec((None, tb, rows * TIME_BLOCK, LANES), lambda a, j: (a, j, 0, 0)),
        out_shape=jax.ShapeDtypeStruct((count, seq_len // TIME_BLOCK, rows * TIME_BLOCK, LANES), f32),
        scratch_shapes=[pltpu.VMEM((n_seq, D_RWKV, PACK_T), f32)],
        compiler_params=_cparams(2),
        name="rwkv_pack",
    )(x)


def _unpack_kernel(yf_ref, yb_ref, of_ref, ob_ref, scr, *, n_seq, reps, rows):
    group = n_seq * N_RWKV_HEADS
    for y_ref, o_ref in ((yf_ref, of_ref), (yb_ref, ob_ref)):
        def body(row, c, y_ref=y_ref):
            line0 = pl.multiple_of(row * TIME_BLOCK, TIME_BLOCK)
            mt = jnp.concatenate([y_ref[u, pl.ds(line0, TIME_BLOCK), :]
                                  for u in range(PACK_T // TIME_BLOCK)], axis=0)
            m = mt.T
            for rep in range(reps):
                part = m[rep * group:(rep + 1) * group].reshape(n_seq, N_RWKV_HEADS, PACK_T)
                scr[:, pl.ds(row * reps + rep, N_RWKV_HEADS, stride=RWKV_HEAD_DIM), :] = part
            return c

        lax.fori_loop(0, rows, body, 0, unroll=PACK_UNROLL)
        for b in range(n_seq):
            o_ref[b] = scr[b].T


def _unpack(y_f, y_b, n_seq, seq_len, reps):
    rows = y_f.shape[1] // TIME_BLOCK
    yspec = pl.BlockSpec((PACK_T // TIME_BLOCK, rows * TIME_BLOCK, LANES), lambda j: (j, 0, 0))
    ospec = pl.BlockSpec((n_seq, PACK_T, D_RWKV), lambda j: (0, j, 0))
    shape = jax.ShapeDtypeStruct((n_seq, seq_len, D_RWKV), f32)
    o_f, o_b = pl.pallas_call(
        functools.partial(_unpack_kernel, n_seq=n_seq, reps=reps, rows=rows),
        grid=(seq_len // PACK_T,),
        in_specs=[yspec, yspec],
        out_specs=[ospec, ospec],
        out_shape=[shape, shape],
        scratch_shapes=[pltpu.VMEM((n_seq, D_RWKV, PACK_T), f32)],
        compiler_params=_cparams(1),
        name="rwkv_unpack",
    )(y_f, y_b)
    return o_f.reshape(n_seq * seq_len, D_RWKV), o_b.reshape(n_seq * seq_len, D_RWKV)


def _rwkv_scan_kernel(wf_ref, wb_ref, kaf_ref, kab_ref, kdf_ref, kdb_ref, nkkf_ref, nkkb_ref,
                      rf_ref, rb_ref, vvf_ref, vvb_ref, s0_ref, yf_ref, yb_ref, s_ref, *,
                      time_blocks, v_rows, unroll):
    @pl.when(pl.program_id(0) == 0)
    def _():
        s_ref[...] = s0_ref[...]

    dirs = ((wf_ref, kaf_ref, kdf_ref, nkkf_ref, rf_ref, vvf_ref, yf_ref),
            (wb_ref, kab_ref, kdb_ref, nkkb_ref, rb_ref, vvb_ref, yb_ref))
    n_k = s_ref.shape[1]

    for u in range(time_blocks):
        def step(t, carry, u=u):
            for g, (w_ref, ka_ref, kd_ref, nkk_ref, r_ref, vv_ref, y_ref) in enumerate(dirs):
                ub = u if g == 0 else time_blocks - 1 - u
                tt = t if g == 0 else TIME_BLOCK - 1 - t
                lanes = slice(g * LANES, (g + 1) * LANES)
                at_t = pl.ds(tt, n_k, stride=TIME_BLOCK)
                w = w_ref[ub, at_t, :]
                ka = ka_ref[ub, at_t, :]
                kd = kd_ref[ub, at_t, :]
                nkk = nkk_ref[ub, at_t, :]
                r = r_ref[ub, at_t, :]

                def rows(vb, c):
                    v0 = pl.multiple_of(vb * SUBLANES, SUBLANES)
                    rows_at_t = pl.ds(v0 * TIME_BLOCK + tt, SUBLANES, stride=TIME_BLOCK)
                    vv = vv_ref[ub, rows_at_t, :]
                    ys = []
                    for j in range(SUBLANES):
                        s = s_ref[v0 + j, :, lanes]
                        sa = jnp.sum(s * nkk, axis=0, keepdims=True)
                        s_new = s * w + ka * sa + kd * vv[j:j + 1]
                        s_ref[v0 + j, :, lanes] = s_new
                        ys.append(jnp.sum(s_new * r, axis=0, keepdims=True))
                    y_ref[ub, rows_at_t, :] = jnp.concatenate(ys, axis=0)
                    return c

                lax.fori_loop(0, v_rows // SUBLANES, rows, 0, unroll=unroll)
            return carry

        lax.fori_loop(0, TIME_BLOCK, step, 0, unroll=2 if v_rows == SUBLANES else 1)


SCAN_STEPS_PER_BLOCK = 16


def _rwkv_scan_packed(kvecs, vv, s0, *, steps):
    K = kvecs.shape[2] // TIME_BLOCK
    L = kvecs.shape[1] * TIME_BLOCK
    VR = vv.shape[1] // TIME_BLOCK
    nblk = L // steps
    tbs = steps // TIME_BLOCK

    def kspec(name, mirrored):
        a = SCAN_IDX[name]
        if mirrored:
            return pl.BlockSpec((None, tbs, K * TIME_BLOCK, LANES), lambda i: (a, nblk - 1 - i, 0, 0))
        return pl.BlockSpec((None, tbs, K * TIME_BLOCK, LANES), lambda i: (a, i, 0, 0))

    vf = pl.BlockSpec((tbs, VR * TIME_BLOCK, LANES), lambda i: (i, 0, 0))
    vb = pl.BlockSpec((tbs, VR * TIME_BLOCK, LANES), lambda i: (nblk - 1 - i, 0, 0))
    sspec = pl.BlockSpec((VR, K, 2 * LANES), lambda i: (0, 0, 0))
    kern = functools.partial(_rwkv_scan_kernel, time_blocks=tbs, v_rows=VR,
                             unroll=min(8, VR // SUBLANES))
    return pl.pallas_call(
        kern,
        grid=(nblk,),
        in_specs=[kspec('w_f', False), kspec('w_b', True), kspec('ka_f', False), kspec('ka_b', True),
                  kspec('kd_f', False), kspec('kd_b', True), kspec('nkk', False), kspec('nkk', True),
                  kspec('r', False), kspec('r', True), vf, vb, sspec],
        out_specs=[vf, vb, sspec],
        out_shape=[jax.ShapeDtypeStruct((L // TIME_BLOCK, VR * TIME_BLOCK, LANES), f32),
                   jax.ShapeDtypeStruct((L // TIME_BLOCK, VR * TIME_BLOCK, LANES), f32),
                   jax.ShapeDtypeStruct((VR, K, 2 * LANES), f32)],
        compiler_params=_cparams(1),
        name="rwkv7_scan",
    )(*([kvecs] * 10), vv, vv, s0)


def _rwkv_scan_path(stacked, seq0, B, L, S0):
    H, K = N_RWKV_HEADS, RWKV_HEAD_DIM
    nd = 2
    reps = LANES // (B * H)
    VR = K // reps
    kvecs = _pack(stacked, 0, N_SCAN_ARRAYS - 1, B, L, seq0, reps, fold_rows=False)
    vv = _pack(stacked, SCAN_IDX['v'], 1, B, L, seq0, reps, fold_rows=True)[0]
    if S0 is None:
        s0 = jnp.zeros((VR, K, nd * LANES), f32)
    else:
        s0 = S0.astype(f32).reshape(B, nd, H, VR, reps, K)
        s0 = jnp.transpose(s0, (3, 5, 1, 4, 0, 2)).reshape(VR, K, nd * LANES)

    y_f, y_b, s_fin = _rwkv_scan_packed(kvecs, vv, s0, steps=SCAN_STEPS_PER_BLOCK)
    y_f, y_b = _unpack(y_f, y_b, B, L, reps)
    s_fin = s_fin.reshape(VR, K, nd, reps, B, H)
    s_fin = jnp.transpose(s_fin, (4, 2, 5, 0, 3, 1)).reshape(B, nd, H, K, K)
    return y_f, y_b, s_fin


ATT_TQ = 256
HEAD_W = 2 * DIFF_HEAD_DIM


def _rope(x, cos, sin_signed, lo_mask):
    n = x.shape[1]
    partner = jnp.where(lo_mask, pltpu.roll(x, n - DIFF_HEAD_DIM // 4, axis=1),
                        pltpu.roll(x, DIFF_HEAD_DIM // 4, axis=1))
    return x * cos + partner * sin_signed


def _attn_heads(q, k, v, lam, subln_g, out_scale):
    lane = lax.broadcasted_iota(jnp.int32, (1, HEAD_W), 1)
    first = lane < DIFF_HEAD_DIM
    outs = []
    for h in range(N_DIFF_HEADS):
        cols = slice(h * HEAD_W, (h + 1) * HEAD_W)
        qh = (q[:, cols] * (DIFF_HEAD_DIM ** -0.5)).astype(bf16)
        kh = k[:, cols]
        vh = v[:, cols].astype(bf16)
        probs = []
        for m in range(2):
            km = jnp.where(first if m == 0 else ~first, kh, 0.0).astype(bf16)
            s = lax.dot_general(qh, km, (((1,), (1,)), ((), ())), preferred_element_type=f32)
            s = s - jnp.max(s, axis=-1, keepdims=True)
            e = jnp.exp(s)
            probs.append(e / jnp.sum(e, axis=-1, keepdims=True))
        attn = (probs[0] - lam * probs[1]).astype(bf16)
        o = jnp.dot(attn, vh, preferred_element_type=f32)
        o = o * lax.rsqrt(jnp.mean(o * o, axis=-1, keepdims=True) + NORM_EPS) * subln_g * out_scale
        outs.append(o)
    return jnp.concatenate(outs, axis=1)


def _attn_kernel(q_ref, k_ref, v_ref, ck_ref, cv_ref, cos_ref, sin_ref, lv_ref, g_ref, o_ref,
                 kall_ref, vall_ref, *, lam_init):
    blk = pl.program_id(0)
    qt = pl.program_id(1)
    lv = lv_ref[...]
    lam = (jnp.exp(jnp.sum(lv[0:1] * lv[1:2], axis=-1, keepdims=True))
           - jnp.exp(jnp.sum(lv[2:3] * lv[3:4], axis=-1, keepdims=True)) + lam_init)
    subln_g = g_ref[...]
    out_scale = 1.0 - lam_init
    row0 = pl.multiple_of(qt * ATT_TQ, ATT_TQ)

    @pl.when(blk < N_CTX_BLOCKS)
    def _():
        q = q_ref[pl.ds(row0, ATT_TQ), :]
        k = k_ref[pl.ds(row0, ATT_TQ), :]
        v = v_ref[pl.ds(row0, ATT_TQ), :]
        o_ref[...] = _attn_heads(q, k, v, lam, subln_g, out_scale).astype(o_ref.dtype)

    @pl.when(blk >= N_CTX_BLOCKS)
    def _():
        lane = lax.broadcasted_iota(jnp.int32, (1, D_DIFF), 1)
        lo_mask = (lane % (DIFF_HEAD_DIM // 2)) < (DIFF_HEAD_DIM // 4)

        @pl.when(qt == 0)
        def _():
            kall_ref[0:LAT_LEN, :] = _rope(k_ref[...], cos_ref[...], sin_ref[...], lo_mask)
            kall_ref[LAT_LEN:LAT_LEN + PAST_LEN, :] = ck_ref[0]
            vall_ref[0:LAT_LEN, :] = v_ref[...]
            vall_ref[LAT_LEN:LAT_LEN + PAST_LEN, :] = cv_ref[0]

        q = _rope(q_ref[pl.ds(row0, ATT_TQ), :], cos_ref[pl.ds(row0, ATT_TQ), :],
                  sin_ref[pl.ds(row0, ATT_TQ), :], lo_mask)
        o_ref[...] = _attn_heads(q, kall_ref[...], vall_ref[...], lam, subln_g,
                                 out_scale).astype(o_ref.dtype)


def _attention(z, cache_k_l, cache_v_l, rope_cos, rope_sin, p, lam_init):
    zcol = lambda off: pl.BlockSpec((SEQ_BLOCK, D_DIFF), lambda i, j, off=off: (i, off // D_DIFF))
    cache = pl.BlockSpec((1, PAST_LEN, D_DIFF), lambda i, j: (jnp.maximum(i - N_CTX_BLOCKS, 0), 0, 0))
    full = lambda shape: pl.BlockSpec(shape, lambda i, j: (0,) * len(shape))
    return pl.pallas_call(
        functools.partial(_attn_kernel, lam_init=lam_init),
        grid=(N_SEQ_BLOCKS, SEQ_BLOCK // ATT_TQ),
        in_specs=[zcol(Z_Q), zcol(Z_K), zcol(Z_V), cache, cache,
                  full((LAT_LEN, D_DIFF)), full((LAT_LEN, D_DIFF)),
                  full((4, DIFF_HEAD_DIM)), full((1, HEAD_W))],
        out_specs=pl.BlockSpec((ATT_TQ, D_DIFF), lambda i, j: (i * (SEQ_BLOCK // ATT_TQ) + j, 0)),
        out_shape=jax.ShapeDtypeStruct((N_TOK, D_DIFF), bf16),
        scratch_shapes=[pltpu.VMEM((LAT_LEN + PAST_LEN, D_DIFF), f32),
                        pltpu.VMEM((LAT_LEN + PAST_LEN, D_DIFF), f32)],
        compiler_params=_cparams(2),
        name="diff_attention",
    )(z, z, z, cache_k_l, cache_v_l, rope_cos, rope_sin, p['diff_lambda'], p['diff_subln_g'])


def _rope_tables():
    t = jnp.arange(LAT_LEN)
    row = (t // GRID_W).astype(f32)
    col = (t % GRID_W).astype(f32)
    nf = DIFF_HEAD_DIM // 4
    freqs = ROPE_THETA ** (-jnp.arange(nf, dtype=f32) / nf)
    d = jnp.arange(DIFF_HEAD_DIM)
    pos = jnp.where((d < DIFF_HEAD_DIM // 2)[None, :], row[:, None], col[:, None])
    ang = pos * freqs[d % nf][None, :]
    sign = jnp.where((d % (2 * nf)) < nf, -1.0, 1.0)[None, :]
    cos = jnp.tile(jnp.cos(ang), (1, D_DIFF // DIFF_HEAD_DIM))
    sin = jnp.tile(jnp.sin(ang) * sign, (1, D_DIFF // DIFF_HEAD_DIM))
    return cos, sin


MERGE_TM = 512


def _merge_kernel(x_ref, lru_ref, yfc_ref, ybc_ref, yfl_ref, ybl_ref, bonus_ref, g_ref, diff_ref,
                  gp0_ref, gp1_ref, gp2_ref,
                  mod_ref, lnp_ref, ones_ref, wlo_ref, wro_ref, wdo_ref, wout_ref, ng_ref, rhi_ref, rlo_ref,
                  xo_ref, h2_ref, logit_ref):
    is_ctx = pl.program_id(0) < N_CTX_TOK // MERGE_TM
    y = jnp.where(is_ctx, yfc_ref[...] + ybc_ref[...], yfl_ref[...] + ybl_ref[...])
    inv = 1.0 / RWKV_HEAD_DIM
    mu = _split_dot(y, ones_ref[...]) * inv
    dlt = y - mu
    var = _split_dot(dlt * dlt, ones_ref[...]) * inv
    yn = dlt * lax.rsqrt(var + RWKV_LN_EPS)
    rwkv_out = ((yn * lnp_ref[0:1, :] + lnp_ref[1:2, :] + bonus_ref[...]) * g_ref[...]).astype(bf16)

    merged = (jax.nn.sigmoid(gp0_ref[...]) * jnp.dot(lru_ref[...], wlo_ref[...], preferred_element_type=f32)
              + jax.nn.sigmoid(gp1_ref[...]) * jnp.dot(rwkv_out, wro_ref[...], preferred_element_type=f32)
              + jax.nn.sigmoid(gp2_ref[...]) * jnp.dot(diff_ref[...], wdo_ref[...], preferred_element_type=f32))
    mix = jnp.dot(merged.astype(bf16), wout_ref[...], preferred_element_type=f32)

    gate1 = mod_ref[0, :, 2 * D_MODEL:3 * D_MODEL]
    shift2 = mod_ref[0, :, 3 * D_MODEL:4 * D_MODEL]
    scale2 = mod_ref[0, :, 4 * D_MODEL:5 * D_MODEL]
    x = x_ref[...] + gate1 * mix
    xo_ref[...] = x
    yn2 = x * lax.rsqrt(jnp.mean(x * x, axis=-1, keepdims=True) + NORM_EPS) * ng_ref[...]
    h2 = yn2 * (1.0 + scale2) + shift2
    hi = h2.astype(bf16)
    lo = (h2 - hi.astype(f32)).astype(bf16)
    h2_ref[...] = hi
    logit_ref[...] = (jnp.dot(hi, rhi_ref[...], preferred_element_type=f32)
                      + jnp.dot(lo, rhi_ref[...], preferred_element_type=f32)
                      + jnp.dot(hi, rlo_ref[...], preferred_element_type=f32))


def _merge(x, z, lru_out, y_ctx, y_lat, bonus, g, diff_out, mod_l, p):
    n_ctx_tiles = N_CTX_TOK // MERGE_TM
    row = lambda w: pl.BlockSpec((MERGE_TM, w), lambda i: (i, 0))
    ctx_row = pl.BlockSpec((MERGE_TM, D_RWKV), lambda i: (jnp.minimum(i, n_ctx_tiles - 1), 0))
    lat_row = pl.BlockSpec((MERGE_TM, D_RWKV), lambda i: (jnp.maximum(i - n_ctx_tiles, 0), 0))
    gate = lambda b: pl.BlockSpec((MERGE_TM, D_MODEL), lambda i, b=b: (i, Z_GATES // D_MODEL + b))
    full = lambda shape: pl.BlockSpec(shape, lambda i: (0,) * len(shape))
    return pl.pallas_call(
        _merge_kernel,
        grid=(N_TOK // MERGE_TM,),
        in_specs=[row(D_MODEL), row(D_LRU), ctx_row, ctx_row, lat_row, lat_row,
                  row(D_RWKV), row(D_RWKV), row(D_DIFF),
                  gate(0), gate(1), gate(2),
                  pl.BlockSpec((1, 1, 6 * D_MODEL), lambda i: (_mod_group_of_rows(i, MERGE_TM), 0, 0)),
                  full((2, D_RWKV)), full((D_RWKV, D_RWKV)),
                  full((D_LRU, D_MODEL)), full((D_RWKV, D_MODEL)), full((D_DIFF, D_MODEL)),
                  full((D_MODEL, D_MODEL)), full((1, D_MODEL)),
                  full((D_MODEL, ROUTER_PAD)), full((D_MODEL, ROUTER_PAD))],
        out_specs=[row(D_MODEL), row(D_MODEL), row(ROUTER_PAD)],
        out_shape=[jax.ShapeDtypeStruct((N_TOK, D_MODEL), f32),
                   jax.ShapeDtypeStruct((N_TOK, D_MODEL), bf16),
                   jax.ShapeDtypeStruct((N_TOK, ROUTER_PAD), f32)],
        compiler_params=_cparams(1),
        name="merge",
    )(x, lru_out, *y_ctx, *y_lat, bonus, g, diff_out, z, z, z, mod_l, p['rwkv_lnp'], p['head_ones'],
      p['w_lru_out'], p['w_rwkv_out'], p['w_diff_out'], p['w_out'], p['norm_g2'],
      p['router_hi'], p['router_lo'])


CAP_CTX = EC_CAPACITY * N_CTX_TOK // N_EXPERTS
CAP_LAT = EC_CAPACITY * (N_TOK - N_CTX_TOK) // N_EXPERTS
N_SLOTS = CAP_CTX + CAP_LAT
F32_MANTISSA_BITS = 23
F32_EXPONENT_BIAS = 127
LADDER_STEPS = 8
BISECT_STEPS = 60


def _cumsum_lanes(x01):
    rows, n = x01.shape
    i = lax.broadcasted_iota(jnp.int32, (LANES, LANES), 0)
    j = lax.broadcasted_iota(jnp.int32, (LANES, LANES), 1)
    upper = jnp.where(i <= j, 1.0, 0.0).astype(bf16)
    off = jnp.zeros((rows, 1), f32)
    outs = []
    for t in range(n // LANES):
        c = jnp.dot(x01[:, t * LANES:(t + 1) * LANES].astype(bf16), upper, preferred_element_type=f32) + off
        outs.append(c)
        off = c[:, LANES - 1:LANES]
    return jnp.concatenate(outs, axis=1)


def _route_kernel(logit_ref, pos_ref, gate_ref, *, cap, slot0):
    n = logit_ref.shape[0]
    lane = lax.broadcasted_iota(jnp.int32, (1, ROUTER_PAD), 1)
    lg = jnp.where(lane < N_EXPERTS, logit_ref[...], -jnp.inf)
    ex = jnp.exp(lg - jnp.max(lg, axis=-1, keepdims=True))
    aff = (ex / jnp.sum(ex, axis=-1, keepdims=True)).T[:N_EXPERTS]

    def n_above(thr):
        return jnp.sum(jnp.where(aff > thr, 1.0, 0.0), axis=1, keepdims=True)

    def rung(j):
        return pltpu.bitcast(jnp.left_shift(j, F32_MANTISSA_BITS), f32)

    def pick_rung(_, jj):
        j_lo, j_hi = jj
        j_mid = (j_lo + j_hi) >> 1
        enough = n_above(rung(j_mid)) >= cap
        return jnp.where(enough, j_mid, j_lo), jnp.where(enough, j_hi, j_mid)

    j_lo, j_hi = lax.fori_loop(0, LADDER_STEPS, pick_rung,
                               (jnp.full((N_EXPERTS, 1), -1, jnp.int32),
                                jnp.full((N_EXPERTS, 1), F32_EXPONENT_BIAS + 1, jnp.int32)))

    def bisect(_, lh):
        lo, hi = lh
        mid = 0.5 * (lo + hi)
        enough = n_above(mid) >= cap
        return jnp.where(enough, mid, lo), jnp.where(enough, hi, mid)

    lo, hi = lax.fori_loop(0, BISECT_STEPS, bisect, (jnp.where(j_lo < 0, -1.0, rung(j_lo)), rung(j_hi)))
    above = aff > hi
    tie = (aff > lo) & (aff <= hi)
    need = cap - n_above(hi)
    keep = above | (tie & (_cumsum_lanes(jnp.where(tie, 1.0, 0.0)) <= need))
    slot = _cumsum_lanes(jnp.where(keep, 1.0, 0.0)) - 1.0 + slot0
    pos_ref[...] = jnp.where(keep, slot, -1.0).astype(jnp.int32)
    gate_ref[...] = jnp.where(keep, aff, 0.0)


def _route(logits, block, n, cap, slot0):
    shape = (N_EXPERTS, n)
    pos, gate = pl.pallas_call(
        functools.partial(_route_kernel, cap=cap, slot0=slot0),
        grid=(1,),
        in_specs=[pl.BlockSpec((n, ROUTER_PAD), lambda i: (block, 0))],
        out_specs=[pl.BlockSpec(shape, lambda i: (0, 0))] * 2,
        out_shape=[jax.ShapeDtypeStruct(shape, jnp.int32), jax.ShapeDtypeStruct(shape, f32)],
        compiler_params=_cparams(1),
        name="route",
    )(logits)
    return pos.reshape(N_EXPERTS, 1, n), gate.reshape(N_EXPERTS, 1, n)


FFN_TF = 512
MOE_TT = 1024


def _one_hot(pos_row, slot0, n_slots):
    slots = lax.broadcasted_iota(jnp.int32, (n_slots, pos_row.shape[1]), 0) + slot0
    return pos_row == slots


N_MOE_TILES = N_TOK // MOE_TT
SLOT_CHUNK_CTX = CAP_CTX // 2
SLOT_CHUNK_LAT = CAP_LAT // 2


def _slot_chunks(is_ctx):
    slot0, n_slots, size = (0, CAP_CTX, SLOT_CHUNK_CTX) if is_ctx else (CAP_CTX, CAP_LAT, SLOT_CHUNK_LAT)
    return [(slot0 + c * size, size) for c in range(n_slots // size)]


def _tile_slot_ranges(pos):
    p = pos.reshape(N_EXPERTS, N_MOE_TILES, MOE_TT)
    lo = jnp.min(jnp.where(p >= 0, p, N_SLOTS), axis=2)
    hi = jnp.max(p, axis=2) + 1
    return jnp.stack([lo, hi]).astype(jnp.int32)


def _ffn_kernel(rng_ref, pos_ref, gate_ref, h_ref, wg_ref, wu_ref, wd_ref, o_ref, xe_ref, gs_ref, hid_ref):
    e = pl.program_id(0)
    f = pl.program_id(1)

    @pl.when(f == 0)
    def _():
        xe_ref[...] = jnp.zeros_like(xe_ref)
        gs_ref[...] = jnp.zeros_like(gs_ref)
        for tile in range(N_MOE_TILES):
            lo = rng_ref[0, e, tile]
            hi = rng_ref[1, e, tile]
            tok = slice(tile * MOE_TT, (tile + 1) * MOE_TT)
            for c0, size in _slot_chunks(tile * MOE_TT < N_CTX_TOK):
                @pl.when((lo < c0 + size) & (hi > c0))
                def _(c0=c0, size=size, tok=tok):
                    sel = _one_hot(pos_ref[0, :, tok], c0, size)
                    rows = slice(c0, c0 + size)
                    xe_ref[rows, :] += jnp.dot(jnp.where(sel, 1.0, 0.0).astype(bf16), h_ref[tok, :],
                                               preferred_element_type=f32)
                    gs_ref[rows, :] += jnp.sum(jnp.where(sel, gate_ref[0, :, tok], 0.0), axis=1,
                                               keepdims=True)

    x = xe_ref[...].astype(bf16)
    gate = jnp.dot(x, wg_ref[0, 0].astype(bf16), preferred_element_type=f32)
    up = jnp.dot(x, wu_ref[0, 0].astype(bf16), preferred_element_type=f32)
    hid = (gate * jax.nn.sigmoid(gate) * up).astype(bf16)
    for half in range(EXPERT_FF // FFN_TF):
        @pl.when(f == half)
        def _(half=half):
            hid_ref[:, half * FFN_TF:(half + 1) * FFN_TF] = hid

    @pl.when(f == pl.num_programs(1) - 1)
    def _():
        out = jnp.dot(hid_ref[...], wd_ref[0, 0].astype(bf16), preferred_element_type=f32)
        o_ref[0] = (out * gs_ref[...]).astype(o_ref.dtype)


def _expert_ffn(rng, pos, gate, h2, w_gate, w_up, w_down, layer):
    tok_row = pl.BlockSpec((1, 1, N_TOK), lambda e, f, rng: (e, 0, 0))
    return pl.pallas_call(
        _ffn_kernel,
        grid_spec=pltpu.PrefetchScalarGridSpec(
            num_scalar_prefetch=1,
            grid=(N_EXPERTS, EXPERT_FF // FFN_TF),
            in_specs=[tok_row, tok_row,
                      pl.BlockSpec((N_TOK, D_MODEL), lambda e, f, rng: (0, 0), pipeline_mode=pl.Buffered(1)),
                      pl.BlockSpec((1, 1, D_MODEL, FFN_TF), lambda e, f, rng: (layer, e, 0, f)),
                      pl.BlockSpec((1, 1, D_MODEL, FFN_TF), lambda e, f, rng: (layer, e, 0, f)),
                      pl.BlockSpec((1, 1, EXPERT_FF, D_MODEL), lambda e, f, rng: (layer, e, 0, 0))],
            out_specs=pl.BlockSpec((1, N_SLOTS, D_MODEL), lambda e, f, rng: (e, 0, 0)),
            scratch_shapes=[pltpu.VMEM((N_SLOTS, D_MODEL), f32),
                            pltpu.VMEM((N_SLOTS, 1), f32),
                            pltpu.VMEM((N_SLOTS, EXPERT_FF), bf16)]),
        out_shape=jax.ShapeDtypeStruct((N_EXPERTS, N_SLOTS, D_MODEL), bf16),
        compiler_params=_cparams(2),
        name="expert_ffn",
    )(rng, pos, gate, h2, w_gate, w_up, w_down)


COMBINE_GROUP = 4


def _combine_kernel(x_ref, pos_ref, ye_ref, mod_ref, g_ref, xo_ref, acc_ref, *, final_norm):
    tile = pl.program_id(0)
    e = pl.program_id(1)

    @pl.when(e == 0)
    def _():
        acc_ref[...] = jnp.zeros_like(acc_ref)

    def scatter(slot0, n_slots):
        sel = jnp.concatenate([jnp.where(_one_hot(pos_ref[g], slot0, n_slots), 1.0, 0.0).astype(bf16)
                               for g in range(COMBINE_GROUP)], axis=0)
        ye = ye_ref[:, slot0:slot0 + n_slots, :].reshape(COMBINE_GROUP * n_slots, D_MODEL)
        acc_ref[...] += lax.dot_general(sel, ye, (((0,), (0,)), ((), ())), preferred_element_type=f32)

    @pl.when(tile * MOE_TT < N_CTX_TOK)
    def _():
        scatter(0, CAP_CTX)

    @pl.when(tile * MOE_TT >= N_CTX_TOK)
    def _():
        scatter(CAP_CTX, CAP_LAT)

    @pl.when(e == pl.num_programs(1) - 1)
    def _():
        gate2 = mod_ref[0, :, 5 * D_MODEL:6 * D_MODEL]
        x = x_ref[...] + gate2 * acc_ref[...]
        if final_norm:
            x = x * lax.rsqrt(jnp.mean(x * x, axis=-1, keepdims=True) + NORM_EPS) * g_ref[...]
        xo_ref[...] = x


def _combine(x, pos, ye, mod_l, final_g, final_norm):
    row = pl.BlockSpec((MOE_TT, D_MODEL), lambda i, e: (i, 0))
    return pl.pallas_call(
        functools.partial(_combine_kernel, final_norm=final_norm),
        grid=(N_MOE_TILES, N_EXPERTS // COMBINE_GROUP),
        in_specs=[row,
                  pl.BlockSpec((COMBINE_GROUP, 1, MOE_TT), lambda i, e: (e, 0, i)),
                  pl.BlockSpec((COMBINE_GROUP, N_SLOTS, D_MODEL), lambda i, e: (e, 0, 0)),
                  pl.BlockSpec((1, 1, 6 * D_MODEL), lambda i, e: (_mod_group_of_rows(i, MOE_TT), 0, 0)),
                  pl.BlockSpec((1, D_MODEL), lambda i, e: (0, 0))],
        out_specs=row,
        out_shape=jax.ShapeDtypeStruct((N_TOK, D_MODEL), f32),
        scratch_shapes=[pltpu.VMEM((MOE_TT, D_MODEL), f32)],
        compiler_params=_cparams(2),
        name="moe_combine",
    )(x, pos, ye, mod_l, final_g)


def _moe_residual(x, h2, logits, w_gate, w_up, w_down, layer, mod_l, final_g, final_norm):
    pos_c, gate_c = _route(logits, 0, N_CTX_TOK, CAP_CTX, 0)
    lat_n = N_TOK - N_CTX_TOK
    pos_l, gate_l = _route(logits, N_CTX_TOK // lat_n, lat_n, CAP_LAT, CAP_CTX)
    pos = jnp.concatenate([pos_c, pos_l], axis=2)
    gate = jnp.concatenate([gate_c, gate_l], axis=2)
    rng = _tile_slot_ranges(pos)
    ye = _expert_ffn(rng, pos, gate, h2, w_gate, w_up, w_down, layer)
    return _combine(x, pos, ye, mod_l, final_g, final_norm)


def _block_diag(blocks):
    n, a, b = blocks.shape
    eye = jnp.eye(n, dtype=blocks.dtype)
    return (eye[:, None, :, None] * blocks[:, :, None, :]).reshape(n * a, n * b)


def _prep_layer(l, w):
    p = {}
    w_in = w['w_in'][l]
    lru_x, lru_g, zr, q, k, v, gates = (
        w_in[:, 0:D_LRU], w_in[:, D_LRU:2 * D_LRU], w_in[:, 2 * D_LRU:2 * D_LRU + RWKV_COLS],
        w_in[:, 2 * D_LRU + RWKV_COLS:2 * D_LRU + RWKV_COLS + D_DIFF],
        w_in[:, 2 * D_LRU + RWKV_COLS + D_DIFF:2 * D_LRU + RWKV_COLS + 2 * D_DIFF],
        w_in[:, 2 * D_LRU + RWKV_COLS + 2 * D_DIFF:2 * D_LRU + RWKV_COLS + 3 * D_DIFF],
        w_in[:, 2 * D_LRU + RWKV_COLS + 3 * D_DIFF:])
    pad = jnp.zeros((D_MODEL, RWKV_COLS_PAD - RWKV_COLS), f32)
    p['w_in'] = jnp.concatenate([zr, pad, gates, lru_x, lru_g, q, k, v], axis=1).astype(bf16)
    p['norm_g1'] = w['norm_g'][l, 0].reshape(1, D_MODEL)
    p['norm_g2'] = w['norm_g'][l, 1].reshape(1, D_MODEL)

    p['lru_conv_w'] = w['lru_conv_w'][l]
    p['lru_conv_b'] = w['lru_conv_b'][l].reshape(1, D_LRU)
    p['lru_wg'] = jnp.concatenate(
        [_block_diag(w[name][l, d]) for d in range(2) for name in ('lru_wa', 'lru_wx')], axis=1).astype(bf16)
    p['lru_bg'] = jnp.concatenate(
        [w[name][l, d] for d in range(2) for name in ('lru_ba', 'lru_bx')]).reshape(1, 4 * D_LRU)
    p['lru_lambda'] = w['lru_lambda'][l]

    p['rwkv_conv_w'] = jnp.pad(w['rwkv_conv_w'][l], ((0, 0), (0, RWKV_COLS_PAD - RWKV_COLS)))

    def lowrank_pair(m):
        rank = m.shape[1]
        fwd, bwd = (m[d].reshape(rank, N_RWKV_CT, RWKV_CT).transpose(1, 0, 2) for d in range(2))
        zero = jnp.zeros_like(fwd)
        return jnp.concatenate([jnp.concatenate([fwd, zero], axis=2),
                                jnp.concatenate([zero, bwd], axis=2)], axis=1).astype(bf16)

    p['rwkv_w2'] = lowrank_pair(w['rwkv_w2'][l])
    p['rwkv_a2'] = lowrank_pair(w['rwkv_a2'][l])
    p['rwkv_g2'] = w['rwkv_g2'][l].astype(bf16)
    p['rwkv_pc'] = jnp.concatenate(
        [w['rwkv_w0'][l], w['rwkv_a0'][l], w['rwkv_k_k'][l][None], w['rwkv_k_a'][l][None],
         w['rwkv_r_k'][l].reshape(1, D_RWKV), jnp.zeros((1, D_RWKV), f32)], axis=0)
    p['rwkv_lnp'] = jnp.stack([w['rwkv_ln_g'][l], w['rwkv_ln_b'][l]], axis=0)
    p['head_ones'] = _block_diag(jnp.ones((N_RWKV_HEADS, RWKV_HEAD_DIM, RWKV_HEAD_DIM), bf16))

    p['diff_lambda'] = w['diff_lambda'][l]
    p['diff_subln_g'] = w['diff_subln_g'][l].reshape(1, HEAD_W)
    p['w_lru_out'] = w['w_lru_out'][l].astype(bf16)
    p['w_rwkv_out'] = w['w_rwkv_out'][l].astype(bf16)
    p['w_diff_out'] = w['w_diff_out'][l].astype(bf16)
    p['w_out'] = w['w_out'][l].astype(bf16)
    rw = jnp.pad(w['router_w'][l], ((0, 0), (0, ROUTER_PAD - N_EXPERTS)))
    p['router_hi'] = rw.astype(bf16)
    p['router_lo'] = (rw - p['router_hi'].astype(f32)).astype(bf16)
    return p


def kernel(x_prompt, x_sample, cache_k, cache_v, state_lru, state_rwkv, c, c_ctx, norm_g, final_norm_g, w_mod, b_mod, w_in, lru_conv_w, lru_conv_b, lru_wa, lru_ba, lru_wx, lru_bx, lru_lambda, rwkv_conv_w, rwkv_w0, rwkv_w2, rwkv_a0, rwkv_a2, rwkv_k_k, rwkv_k_a, rwkv_r_k, rwkv_g2, rwkv_ln_g, rwkv_ln_b, diff_lambda, diff_subln_g, w_lru_out, w_rwkv_out, w_diff_out, w_out, router_w, exp_w_gate, exp_w_up, exp_w_down):
    w = dict(norm_g=norm_g, w_in=w_in, lru_conv_w=lru_conv_w, lru_conv_b=lru_conv_b, lru_wa=lru_wa,
             lru_ba=lru_ba, lru_wx=lru_wx, lru_bx=lru_bx, lru_lambda=lru_lambda, rwkv_conv_w=rwkv_conv_w,
             rwkv_w0=rwkv_w0, rwkv_w2=rwkv_w2, rwkv_a0=rwkv_a0, rwkv_a2=rwkv_a2, rwkv_k_k=rwkv_k_k,
             rwkv_k_a=rwkv_k_a, rwkv_r_k=rwkv_r_k, rwkv_g2=rwkv_g2, rwkv_ln_g=rwkv_ln_g,
             rwkv_ln_b=rwkv_ln_b, diff_lambda=diff_lambda, diff_subln_g=diff_subln_g, w_lru_out=w_lru_out,
             w_rwkv_out=w_rwkv_out, w_diff_out=w_diff_out, w_out=w_out, router_w=router_w)

    x = jnp.concatenate([x_prompt.reshape(N_CTX_TOK, D_MODEL), x_sample.reshape(-1, D_MODEL)], axis=0)
    cvec = jnp.concatenate([c_ctx[None], c, jnp.zeros((SUBLANES - N_MOD_GROUPS, D_MODEL), f32)], axis=0)
    mod = _modulation(cvec, w_mod, b_mod)
    rope_cos, rope_sin = _rope_tables()
    final_g = final_norm_g.reshape(1, D_MODEL)
    ctx_rows = slice(0, N_CTX_TOK)
    lat_rows = slice(N_CTX_TOK, N_TOK)

    new_k, new_v, new_lru, new_rwkv = [], [], [], []
    for l in range(DEPTH):
        p = _prep_layer(l, w)
        lam_init = 0.8 - 0.6 * math.exp(-0.3 * l)
        mod_l = mod[l].reshape(SUBLANES, 1, 6 * D_MODEL)

        z = _in_projection(x, mod_l, p['norm_g1'], p['w_in'])
        new_k.append(z[ctx_rows, Z_K:Z_K + D_DIFF].reshape(N_CTX_SEQ, CTX_LEN, N_DIFF_HEADS, 2, DIFF_HEAD_DIM))
        new_v.append(z[ctx_rows, Z_V:Z_V + D_DIFF].reshape(N_CTX_SEQ, CTX_LEN, N_DIFF_HEADS, HEAD_W))

        h0_blocks = jnp.concatenate(
            [jnp.zeros((N_CTX_BLOCKS, 2, D_LRU), f32), state_lru[:, l].astype(f32)], axis=0)
        lru_out, lru_fin = _lru_branch(z, p, h0_blocks)
        n_per = SEQ_BLOCK // CTX_LEN
        fin = lru_fin[:N_CTX_BLOCKS].reshape(N_CTX_BLOCKS, 2, n_per, D_LRU)
        new_lru.append(jnp.transpose(fin, (0, 2, 1, 3)).reshape(N_CTX_SEQ, 2, D_LRU))

        scan_in, bonus, out_gate = _rwkv_pre(z, p)
        yf_c, yb_c, s_c = _rwkv_scan_path(scan_in, 0, N_CTX_SEQ, CTX_LEN, None)
        yf_l, yb_l, _ = _rwkv_scan_path(scan_in, N_CTX_TOK // LAT_LEN, N_LAT_SEQ, LAT_LEN, state_rwkv[:, l])
        new_rwkv.append(s_c)

        diff_out = _attention(z, cache_k[:, l].reshape(N_LAT_SEQ, PAST_LEN, D_DIFF),
                              cache_v[:, l].reshape(N_LAT_SEQ, PAST_LEN, D_DIFF),
                              rope_cos, rope_sin, p, lam_init)

        x, h2, logits = _merge(x, z, lru_out, (yf_c, yb_c), (yf_l, yb_l), bonus, out_gate, diff_out, mod_l, p)
        x = _moe_residual(x, h2, logits, exp_w_gate, exp_w_up, exp_w_down, l, mod_l, final_g,
                          final_norm=(l == DEPTH - 1))

    y_prompt = x[ctx_rows].reshape(x_prompt.shape)
    y_sample = x[lat_rows].reshape(x_sample.shape)
    return (y_prompt, y_sample, jnp.stack(new_k, axis=1), jnp.stack(new_v, axis=1),
            jnp.stack(new_lru, axis=1), jnp.stack(new_rwkv, axis=1))
```
